```python
import math
import jax, jax.numpy as jnp
from jax import lax
import numpy as np

D_MODEL = 1024
BATCH = 2
SEQ = 16384
DEPTH = 2

CHUNK = 64
QBLOCK = 128
HEAD_DIM = 64
POOL_GROUPS = 4
POOL_CH = D_MODEL // 16
POOL_W = POOL_GROUPS * POOL_CH
POOL_WINDOWS = (2, 4, 8, 16)
SB_W = 3 * D_MODEL // 8
SB_HEADS = SB_W // HEAD_DIM
RET_W = 3 * D_MODEL // 8
RET_HEADS = RET_W // HEAD_DIM
MIX_W = POOL_W + SB_W + RET_W
IN_W = POOL_W + 3 * SB_W + 4 * RET_W
ROPE_BASE = 10000.0
N_EXPERTS = 32
TOP_K = 4
D_FF = D_MODEL
SWIGLU_LIMIT = 7.0
SWIGLU_ALPHA = 1.702
EXPERT_BLOCK = 256
DN_ALPHA = (2.0 * DEPTH) ** 0.25
DN_BETA = (8.0 * DEPTH) ** -0.25
LN_EPS = 1e-5

kernel_name = "hybrid_pool_stickbreak_retention_moe_deepnorm"


def layer_norm(x, g, b):
    xf = x.astype(jnp.float32)
    mu = jnp.mean(xf, axis=-1, keepdims=True)
    var = jnp.mean(jnp.square(xf - mu), axis=-1, keepdims=True)
    y = (xf - mu) * lax.rsqrt(var + LN_EPS) * g.astype(jnp.float32) + b.astype(jnp.float32)
    return y.astype(x.dtype)


def rotary(x, pos):
    half = x.shape[-1] // 2
    inv = ROPE_BASE ** (-jnp.arange(half, dtype=jnp.float32) / half)
    ang = pos[:, None] * inv[None, :]
    cos = jnp.cos(ang)[None, :, None, :]
    sin = jnp.sin(ang)[None, :, None, :]
    x1, x2 = x[..., :half], x[..., half:]
    return jnp.concatenate([x1 * cos - x2 * sin, x1 * sin + x2 * cos], axis=-1)


def pool_mixer(u, pool_w, pool_scale):
    S_ = u.shape[1]
    uf = u.astype(jnp.float32)
    cs = jnp.pad(jnp.cumsum(uf, axis=1), ((0, 0), (1, 0), (0, 0)))
    t = jnp.arange(S_)
    outs = []
    for g, w in enumerate(POOL_WINDOWS):
        sl = slice(g * POOL_CH, (g + 1) * POOL_CH)
        lo = jnp.maximum(t + 1 - w, 0)
        win_sum = cs[:, 1:, sl] - cs[:, lo, sl]
        cnt = jnp.minimum(t + 1, w).astype(jnp.float32)
        outs.append(win_sum / cnt[None, :, None] - uf[..., sl])
    pooled = jnp.stack(outs, axis=2)
    mixed = jnp.einsum('bsgc,gcd->bsgd', pooled, pool_w.astype(jnp.float32))
    out = mixed.reshape(u.shape) * pool_scale.astype(jnp.float32)
    return out.astype(u.dtype)


def stick_breaking(q, k, v):
    B_, S_, H, dh = q.shape
    qf = jnp.moveaxis(q, 1, 2).astype(jnp.float32) * (dh ** -0.5)
    kf = jnp.moveaxis(k, 1, 2).astype(jnp.float32)
    vf = jnp.moveaxis(v, 1, 2).astype(jnp.float32)
    nqb = S_ // QBLOCK
    j = jnp.arange(QBLOCK)
    tri_in = (j[:, None] >= j[None, :]).astype(jnp.float32)
    outs = []
    for bi in range(nqb):
        nb = bi + 1
        L = nb * QBLOCK
        qi = qf[:, :, bi * QBLOCK:L]
        z = jnp.einsum('bhqd,bhkd->bhqk', qi, kf[:, :, :L])
        qpos = bi * QBLOCK + j
        causal = jnp.arange(L)[None, :] < qpos[:, None]
        lk = jnp.where(causal, jax.nn.log_sigmoid(-z), 0.0).reshape(B_, H, QBLOCK, nb, QBLOCK)
        r_in = jnp.einsum('bhqnj,js->bhqns', lk, tri_in)
        n = jnp.arange(nb)
        tri_blk = (n[:, None] > n[None, :]).astype(jnp.float32)
        carry = jnp.einsum('bhqm,mn->bhqn', jnp.sum(lk, axis=-1), tri_blk)
        r = (r_in + carry[..., None]).reshape(B_, H, QBLOCK, L)
        a = jnp.where(causal, jnp.exp(z + r), 0.0)
        outs.append(jnp.einsum('bhqk,bhkd->bhqd', a, vf[:, :, :L]))
    o = jnp.concatenate(outs, axis=2)
    o = jnp.moveaxis(o, 1, 2).reshape(B_, S_, H * dh)
    return o.astype(q.dtype)


def retention(q, k, v, gate, norm_g):
    B_, S_, H, dh = q.shape
    pos = jnp.arange(S_, dtype=jnp.float32)
    qf = rotary(q.astype(jnp.float32), pos)
    kf = rotary(k.astype(jnp.float32), pos) * (dh ** -0.5)
    vf = v.astype(jnp.float32)
    log_g = jnp.log(1.0 - 2.0 ** (-5.0 - jnp.arange(H, dtype=jnp.float32)))
    nc = S_ // CHUNK
    qc = qf.reshape(B_, nc, CHUNK, H, dh)
    kc = kf.reshape(B_, nc, CHUNK, H, dh)
    vc = vf.reshape(B_, nc, CHUNK, H, dh)
    c = jnp.arange(CHUNK, dtype=jnp.float32)
    dmat = jnp.exp(jnp.abs(c[:, None] - c[None, :])[None] * log_g[:, None, None])
    scores = jnp.einsum('bnchd,bnmhd->bnhcm', qc, kc) * dmat
    o_intra = jnp.einsum('bnhcm,bnmhd->bnchd', scores, vc)
    v_dec = vc * jnp.exp((CHUNK - 1 - c)[:, None] * log_g[None, :])[None, None, :, :, None]
    kv = jnp.einsum('bnchd,bnche->bnhde', kc, v_dec)
    g_chunk = jnp.exp(CHUNK * log_g)[:, None, None]

    def step(r, kv_n):
        return g_chunk * r + kv_n, r

    _, r_prev = lax.scan(step, jnp.zeros((B_, H, dh, dh), jnp.float32), jnp.moveaxis(kv, 1, 0))
    r_prev = jnp.moveaxis(r_prev, 0, 1)
    xi = jnp.exp((c + 1.0)[:, None] * log_g[None, :])
    o_inter = jnp.einsum('bnchd,bnhde->bnche', qc * xi[None, None, :, :, None], r_prev)
    o = (o_intra + o_inter).reshape(B_, S_, H, dh)
    mu = jnp.mean(o, axis=-1, keepdims=True)
    var = jnp.mean(jnp.square(o - mu), axis=-1, keepdims=True)
    o = (o - mu) * lax.rsqrt(var + LN_EPS) * norm_g.astype(jnp.float32).reshape(H, dh)
    out = jax.nn.silu(gate.astype(jnp.float32)) * o.reshape(B_, S_, H * dh)
    return out.astype(q.dtype)


def expert_ffn(x2, router_w, router_b, w_gate_up, b_gate_up, w_down, b_down):
    N, D = x2.shape
    logits = (x2 @ router_w).astype(jnp.float32) + router_b.astype(jnp.float32)
    top_v, top_i = lax.top_k(logits, TOP_K)
    gates = jax.nn.softmax(top_v, axis=-1)
    A = N * TOP_K
    flat_e = top_i.reshape(-1).astype(jnp.int32)
    flat_g = gates.reshape(-1)
    flat_tok = jnp.arange(A, dtype=jnp.int32) // TOP_K
    order = jnp.argsort(flat_e)
    se, stok, sg = flat_e[order], flat_tok[order], flat_g[order]
    counts = jnp.zeros((N_EXPERTS,), jnp.int32).at[flat_e].add(1)
    starts = jnp.cumsum(counts) - counts
    pcounts = (counts + EXPERT_BLOCK - 1) // EXPERT_BLOCK * EXPERT_BLOCK
    pends = jnp.cumsum(pcounts)
    pstarts = pends - pcounts
    dest = pstarts[se] + (jnp.arange(A, dtype=jnp.int32) - starts[se])
    P = A + N_EXPERTS * EXPERT_BLOCK
    NB = P // EXPERT_BLOCK
    row_tok = jnp.zeros((P,), jnp.int32).at[dest].set(stok)
    row_gate = jnp.zeros((P,), jnp.float32).at[dest].set(sg)
    blk_e = jnp.minimum(jnp.searchsorted(pends, jnp.arange(NB, dtype=jnp.int32) * EXPERT_BLOCK, side='right'), N_EXPERTS - 1)

    def expert_block(args):
        e, toks = args
        xb = x2[toks]
        h = xb @ w_gate_up[e] + b_gate_up[e]
        g, up = h[:, :D_FF], h[:, D_FF:]
        g = jnp.minimum(g, SWIGLU_LIMIT)
        up = jnp.clip(up, -SWIGLU_LIMIT, SWIGLU_LIMIT)
        act = (up + 1.0) * (g * jax.nn.sigmoid(SWIGLU_ALPHA * g))
        return act @ w_down[e] + b_down[e]

    y = lax.map(expert_block, (blk_e, row_tok.reshape(NB, EXPERT_BLOCK)))
    y = y.reshape(P, D) * row_gate[:, None].astype(y.dtype)
    return jnp.zeros_like(x2).at[row_tok].add(y)


def hybrid_layer(x, w_in, b_in, pool_w, pool_scale, ret_norm_g, w_out, b_out, ln1_g, ln1_b,
                 router_w, router_b, w_gate_up, b_gate_up, w_down, b_down, ln2_g, ln2_b):
    B_, S_, D = x.shape
    h = x @ w_in + b_in
    o0 = POOL_W
    u_pool = h[..., :o0]
    sb_q = h[..., o0:o0 + SB_W].reshape(B_, S_, SB_HEADS, HEAD_DIM)
    sb_k = h[..., o0 + SB_W:o0 + 2 * SB_W].reshape(B_, S_, SB_HEADS, HEAD_DIM)
    sb_v = h[..., o0 + 2 * SB_W:o0 + 3 * SB_W].reshape(B_, S_, SB_HEADS, HEAD_DIM)
    o1 = o0 + 3 * SB_W
    r_q = h[..., o1:o1 + RET_W].reshape(B_, S_, RET_HEADS, HEAD_DIM)
    r_k = h[..., o1 + RET_W:o1 + 2 * RET_W].reshape(B_, S_, RET_HEADS, HEAD_DIM)
    r_v = h[..., o1 + 2 * RET_W:o1 + 3 * RET_W].reshape(B_, S_, RET_HEADS, HEAD_DIM)
    r_g = h[..., o1 + 3 * RET_W:o1 + 4 * RET_W]
    mixed = jnp.concatenate([
        pool_mixer(u_pool, pool_w, pool_scale),
        stick_breaking(sb_q, sb_k, sb_v),
        retention(r_q, r_k, r_v, r_g, ret_norm_g),
    ], axis=-1)
    x = layer_norm(DN_ALPHA * x + (mixed @ w_out + b_out), ln1_g, ln1_b)
    ffn = expert_ffn(x.reshape(B_ * S_, D), router_w, router_b, w_gate_up, b_gate_up, w_down, b_down)
    x = layer_norm(DN_ALPHA * x + ffn.reshape(B_, S_, D), ln2_g, ln2_b)
    return x


def setup_inputs(seed: int = 0) -> dict:
    key = jax.random.key(seed)
    ks = jax.random.split(key, 20)
    nrm = jax.random.normal
    col_scale = jnp.concatenate([
        jnp.full((POOL_W,), DN_BETA), jnp.ones((2 * SB_W,)), jnp.full((SB_W,), DN_BETA),
        jnp.ones((2 * RET_W,)), jnp.full((RET_W,), DN_BETA), jnp.ones((RET_W,)),
    ]).astype(jnp.float32)
    return {
        "x": nrm(ks[0], (BATCH, SEQ, D_MODEL), jnp.float32),
        "w_in": nrm(ks[1], (DEPTH, D_MODEL, IN_W), jnp.float32) * (D_MODEL ** -0.5) * col_scale,
        "b_in": 0.02 * nrm(ks[2], (DEPTH, IN_W), jnp.float32),
        "pool_w": nrm(ks[3], (DEPTH, POOL_GROUPS, POOL_CH, POOL_CH), jnp.float32) * (POOL_CH ** -0.5),
        "pool_scale": 1.0 + 0.1 * nrm(ks[4], (DEPTH, POOL_W), jnp.float32),
        "ret_norm_g": 1.0 + 0.1 * nrm(ks[5], (DEPTH, RET_W), jnp.float32),
        "w_out": nrm(ks[6], (DEPTH, MIX_W, D_MODEL), jnp.float32) * (MIX_W ** -0.5) * DN_BETA,
        "b_out": 0.02 * nrm(ks[7], (DEPTH, D_MODEL), jnp.float32),
        "ln1_g": 1.0 + 0.02 * nrm(ks[8], (DEPTH, D_MODEL), jnp.float32),
        "ln1_b": 0.02 * nrm(ks[9], (DEPTH, D_MODEL), jnp.float32),
        "router_w": nrm(ks[10], (DEPTH, D_MODEL, N_EXPERTS), jnp.float32) * (D_MODEL ** -0.5),
        "router_b": 0.01 * nrm(ks[11], (DEPTH, N_EXPERTS), jnp.float32),
        "w_gate_up": nrm(ks[12], (DEPTH, N_EXPERTS, D_MODEL, 2 * D_FF), jnp.float32) * (D_MODEL ** -0.5) * DN_BETA,
        "b_gate_up": 0.02 * nrm(ks[13], (DEPTH, N_EXPERTS, 2 * D_FF), jnp.float32),
        "w_down": nrm(ks[14], (DEPTH, N_EXPERTS, D_FF, D_MODEL), jnp.float32) * (D_FF ** -0.5) * DN_BETA,
        "b_down": 0.02 * nrm(ks[15], (DEPTH, N_EXPERTS, D_MODEL), jnp.float32),
        "ln2_g": 1.0 + 0.02 * nrm(ks[16], (DEPTH, D_MODEL), jnp.float32),
        "ln2_b": 0.02 * nrm(ks[17], (DEPTH, D_MODEL), jnp.float32),
    }


def reference(x, w_in, b_in, pool_w, pool_scale, ret_norm_g, w_out, b_out, ln1_g, ln1_b,
              router_w, router_b, w_gate_up, b_gate_up, w_down, b_down, ln2_g, ln2_b):
    for l in range(DEPTH):
        x = hybrid_layer(x, w_in[l], b_in[l], pool_w[l], pool_scale[l], ret_norm_g[l], w_out[l], b_out[l],
                         ln1_g[l], ln1_b[l], router_w[l], router_b[l], w_gate_up[l], b_gate_up[l],
                         w_down[l], b_down[l], ln2_g[l], ln2_b[l])
    return x
```

```python
import functools
import math

import numpy as np
import jax
import jax.numpy as jnp
from jax import lax
from jax.experimental import pallas as pl
from jax.experimental.pallas import tpu as pltpu

D_MODEL = 1024
DEPTH = 2
CHUNK = 64
HEAD_DIM = 64
POOL_CH = 64
POOL_W = 256
POOL_WINDOWS = (2, 4, 8, 16)
POOL_HALO = 16
SB_W = 384
RET_W = 384
RET_HEADS = 6
IN_W = POOL_W + 3 * SB_W + 4 * RET_W
ROPE_BASE = 10000.0
N_EXPERTS = 32
TOP_K = 4
D_FF = D_MODEL
SWIGLU_LIMIT = 7.0
SWIGLU_ALPHA = 1.702
DN_ALPHA = (2.0 * DEPTH) ** 0.25
LN_EPS = 1e-5

LANES = 128
VMEM_LIMIT = 56 * 1024 * 1024

TM_ROW = 512
TS_POOL = 512
TQ_SB = 128
TS_RET = 512
TM_MOE = 1024
RC_MOE = 160
SB_SKIP = -100.0

BF16 = jnp.bfloat16
F32 = jnp.float32

_NT = (((1,), (1,)), ((), ()))
_TN = (((0,), (0,)), ((), ()))


def _dot(a, b):
    return jnp.dot(a, b, preferred_element_type=F32)


def _params(*sem):
    return pltpu.CompilerParams(dimension_semantics=sem, vmem_limit_bytes=VMEM_LIMIT)


def _inproj_kernel(x_ref, w_ref, b_ref, pool_ref, sb_ref, ret_ref):
    xb = x_ref[...].astype(BF16)
    o1 = POOL_W + 3 * SB_W
    pool_ref[...] = _dot(xb, w_ref[:, :POOL_W]) + b_ref[:, :POOL_W]
    sb_ref[...] = (_dot(xb, w_ref[:, POOL_W:o1]) + b_ref[:, POOL_W:o1]).astype(BF16)
    ret_ref[...] = _dot(xb, w_ref[:, o1:]) + b_ref[:, o1:]


def _inproj(x2, w_bf, b):
    n = x2.shape[0]
    return pl.pallas_call(
        _inproj_kernel,
        grid=(n // TM_ROW,),
        in_specs=[
            pl.BlockSpec((TM_ROW, D_MODEL), lambda i: (i, 0)),
            pl.BlockSpec((D_MODEL, IN_W), lambda i: (0, 0)),
            pl.BlockSpec((1, IN_W), lambda i: (0, 0)),
        ],
        out_specs=[
            pl.BlockSpec((TM_ROW, POOL_W), lambda i: (i, 0)),
            pl.BlockSpec((TM_ROW, 3 * SB_W), lambda i: (i, 0)),
            pl.BlockSpec((TM_ROW, 4 * RET_W), lambda i: (i, 0)),
        ],
        out_shape=[
            jax.ShapeDtypeStruct((n, POOL_W), F32),
            jax.ShapeDtypeStruct((n, 3 * SB_W), BF16),
            jax.ShapeDtypeStruct((n, 4 * RET_W), F32),
        ],
        compiler_params=_params("parallel"),
        name="inproj",
    )(x2, w_bf, b)


def _pool_kernel(cur_ref, halo_ref, w_ref, scale_ref, o_ref):
    i = pl.program_id(1)
    cur = cur_ref[0]
    halo = jnp.where(i > 0, halo_ref[0], 0.0)
    ext = jnp.concatenate([halo, cur], axis=0)
    ts = cur.shape[0]
    a2 = ext[1:] + ext[:-1]
    a4 = a2[2:] + a2[:-2]
    a8 = a4[4:] + a4[:-4]
    a16 = a8[8:] + a8[:-8]
    lane = lax.broadcasted_iota(jnp.int32, (ts, POOL_W), 1)
    grp = lane // POOL_CH
    win = jnp.where(grp == 0, a2[15:15 + ts],
                    jnp.where(grp == 1, a4[13:13 + ts],
                              jnp.where(grp == 2, a8[9:9 + ts], a16[1:1 + ts])))
    width = jnp.where(grp == 0, 2, jnp.where(grp == 1, 4, jnp.where(grp == 2, 8, 16)))
    t = i * ts + lax.broadcasted_iota(jnp.int32, (ts, POOL_W), 0)
    cnt = jnp.minimum(t + 1, width).astype(F32)
    pooled = win / cnt - cur
    mixed = _dot(pooled.astype(BF16), w_ref[...]) * scale_ref[...]
    o_ref[0] = mixed.astype(BF16)


def _pool(u, w_bd, scale, batch, seq):
    u3 = u.reshape(batch, seq, POOL_W)
    per = TS_POOL // POOL_HALO
    out = pl.pallas_call(
        _pool_kernel,
        grid=(batch, seq // TS_POOL),
        in_specs=[
            pl.BlockSpec((1, TS_POOL, POOL_W), lambda b, i: (b, i, 0)),
            pl.BlockSpec((1, POOL_HALO, POOL_W), lambda b, i: (b, jnp.maximum(i * per - 1, 0), 0)),
            pl.BlockSpec((POOL_W, POOL_W), lambda b, i: (0, 0)),
            pl.BlockSpec((1, POOL_W), lambda b, i: (0, 0)),
        ],
        out_specs=pl.BlockSpec((1, TS_POOL, POOL_W), lambda b, i: (b, i, 0)),
        out_shape=jax.ShapeDtypeStruct((batch, seq, POOL_W), BF16),
        compiler_params=_params("parallel", "parallel"),
        name="pool_mixer",
    )(u3, u3, w_bd, scale)
    return out.reshape(batch * seq, POOL_W)


def _sb_kernel(q_ref, k_ref, v_ref, tri_ref, o_ref, acc_ref, carry_ref):
    qi = pl.program_id(2)
    tq = q_ref.shape[1]
    lane = lax.broadcasted_iota(jnp.int32, (tq, LANES), 1)
    q = q_ref[0] * jnp.asarray(HEAD_DIM ** -0.5, BF16)
    zero = jnp.zeros_like(q)
    qh = (jnp.where(lane < HEAD_DIM, q, zero), jnp.where(lane >= HEAD_DIM, q, zero))
    row = lax.broadcasted_iota(jnp.int32, (tq, tq), 0)
    col = lax.broadcasted_iota(jnp.int32, (tq, tq), 1)
    below = col < row

    acc_ref[...] = jnp.zeros_like(acc_ref)
    carry_ref[...] = jnp.zeros_like(carry_ref)

    def tile(kb, diagonal):
        start = pl.multiple_of(kb * tq, tq)
        k = k_ref[0, pl.ds(start, tq), :]
        v = v_ref[0, pl.ds(start, tq), :]
        worst = None
        for h in range(2):
            z = lax.dot_general(qh[h], k, _NT, preferred_element_type=F32)
            lk = jnp.minimum(-z, 0.0) - jnp.log(1.0 + jnp.exp(-jnp.abs(z)))
            if diagonal:
                lk = jnp.where(below, lk, 0.0)
            hi = lk.astype(BF16)
            lo = (lk - hi.astype(F32)).astype(BF16)
            rs = _dot(jnp.concatenate([hi, lo], axis=1), tri_ref[...])
            carry = carry_ref[h]
            a = jnp.exp(z + rs[:, :tq] + carry)
            if diagonal:
                a = jnp.where(below, a, 0.0)
            acc_ref[h] += _dot(a.astype(BF16), v)
            carry = carry + rs[:, tq:]
            carry_ref[h] = carry
            m = jnp.max(carry)
            worst = m if worst is None else jnp.maximum(worst, m)
        return worst

    worst = tile(qi, True)

    def cond(state):
        kb, worst = state
        return jnp.logical_and(kb >= 0, worst > SB_SKIP)

    def body(state):
        kb, _ = state
        return kb - 1, tile(kb, False)

    lax.while_loop(cond, body, (qi - 1, worst))
    o_ref[0] = jnp.where(lane < HEAD_DIM, acc_ref[0], acc_ref[1]).astype(BF16)


def _stick_breaking(sb, tri2, batch, seq):
    sb3 = sb.reshape(batch, seq, 3 * SB_W)
    pairs = SB_W // LANES
    out = pl.pallas_call(
        _sb_kernel,
        grid=(batch, pairs, seq // TQ_SB),
        in_specs=[
            pl.BlockSpec((1, TQ_SB, LANES), lambda b, p, i: (b, i, p)),
            pl.BlockSpec((1, seq, LANES), lambda b, p, i: (b, 0, pairs + p)),
            pl.BlockSpec((1, seq, LANES), lambda b, p, i: (b, 0, 2 * pairs + p)),
            pl.BlockSpec((2 * TQ_SB, 2 * TQ_SB), lambda b, p, i: (0, 0)),
        ],
        out_specs=pl.BlockSpec((1, TQ_SB, LANES), lambda b, p, i: (b, i, p)),
        out_shape=jax.ShapeDtypeStruct((batch, seq, SB_W), BF16),
        scratch_shapes=[
            pltpu.VMEM((2, TQ_SB, LANES), F32),
            pltpu.VMEM((2, TQ_SB, TQ_SB), F32),
        ],
        compiler_params=_params("parallel", "parallel", "arbitrary"),
        name="stick_breaking",
    )(sb3, sb3, sb3, tri2)
    return out.reshape(batch * seq, SB_W)


def _ret_kernel(q_ref, k_ref, v_ref, g_ref, cos_ref, sin_ref, dmat_ref, dec_ref, xi_ref, gch_ref,
                ng_ref, o_ref, state_ref, obuf_ref):
    si = pl.program_id(2)
    ts = q_ref.shape[1]

    @pl.when(si == 0)
    def _():
        state_ref[...] = jnp.zeros_like(state_ref)

    lane = lax.broadcasted_iota(jnp.int32, (ts, LANES), 1)
    first_half = (lane % HEAD_DIM) < (HEAD_DIM // 2)
    cos = cos_ref[...]
    sin = sin_ref[...]

    def rot(x):
        partner = jnp.where(first_half, pltpu.roll(x, LANES - HEAD_DIM // 2, 1),
                            pltpu.roll(x, HEAD_DIM // 2, 1))
        return x * cos + partner * sin

    qr = rot(q_ref[0])
    kr = rot(k_ref[0]) * (HEAD_DIM ** -0.5)
    v = v_ref[0]

    lane_c = lax.broadcasted_iota(jnp.int32, (CHUNK, LANES), 1)
    head0 = lane_c < HEAD_DIM
    r_i = lax.broadcasted_iota(jnp.int32, (LANES, LANES), 0) // HEAD_DIM
    c_i = lax.broadcasted_iota(jnp.int32, (LANES, LANES), 1) // HEAD_DIM
    same_head = r_i == c_i
    dmat = dmat_ref[0]
    dec = dec_ref[0]
    xi = xi_ref[0]
    gch = gch_ref[0]

    state = state_ref[...]
    for n in range(ts // CHUNK):
        sl = slice(n * CHUNK, (n + 1) * CHUNK)
        qc, kc, vc = qr[sl], kr[sl], v[sl]
        kcb = kc.astype(BF16)
        vcb = vc.astype(BF16)
        qstack = jnp.concatenate([jnp.where(head0, qc, 0.0), jnp.where(head0, 0.0, qc)], axis=0)
        scores = lax.dot_general(qstack.astype(BF16), kcb, _NT, preferred_element_type=F32) * dmat
        o2 = _dot(scores.astype(BF16), vcb)
        o_intra = jnp.where(head0, o2[:CHUNK], o2[CHUNK:])
        o_inter = _dot((qc * xi).astype(BF16), state.astype(BF16))
        obuf_ref[sl, :] = o_intra + o_inter
        kv = lax.dot_general(kcb, (vc * dec).astype(BF16), _TN, preferred_element_type=F32)
        state = gch * state + jnp.where(same_head, kv, 0.0)
    state_ref[...] = state

    o = obuf_ref[...]
    lane_head0 = lane < HEAD_DIM
    inv = 1.0 / HEAD_DIM

    def head_mean(x):
        m0 = jnp.sum(jnp.where(lane_head0, x, 0.0), axis=-1, keepdims=True) * inv
        m1 = jnp.sum(jnp.where(lane_head0, 0.0, x), axis=-1, keepdims=True) * inv
        return jnp.where(lane_head0, m0, m1)

    mu = head_mean(o)
    oc = o - mu
    var = head_mean(oc * oc)
    normed = oc * lax.rsqrt(var + LN_EPS) * ng_ref[...]
    gate = g_ref[0]
    o_ref[0] = (gate / (1.0 + jnp.exp(-gate)) * normed).astype(BF16)


def _retention(ret, consts, norm_g, batch, seq):
    cos, sin, dmat, dec, xi, gch = consts
    ret3 = ret.reshape(batch, seq, 4 * RET_W)
    pairs = RET_W // LANES
    out = pl.pallas_call(
        _ret_kernel,
        grid=(batch, pairs, seq // TS_RET),
        in_specs=[
            pl.BlockSpec((1, TS_RET, LANES), lambda b, p, i: (b, i, p)),
            pl.BlockSpec((1, TS_RET, LANES), lambda b, p, i: (b, i, pairs + p)),
            pl.BlockSpec((1, TS_RET, LANES), lambda b, p, i: (b, i, 2 * pairs + p)),
            pl.BlockSpec((1, TS_RET, LANES), lambda b, p, i: (b, i, 3 * pairs + p)),
            pl.BlockSpec((TS_RET, LANES), lambda b, p, i: (i, 0)),
            pl.BlockSpec((TS_RET, LANES), lambda b, p, i: (i, 0)),
            pl.BlockSpec((1, 2 * CHUNK, CHUNK), lambda b, p, i: (p, 0, 0)),
            pl.BlockSpec((1, CHUNK, LANES), lambda b, p, i: (p, 0, 0)),
            pl.BlockSpec((1, CHUNK, LANES), lambda b, p, i: (p, 0, 0)),
            pl.BlockSpec((1, 1, LANES), lambda b, p, i: (p, 0, 0)),
            pl.BlockSpec((1, LANES), lambda b, p, i: (0, p)),
        ],
        out_specs=pl.BlockSpec((1, TS_RET, LANES), lambda b, p, i: (b, i, p)),
        out_shape=jax.ShapeDtypeStruct((batch, seq, RET_W), BF16),
        scratch_shapes=[
            pltpu.VMEM((LANES, LANES), F32),
            pltpu.VMEM((TS_RET, LANES), F32),
        ],
        compiler_params=_params("parallel", "parallel", "arbitrary"),
        name="retention",
    )(ret3, ret3, ret3, ret3, cos, sin, dmat, dec, xi, gch, norm_g)
    return out.reshape(batch * seq, RET_W)


def _retention_consts(seq):
    half = HEAD_DIM // 2
    pos = jnp.arange(seq, dtype=F32)
    inv = ROPE_BASE ** (-jnp.arange(half, dtype=F32) / half)
    ang = pos[:, None] * inv[None, :]
    cos_h, sin_h = jnp.cos(ang), jnp.sin(ang)
    cos = jnp.tile(cos_h, (1, 2 * LANES // HEAD_DIM))
    sin = jnp.tile(jnp.concatenate([-sin_h, sin_h], axis=1), (1, LANES // HEAD_DIM))
    log_g = jnp.log(1.0 - 2.0 ** (-5.0 - jnp.arange(RET_HEADS, dtype=F32)))
    c = jnp.arange(CHUNK, dtype=F32)
    dmat = jnp.exp(jnp.abs(c[:, None] - c[None, :])[None] * log_g[:, None, None])
    dmat = dmat.reshape(RET_HEADS // 2, 2 * CHUNK, CHUNK)
    lane_log_g = jnp.repeat(log_g, HEAD_DIM).reshape(RET_HEADS // 2, 1, LANES)
    dec = jnp.exp((CHUNK - 1 - c)[None, :, None] * lane_log_g)
    xi = jnp.exp((c + 1.0)[None, :, None] * lane_log_g)
    gch = jnp.exp(CHUNK * lane_log_g)
    return cos, sin, dmat, dec, xi, gch


def _layer_norm(z, g, b):
    mu = jnp.mean(z, axis=-1, keepdims=True)
    zc = z - mu
    var = jnp.mean(zc * zc, axis=-1, keepdims=True)
    return zc * lax.rsqrt(var + LN_EPS) * g + b


def _outproj_kernel(x_ref, p_ref, s_ref, r_ref, w_ref, b_ref, g_ref, be_ref, rw_ref, rb_ref,
                    x1_ref, x1b_ref, gates_ref):
    o1 = POOL_W + SB_W
    y = (_dot(p_ref[...], w_ref[:POOL_W, :]) + _dot(s_ref[...], w_ref[POOL_W:o1, :])
         + _dot(r_ref[...], w_ref[o1:, :]))
    x1 = _layer_norm(DN_ALPHA * x_ref[...] + (y + b_ref[...]), g_ref[...], be_ref[...])
    x1_ref[...] = x1
    x1b_ref[...] = x1.astype(BF16)

    logits = jnp.dot(x1, rw_ref[...], precision=lax.Precision.HIGHEST,
                     preferred_element_type=F32) + rb_ref[...]
    tm = logits.shape[0]
    lane = lax.broadcasted_iota(jnp.int32, (tm, LANES), 1)
    vals = jnp.where(lane < N_EXPERTS, logits, -jnp.inf)
    top_v, top_sel = [], []
    for _ in range(TOP_K):
        m = jnp.max(vals, axis=-1, keepdims=True)
        idx = jnp.min(jnp.where(vals == m, lane, LANES), axis=-1, keepdims=True)
        sel = lane == idx
        vals = jnp.where(sel, -jnp.inf, vals)
        top_v.append(m)
        top_sel.append(sel)
    ex = [jnp.exp(m - top_v[0]) for m in top_v]
    den = ex[0] + ex[1] + ex[2] + ex[3]
    gates = jnp.full((tm, LANES), -1.0, F32)
    for sel, e in zip(top_sel, ex):
        gates = jnp.where(sel, e / den, gates)
    gates_ref[...] = gates


def _outproj(x2, pool_o, sb_o, ret_o, w_bf, b, g, be, rw, rb):
    n = x2.shape[0]
    row = lambda i: (i, 0)
    fixed = lambda i: (0, 0)
    return pl.pallas_call(
        _outproj_kernel,
        grid=(n // TM_ROW,),
        in_specs=[
            pl.BlockSpec((TM_ROW, D_MODEL), row),
            pl.BlockSpec((TM_ROW, POOL_W), row),
            pl.BlockSpec((TM_ROW, SB_W), row),
            pl.BlockSpec((TM_ROW, RET_W), row),
            pl.BlockSpec((D_MODEL, D_MODEL), fixed),
            pl.BlockSpec((1, D_MODEL), fixed),
            pl.BlockSpec((1, D_MODEL), fixed),
            pl.BlockSpec((1, D_MODEL), fixed),
            pl.BlockSpec((D_MODEL, LANES), fixed),
            pl.BlockSpec((1, LANES), fixed),
        ],
        out_specs=[
            pl.BlockSpec((TM_ROW, D_MODEL), row),
            pl.BlockSpec((TM_ROW, D_MODEL), row),
            pl.BlockSpec((TM_ROW, LANES), row),
        ],
        out_shape=[
            jax.ShapeDtypeStruct((n, D_MODEL), F32),
            jax.ShapeDtypeStruct((n, D_MODEL), BF16),
            jax.ShapeDtypeStruct((n, LANES), F32),
        ],
        compiler_params=_params("parallel"),
        name="outproj_ln_router",
    )(x2, pool_o, sb_o, ret_o, w_bf, b, g, be, rw, rb)


ROUTE_SUB = 256


def _route_kernel(gates_ref, tril_ref, post_ref, gatet_ref, phi_ref, plo_ref, cnt_ref):
    tm = gates_ref.shape[0]
    carry = jnp.zeros((1, LANES), F32)
    for j in range(tm // ROUTE_SUB):
        sl = slice(j * ROUTE_SUB, (j + 1) * ROUTE_SUB)
        sel = gates_ref[sl, :] >= 0.0
        self_ = jnp.where(sel, 1.0, 0.0)
        incl = _dot(tril_ref[...], self_.astype(BF16))
        pos = incl - self_ + carry
        carry = carry + incl[ROUTE_SUB - 1:ROUTE_SUB, :]
        hi = jnp.floor((pos + 0.5) * (1.0 / RC_MOE))
        lo = pos - hi * RC_MOE
        phi_ref[sl, :] = jnp.where(sel, hi, -1.0).astype(BF16)
        plo_ref[sl, :] = jnp.where(sel, lo, -1.0).astype(BF16)
        post = jnp.where(sel, pos, -1.0).T
        post_ref[0, :, sl] = post[:N_EXPERTS, :]
        gatet_ref[0, :, sl] = gates_ref[sl, :].T[:N_EXPERTS, :]
    cnt_ref[0] = carry.astype(jnp.int32)


def _route(gates, tril):
    n = gates.shape[0]
    nt = n // TM_MOE
    return pl.pallas_call(
        _route_kernel,
        grid=(nt,),
        in_specs=[
            pl.BlockSpec((TM_MOE, LANES), lambda i: (i, 0)),
            pl.BlockSpec((ROUTE_SUB, ROUTE_SUB), lambda i: (0, 0)),
        ],
        out_specs=[
            pl.BlockSpec((1, N_EXPERTS, TM_MOE), lambda i: (i, 0, 0)),
            pl.BlockSpec((1, N_EXPERTS, TM_MOE), lambda i: (i, 0, 0)),
            pl.BlockSpec((TM_MOE, LANES), lambda i: (i, 0)),
            pl.BlockSpec((TM_MOE, LANES), lambda i: (i, 0)),
            pl.BlockSpec((1, 1, LANES), lambda i: (i, 0, 0)),
        ],
        out_shape=[
            jax.ShapeDtypeStruct((nt, N_EXPERTS, TM_MOE), F32),
            jax.ShapeDtypeStruct((nt, N_EXPERTS, TM_MOE), F32),
            jax.ShapeDtypeStruct((n, LANES), BF16),
            jax.ShapeDtypeStruct((n, LANES), BF16),
            jax.ShapeDtypeStruct((nt, 1, LANES), jnp.int32),
        ],
        compiler_params=_params("parallel"),
        name="route",
    )(gates, tril)


def _moe_kernel(cnt_ref, xb_ref, post_ref, gatet_ref, phi_ref, plo_ref, wgu_ref, bgu_ref, wd_ref,
                bd_ref, o_ref):
    i = pl.program_id(0)
    e = pl.program_id(1)
    tm = xb_ref.shape[0]

    @pl.when(e == 0)
    def _():
        o_ref[...] = jnp.zeros_like(o_ref)

    cnt = cnt_ref[i * N_EXPERTS + e]

    @pl.when(cnt > 0)
    def _():
        prow = post_ref[0, pl.ds(e, 1), :].astype(jnp.int32)
        grow = gatet_ref[0, pl.ds(e, 1), :]
        pick = jnp.where(lax.broadcasted_iota(jnp.int32, (LANES, RC_MOE), 0) == e, 1.0, 0.0).astype(BF16)
        col_hi = _dot(phi_ref[...], pick)
        col_lo = _dot(plo_ref[...], pick)
        r_rows = lax.broadcasted_iota(jnp.int32, (RC_MOE, tm), 0)
        r_cols = lax.broadcasted_iota(jnp.int32, (tm, RC_MOE), 1).astype(F32)

        def chunk(c, carry):
            onehot = (r_rows + c * RC_MOE) == prow
            xg = _dot(jnp.where(onehot, 1.0, 0.0).astype(BF16), xb_ref[...]).astype(BF16)
            h = _dot(xg, wgu_ref[0]) + bgu_ref[0]
            g = jnp.minimum(h[:, :D_FF], SWIGLU_LIMIT)
            up = jnp.clip(h[:, D_FF:], -SWIGLU_LIMIT, SWIGLU_LIMIT)
            act = (up + 1.0) * (g / (1.0 + jnp.exp(-SWIGLU_ALPHA * g)))
            y = _dot(act.astype(BF16), wd_ref[0]) + bd_ref[0]
            gate = jnp.sum(jnp.where(onehot, grow, 0.0), axis=1, keepdims=True)
            yg = (y * gate).astype(BF16)
            back = jnp.where(col_hi == c.astype(F32), col_lo, -1.0) == r_cols
            o_ref[...] += _dot(jnp.where(back, 1.0, 0.0).astype(BF16), yg)
            return carry

        lax.fori_loop(0, (cnt + RC_MOE - 1) // RC_MOE, chunk, 0)


def _moe(cnt, x1b, post, gatet, phi, plo, wgu, bgu, wd, bd):
    n = x1b.shape[0]
    nt = n // TM_MOE
    grid_spec = pltpu.PrefetchScalarGridSpec(
        num_scalar_prefetch=1,
        grid=(nt, N_EXPERTS),
        in_specs=[
            pl.BlockSpec((TM_MOE, D_MODEL), lambda i, e, c: (i, 0)),
            pl.BlockSpec((1, N_EXPERTS, TM_MOE), lambda i, e, c: (i, 0, 0)),
            pl.BlockSpec((1, N_EXPERTS, TM_MOE), lambda i, e, c: (i, 0, 0)),
            pl.BlockSpec((TM_MOE, LANES), lambda i, e, c: (i, 0)),
            pl.BlockSpec((TM_MOE, LANES), lambda i, e, c: (i, 0)),
            pl.BlockSpec((1, D_MODEL, 2 * D_FF), lambda i, e, c: (e, 0, 0)),
            pl.BlockSpec((1, 1, 2 * D_FF), lambda i, e, c: (e, 0, 0)),
            pl.BlockSpec((1, D_FF, D_MODEL), lambda i, e, c: (e, 0, 0)),
            pl.BlockSpec((1, 1, D_MODEL), lambda i, e, c: (e, 0, 0)),
        ],
        out_specs=pl.BlockSpec((TM_MOE, D_MODEL), lambda i, e, c: (i, 0)),
    )
    return pl.pallas_call(
        _moe_kernel,
        grid_spec=grid_spec,
        out_shape=jax.ShapeDtypeStruct((n, D_MODEL), F32),
        compiler_params=_params("parallel", "arbitrary"),
        name="moe_ffn",
    )(cnt, x1b, post, gatet, phi, plo, wgu, bgu, wd, bd)


def _ln2_kernel(x_ref, f_ref, g_ref, b_ref, o_ref):
    o_ref[...] = _layer_norm(DN_ALPHA * x_ref[...] + f_ref[...], g_ref[...], b_ref[...])


def _ln2(x1, ffn, g, b):
    n = x1.shape[0]
    row = lambda i: (i, 0)
    fixed = lambda i: (0, 0)
    return pl.pallas_call(
        _ln2_kernel,
        grid=(n // TM_ROW,),
        in_specs=[
            pl.BlockSpec((TM_ROW, D_MODEL), row),
            pl.BlockSpec((TM_ROW, D_MODEL), row),
            pl.BlockSpec((1, D_MODEL), fixed),
            pl.BlockSpec((1, D_MODEL), fixed),
        ],
        out_specs=pl.BlockSpec((TM_ROW, D_MODEL), row),
        out_shape=jax.ShapeDtypeStruct((n, D_MODEL), F32),
        compiler_params=_params("parallel"),
        name="residual_ln",
    )(x1, ffn, g, b)


def _block_diag(pool_w):
    groups = pool_w.shape[0]
    out = jnp.zeros((POOL_W, POOL_W), pool_w.dtype)
    for gi in range(groups):
        out = out.at[gi * POOL_CH:(gi + 1) * POOL_CH, gi * POOL_CH:(gi + 1) * POOL_CH].set(pool_w[gi])
    return out


def _layer(x2, batch, seq, consts, w_in, b_in, pool_w, pool_scale, ret_norm_g, w_out, b_out, ln1_g,
           ln1_b, router_w, router_b, w_gate_up, b_gate_up, w_down, b_down, ln2_g, ln2_b):
    ret_consts, tri2, tril = consts
    row = lambda a: a.reshape(1, -1).astype(F32)
    u_pool, sb, ret = _inproj(x2, w_in.astype(BF16), row(b_in))
    pool_o = _pool(u_pool, _block_diag(pool_w).astype(BF16), row(pool_scale), batch, seq)
    sb_o = _stick_breaking(sb, tri2, batch, seq)
    ret_o = _retention(ret, ret_consts, row(ret_norm_g), batch, seq)
    rw = jnp.pad(router_w.astype(F32), ((0, 0), (0, LANES - N_EXPERTS)))
    rb = jnp.pad(router_b.astype(F32), (0, LANES - N_EXPERTS)).reshape(1, LANES)
    x1, x1b, gates = _outproj(x2, pool_o, sb_o, ret_o, w_out.astype(BF16), row(b_out), row(ln1_g),
                              row(ln1_b), rw, rb)
    post, gatet, phi, plo, cnt = _route(gates, tril)
    cnt_flat = cnt[:, 0, :N_EXPERTS].reshape(-1)
    ffn = _moe(cnt_flat, x1b, post, gatet, phi, plo, w_gate_up.astype(BF16),
               b_gate_up.reshape(N_EXPERTS, 1, 2 * D_FF).astype(F32), w_down.astype(BF16),
               b_down.reshape(N_EXPERTS, 1, D_MODEL).astype(F32))
    return _ln2(x1, ffn, row(ln2_g), row(ln2_b))


def kernel(x, w_in, b_in, pool_w, pool_scale, ret_norm_g, w_out, b_out, ln1_g, ln1_b, router_w, router_b,
           w_gate_up, b_gate_up, w_down, b_down, ln2_g, ln2_b):
    batch, seq, d = x.shape
    assert d == D_MODEL and seq % TS_RET == 0 and seq % TQ_SB == 0 and (batch * seq) % TM_MOE == 0
    j = np.arange(TQ_SB)
    tri = (j[:, None] >= j[None, :]).astype(np.float32)
    half = np.concatenate([tri, np.ones_like(tri)], axis=1)
    tri2 = jnp.asarray(np.concatenate([half, half], axis=0), BF16)
    r = np.arange(ROUTE_SUB)
    tril = jnp.asarray((r[:, None] >= r[None, :]).astype(np.float32), BF16)
    consts = (_retention_consts(seq), tri2, tril)
    x2 = x.reshape(batch * seq, d)
    for l in range(DEPTH):
        x2 = _layer(x2, batch, seq, consts, w_in[l], b_in[l], pool_w[l], pool_scale[l], ret_norm_g[l],
                    w_out[l], b_out[l], ln1_g[l], ln1_b[l], router_w[l], router_b[l], w_gate_up[l],
                    b_gate_up[l], w_down[l], b_down[l], ln2_g[l], ln2_b[l])
    return x2.reshape(batch, seq, d)
```

```python
import numpy as np
import jax
import jax.numpy as jnp
from jax import lax
from jax.experimental import pallas as pl
from jax.experimental.pallas import tpu as pltpu

D_MODEL = 1024
DEPTH = 2
CHUNK = 64
HEAD_DIM = 64
POOL_CH = 64
POOL_W = 256
POOL_HALO = 16
SB_W = 384
RET_W = 384
RET_HEADS = 6
IN_W = POOL_W + 3 * SB_W + 4 * RET_W
ROPE_BASE = 10000.0
N_EXPERTS = 32
TOP_K = 4
D_FF = D_MODEL
SWIGLU_LIMIT = 7.0
SWIGLU_ALPHA = 1.702
DN_ALPHA = (2.0 * DEPTH) ** 0.25
LN_EPS = 1e-5

LANES = 128
VMEM_LIMIT = 56 * 1024 * 1024

TM_ROW = 512
TS_POOL = 512
TQ_SB = 128
TS_RET = 512
TM_MOE = 512
RC_MOE = 80
FFN_TILES = 8
SLOTS = N_EXPERTS * RC_MOE
SB_SKIP = -100.0

BF16 = jnp.bfloat16
F32 = jnp.float32

_NT = (((1,), (1,)), ((), ()))
_TN = (((0,), (0,)), ((), ()))


def _dot(a, b):
    return jnp.dot(a, b, preferred_element_type=F32)


def _params(*sem):
    return pltpu.CompilerParams(dimension_semantics=sem, vmem_limit_bytes=VMEM_LIMIT)


def _inproj_kernel(x_ref, w_ref, b_ref, pool_ref, sb_ref, ret_ref):
    xb = x_ref[...].astype(BF16)
    o1 = POOL_W + 3 * SB_W
    pool_ref[...] = _dot(xb, w_ref[:, :POOL_W]) + b_ref[:, :POOL_W]
    sb_ref[...] = (_dot(xb, w_ref[:, POOL_W:o1]) + b_ref[:, POOL_W:o1]).astype(BF16)
    ret_ref[...] = _dot(xb, w_ref[:, o1:]) + b_ref[:, o1:]


def _inproj(x2, w_bf, b):
    n = x2.shape[0]
    return pl.pallas_call(
        _inproj_kernel,
        grid=(n // TM_ROW,),
        in_specs=[
            pl.BlockSpec((TM_ROW, D_MODEL), lambda i: (i, 0)),
            pl.BlockSpec((D_MODEL, IN_W), lambda i: (0, 0)),
            pl.BlockSpec((1, IN_W), lambda i: (0, 0)),
        ],
        out_specs=[
            pl.BlockSpec((TM_ROW, POOL_W), lambda i: (i, 0)),
            pl.BlockSpec((TM_ROW, 3 * SB_W), lambda i: (i, 0)),
            pl.BlockSpec((TM_ROW, 4 * RET_W), lambda i: (i, 0)),
        ],
        out_shape=[
            jax.ShapeDtypeStruct((n, POOL_W), F32),
            jax.ShapeDtypeStruct((n, 3 * SB_W), BF16),
            jax.ShapeDtypeStruct((n, 4 * RET_W), F32),
        ],
        compiler_params=_params("parallel"),
        name="inproj",
    )(x2, w_bf, b)


def _pool_kernel(cur_ref, halo_ref, w_ref, scale_ref, o_ref):
    i = pl.program_id(1)
    cur = cur_ref[0]
    halo = jnp.where(i > 0, halo_ref[0], 0.0)
    ext = jnp.concatenate([halo, cur], axis=0)
    ts = cur.shape[0]
    a2 = ext[1:] + ext[:-1]
    a4 = a2[2:] + a2[:-2]
    a8 = a4[4:] + a4[:-4]
    a16 = a8[8:] + a8[:-8]
    lane = lax.broadcasted_iota(jnp.int32, (ts, POOL_W), 1)
    grp = lane // POOL_CH
    win = jnp.where(grp == 0, a2[15:15 + ts],
                    jnp.where(grp == 1, a4[13:13 + ts],
                              jnp.where(grp == 2, a8[9:9 + ts], a16[1:1 + ts])))
    width = jnp.where(grp == 0, 2, jnp.where(grp == 1, 4, jnp.where(grp == 2, 8, 16)))
    t = i * ts + lax.broadcasted_iota(jnp.int32, (ts, POOL_W), 0)
    cnt = jnp.minimum(t + 1, width).astype(F32)
    pooled = win / cnt - cur
    mixed = _dot(pooled.astype(BF16), w_ref[...]) * scale_ref[...]
    o_ref[0] = mixed.astype(BF16)


def _pool(u, w_bd, scale, batch, seq):
    u3 = u.reshape(batch, seq, POOL_W)
    per = TS_POOL // POOL_HALO
    out = pl.pallas_call(
        _pool_kernel,
        grid=(batch, seq // TS_POOL),
        in_specs=[
            pl.BlockSpec((1, TS_POOL, POOL_W), lambda b, i: (b, i, 0)),
            pl.BlockSpec((1, POOL_HALO, POOL_W), lambda b, i: (b, jnp.maximum(i * per - 1, 0), 0)),
            pl.BlockSpec((POOL_W, POOL_W), lambda b, i: (0, 0)),
            pl.BlockSpec((1, POOL_W), lambda b, i: (0, 0)),
        ],
        out_specs=pl.BlockSpec((1, TS_POOL, POOL_W), lambda b, i: (b, i, 0)),
        out_shape=jax.ShapeDtypeStruct((batch, seq, POOL_W), BF16),
        compiler_params=_params("parallel", "parallel"),
        name="pool_mixer",
    )(u3, u3, w_bd, scale)
    return out.reshape(batch * seq, POOL_W)


def _sb_kernel(q_ref, k_ref, v_ref, tri_ref, o_ref, acc_ref, carry_ref):
    qi = pl.program_id(2)
    tq = q_ref.shape[1]
    lane = lax.broadcasted_iota(jnp.int32, (tq, LANES), 1)
    q = q_ref[0] * jnp.asarray(HEAD_DIM ** -0.5, BF16)
    zero = jnp.zeros_like(q)
    qh = (jnp.where(lane < HEAD_DIM, q, zero), jnp.where(lane >= HEAD_DIM, q, zero))
    row = lax.broadcasted_iota(jnp.int32, (tq, tq), 0)
    col = lax.broadcasted_iota(jnp.int32, (tq, tq), 1)
    below = col < row

    acc_ref[...] = jnp.zeros_like(acc_ref)
    carry_ref[...] = jnp.zeros_like(carry_ref)

    def tile(kb, diagonal):
        start = pl.multiple_of(kb * tq, tq)
        k = k_ref[0, pl.ds(start, tq), :]
        v = v_ref[0, pl.ds(start, tq), :]
        worst = None
        for h in range(2):
            z = lax.dot_general(qh[h], k, _NT, preferred_element_type=F32)
            lk = jnp.minimum(-z, 0.0) - jnp.log(1.0 + jnp.exp(-jnp.abs(z)))
            if diagonal:
                lk = jnp.where(below, lk, 0.0)
            hi = lk.astype(BF16)
            lo = (lk - hi.astype(F32)).astype(BF16)
            rs = _dot(jnp.concatenate([hi, lo], axis=1), tri_ref[...])
            carry = carry_ref[h]
            a = jnp.exp(z + rs[:, :tq] + carry)
            if diagonal:
                a = jnp.where(below, a, 0.0)
            acc_ref[h] += _dot(a.astype(BF16), v)
            carry = carry + rs[:, tq:]
            carry_ref[h] = carry
            m = jnp.max(carry)
            worst = m if worst is None else jnp.maximum(worst, m)
        return worst

    worst = tile(qi, True)

    def cond(state):
        kb, worst = state
        return jnp.logical_and(kb >= 0, worst > SB_SKIP)

    def body(state):
        kb, _ = state
        return kb - 1, tile(kb, False)

    lax.while_loop(cond, body, (qi - 1, worst))
    o_ref[0] = jnp.where(lane < HEAD_DIM, acc_ref[0], acc_ref[1]).astype(BF16)


def _stick_breaking(sb, tri2, batch, seq):
    sb3 = sb.reshape(batch, seq, 3 * SB_W)
    pairs = SB_W // LANES
    out = pl.pallas_call(
        _sb_kernel,
        grid=(batch, pairs, seq // TQ_SB),
        in_specs=[
            pl.BlockSpec((1, TQ_SB, LANES), lambda b, p, i: (b, i, p)),
            pl.BlockSpec((1, seq, LANES), lambda b, p, i: (b, 0, pairs + p)),
            pl.BlockSpec((1, seq, LANES), lambda b, p, i: (b, 0, 2 * pairs + p)),
            pl.BlockSpec((2 * TQ_SB, 2 * TQ_SB), lambda b, p, i: (0, 0)),
        ],
        out_specs=pl.BlockSpec((1, TQ_SB, LANES), lambda b, p, i: (b, i, p)),
        out_shape=jax.ShapeDtypeStruct((batch, seq, SB_W), BF16),
        scratch_shapes=[
            pltpu.VMEM((2, TQ_SB, LANES), F32),
            pltpu.VMEM((2, TQ_SB, TQ_SB), F32),
        ],
        compiler_params=_params("parallel", "parallel", "arbitrary"),
        name="stick_breaking",
    )(sb3, sb3, sb3, tri2)
    return out.reshape(batch * seq, SB_W)


def _ret_kernel(q_ref, k_ref, v_ref, g_ref, cos_ref, sin_ref, dmat_ref, dec_ref, xi_ref, gch_ref,
                ng_ref, o_ref, state_ref, obuf_ref):
    si = pl.program_id(2)
    ts = q_ref.shape[1]

    @pl.when(si == 0)
    def _():
        state_ref[...] = jnp.zeros_like(state_ref)

    lane = lax.broadcasted_iota(jnp.int32, (ts, LANES), 1)
    first_half = (lane % HEAD_DIM) < (HEAD_DIM // 2)
    cos = cos_ref[...]
    sin = sin_ref[...]

    def rot(x):
        partner = jnp.where(first_half, pltpu.roll(x, LANES - HEAD_DIM // 2, 1),
                            pltpu.roll(x, HEAD_DIM // 2, 1))
        return x * cos + partner * sin

    qr = rot(q_ref[0])
    kr = rot(k_ref[0]) * (HEAD_DIM ** -0.5)
    v = v_ref[0]

    lane_c = lax.broadcasted_iota(jnp.int32, (CHUNK, LANES), 1)
    head0 = lane_c < HEAD_DIM
    r_i = lax.broadcasted_iota(jnp.int32, (LANES, LANES), 0) // HEAD_DIM
    c_i = lax.broadcasted_iota(jnp.int32, (LANES, LANES), 1) // HEAD_DIM
    same_head = r_i == c_i
    dmat = dmat_ref[0]
    dec = dec_ref[0]
    xi = xi_ref[0]
    gch = gch_ref[0]

    state = state_ref[...]
    for n in range(ts // CHUNK):
        sl = slice(n * CHUNK, (n + 1) * CHUNK)
        qc, kc, vc = qr[sl], kr[sl], v[sl]
        kcb = kc.astype(BF16)
        vcb = vc.astype(BF16)
        qstack = jnp.concatenate([jnp.where(head0, qc, 0.0), jnp.where(head0, 0.0, qc)], axis=0)
        scores = lax.dot_general(qstack.astype(BF16), kcb, _NT, preferred_element_type=F32) * dmat
        o2 = _dot(scores.astype(BF16), vcb)
        o_intra = jnp.where(head0, o2[:CHUNK], o2[CHUNK:])
        o_inter = _dot((qc * xi).astype(BF16), state.astype(BF16))
        obuf_ref[sl, :] = o_intra + o_inter
        kv = lax.dot_general(kcb, (vc * dec).astype(BF16), _TN, preferred_element_type=F32)
        state = gch * state + jnp.where(same_head, kv, 0.0)
    state_ref[...] = state

    o = obuf_ref[...]
    lane_head0 = lane < HEAD_DIM
    inv = 1.0 / HEAD_DIM

    def head_mean(x):
        m0 = jnp.sum(jnp.where(lane_head0, x, 0.0), axis=-1, keepdims=True) * inv
        m1 = jnp.sum(jnp.where(lane_head0, 0.0, x), axis=-1, keepdims=True) * inv
        return jnp.where(lane_head0, m0, m1)

    mu = head_mean(o)
    oc = o - mu
    var = head_mean(oc * oc)
    normed = oc * lax.rsqrt(var + LN_EPS) * ng_ref[...]
    gate = g_ref[0]
    o_ref[0] = (gate / (1.0 + jnp.exp(-gate)) * normed).astype(BF16)


def _retention(ret, consts, norm_g, batch, seq):
    cos, sin, dmat, dec, xi, gch = consts
    ret3 = ret.reshape(batch, seq, 4 * RET_W)
    pairs = RET_W // LANES
    out = pl.pallas_call(
        _ret_kernel,
        grid=(batch, pairs, seq // TS_RET),
        in_specs=[
            pl.BlockSpec((1, TS_RET, LANES), lambda b, p, i: (b, i, p)),
            pl.BlockSpec((1, TS_RET, LANES), lambda b, p, i: (b, i, pairs + p)),
            pl.BlockSpec((1, TS_RET, LANES), lambda b, p, i: (b, i, 2 * pairs + p)),
            pl.BlockSpec((1, TS_RET, LANES), lambda b, p, i: (b, i, 3 * pairs + p)),
            pl.BlockSpec((TS_RET, LANES), lambda b, p, i: (i, 0)),
            pl.BlockSpec((TS_RET, LANES), lambda b, p, i: (i, 0)),
            pl.BlockSpec((1, 2 * CHUNK, CHUNK), lambda b, p, i: (p, 0, 0)),
            pl.BlockSpec((1, CHUNK, LANES), lambda b, p, i: (p, 0, 0)),
            pl.BlockSpec((1, CHUNK, LANES), lambda b, p, i: (p, 0, 0)),
            pl.BlockSpec((1, 1, LANES), lambda b, p, i: (p, 0, 0)),
            pl.BlockSpec((1, LANES), lambda b, p, i: (0, p)),
        ],
        out_specs=pl.BlockSpec((1, TS_RET, LANES), lambda b, p, i: (b, i, p)),
        out_shape=jax.ShapeDtypeStruct((batch, seq, RET_W), BF16),
        scratch_shapes=[
            pltpu.VMEM((LANES, LANES), F32),
            pltpu.VMEM((TS_RET, LANES), F32),
        ],
        compiler_params=_params("parallel", "parallel", "arbitrary"),
        name="retention",
    )(ret3, ret3, ret3, ret3, cos, sin, dmat, dec, xi, gch, norm_g)
    return out.reshape(batch * seq, RET_W)


def _retention_consts(seq):
    half = HEAD_DIM // 2
    pos = jnp.arange(seq, dtype=F32)
    inv = ROPE_BASE ** (-jnp.arange(half, dtype=F32) / half)
    ang = pos[:, None] * inv[None, :]
    cos_h, sin_h = jnp.cos(ang), jnp.sin(ang)
    cos = jnp.tile(cos_h, (1, 2 * LANES // HEAD_DIM))
    sin = jnp.tile(jnp.concatenate([-sin_h, sin_h], axis=1), (1, LANES // HEAD_DIM))
    log_g = jnp.log(1.0 - 2.0 ** (-5.0 - jnp.arange(RET_HEADS, dtype=F32)))
    c = jnp.arange(CHUNK, dtype=F32)
    dmat = jnp.exp(jnp.abs(c[:, None] - c[None, :])[None] * log_g[:, None, None])
    dmat = dmat.reshape(RET_HEADS // 2, 2 * CHUNK, CHUNK)
    lane_log_g = jnp.repeat(log_g, HEAD_DIM).reshape(RET_HEADS // 2, 1, LANES)
    dec = jnp.exp((CHUNK - 1 - c)[None, :, None] * lane_log_g)
    xi = jnp.exp((c + 1.0)[None, :, None] * lane_log_g)
    gch = jnp.exp(CHUNK * lane_log_g)
    return cos, sin, dmat, dec, xi, gch


def _layer_norm(z, g, b):
    mu = jnp.mean(z, axis=-1, keepdims=True)
    zc = z - mu
    var = jnp.mean(zc * zc, axis=-1, keepdims=True)
    return zc * lax.rsqrt(var + LN_EPS) * g + b


def _outproj_kernel(x_ref, p_ref, s_ref, r_ref, w_ref, b_ref, g_ref, be_ref, rw_ref, rb_ref,
                    x1_ref, x1b_ref, gates_ref):
    o1 = POOL_W + SB_W
    y = (_dot(p_ref[...], w_ref[:POOL_W, :]) + _dot(s_ref[...], w_ref[POOL_W:o1, :])
         + _dot(r_ref[...], w_ref[o1:, :]))
    x1 = _layer_norm(DN_ALPHA * x_ref[...] + (y + b_ref[...]), g_ref[...], be_ref[...])
    x1_ref[...] = x1
    x1b_ref[...] = x1.astype(BF16)

    logits = jnp.dot(x1, rw_ref[...], precision=lax.Precision.HIGHEST,
                     preferred_element_type=F32) + rb_ref[...]
    tm = logits.shape[0]
    lane = lax.broadcasted_iota(jnp.int32, (tm, LANES), 1)
    vals = jnp.where(lane < N_EXPERTS, logits, -jnp.inf)
    top_v, top_sel = [], []
    for _ in range(TOP_K):
        m = jnp.max(vals, axis=-1, keepdims=True)
        idx = jnp.min(jnp.where(vals == m, lane, LANES), axis=-1, keepdims=True)
        sel = lane == idx
        vals = jnp.where(sel, -jnp.inf, vals)
        top_v.append(m)
        top_sel.append(sel)
    ex = [jnp.exp(m - top_v[0]) for m in top_v]
    den = ex[0] + ex[1] + ex[2] + ex[3]
    gates = jnp.full((tm, LANES), -1.0, F32)
    for sel, e in zip(top_sel, ex):
        gates = jnp.where(sel, e / den, gates)
    gates_ref[...] = gates


def _outproj(x2, pool_o, sb_o, ret_o, w_bf, b, g, be, rw, rb):
    n = x2.shape[0]
    row = lambda i: (i, 0)
    fixed = lambda i: (0, 0)
    return pl.pallas_call(
        _outproj_kernel,
        grid=(n // TM_ROW,),
        in_specs=[
            pl.BlockSpec((TM_ROW, D_MODEL), row),
            pl.BlockSpec((TM_ROW, POOL_W), row),
            pl.BlockSpec((TM_ROW, SB_W), row),
            pl.BlockSpec((TM_ROW, RET_W), row),
            pl.BlockSpec((D_MODEL, D_MODEL), fixed),
            pl.BlockSpec((1, D_MODEL), fixed),
            pl.BlockSpec((1, D_MODEL), fixed),
            pl.BlockSpec((1, D_MODEL), fixed),
            pl.BlockSpec((D_MODEL, LANES), fixed),
            pl.BlockSpec((1, LANES), fixed),
        ],
        out_specs=[
            pl.BlockSpec((TM_ROW, D_MODEL), row),
            pl.BlockSpec((TM_ROW, D_MODEL), row),
            pl.BlockSpec((TM_ROW, LANES), row),
        ],
        out_shape=[
            jax.ShapeDtypeStruct((n, D_MODEL), F32),
            jax.ShapeDtypeStruct((n, D_MODEL), BF16),
            jax.ShapeDtypeStruct((n, LANES), F32),
        ],
        compiler_params=_params("parallel"),
        name="outproj_ln_router",
    )(x2, pool_o, sb_o, ret_o, w_bf, b, g, be, rw, rb)


def _dispatch_kernel(gates_ref, xb_ref, tril_ref, xs_ref, gs_ref, pm_ref, post_ref, gatet_ref, cnt_ref,
                     oh_ref):
    tm = gates_ref.shape[0]
    gates = gates_ref[...]
    sel = gates >= 0.0
    self_ = jnp.where(sel, 1.0, 0.0)
    incl = _dot(tril_ref[...], self_.astype(BF16))
    pos = incl - self_
    cnt_ref[0] = incl[tm - 1:tm, :].astype(jnp.int32)
    pm_ref[...] = jnp.where(jnp.logical_and(sel, pos < RC_MOE), pos, -1.0).astype(BF16)
    post = jnp.where(sel, pos, -1.0).T[:N_EXPERTS, :]
    gatet = gates.T[:N_EXPERTS, :]
    post_ref[0] = post
    gatet_ref[0] = gatet
    post_i = post.astype(jnp.int32)
    slot = lax.broadcasted_iota(jnp.int32, (RC_MOE, tm), 0)
    for e in range(N_EXPERTS):
        onehot = slot == post_i[e:e + 1, :]
        oh_ref[e * RC_MOE:(e + 1) * RC_MOE, :] = jnp.where(onehot, 1.0, 0.0).astype(BF16)
        gate = jnp.sum(jnp.where(onehot, gatet[e:e + 1, :], 0.0), axis=1, keepdims=True)
        gs_ref[0, e] = jnp.broadcast_to(gate, (RC_MOE, LANES))
    xs = _dot(oh_ref[...], xb_ref[...]).astype(BF16)
    xs_ref[0] = xs.reshape(N_EXPERTS, RC_MOE, D_MODEL)


def _dispatch(gates, x1b, tril):
    n = gates.shape[0]
    nt = n // TM_MOE
    return pl.pallas_call(
        _dispatch_kernel,
        grid=(nt,),
        in_specs=[
            pl.BlockSpec((TM_MOE, LANES), lambda i: (i, 0)),
            pl.BlockSpec((TM_MOE, D_MODEL), lambda i: (i, 0)),
            pl.BlockSpec((TM_MOE, TM_MOE), lambda i: (0, 0)),
        ],
        out_specs=[
            pl.BlockSpec((1, N_EXPERTS, RC_MOE, D_MODEL), lambda i: (i, 0, 0, 0)),
            pl.BlockSpec((1, N_EXPERTS, RC_MOE, LANES), lambda i: (i, 0, 0, 0)),
            pl.BlockSpec((TM_MOE, LANES), lambda i: (i, 0)),
            pl.BlockSpec((1, N_EXPERTS, TM_MOE), lambda i: (i, 0, 0)),
            pl.BlockSpec((1, N_EXPERTS, TM_MOE), lambda i: (i, 0, 0)),
            pl.BlockSpec((1, 1, LANES), lambda i: (i, 0, 0)),
        ],
        out_shape=[
            jax.ShapeDtypeStruct((nt, N_EXPERTS, RC_MOE, D_MODEL), BF16),
            jax.ShapeDtypeStruct((nt, N_EXPERTS, RC_MOE, LANES), F32),
            jax.ShapeDtypeStruct((n, LANES), BF16),
            jax.ShapeDtypeStruct((nt, N_EXPERTS, TM_MOE), F32),
            jax.ShapeDtypeStruct((nt, N_EXPERTS, TM_MOE), F32),
            jax.ShapeDtypeStruct((nt, 1, LANES), jnp.int32),
        ],
        scratch_shapes=[pltpu.VMEM((SLOTS, TM_MOE), BF16)],
        compiler_params=_params("parallel"),
        name="dispatch",
    )(gates, x1b, tril)


def _swiglu_ffn(x, wgu, bgu, wd, bd):
    h = _dot(x, wgu) + bgu
    g = jnp.minimum(h[:, :D_FF], SWIGLU_LIMIT)
    up = jnp.clip(h[:, D_FF:], -SWIGLU_LIMIT, SWIGLU_LIMIT)
    act = (up + 1.0) * (g / (1.0 + jnp.exp(-SWIGLU_ALPHA * g)))
    return _dot(act.astype(BF16), wd) + bd


def _ffn_kernel(xs_ref, gs_ref, wgu_ref, bgu_ref, wd_ref, bd_ref, ys_ref):
    rows = FFN_TILES * RC_MOE
    x = xs_ref[:, 0].reshape(rows, D_MODEL)
    y = _swiglu_ffn(x, wgu_ref[0], bgu_ref[0], wd_ref[0], bd_ref[0])
    gate = gs_ref[:, 0].reshape(rows, LANES)[:, :1]
    ys_ref[:, 0] = (y * gate).astype(BF16).reshape(FFN_TILES, RC_MOE, D_MODEL)


def _ffn(xs, gs, wgu, bgu, wd, bd):
    nt = xs.shape[0]
    return pl.pallas_call(
        _ffn_kernel,
        grid=(N_EXPERTS, nt // FFN_TILES),
        in_specs=[
            pl.BlockSpec((FFN_TILES, 1, RC_MOE, D_MODEL), lambda e, j: (j, e, 0, 0)),
            pl.BlockSpec((FFN_TILES, 1, RC_MOE, LANES), lambda e, j: (j, e, 0, 0)),
            pl.BlockSpec((1, D_MODEL, 2 * D_FF), lambda e, j: (e, 0, 0)),
            pl.BlockSpec((1, 1, 2 * D_FF), lambda e, j: (e, 0, 0)),
            pl.BlockSpec((1, D_FF, D_MODEL), lambda e, j: (e, 0, 0)),
            pl.BlockSpec((1, 1, D_MODEL), lambda e, j: (e, 0, 0)),
        ],
        out_specs=pl.BlockSpec((FFN_TILES, 1, RC_MOE, D_MODEL), lambda e, j: (j, e, 0, 0)),
        out_shape=jax.ShapeDtypeStruct(xs.shape, BF16),
        compiler_params=_params("parallel", "parallel"),
        name="expert_ffn",
    )(xs, gs, wgu, bgu, wd, bd)


def _combine_kernel(cnt_ref, ys_ref, pm_ref, pick_ref, rcol_ref, x1_ref, g_ref, b_ref, xb_ref, post_ref,
                    gatet_ref, bgu_ref, bd_ref, wgu_hbm, wd_hbm, o_ref, acc_ref, wgu_buf, wd_buf, sem):
    i = pl.program_id(0)
    tm = pm_ref.shape[0]
    ranks = _dot(pm_ref[...], pick_ref[...])
    back = jnp.where(ranks == rcol_ref[...], 1.0, 0.0).astype(BF16)
    acc_ref[...] = _dot(back, ys_ref[0].reshape(SLOTS, D_MODEL))

    slot = lax.broadcasted_iota(jnp.int32, (RC_MOE, tm), 0)

    def per_expert(e, carry):
        cnt = cnt_ref[i * N_EXPERTS + e]

        @pl.when(cnt > RC_MOE)
        def _():
            copies = (pltpu.make_async_copy(wgu_hbm.at[e], wgu_buf, sem.at[0]),
                      pltpu.make_async_copy(wd_hbm.at[e], wd_buf, sem.at[1]))
            for cp in copies:
                cp.start()
            for cp in copies:
                cp.wait()
            prow = post_ref[0, pl.ds(e, 1), :].astype(jnp.int32)
            grow = gatet_ref[0, pl.ds(e, 1), :]

            def chunk(c, carry2):
                onehot = (slot + c * RC_MOE) == prow
                oh = jnp.where(onehot, 1.0, 0.0).astype(BF16)
                xg = _dot(oh, xb_ref[...]).astype(BF16)
                y = _swiglu_ffn(xg, wgu_buf[...], bgu_ref[e], wd_buf[...], bd_ref[e])
                gate = jnp.sum(jnp.where(onehot, grow, 0.0), axis=1, keepdims=True)
                yg = (y * gate).astype(BF16)
                acc_ref[...] += lax.dot_general(oh, yg, _TN, preferred_element_type=F32)
                return carry2

            lax.fori_loop(1, (cnt + RC_MOE - 1) // RC_MOE, chunk, 0)

        return carry

    lax.fori_loop(0, N_EXPERTS, per_expert, 0)
    o_ref[...] = _layer_norm(DN_ALPHA * x1_ref[...] + acc_ref[...], g_ref[...], b_ref[...])


def _combine(cnt, ys, pm, pick, rcol, x1, g, b, x1b, post, gatet, bgu, bd, wgu, wd):
    n = x1.shape[0]
    nt = n // TM_MOE
    row = lambda i, c: (i, 0)
    fixed2 = lambda i, c: (0, 0)
    fixed3 = lambda i, c: (0, 0, 0)
    grid_spec = pltpu.PrefetchScalarGridSpec(
        num_scalar_prefetch=1,
        grid=(nt,),
        in_specs=[
            pl.BlockSpec((1, N_EXPERTS, RC_MOE, D_MODEL), lambda i, c: (i, 0, 0, 0)),
            pl.BlockSpec((TM_MOE, LANES), row),
            pl.BlockSpec((LANES, SLOTS), fixed2),
            pl.BlockSpec((1, SLOTS), fixed2),
            pl.BlockSpec((TM_MOE, D_MODEL), row),
            pl.BlockSpec((1, D_MODEL), fixed2),
            pl.BlockSpec((1, D_MODEL), fixed2),
            pl.BlockSpec((TM_MOE, D_MODEL), row),
            pl.BlockSpec((1, N_EXPERTS, TM_MOE), lambda i, c: (i, 0, 0)),
            pl.BlockSpec((1, N_EXPERTS, TM_MOE), lambda i, c: (i, 0, 0)),
            pl.BlockSpec((N_EXPERTS, 1, 2 * D_FF), fixed3),
            pl.BlockSpec((N_EXPERTS, 1, D_MODEL), fixed3),
            pl.BlockSpec(memory_space=pl.ANY),
            pl.BlockSpec(memory_space=pl.ANY),
        ],
        out_specs=pl.BlockSpec((TM_MOE, D_MODEL), row),
        scratch_shapes=[
            pltpu.VMEM((TM_MOE, D_MODEL), F32),
            pltpu.VMEM((D_MODEL, 2 * D_FF), BF16),
            pltpu.VMEM((D_FF, D_MODEL), BF16),
            pltpu.SemaphoreType.DMA((2,)),
        ],
    )
    return pl.pallas_call(
        _combine_kernel,
        grid_spec=grid_spec,
        out_shape=jax.ShapeDtypeStruct((n, D_MODEL), F32),
        compiler_params=_params("arbitrary"),
        name="combine_ln",
    )(cnt, ys, pm, pick, rcol, x1, g, b, x1b, post, gatet, bgu, bd, wgu, wd)


def _block_diag(pool_w):
    groups = pool_w.shape[0]
    out = jnp.zeros((POOL_W, POOL_W), pool_w.dtype)
    for gi in range(groups):
        out = out.at[gi * POOL_CH:(gi + 1) * POOL_CH, gi * POOL_CH:(gi + 1) * POOL_CH].set(pool_w[gi])
    return out


def _layer(x2, batch, seq, consts, w_in, b_in, pool_w, pool_scale, ret_norm_g, w_out, b_out, ln1_g,
           ln1_b, router_w, router_b, w_gate_up, b_gate_up, w_down, b_down, ln2_g, ln2_b):
    ret_consts, tri2, tril, pick, rcol = consts
    row = lambda a: a.reshape(1, -1).astype(F32)
    u_pool, sb, ret = _inproj(x2, w_in.astype(BF16), row(b_in))
    pool_o = _pool(u_pool, _block_diag(pool_w).astype(BF16), row(pool_scale), batch, seq)
    sb_o = _stick_breaking(sb, tri2, batch, seq)
    ret_o = _retention(ret, ret_consts, row(ret_norm_g), batch, seq)
    rw = jnp.pad(router_w.astype(F32), ((0, 0), (0, LANES - N_EXPERTS)))
    rb = jnp.pad(router_b.astype(F32), (0, LANES - N_EXPERTS)).reshape(1, LANES)
    x1, x1b, gates = _outproj(x2, pool_o, sb_o, ret_o, w_out.astype(BF16), row(b_out), row(ln1_g),
                              row(ln1_b), rw, rb)
    xs, gs, pm, post, gatet, cnt = _dispatch(gates, x1b, tril)
    wgu = w_gate_up.astype(BF16)
    wd = w_down.astype(BF16)
    bgu = b_gate_up.reshape(N_EXPERTS, 1, 2 * D_FF).astype(F32)
    bd = b_down.reshape(N_EXPERTS, 1, D_MODEL).astype(F32)
    ys = _ffn(xs, gs, wgu, bgu, wd, bd)
    cnt_flat = cnt[:, 0, :N_EXPERTS].reshape(-1)
    return _combine(cnt_flat, ys, pm, pick, rcol, x1, row(ln2_g), row(ln2_b), x1b, post, gatet, bgu, bd,
                    wgu, wd)


def kernel(x, w_in, b_in, pool_w, pool_scale, ret_norm_g, w_out, b_out, ln1_g, ln1_b, router_w, router_b,
           w_gate_up, b_gate_up, w_down, b_down, ln2_g, ln2_b):
    batch, seq, d = x.shape
    n = batch * seq
    assert d == D_MODEL and seq % TS_RET == 0 and seq % TQ_SB == 0 and seq % TS_POOL == 0
    assert n % (TM_MOE * FFN_TILES) == 0 and n % TM_ROW == 0
    j = np.arange(TQ_SB)
    tri = (j[:, None] >= j[None, :]).astype(np.float32)
    half = np.concatenate([tri, np.ones_like(tri)], axis=1)
    tri2 = jnp.asarray(np.concatenate([half, half], axis=0), BF16)
    r = np.arange(TM_MOE)
    tril = jnp.asarray((r[:, None] >= r[None, :]).astype(np.float32), BF16)
    s = np.arange(SLOTS)
    pick = jnp.asarray((np.arange(LANES)[:, None] == (s // RC_MOE)[None, :]).astype(np.float32), BF16)
    rcol = jnp.asarray((s % RC_MOE).astype(np.float32).reshape(1, SLOTS))
    consts = (_retention_consts(seq), tri2, tril, pick, rcol)
    x2 = x.reshape(n, d)
    for l in range(DEPTH):
        x2 = _layer(x2, batch, seq, consts, w_in[l], b_in[l], pool_w[l], pool_scale[l], ret_norm_g[l],
                    w_out[l], b_out[l], ln1_g[l], ln1_b[l], router_w[l], router_b[l], w_gate_up[l],
                    b_gate_up[l], w_down[l], b_down[l], ln2_g[l], ln2_b[l])
    return x2.reshape(batch, seq, d)
```

```python
import numpy as np
import jax
import jax.numpy as jnp
from jax import lax
from jax.experimental import pallas as pl
from jax.experimental.pallas import tpu as pltpu

D_MODEL = 1024
DEPTH = 2
CHUNK = 64
HEAD_DIM = 64
POOL_CH = 64
POOL_W = 256
POOL_HALO = 16
SB_W = 384
RET_W = 384
RET_HEADS = 6
IN_W = POOL_W + 3 * SB_W + 4 * RET_W
ROPE_BASE = 10000.0
N_EXPERTS = 32
TOP_K = 4
D_FF = D_MODEL
SWIGLU_LIMIT = 7.0
SWIGLU_ALPHA = 1.702
DN_ALPHA = (2.0 * DEPTH) ** 0.25
LN_EPS = 1e-5

LANES = 128
VMEM_LIMIT = 56 * 1024 * 1024

TM_ROW = 512
TS_POOL = 512
TQ_SB = 128
SB_SPAN = 384
TS_RET = 512
TM_MOE = 512
RC_MOE = 80
FFN_TILES = 8
SLOTS = N_EXPERTS * RC_MOE
SB_SKIP = -100.0

BF16 = jnp.bfloat16
F32 = jnp.float32

_NT = (((1,), (1,)), ((), ()))
_TN = (((0,), (0,)), ((), ()))


def _dot(a, b):
    return jnp.dot(a, b, preferred_element_type=F32)


def _params(*sem):
    return pltpu.CompilerParams(dimension_semantics=sem, vmem_limit_bytes=VMEM_LIMIT)


def _inproj_kernel(x_ref, w_ref, b_ref, pool_ref, sb_ref, ret_ref):
    xb = x_ref[...].astype(BF16)
    o1 = POOL_W + 3 * SB_W
    pool_ref[...] = _dot(xb, w_ref[:, :POOL_W]) + b_ref[:, :POOL_W]
    sb_ref[...] = (_dot(xb, w_ref[:, POOL_W:o1]) + b_ref[:, POOL_W:o1]).astype(BF16)
    ret_ref[...] = _dot(xb, w_ref[:, o1:]) + b_ref[:, o1:]


def _inproj(x2, w_bf, b):
    n = x2.shape[0]
    return pl.pallas_call(
        _inproj_kernel,
        grid=(n // TM_ROW,),
        in_specs=[
            pl.BlockSpec((TM_ROW, D_MODEL), lambda i: (i, 0)),
            pl.BlockSpec((D_MODEL, IN_W), lambda i: (0, 0)),
            pl.BlockSpec((1, IN_W), lambda i: (0, 0)),
        ],
        out_specs=[
            pl.BlockSpec((TM_ROW, POOL_W), lambda i: (i, 0)),
            pl.BlockSpec((TM_ROW, 3 * SB_W), lambda i: (i, 0)),
            pl.BlockSpec((TM_ROW, 4 * RET_W), lambda i: (i, 0)),
        ],
        out_shape=[
            jax.ShapeDtypeStruct((n, POOL_W), F32),
            jax.ShapeDtypeStruct((n, 3 * SB_W), BF16),
            jax.ShapeDtypeStruct((n, 4 * RET_W), F32),
        ],
        compiler_params=_params("parallel"),
        name="inproj",
    )(x2, w_bf, b)


def _pool_kernel(cur_ref, halo_ref, w_ref, scale_ref, o_ref):
    i = pl.program_id(1)
    cur = cur_ref[0]
    halo = jnp.where(i > 0, halo_ref[0], 0.0)
    ext = jnp.concatenate([halo, cur], axis=0)
    ts = cur.shape[0]
    a2 = ext[1:] + ext[:-1]
    a4 = a2[2:] + a2[:-2]
    a8 = a4[4:] + a4[:-4]
    a16 = a8[8:] + a8[:-8]
    lane = lax.broadcasted_iota(jnp.int32, (ts, POOL_W), 1)
    grp = lane // POOL_CH
    win = jnp.where(grp == 0, a2[15:15 + ts],
                    jnp.where(grp == 1, a4[13:13 + ts],
                              jnp.where(grp == 2, a8[9:9 + ts], a16[1:1 + ts])))
    width = jnp.where(grp == 0, 2, jnp.where(grp == 1, 4, jnp.where(grp == 2, 8, 16)))
    t = i * ts + lax.broadcasted_iota(jnp.int32, (ts, POOL_W), 0)
    cnt = jnp.minimum(t + 1, width).astype(F32)
    pooled = win / cnt - cur
    mixed = _dot(pooled.astype(BF16), w_ref[...]) * scale_ref[...]
    o_ref[0] = mixed.astype(BF16)


def _pool(u, w_bd, scale, batch, seq):
    u3 = u.reshape(batch, seq, POOL_W)
    per = TS_POOL // POOL_HALO
    out = pl.pallas_call(
        _pool_kernel,
        grid=(batch, seq // TS_POOL),
        in_specs=[
            pl.BlockSpec((1, TS_POOL, POOL_W), lambda b, i: (b, i, 0)),
            pl.BlockSpec((1, POOL_HALO, POOL_W), lambda b, i: (b, jnp.maximum(i * per - 1, 0), 0)),
            pl.BlockSpec((POOL_W, POOL_W), lambda b, i: (0, 0)),
            pl.BlockSpec((1, POOL_W), lambda b, i: (0, 0)),
        ],
        out_specs=pl.BlockSpec((1, TS_POOL, POOL_W), lambda b, i: (b, i, 0)),
        out_shape=jax.ShapeDtypeStruct((batch, seq, POOL_W), BF16),
        compiler_params=_params("parallel", "parallel"),
        name="pool_mixer",
    )(u3, u3, w_bd, scale)
    return out.reshape(batch * seq, POOL_W)


def _sb_kernel(q_ref, k_ref, v_ref, tri_ref, o_ref, acc_ref, carry_ref):
    qi = pl.program_id(1)
    tq = q_ref.shape[1]
    groups = q_ref.shape[2] // LANES
    lane = lax.broadcasted_iota(jnp.int32, (tq, LANES), 1)
    qh = []
    for p in range(groups):
        q = q_ref[0, :, p * LANES:(p + 1) * LANES] * jnp.asarray(HEAD_DIM ** -0.5, BF16)
        zero = jnp.zeros_like(q)
        qh += [jnp.where(lane < HEAD_DIM, q, zero), jnp.where(lane >= HEAD_DIM, q, zero)]

    def log_keep(z):
        return jnp.minimum(-z, 0.0) - jnp.log(1.0 + jnp.exp(-jnp.abs(z)))

    def suffix_sums(lk):
        hi = lk.astype(BF16)
        lo = (lk - hi.astype(F32)).astype(BF16)
        out = []
        for b in range(lk.shape[1] // tq):
            sl = slice(b * tq, (b + 1) * tq)
            rs = _dot(jnp.concatenate([hi[:, sl], lo[:, sl]], axis=1), tri_ref[...])
            out.append((rs[:, :tq], rs[:, tq:]))
        return out

    s0 = pl.multiple_of(jnp.maximum(qi * tq + tq - SB_SPAN, 0), tq)
    qpos = qi * tq + lax.broadcasted_iota(jnp.int32, (tq, SB_SPAN), 0)
    kpos = s0 + lax.broadcasted_iota(jnp.int32, (tq, SB_SPAN), 1)
    below = kpos < qpos
    worst = None
    for h in range(2 * groups):
        cols = slice((h // 2) * LANES, (h // 2 + 1) * LANES)
        k = k_ref[0, pl.ds(s0, SB_SPAN), cols]
        v = v_ref[0, pl.ds(s0, SB_SPAN), cols]
        z = lax.dot_general(qh[h], k, _NT, preferred_element_type=F32)
        blocks = suffix_sums(jnp.where(below, log_keep(z), 0.0))
        run = None
        r = [None] * len(blocks)
        for b in reversed(range(len(blocks))):
            r_in, tot = blocks[b]
            r[b] = r_in if run is None else r_in + run
            run = tot if run is None else run + tot
        a = jnp.where(below, jnp.exp(z + jnp.concatenate(r, axis=1)), 0.0)
        acc_ref[h] = _dot(a.astype(BF16), v)
        carry_ref[h] = run
        m = jnp.max(run)
        worst = m if worst is None else jnp.maximum(worst, m)

    def tile(kb):
        start = pl.multiple_of(kb * tq, tq)
        worst = None
        for h in range(2 * groups):
            cols = slice((h // 2) * LANES, (h // 2 + 1) * LANES)
            k = k_ref[0, pl.ds(start, tq), cols]
            v = v_ref[0, pl.ds(start, tq), cols]
            z = lax.dot_general(qh[h], k, _NT, preferred_element_type=F32)
            (r_in, tot), = suffix_sums(log_keep(z))
            carry = carry_ref[h]
            a = jnp.exp(z + r_in + carry)
            acc_ref[h] += _dot(a.astype(BF16), v)
            carry = carry + tot
            carry_ref[h] = carry
            m = jnp.max(carry)
            worst = m if worst is None else jnp.maximum(worst, m)
        return worst

    def cond(state):
        kb, worst = state
        return jnp.logical_and(kb >= 0, worst > SB_SKIP)

    def body(state):
        kb, _ = state
        return kb - 1, tile(kb)

    lax.while_loop(cond, body, (s0 // tq - 1, worst))
    for p in range(groups):
        o_ref[0, :, p * LANES:(p + 1) * LANES] = jnp.where(
            lane < HEAD_DIM, acc_ref[2 * p], acc_ref[2 * p + 1]).astype(BF16)


def _stick_breaking(sb, tri2, batch, seq):
    sb3 = sb.reshape(batch, seq, 3 * SB_W)
    heads = SB_W // HEAD_DIM
    once = pl.Buffered(1)
    out = pl.pallas_call(
        _sb_kernel,
        grid=(batch, seq // TQ_SB),
        in_specs=[
            pl.BlockSpec((1, TQ_SB, SB_W), lambda b, i: (b, i, 0)),
            pl.BlockSpec((1, seq, SB_W), lambda b, i: (b, 0, 1), pipeline_mode=once),
            pl.BlockSpec((1, seq, SB_W), lambda b, i: (b, 0, 2), pipeline_mode=once),
            pl.BlockSpec((2 * TQ_SB, 2 * TQ_SB), lambda b, i: (0, 0)),
        ],
        out_specs=pl.BlockSpec((1, TQ_SB, SB_W), lambda b, i: (b, i, 0)),
        out_shape=jax.ShapeDtypeStruct((batch, seq, SB_W), BF16),
        scratch_shapes=[
            pltpu.VMEM((heads, TQ_SB, LANES), F32),
            pltpu.VMEM((heads, TQ_SB, TQ_SB), F32),
        ],
        compiler_params=_params("parallel", "arbitrary"),
        name="stick_breaking",
    )(sb3, sb3, sb3, tri2)
    return out.reshape(batch * seq, SB_W)


def _ret_kernel(q_ref, k_ref, v_ref, g_ref, cos_ref, sin_ref, dmat_ref, dec_ref, xi_ref, gch_ref,
                ng_ref, o_ref, state_ref, obuf_ref):
    si = pl.program_id(2)
    ts = q_ref.shape[1]

    @pl.when(si == 0)
    def _():
        state_ref[...] = jnp.zeros_like(state_ref)

    lane = lax.broadcasted_iota(jnp.int32, (ts, LANES), 1)
    first_half = (lane % HEAD_DIM) < (HEAD_DIM // 2)
    cos = cos_ref[...]
    sin = sin_ref[...]

    def rot(x):
        partner = jnp.where(first_half, pltpu.roll(x, LANES - HEAD_DIM // 2, 1),
                            pltpu.roll(x, HEAD_DIM // 2, 1))
        return x * cos + partner * sin

    qr = rot(q_ref[0])
    kr = rot(k_ref[0]) * (HEAD_DIM ** -0.5)
    v = v_ref[0]

    lane_c = lax.broadcasted_iota(jnp.int32, (CHUNK, LANES), 1)
    head0 = lane_c < HEAD_DIM
    r_i = lax.broadcasted_iota(jnp.int32, (LANES, LANES), 0) // HEAD_DIM
    c_i = lax.broadcasted_iota(jnp.int32, (LANES, LANES), 1) // HEAD_DIM
    same_head = r_i == c_i
    dmat = dmat_ref[0]
    dec = dec_ref[0]
    xi = xi_ref[0]
    gch = gch_ref[0]

    state = state_ref[...]
    for n in range(ts // CHUNK):
        sl = slice(n * CHUNK, (n + 1) * CHUNK)
        qc, kc, vc = qr[sl], kr[sl], v[sl]
        kcb = kc.astype(BF16)
        vcb = vc.astype(BF16)
        qstack = jnp.concatenate([jnp.where(head0, qc, 0.0), jnp.where(head0, 0.0, qc)], axis=0)
        scores = lax.dot_general(qstack.astype(BF16), kcb, _NT, preferred_element_type=F32) * dmat
        o2 = _dot(scores.astype(BF16), vcb)
        o_intra = jnp.where(head0, o2[:CHUNK], o2[CHUNK:])
        o_inter = _dot((qc * xi).astype(BF16), state.astype(BF16))
        obuf_ref[sl, :] = o_intra + o_inter
        kv = lax.dot_general(kcb, (vc * dec).astype(BF16), _TN, preferred_element_type=F32)
        state = gch * state + jnp.where(same_head, kv, 0.0)
    state_ref[...] = state

    o = obuf_ref[...]
    lane_head0 = lane < HEAD_DIM
    inv = 1.0 / HEAD_DIM

    def head_mean(x):
        m0 = jnp.sum(jnp.where(lane_head0, x, 0.0), axis=-1, keepdims=True) * inv
        m1 = jnp.sum(jnp.where(lane_head0, 0.0, x), axis=-1, keepdims=True) * inv
        return jnp.where(lane_head0, m0, m1)

    mu = head_mean(o)
    oc = o - mu
    var = head_mean(oc * oc)
    normed = oc * lax.rsqrt(var + LN_EPS) * ng_ref[...]
    gate = g_ref[0]
    o_ref[0] = (gate / (1.0 + jnp.exp(-gate)) * normed).astype(BF16)


def _retention(ret, consts, norm_g, batch, seq):
    cos, sin, dmat, dec, xi, gch = consts
    ret3 = ret.reshape(batch, seq, 4 * RET_W)
    pairs = RET_W // LANES
    out = pl.pallas_call(
        _ret_kernel,
        grid=(batch, pairs, seq // TS_RET),
        in_specs=[
            pl.BlockSpec((1, TS_RET, LANES), lambda b, p, i: (b, i, p)),
            pl.BlockSpec((1, TS_RET, LANES), lambda b, p, i: (b, i, pairs + p)),
            pl.BlockSpec((1, TS_RET, LANES), lambda b, p, i: (b, i, 2 * pairs + p)),
            pl.BlockSpec((1, TS_RET, LANES), lambda b, p, i: (b, i, 3 * pairs + p)),
            pl.BlockSpec((TS_RET, LANES), lambda b, p, i: (i, 0)),
            pl.BlockSpec((TS_RET, LANES), lambda b, p, i: (i, 0)),
            pl.BlockSpec((1, 2 * CHUNK, CHUNK), lambda b, p, i: (p, 0, 0)),
            pl.BlockSpec((1, CHUNK, LANES), lambda b, p, i: (p, 0, 0)),
            pl.BlockSpec((1, CHUNK, LANES), lambda b, p, i: (p, 0, 0)),
            pl.BlockSpec((1, 1, LANES), lambda b, p, i: (p, 0, 0)),
            pl.BlockSpec((1, LANES), lambda b, p, i: (0, p)),
        ],
        out_specs=pl.BlockSpec((1, TS_RET, LANES), lambda b, p, i: (b, i, p)),
        out_shape=jax.ShapeDtypeStruct((batch, seq, RET_W), BF16),
        scratch_shapes=[
            pltpu.VMEM((LANES, LANES), F32),
            pltpu.VMEM((TS_RET, LANES), F32),
        ],
        compiler_params=_params("parallel", "parallel", "arbitrary"),
        name="retention",
    )(ret3, ret3, ret3, ret3, cos, sin, dmat, dec, xi, gch, norm_g)
    return out.reshape(batch * seq, RET_W)


def _retention_consts(seq):
    half = HEAD_DIM // 2
    pos = jnp.arange(seq, dtype=F32)
    inv = ROPE_BASE ** (-jnp.arange(half, dtype=F32) / half)
    ang = pos[:, None] * inv[None, :]
    cos_h, sin_h = jnp.cos(ang), jnp.sin(ang)
    cos = jnp.tile(cos_h, (1, 2 * LANES // HEAD_DIM))
    sin = jnp.tile(jnp.concatenate([-sin_h, sin_h], axis=1), (1, LANES // HEAD_DIM))
    log_g = jnp.log(1.0 - 2.0 ** (-5.0 - jnp.arange(RET_HEADS, dtype=F32)))
    c = jnp.arange(CHUNK, dtype=F32)
    dmat = jnp.exp(jnp.abs(c[:, None] - c[None, :])[None] * log_g[:, None, None])
    dmat = dmat.reshape(RET_HEADS // 2, 2 * CHUNK, CHUNK)
    lane_log_g = jnp.repeat(log_g, HEAD_DIM).reshape(RET_HEADS // 2, 1, LANES)
    dec = jnp.exp((CHUNK - 1 - c)[None, :, None] * lane_log_g)
    xi = jnp.exp((c + 1.0)[None, :, None] * lane_log_g)
    gch = jnp.exp(CHUNK * lane_log_g)
    return cos, sin, dmat, dec, xi, gch


def _layer_norm(z, g, b):
    mu = jnp.mean(z, axis=-1, keepdims=True)
    zc = z - mu
    var = jnp.mean(zc * zc, axis=-1, keepdims=True)
    return zc * lax.rsqrt(var + LN_EPS) * g + b


def _outproj_kernel(x_ref, p_ref, s_ref, r_ref, w_ref, b_ref, g_ref, be_ref, rw_ref, rb_ref,
                    x1_ref, x1b_ref, gates_ref):
    o1 = POOL_W + SB_W
    y = (_dot(p_ref[...], w_ref[:POOL_W, :]) + _dot(s_ref[...], w_ref[POOL_W:o1, :])
         + _dot(r_ref[...], w_ref[o1:, :]))
    x1 = _layer_norm(DN_ALPHA * x_ref[...] + (y + b_ref[...]), g_ref[...], be_ref[...])
    x1_ref[...] = x1
    x_hi = x1.astype(BF16)
    x1b_ref[...] = x_hi

    x_mid = (x1 - x_hi.astype(F32)).astype(BF16)
    t = _dot(x_hi, rw_ref[...])
    logits = t[:, :LANES] + t[:, LANES:] + _dot(x_mid, rw_ref[:, :LANES]) + rb_ref[...]
    tm = logits.shape[0]
    lane = lax.broadcasted_iota(jnp.int32, (tm, LANES), 1).astype(F32)
    vals = jnp.where(lane < N_EXPERTS, logits, -jnp.inf)
    top_v, top_sel = [], []
    for _ in range(TOP_K):
        m = jnp.max(vals, axis=-1, keepdims=True)
        idx = jnp.min(jnp.where(vals == m, lane, float(LANES)), axis=-1, keepdims=True)
        sel = lane == idx
        vals = jnp.where(sel, -jnp.inf, vals)
        top_v.append(m)
        top_sel.append(sel)
    ex = [jnp.exp(m - top_v[0]) for m in top_v]
    den = ex[0] + ex[1] + ex[2] + ex[3]
    gates = jnp.full((tm, LANES), -1.0, F32)
    for sel, e in zip(top_sel, ex):
        gates = jnp.where(sel, e / den, gates)
    gates_ref[...] = gates


def _outproj(x2, pool_o, sb_o, ret_o, w_bf, b, g, be, rw, rb):
    n = x2.shape[0]
    row = lambda i: (i, 0)
    fixed = lambda i: (0, 0)
    return pl.pallas_call(
        _outproj_kernel,
        grid=(n // TM_ROW,),
        in_specs=[
            pl.BlockSpec((TM_ROW, D_MODEL), row),
            pl.BlockSpec((TM_ROW, POOL_W), row),
            pl.BlockSpec((TM_ROW, SB_W), row),
            pl.BlockSpec((TM_ROW, RET_W), row),
            pl.BlockSpec((D_MODEL, D_MODEL), fixed),
            pl.BlockSpec((1, D_MODEL), fixed),
            pl.BlockSpec((1, D_MODEL), fixed),
            pl.BlockSpec((1, D_MODEL), fixed),
            pl.BlockSpec((D_MODEL, 2 * LANES), fixed),
            pl.BlockSpec((1, LANES), fixed),
        ],
        out_specs=[
            pl.BlockSpec((TM_ROW, D_MODEL), row),
            pl.BlockSpec((TM_ROW, D_MODEL), row),
            pl.BlockSpec((TM_ROW, LANES), row),
        ],
        out_shape=[
            jax.ShapeDtypeStruct((n, D_MODEL), F32),
            jax.ShapeDtypeStruct((n, D_MODEL), BF16),
            jax.ShapeDtypeStruct((n, LANES), F32),
        ],
        compiler_params=_params("parallel"),
        name="outproj_ln_router",
    )(x2, pool_o, sb_o, ret_o, w_bf, b, g, be, rw, rb)


def _dispatch_kernel(gates_ref, xb_ref, tril_ref, xs_ref, gs_ref, pm_ref, post_ref, gatet_ref, cnt_ref,
                     oh_ref):
    tm = gates_ref.shape[0]
    gates = gates_ref[...]
    sel = gates >= 0.0
    self_ = jnp.where(sel, 1.0, 0.0)
    incl = _dot(tril_ref[...], self_.astype(BF16))
    pos = incl - self_
    cnt_ref[0] = incl[tm - 1:tm, :].astype(jnp.int32)
    pm_ref[...] = jnp.where(jnp.logical_and(sel, pos < RC_MOE), pos, -1.0).astype(BF16)
    post = jnp.where(sel, pos, -1.0).T[:N_EXPERTS, :]
    gatet = gates.T[:N_EXPERTS, :]
    post_ref[0] = post
    gatet_ref[0] = gatet
    post_i = post.astype(jnp.int32)
    slot = lax.broadcasted_iota(jnp.int32, (RC_MOE, tm), 0)
    for e in range(N_EXPERTS):
        onehot = slot == post_i[e:e + 1, :]
        oh_ref[e * RC_MOE:(e + 1) * RC_MOE, :] = jnp.where(onehot, 1.0, 0.0).astype(BF16)
        gate = jnp.sum(jnp.where(onehot, gatet[e:e + 1, :], 0.0), axis=1, keepdims=True)
        gs_ref[0, e] = jnp.broadcast_to(gate, (RC_MOE, LANES))
    xs = _dot(oh_ref[...], xb_ref[...]).astype(BF16)
    xs_ref[0] = xs.reshape(N_EXPERTS, RC_MOE, D_MODEL)


def _dispatch(gates, x1b, tril):
    n = gates.shape[0]
    nt = n // TM_MOE
    return pl.pallas_call(
        _dispatch_kernel,
        grid=(nt,),
        in_specs=[
            pl.BlockSpec((TM_MOE, LANES), lambda i: (i, 0)),
            pl.BlockSpec((TM_MOE, D_MODEL), lambda i: (i, 0)),
            pl.BlockSpec((TM_MOE, TM_MOE), lambda i: (0, 0)),
        ],
        out_specs=[
            pl.BlockSpec((1, N_EXPERTS, RC_MOE, D_MODEL), lambda i: (i, 0, 0, 0)),
            pl.BlockSpec((1, N_EXPERTS, RC_MOE, LANES), lambda i: (i, 0, 0, 0)),
            pl.BlockSpec((TM_MOE, LANES), lambda i: (i, 0)),
            pl.BlockSpec((1, N_EXPERTS, TM_MOE), lambda i: (i, 0, 0)),
            pl.BlockSpec((1, N_EXPERTS, TM_MOE), lambda i: (i, 0, 0)),
            pl.BlockSpec((1, 1, LANES), lambda i: (i, 0, 0)),
        ],
        out_shape=[
            jax.ShapeDtypeStruct((nt, N_EXPERTS, RC_MOE, D_MODEL), BF16),
            jax.ShapeDtypeStruct((nt, N_EXPERTS, RC_MOE, LANES), F32),
            jax.ShapeDtypeStruct((n, LANES), BF16),
            jax.ShapeDtypeStruct((nt, N_EXPERTS, TM_MOE), F32),
            jax.ShapeDtypeStruct((nt, N_EXPERTS, TM_MOE), F32),
            jax.ShapeDtypeStruct((nt, 1, LANES), jnp.int32),
        ],
        scratch_shapes=[pltpu.VMEM((SLOTS, TM_MOE), BF16)],
        compiler_params=_params("parallel"),
        name="dispatch",
    )(gates, x1b, tril)


def _swiglu_ffn(x, wgu, bgu, wd, bd):
    h = _dot(x, wgu) + bgu
    g = jnp.minimum(h[:, :D_FF], SWIGLU_LIMIT)
    up = jnp.clip(h[:, D_FF:], -SWIGLU_LIMIT, SWIGLU_LIMIT)
    act = (up + 1.0) * (g / (1.0 + jnp.exp(-SWIGLU_ALPHA * g)))
    return _dot(act.astype(BF16), wd) + bd


def _ffn_kernel(xs_ref, gs_ref, wgu_ref, bgu_ref, wd_ref, bd_ref, ys_ref, wgu_bf_ref, wd_bf_ref):
    @pl.when(pl.program_id(1) == 0)
    def _():
        wgu_bf_ref[0] = wgu_ref[0, 0].astype(BF16)
        wd_bf_ref[0] = wd_ref[0, 0].astype(BF16)

    rows = FFN_TILES * RC_MOE
    x = xs_ref[:, 0].reshape(rows, D_MODEL)
    y = _swiglu_ffn(x, wgu_bf_ref[0], bgu_ref[0], wd_bf_ref[0], bd_ref[0])
    gate = gs_ref[:, 0].reshape(rows, LANES)[:, :1]
    ys_ref[:, 0] = (y * gate).astype(BF16).reshape(FFN_TILES, RC_MOE, D_MODEL)


def _ffn(xs, gs, w_gate_up, bgu, w_down, bd, layer):
    nt = xs.shape[0]
    return pl.pallas_call(
        _ffn_kernel,
        grid=(N_EXPERTS, nt // FFN_TILES),
        in_specs=[
            pl.BlockSpec((FFN_TILES, 1, RC_MOE, D_MODEL), lambda e, j: (j, e, 0, 0)),
            pl.BlockSpec((FFN_TILES, 1, RC_MOE, LANES), lambda e, j: (j, e, 0, 0)),
            pl.BlockSpec((1, 1, D_MODEL, 2 * D_FF), lambda e, j: (layer, e, 0, 0)),
            pl.BlockSpec((1, 1, 2 * D_FF), lambda e, j: (e, 0, 0)),
            pl.BlockSpec((1, 1, D_FF, D_MODEL), lambda e, j: (layer, e, 0, 0)),
            pl.BlockSpec((1, 1, D_MODEL), lambda e, j: (e, 0, 0)),
        ],
        out_specs=[
            pl.BlockSpec((FFN_TILES, 1, RC_MOE, D_MODEL), lambda e, j: (j, e, 0, 0)),
            pl.BlockSpec((1, D_MODEL, 2 * D_FF), lambda e, j: (e, 0, 0)),
            pl.BlockSpec((1, D_FF, D_MODEL), lambda e, j: (e, 0, 0)),
        ],
        out_shape=[
            jax.ShapeDtypeStruct(xs.shape, BF16),
            jax.ShapeDtypeStruct((N_EXPERTS, D_MODEL, 2 * D_FF), BF16),
            jax.ShapeDtypeStruct((N_EXPERTS, D_FF, D_MODEL), BF16),
        ],
        compiler_params=_params("parallel", "arbitrary"),
        name="expert_ffn",
    )(xs, gs, w_gate_up, bgu, w_down, bd)


def _combine_kernel(cnt_ref, ys_ref, pm_ref, pick_ref, rcol_ref, x1_ref, g_ref, b_ref, xb_ref, post_ref,
                    gatet_ref, bgu_ref, bd_ref, wgu_hbm, wd_hbm, o_ref, acc_ref, wgu_buf, wd_buf, sem):
    i = pl.program_id(0)
    tm = pm_ref.shape[0]
    ranks = _dot(pm_ref[...], pick_ref[...])
    back = jnp.where(ranks == rcol_ref[...], 1.0, 0.0).astype(BF16)
    acc_ref[...] = _dot(back, ys_ref[0].reshape(SLOTS, D_MODEL))

    slot = lax.broadcasted_iota(jnp.int32, (RC_MOE, tm), 0)

    def per_expert(e, carry):
        cnt = cnt_ref[i * N_EXPERTS + e]

        @pl.when(cnt > RC_MOE)
        def _():
            copies = (pltpu.make_async_copy(wgu_hbm.at[e], wgu_buf, sem.at[0]),
                      pltpu.make_async_copy(wd_hbm.at[e], wd_buf, sem.at[1]))
            for cp in copies:
                cp.start()
            for cp in copies:
                cp.wait()
            prow = post_ref[0, pl.ds(e, 1), :].astype(jnp.int32)
            grow = gatet_ref[0, pl.ds(e, 1), :]

            def chunk(c, carry2):
                onehot = (slot + c * RC_MOE) == prow
                oh = jnp.where(onehot, 1.0, 0.0).astype(BF16)
                xg = _dot(oh, xb_ref[...]).astype(BF16)
                y = _swiglu_ffn(xg, wgu_buf[...], bgu_ref[e], wd_buf[...], bd_ref[e])
                gate = jnp.sum(jnp.where(onehot, grow, 0.0), axis=1, keepdims=True)
                yg = (y * gate).astype(BF16)
                acc_ref[...] += lax.dot_general(oh, yg, _TN, preferred_element_type=F32)
                return carry2

            lax.fori_loop(1, (cnt + RC_MOE - 1) // RC_MOE, chunk, 0)

        return carry

    lax.fori_loop(0, N_EXPERTS, per_expert, 0)
    o_ref[...] = _layer_norm(DN_ALPHA * x1_ref[...] + acc_ref[...], g_ref[...], b_ref[...])


def _combine(cnt, ys, pm, pick, rcol, x1, g, b, x1b, post, gatet, bgu, bd, wgu, wd):
    n = x1.shape[0]
    nt = n // TM_MOE
    row = lambda i, c: (i, 0)
    fixed2 = lambda i, c: (0, 0)
    fixed3 = lambda i, c: (0, 0, 0)
    grid_spec = pltpu.PrefetchScalarGridSpec(
        num_scalar_prefetch=1,
        grid=(nt,),
        in_specs=[
            pl.BlockSpec((1, N_EXPERTS, RC_MOE, D_MODEL), lambda i, c: (i, 0, 0, 0)),
            pl.BlockSpec((TM_MOE, LANES), row),
            pl.BlockSpec((LANES, SLOTS), fixed2),
            pl.BlockSpec((1, SLOTS), fixed2),
            pl.BlockSpec((TM_MOE, D_MODEL), row),
            pl.BlockSpec((1, D_MODEL), fixed2),
            pl.BlockSpec((1, D_MODEL), fixed2),
            pl.BlockSpec((TM_MOE, D_MODEL), row),
            pl.BlockSpec((1, N_EXPERTS, TM_MOE), lambda i, c: (i, 0, 0)),
            pl.BlockSpec((1, N_EXPERTS, TM_MOE), lambda i, c: (i, 0, 0)),
            pl.BlockSpec((N_EXPERTS, 1, 2 * D_FF), fixed3),
            pl.BlockSpec((N_EXPERTS, 1, D_MODEL), fixed3),
            pl.BlockSpec(memory_space=pl.ANY),
            pl.BlockSpec(memory_space=pl.ANY),
        ],
        out_specs=pl.BlockSpec((TM_MOE, D_MODEL), row),
        scratch_shapes=[
            pltpu.VMEM((TM_MOE, D_MODEL), F32),
            pltpu.VMEM((D_MODEL, 2 * D_FF), BF16),
            pltpu.VMEM((D_FF, D_MODEL), BF16),
            pltpu.SemaphoreType.DMA((2,)),
        ],
    )
    return pl.pallas_call(
        _combine_kernel,
        grid_spec=grid_spec,
        out_shape=jax.ShapeDtypeStruct((n, D_MODEL), F32),
        compiler_params=_params("arbitrary"),
        name="combine_ln",
    )(cnt, ys, pm, pick, rcol, x1, g, b, x1b, post, gatet, bgu, bd, wgu, wd)


def _block_diag(pool_w):
    groups = pool_w.shape[0]
    out = jnp.zeros((POOL_W, POOL_W), pool_w.dtype)
    for gi in range(groups):
        out = out.at[gi * POOL_CH:(gi + 1) * POOL_CH, gi * POOL_CH:(gi + 1) * POOL_CH].set(pool_w[gi])
    return out


def _layer(x2, batch, seq, consts, layer, w_in, b_in, pool_w, pool_scale, ret_norm_g, w_out, b_out, ln1_g,
           ln1_b, router_w, router_b, w_gate_up, b_gate_up, w_down, b_down, ln2_g, ln2_b):
    ret_consts, tri2, tril, pick, rcol = consts
    row = lambda a: a.reshape(1, -1).astype(F32)
    u_pool, sb, ret = _inproj(x2, w_in.astype(BF16), row(b_in))
    pool_o = _pool(u_pool, _block_diag(pool_w).astype(BF16), row(pool_scale), batch, seq)
    sb_o = _stick_breaking(sb, tri2, batch, seq)
    ret_o = _retention(ret, ret_consts, row(ret_norm_g), batch, seq)
    rw = jnp.pad(router_w.astype(F32), ((0, 0), (0, LANES - N_EXPERTS)))
    rw_hi = rw.astype(BF16)
    rw = jnp.concatenate([rw_hi, (rw - rw_hi.astype(F32)).astype(BF16)], axis=1)
    rb = jnp.pad(router_b.astype(F32), (0, LANES - N_EXPERTS)).reshape(1, LANES)
    x1, x1b, gates = _outproj(x2, pool_o, sb_o, ret_o, w_out.astype(BF16), row(b_out), row(ln1_g),
                              row(ln1_b), rw, rb)
    xs, gs, pm, post, gatet, cnt = _dispatch(gates, x1b, tril)
    bgu = b_gate_up.reshape(N_EXPERTS, 1, 2 * D_FF).astype(F32)
    bd = b_down.reshape(N_EXPERTS, 1, D_MODEL).astype(F32)
    ys, wgu, wd = _ffn(xs, gs, w_gate_up, bgu, w_down, bd, layer)
    cnt_flat = cnt[:, 0, :N_EXPERTS].reshape(-1)
    return _combine(cnt_flat, ys, pm, pick, rcol, x1, row(ln2_g), row(ln2_b), x1b, post, gatet, bgu, bd,
                    wgu, wd)


def kernel(x, w_in, b_in, pool_w, pool_scale, ret_norm_g, w_out, b_out, ln1_g, ln1_b, router_w, router_b,
           w_gate_up, b_gate_up, w_down, b_down, ln2_g, ln2_b):
    batch, seq, d = x.shape
    n = batch * seq
    assert d == D_MODEL and seq % TS_RET == 0 and seq % TQ_SB == 0 and seq % TS_POOL == 0
    assert n % (TM_MOE * FFN_TILES) == 0 and n % TM_ROW == 0
    j = np.arange(TQ_SB)
    tri = (j[:, None] >= j[None, :]).astype(np.float32)
    half = np.concatenate([tri, np.ones_like(tri)], axis=1)
    tri2 = jnp.asarray(np.concatenate([half, half], axis=0), BF16)
    r = np.arange(TM_MOE)
    tril = jnp.asarray((r[:, None] >= r[None, :]).astype(np.float32), BF16)
    s = np.arange(SLOTS)
    pick = jnp.asarray((np.arange(LANES)[:, None] == (s // RC_MOE)[None, :]).astype(np.float32), BF16)
    rcol = jnp.asarray((s % RC_MOE).astype(np.float32).reshape(1, SLOTS))
    consts = (_retention_consts(seq), tri2, tril, pick, rcol)
    x2 = x.reshape(n, d)
    for l in range(DEPTH):
        x2 = _layer(x2, batch, seq, consts, l, w_in[l], b_in[l], pool_w[l], pool_scale[l], ret_norm_g[l],
                    w_out[l], b_out[l], ln1_g[l], ln1_b[l], router_w[l], router_b[l], w_gate_up,
                    b_gate_up[l], w_down, b_down[l], ln2_g[l], ln2_b[l])
    return x2.reshape(batch, seq, d)
```

```python
import numpy as np
import jax
import jax.numpy as jnp
from jax import lax
from jax.experimental import pallas as pl
from jax.experimental.pallas import tpu as pltpu

D_MODEL = 1024
DEPTH = 2
CHUNK = 64
HEAD_DIM = 64
POOL_CH = 64
POOL_W = 256
POOL_HALO = 16
SB_W = 384
RET_W = 384
RET_HEADS = 6
IN_W = POOL_W + 3 * SB_W + 4 * RET_W
ROPE_BASE = 10000.0
N_EXPERTS = 32
TOP_K = 4
D_FF = D_MODEL
SWIGLU_LIMIT = 7.0
SWIGLU_ALPHA = 1.702
DN_ALPHA = (2.0 * DEPTH) ** 0.25
LN_EPS = 1e-5

LANES = 128
VMEM_LIMIT = 56 * 1024 * 1024

TM_ROW = 512
TS_POOL = 512
TQ_SB = 128
SB_SPAN = 384
TS_RET = 512
TM_MOE = 512
RC_MOE = 80
FFN_TILES = 8
SLOTS = N_EXPERTS * RC_MOE
W_PIECES = 4
SB_SKIP = -100.0

BF16 = jnp.bfloat16
F32 = jnp.float32

_NT = (((1,), (1,)), ((), ()))
_TN = (((0,), (0,)), ((), ()))


def _dot(a, b):
    return jnp.dot(a, b, preferred_element_type=F32)


def _params(*sem):
    return pltpu.CompilerParams(dimension_semantics=sem, vmem_limit_bytes=VMEM_LIMIT)


def _inproj_kernel(x_ref, w_ref, b_ref, pool_ref, sb_ref, ret_ref):
    xb = x_ref[...].astype(BF16)
    o1 = POOL_W + 3 * SB_W
    pool_ref[...] = _dot(xb, w_ref[:, :POOL_W]) + b_ref[:, :POOL_W]
    sb_ref[...] = (_dot(xb, w_ref[:, POOL_W:o1]) + b_ref[:, POOL_W:o1]).astype(BF16)
    ret_ref[...] = _dot(xb, w_ref[:, o1:]) + b_ref[:, o1:]


def _inproj(x2, w_bf, b):
    n = x2.shape[0]
    return pl.pallas_call(
        _inproj_kernel,
        grid=(n // TM_ROW,),
        in_specs=[
            pl.BlockSpec((TM_ROW, D_MODEL), lambda i: (i, 0)),
            pl.BlockSpec((D_MODEL, IN_W), lambda i: (0, 0)),
            pl.BlockSpec((1, IN_W), lambda i: (0, 0)),
        ],
        out_specs=[
            pl.BlockSpec((TM_ROW, POOL_W), lambda i: (i, 0)),
            pl.BlockSpec((TM_ROW, 3 * SB_W), lambda i: (i, 0)),
            pl.BlockSpec((TM_ROW, 4 * RET_W), lambda i: (i, 0)),
        ],
        out_shape=[
            jax.ShapeDtypeStruct((n, POOL_W), F32),
            jax.ShapeDtypeStruct((n, 3 * SB_W), BF16),
            jax.ShapeDtypeStruct((n, 4 * RET_W), F32),
        ],
        compiler_params=_params("parallel"),
        name="inproj",
    )(x2, w_bf, b)


def _pool_kernel(cur_ref, halo_ref, w_ref, scale_ref, o_ref):
    i = pl.program_id(1)
    cur = cur_ref[0]
    halo = jnp.where(i > 0, halo_ref[0], 0.0)
    ext = jnp.concatenate([halo, cur], axis=0)
    ts = cur.shape[0]
    a2 = ext[1:] + ext[:-1]
    a4 = a2[2:] + a2[:-2]
    a8 = a4[4:] + a4[:-4]
    a16 = a8[8:] + a8[:-8]
    lane = lax.broadcasted_iota(jnp.int32, (ts, POOL_W), 1)
    grp = lane // POOL_CH
    win = jnp.where(grp == 0, a2[15:15 + ts],
                    jnp.where(grp == 1, a4[13:13 + ts],
                              jnp.where(grp == 2, a8[9:9 + ts], a16[1:1 + ts])))
    width = jnp.where(grp == 0, 2, jnp.where(grp == 1, 4, jnp.where(grp == 2, 8, 16)))
    t = i * ts + lax.broadcasted_iota(jnp.int32, (ts, POOL_W), 0)
    cnt = jnp.minimum(t + 1, width).astype(F32)
    pooled = win / cnt - cur
    mixed = _dot(pooled.astype(BF16), w_ref[...]) * scale_ref[...]
    o_ref[0] = mixed.astype(BF16)


def _pool(u, w_bd, scale, batch, seq):
    u3 = u.reshape(batch, seq, POOL_W)
    per = TS_POOL // POOL_HALO
    out = pl.pallas_call(
        _pool_kernel,
        grid=(batch, seq // TS_POOL),
        in_specs=[
            pl.BlockSpec((1, TS_POOL, POOL_W), lambda b, i: (b, i, 0)),
            pl.BlockSpec((1, POOL_HALO, POOL_W), lambda b, i: (b, jnp.maximum(i * per - 1, 0), 0)),
            pl.BlockSpec((POOL_W, POOL_W), lambda b, i: (0, 0)),
            pl.BlockSpec((1, POOL_W), lambda b, i: (0, 0)),
        ],
        out_specs=pl.BlockSpec((1, TS_POOL, POOL_W), lambda b, i: (b, i, 0)),
        out_shape=jax.ShapeDtypeStruct((batch, seq, POOL_W), BF16),
        compiler_params=_params("parallel", "parallel"),
        name="pool_mixer",
    )(u3, u3, w_bd, scale)
    return out.reshape(batch * seq, POOL_W)


def _sb_kernel(q_ref, k_ref, v_ref, tri_ref, o_ref, acc_ref, carry_ref):
    qi = pl.program_id(1)
    tq = q_ref.shape[1]
    groups = q_ref.shape[2] // LANES
    lane = lax.broadcasted_iota(jnp.int32, (tq, LANES), 1)
    qh = []
    for p in range(groups):
        q = q_ref[0, :, p * LANES:(p + 1) * LANES] * jnp.asarray(HEAD_DIM ** -0.5, BF16)
        zero = jnp.zeros_like(q)
        qh += [jnp.where(lane < HEAD_DIM, q, zero), jnp.where(lane >= HEAD_DIM, q, zero)]

    def log_keep(z):
        return jnp.minimum(-z, 0.0) - jnp.log(1.0 + jnp.exp(-jnp.abs(z)))

    def suffix_sums(lk):
        hi = lk.astype(BF16)
        lo = (lk - hi.astype(F32)).astype(BF16)
        out = []
        for b in range(lk.shape[1] // tq):
            sl = slice(b * tq, (b + 1) * tq)
            rs = _dot(jnp.concatenate([hi[:, sl], lo[:, sl]], axis=1), tri_ref[...])
            out.append((rs[:, :tq], rs[:, tq:]))
        return out

    s0 = pl.multiple_of(jnp.maximum(qi * tq + tq - SB_SPAN, 0), tq)
    qpos = qi * tq + lax.broadcasted_iota(jnp.int32, (tq, SB_SPAN), 0)
    kpos = s0 + lax.broadcasted_iota(jnp.int32, (tq, SB_SPAN), 1)
    below = kpos < qpos
    heads = range(2 * groups)
    cols = [slice((h // 2) * LANES, (h // 2 + 1) * LANES) for h in heads]
    zs = [lax.dot_general(qh[h], k_ref[0, pl.ds(s0, SB_SPAN), cols[h]], _NT, preferred_element_type=F32)
          for h in heads]
    sums = [suffix_sums(jnp.where(below, log_keep(z), 0.0)) for z in zs]
    weights, runs = [], []
    for h in heads:
        run = None
        r = [None] * len(sums[h])
        for b in reversed(range(len(sums[h]))):
            r_in, tot = sums[h][b]
            r[b] = r_in if run is None else r_in + run
            run = tot if run is None else run + tot
        weights.append(jnp.where(below, jnp.exp(zs[h] + jnp.concatenate(r, axis=1)), 0.0).astype(BF16))
        runs.append(run)
    worst = None
    for h in heads:
        acc_ref[h] = _dot(weights[h], v_ref[0, pl.ds(s0, SB_SPAN), cols[h]])
        carry_ref[h] = runs[h]
        m = jnp.max(runs[h])
        worst = m if worst is None else jnp.maximum(worst, m)

    def tile(kb):
        start = pl.multiple_of(kb * tq, tq)
        worst = None
        for h in range(2 * groups):
            cols = slice((h // 2) * LANES, (h // 2 + 1) * LANES)
            k = k_ref[0, pl.ds(start, tq), cols]
            v = v_ref[0, pl.ds(start, tq), cols]
            z = lax.dot_general(qh[h], k, _NT, preferred_element_type=F32)
            (r_in, tot), = suffix_sums(log_keep(z))
            carry = carry_ref[h]
            a = jnp.exp(z + r_in + carry)
            acc_ref[h] += _dot(a.astype(BF16), v)
            carry = carry + tot
            carry_ref[h] = carry
            m = jnp.max(carry)
            worst = m if worst is None else jnp.maximum(worst, m)
        return worst

    def cond(state):
        kb, worst = state
        return jnp.logical_and(kb >= 0, worst > SB_SKIP)

    def body(state):
        kb, _ = state
        return kb - 1, tile(kb)

    lax.while_loop(cond, body, (s0 // tq - 1, worst))
    for p in range(groups):
        o_ref[0, :, p * LANES:(p + 1) * LANES] = jnp.where(
            lane < HEAD_DIM, acc_ref[2 * p], acc_ref[2 * p + 1]).astype(BF16)


def _stick_breaking(sb, tri2, batch, seq):
    sb3 = sb.reshape(batch, seq, 3 * SB_W)
    heads = SB_W // HEAD_DIM
    once = pl.Buffered(1)
    out = pl.pallas_call(
        _sb_kernel,
        grid=(batch, seq // TQ_SB),
        in_specs=[
            pl.BlockSpec((1, TQ_SB, SB_W), lambda b, i: (b, i, 0)),
            pl.BlockSpec((1, seq, SB_W), lambda b, i: (b, 0, 1), pipeline_mode=once),
            pl.BlockSpec((1, seq, SB_W), lambda b, i: (b, 0, 2), pipeline_mode=once),
            pl.BlockSpec((2 * TQ_SB, 2 * TQ_SB), lambda b, i: (0, 0)),
        ],
        out_specs=pl.BlockSpec((1, TQ_SB, SB_W), lambda b, i: (b, i, 0)),
        out_shape=jax.ShapeDtypeStruct((batch, seq, SB_W), BF16),
        scratch_shapes=[
            pltpu.VMEM((heads, TQ_SB, LANES), F32),
            pltpu.VMEM((heads, TQ_SB, TQ_SB), F32),
        ],
        compiler_params=_params("parallel", "arbitrary"),
        name="stick_breaking",
    )(sb3, sb3, sb3, tri2)
    return out.reshape(batch * seq, SB_W)


def _ret_kernel(q_ref, k_ref, v_ref, g_ref, cos_ref, sin_ref, dmat_ref, dec_ref, xi_ref, gch_ref,
                ng_ref, o_ref, state_ref, obuf_ref):
    si = pl.program_id(2)
    ts = q_ref.shape[1]

    @pl.when(si == 0)
    def _():
        state_ref[...] = jnp.zeros_like(state_ref)

    lane = lax.broadcasted_iota(jnp.int32, (ts, LANES), 1)
    first_half = (lane % HEAD_DIM) < (HEAD_DIM // 2)
    cos = cos_ref[...]
    sin = sin_ref[...]

    def rot(x):
        partner = jnp.where(first_half, pltpu.roll(x, LANES - HEAD_DIM // 2, 1),
                            pltpu.roll(x, HEAD_DIM // 2, 1))
        return x * cos + partner * sin

    qr = rot(q_ref[0])
    kr = rot(k_ref[0]) * (HEAD_DIM ** -0.5)
    v = v_ref[0]

    lane_c = lax.broadcasted_iota(jnp.int32, (CHUNK, LANES), 1)
    head0 = lane_c < HEAD_DIM
    r_i = lax.broadcasted_iota(jnp.int32, (LANES, LANES), 0) // HEAD_DIM
    c_i = lax.broadcasted_iota(jnp.int32, (LANES, LANES), 1) // HEAD_DIM
    same_head = r_i == c_i
    dmat = dmat_ref[0]
    dec = dec_ref[0]
    xi = xi_ref[0]
    gch = gch_ref[0]

    chunks = [slice(n * CHUNK, (n + 1) * CHUNK) for n in range(ts // CHUNK)]
    kcb = [kr[sl].astype(BF16) for sl in chunks]
    vcb = [v[sl].astype(BF16) for sl in chunks]
    kvs = [lax.dot_general(kcb[n], (v[sl] * dec).astype(BF16), _TN, preferred_element_type=F32)
           for n, sl in enumerate(chunks)]
    scores = []
    for n, sl in enumerate(chunks):
        qc = qr[sl]
        qstack = jnp.concatenate([jnp.where(head0, qc, 0.0), jnp.where(head0, 0.0, qc)], axis=0)
        scores.append((lax.dot_general(qstack.astype(BF16), kcb[n], _NT, preferred_element_type=F32)
                       * dmat).astype(BF16))
    o2s = [_dot(scores[n], vcb[n]) for n in range(len(chunks))]
    states = []
    state = state_ref[...]
    for n in range(len(chunks)):
        states.append(state.astype(BF16))
        state = gch * state + jnp.where(same_head, kvs[n], 0.0)
    state_ref[...] = state
    for n, sl in enumerate(chunks):
        o_intra = jnp.where(head0, o2s[n][:CHUNK], o2s[n][CHUNK:])
        o_inter = _dot((qr[sl] * xi).astype(BF16), states[n])
        obuf_ref[sl, :] = o_intra + o_inter

    o = obuf_ref[...]
    lane_head0 = lane < HEAD_DIM
    inv = 1.0 / HEAD_DIM

    def head_mean(x):
        m0 = jnp.sum(jnp.where(lane_head0, x, 0.0), axis=-1, keepdims=True) * inv
        m1 = jnp.sum(jnp.where(lane_head0, 0.0, x), axis=-1, keepdims=True) * inv
        return jnp.where(lane_head0, m0, m1)

    mu = head_mean(o)
    oc = o - mu
    var = head_mean(oc * oc)
    normed = oc * lax.rsqrt(var + LN_EPS) * ng_ref[...]
    gate = g_ref[0]
    o_ref[0] = (gate / (1.0 + jnp.exp(-gate)) * normed).astype(BF16)


def _retention(ret, consts, norm_g, batch, seq):
    cos, sin, dmat, dec, xi, gch = consts
    ret3 = ret.reshape(batch, seq, 4 * RET_W)
    pairs = RET_W // LANES
    out = pl.pallas_call(
        _ret_kernel,
        grid=(batch, pairs, seq // TS_RET),
        in_specs=[
            pl.BlockSpec((1, TS_RET, LANES), lambda b, p, i: (b, i, p)),
            pl.BlockSpec((1, TS_RET, LANES), lambda b, p, i: (b, i, pairs + p)),
            pl.BlockSpec((1, TS_RET, LANES), lambda b, p, i: (b, i, 2 * pairs + p)),
            pl.BlockSpec((1, TS_RET, LANES), lambda b, p, i: (b, i, 3 * pairs + p)),
            pl.BlockSpec((TS_RET, LANES), lambda b, p, i: (i, 0)),
            pl.BlockSpec((TS_RET, LANES), lambda b, p, i: (i, 0)),
            pl.BlockSpec((1, 2 * CHUNK, CHUNK), lambda b, p, i: (p, 0, 0)),
            pl.BlockSpec((1, CHUNK, LANES), lambda b, p, i: (p, 0, 0)),
            pl.BlockSpec((1, CHUNK, LANES), lambda b, p, i: (p, 0, 0)),
            pl.BlockSpec((1, 1, LANES), lambda b, p, i: (p, 0, 0)),
            pl.BlockSpec((1, LANES), lambda b, p, i: (0, p)),
        ],
        out_specs=pl.BlockSpec((1, TS_RET, LANES), lambda b, p, i: (b, i, p)),
        out_shape=jax.ShapeDtypeStruct((batch, seq, RET_W), BF16),
        scratch_shapes=[
            pltpu.VMEM((LANES, LANES), F32),
            pltpu.VMEM((TS_RET, LANES), F32),
        ],
        compiler_params=_params("parallel", "parallel", "arbitrary"),
        name="retention",
    )(ret3, ret3, ret3, ret3, cos, sin, dmat, dec, xi, gch, norm_g)
    return out.reshape(batch * seq, RET_W)


def _retention_consts(seq):
    half = HEAD_DIM // 2
    pos = jnp.arange(seq, dtype=F32)
    inv = ROPE_BASE ** (-jnp.arange(half, dtype=F32) / half)
    ang = pos[:, None] * inv[None, :]
    cos_h, sin_h = jnp.cos(ang), jnp.sin(ang)
    cos = jnp.tile(cos_h, (1, 2 * LANES // HEAD_DIM))
    sin = jnp.tile(jnp.concatenate([-sin_h, sin_h], axis=1), (1, LANES // HEAD_DIM))
    log_g = jnp.log(1.0 - 2.0 ** (-5.0 - jnp.arange(RET_HEADS, dtype=F32)))
    c = jnp.arange(CHUNK, dtype=F32)
    dmat = jnp.exp(jnp.abs(c[:, None] - c[None, :])[None] * log_g[:, None, None])
    dmat = dmat.reshape(RET_HEADS // 2, 2 * CHUNK, CHUNK)
    lane_log_g = jnp.repeat(log_g, HEAD_DIM).reshape(RET_HEADS // 2, 1, LANES)
    dec = jnp.exp((CHUNK - 1 - c)[None, :, None] * lane_log_g)
    xi = jnp.exp((c + 1.0)[None, :, None] * lane_log_g)
    gch = jnp.exp(CHUNK * lane_log_g)
    return cos, sin, dmat, dec, xi, gch


def _layer_norm(z, g, b):
    mu = jnp.mean(z, axis=-1, keepdims=True)
    zc = z - mu
    var = jnp.mean(zc * zc, axis=-1, keepdims=True)
    return zc * lax.rsqrt(var + LN_EPS) * g + b


def _outproj_kernel(x_ref, p_ref, s_ref, r_ref, w_ref, b_ref, g_ref, be_ref, rw_ref, rb_ref,
                    x1_ref, x1b_ref, gates_ref):
    o1 = POOL_W + SB_W
    y = (_dot(p_ref[...], w_ref[:POOL_W, :]) + _dot(s_ref[...], w_ref[POOL_W:o1, :])
         + _dot(r_ref[...], w_ref[o1:, :]))
    x1 = _layer_norm(DN_ALPHA * x_ref[...] + (y + b_ref[...]), g_ref[...], be_ref[...])
    x1_ref[...] = x1
    x_hi = x1.astype(BF16)
    x1b_ref[...] = x_hi

    x_mid = (x1 - x_hi.astype(F32)).astype(BF16)
    t = _dot(x_hi, rw_ref[...])
    logits = t[:, :LANES] + t[:, LANES:] + _dot(x_mid, rw_ref[:, :LANES]) + rb_ref[...]
    tm = logits.shape[0]
    lane = lax.broadcasted_iota(jnp.int32, (tm, LANES), 1).astype(F32)
    vals = jnp.where(lane < N_EXPERTS, logits, -jnp.inf)
    top_v, top_sel = [], []
    for _ in range(TOP_K):
        m = jnp.max(vals, axis=-1, keepdims=True)
        idx = jnp.min(jnp.where(vals == m, lane, float(LANES)), axis=-1, keepdims=True)
        sel = lane == idx
        vals = jnp.where(sel, -jnp.inf, vals)
        top_v.append(m)
        top_sel.append(sel)
    ex = [jnp.exp(m - top_v[0]) for m in top_v]
    den = ex[0] + ex[1] + ex[2] + ex[3]
    gates = jnp.full((tm, LANES), -1.0, F32)
    for sel, e in zip(top_sel, ex):
        gates = jnp.where(sel, e / den, gates)
    gates_ref[...] = gates


def _outproj(x2, pool_o, sb_o, ret_o, w_bf, b, g, be, rw, rb):
    n = x2.shape[0]
    row = lambda i: (i, 0)
    fixed = lambda i: (0, 0)
    return pl.pallas_call(
        _outproj_kernel,
        grid=(n // TM_ROW,),
        in_specs=[
            pl.BlockSpec((TM_ROW, D_MODEL), row),
            pl.BlockSpec((TM_ROW, POOL_W), row),
            pl.BlockSpec((TM_ROW, SB_W), row),
            pl.BlockSpec((TM_ROW, RET_W), row),
            pl.BlockSpec((D_MODEL, D_MODEL), fixed),
            pl.BlockSpec((1, D_MODEL), fixed),
            pl.BlockSpec((1, D_MODEL), fixed),
            pl.BlockSpec((1, D_MODEL), fixed),
            pl.BlockSpec((D_MODEL, 2 * LANES), fixed),
            pl.BlockSpec((1, LANES), fixed),
        ],
        out_specs=[
            pl.BlockSpec((TM_ROW, D_MODEL), row),
            pl.BlockSpec((TM_ROW, D_MODEL), row),
            pl.BlockSpec((TM_ROW, LANES), row),
        ],
        out_shape=[
            jax.ShapeDtypeStruct((n, D_MODEL), F32),
            jax.ShapeDtypeStruct((n, D_MODEL), BF16),
            jax.ShapeDtypeStruct((n, LANES), F32),
        ],
        compiler_params=_params("parallel"),
        name="outproj_ln_router",
    )(x2, pool_o, sb_o, ret_o, w_bf, b, g, be, rw, rb)


def _dispatch_kernel(gates_ref, xb_ref, tril_ref, xs_ref, gs_ref, pm_ref, post_ref, gatet_ref, cnt_ref,
                     oh_ref):
    tm = gates_ref.shape[0]
    gates = gates_ref[...]
    sel = gates >= 0.0
    self_ = jnp.where(sel, 1.0, 0.0)
    incl = _dot(tril_ref[...], self_.astype(BF16))
    pos = incl - self_
    cnt_ref[0] = incl[tm - 1:tm, :].astype(jnp.int32)
    pm_ref[...] = jnp.where(jnp.logical_and(sel, pos < RC_MOE), pos, -1.0).astype(BF16)
    post = jnp.where(sel, pos, -1.0).T[:N_EXPERTS, :]
    gatet = gates.T[:N_EXPERTS, :]
    post_ref[0] = post
    gatet_ref[0] = gatet
    post_i = post.astype(jnp.int32)
    slot = lax.broadcasted_iota(jnp.int32, (RC_MOE, tm), 0)
    for e in range(N_EXPERTS):
        onehot = slot == post_i[e:e + 1, :]
        oh_ref[e * RC_MOE:(e + 1) * RC_MOE, :] = jnp.where(onehot, 1.0, 0.0).astype(BF16)
        gate = jnp.sum(jnp.where(onehot, gatet[e:e + 1, :], 0.0), axis=1, keepdims=True)
        gs_ref[0, e] = jnp.broadcast_to(gate, (RC_MOE, LANES))
    xs = _dot(oh_ref[...], xb_ref[...]).astype(BF16)
    xs_ref[0] = xs.reshape(N_EXPERTS, RC_MOE, D_MODEL)


def _dispatch(gates, x1b, tril):
    n = gates.shape[0]
    nt = n // TM_MOE
    return pl.pallas_call(
        _dispatch_kernel,
        grid=(nt,),
        in_specs=[
            pl.BlockSpec((TM_MOE, LANES), lambda i: (i, 0)),
            pl.BlockSpec((TM_MOE, D_MODEL), lambda i: (i, 0)),
            pl.BlockSpec((TM_MOE, TM_MOE), lambda i: (0, 0)),
        ],
        out_specs=[
            pl.BlockSpec((1, N_EXPERTS, RC_MOE, D_MODEL), lambda i: (i, 0, 0, 0)),
            pl.BlockSpec((1, N_EXPERTS, RC_MOE, LANES), lambda i: (i, 0, 0, 0)),
            pl.BlockSpec((TM_MOE, LANES), lambda i: (i, 0)),
            pl.BlockSpec((1, N_EXPERTS, TM_MOE), lambda i: (i, 0, 0)),
            pl.BlockSpec((1, N_EXPERTS, TM_MOE), lambda i: (i, 0, 0)),
            pl.BlockSpec((1, 1, LANES), lambda i: (i, 0, 0)),
        ],
        out_shape=[
            jax.ShapeDtypeStruct((nt, N_EXPERTS, RC_MOE, D_MODEL), BF16),
            jax.ShapeDtypeStruct((nt, N_EXPERTS, RC_MOE, LANES), F32),
            jax.ShapeDtypeStruct((n, LANES), BF16),
            jax.ShapeDtypeStruct((nt, N_EXPERTS, TM_MOE), F32),
            jax.ShapeDtypeStruct((nt, N_EXPERTS, TM_MOE), F32),
            jax.ShapeDtypeStruct((nt, 1, LANES), jnp.int32),
        ],
        scratch_shapes=[pltpu.VMEM((SLOTS, TM_MOE), BF16)],
        compiler_params=_params("parallel"),
        name="dispatch",
    )(gates, x1b, tril)


def _swiglu_ffn(x, wgu, bgu, wd, bd):
    h = _dot(x, wgu) + bgu
    g = jnp.minimum(h[:, :D_FF], SWIGLU_LIMIT)
    up = jnp.clip(h[:, D_FF:], -SWIGLU_LIMIT, SWIGLU_LIMIT)
    act = (up + 1.0) * (g / (1.0 + jnp.exp(-SWIGLU_ALPHA * g)))
    return _dot(act.astype(BF16), wd) + bd


def _ffn_kernel(xs_ref, gs_ref, wgu_ref, bgu_ref, wd_ref, bd_ref, ys_ref, wgu_bf_ref, wd_bf_ref):
    @pl.when(pl.program_id(1) == 0)
    def _():
        wgu_bf_ref[0] = wgu_ref[0, 0].astype(BF16)
        wd_bf_ref[0] = wd_ref[0, 0].astype(BF16)

    rows = FFN_TILES * RC_MOE
    x = xs_ref[:, 0].reshape(rows, D_MODEL)
    y = _swiglu_ffn(x, wgu_bf_ref[0], bgu_ref[0], wd_bf_ref[0], bd_ref[0])
    gate = gs_ref[:, 0].reshape(rows, LANES)[:, :1]
    ys_ref[:, 0] = (y * gate).astype(BF16).reshape(FFN_TILES, RC_MOE, D_MODEL)


def _ffn(xs, gs, w_gate_up, bgu, w_down, bd, layer):
    nt = xs.shape[0]
    return pl.pallas_call(
        _ffn_kernel,
        grid=(N_EXPERTS, nt // FFN_TILES),
        in_specs=[
            pl.BlockSpec((FFN_TILES, 1, RC_MOE, D_MODEL), lambda e, j: (j, e, 0, 0)),
            pl.BlockSpec((FFN_TILES, 1, RC_MOE, LANES), lambda e, j: (j, e, 0, 0)),
            pl.BlockSpec((1, 1, D_MODEL, 2 * D_FF), lambda e, j: (layer, e, 0, 0)),
            pl.BlockSpec((1, 1, 2 * D_FF), lambda e, j: (e, 0, 0)),
            pl.BlockSpec((1, 1, D_FF, D_MODEL), lambda e, j: (layer, e, 0, 0)),
            pl.BlockSpec((1, 1, D_MODEL), lambda e, j: (e, 0, 0)),
        ],
        out_specs=[
            pl.BlockSpec((FFN_TILES, 1, RC_MOE, D_MODEL), lambda e, j: (j, e, 0, 0)),
            pl.BlockSpec((1, D_MODEL, 2 * D_FF), lambda e, j: (e, 0, 0)),
            pl.BlockSpec((1, D_FF, D_MODEL), lambda e, j: (e, 0, 0)),
        ],
        out_shape=[
            jax.ShapeDtypeStruct(xs.shape, BF16),
            jax.ShapeDtypeStruct((N_EXPERTS, D_MODEL, 2 * D_FF), BF16),
            jax.ShapeDtypeStruct((N_EXPERTS, D_FF, D_MODEL), BF16),
        ],
        compiler_params=_params("parallel", "arbitrary"),
        name="expert_ffn",
    )(xs, gs, w_gate_up, bgu, w_down, bd)


def _combine_kernel(cnt_ref, first_ref, ys_ref, pm_ref, pick_ref, rcol_ref, x1_ref, g_ref, b_ref, xb_ref,
                    post_ref, gatet_ref, bgu_ref, bd_ref, wgu_hbm, wd_hbm, o_ref, acc_ref, wgu_buf, wd_buf,
                    sem):
    i = pl.program_id(0)
    tm = pm_ref.shape[0]

    def weight_copies(e):
        out = []
        for c in range(W_PIECES):
            gu = pl.ds(c * (D_MODEL // W_PIECES), D_MODEL // W_PIECES)
            dn = pl.ds(c * (D_FF // W_PIECES), D_FF // W_PIECES)
            out.append(pltpu.make_async_copy(wgu_hbm.at[e, gu], wgu_buf.at[gu], sem.at[0]))
            out.append(pltpu.make_async_copy(wd_hbm.at[e, dn], wd_buf.at[dn], sem.at[1]))
        return out

    first = first_ref[i]

    @pl.when(first < N_EXPERTS)
    def _():
        for cp in weight_copies(first):
            cp.start()

    ranks = _dot(pm_ref[...], pick_ref[...])
    back = jnp.where(ranks == rcol_ref[...], 1.0, 0.0).astype(BF16)
    acc_ref[...] = _dot(back, ys_ref[0].reshape(SLOTS, D_MODEL))

    slot = lax.broadcasted_iota(jnp.int32, (RC_MOE, tm), 0)

    def per_expert(e, carry):
        cnt = cnt_ref[i * N_EXPERTS + e]

        @pl.when(cnt > RC_MOE)
        def _():
            @pl.when(e != first)
            def _():
                for cp in weight_copies(e):
                    cp.start()

            for cp in weight_copies(e):
                cp.wait()
            prow = post_ref[0, pl.ds(e, 1), :].astype(jnp.int32)
            grow = gatet_ref[0, pl.ds(e, 1), :]

            def chunk(c, carry2):
                onehot = (slot + c * RC_MOE) == prow
                oh = jnp.where(onehot, 1.0, 0.0).astype(BF16)
                xg = _dot(oh, xb_ref[...]).astype(BF16)
                y = _swiglu_ffn(xg, wgu_buf[...], bgu_ref[e], wd_buf[...], bd_ref[e])
                gate = jnp.sum(jnp.where(onehot, grow, 0.0), axis=1, keepdims=True)
                yg = (y * gate).astype(BF16)
                acc_ref[...] += lax.dot_general(oh, yg, _TN, preferred_element_type=F32)
                return carry2

            lax.fori_loop(1, (cnt + RC_MOE - 1) // RC_MOE, chunk, 0)

        return carry

    lax.fori_loop(0, N_EXPERTS, per_expert, 0)
    o_ref[...] = _layer_norm(DN_ALPHA * x1_ref[...] + acc_ref[...], g_ref[...], b_ref[...])


def _combine(cnt, first, ys, pm, pick, rcol, x1, g, b, x1b, post, gatet, bgu, bd, wgu, wd):
    n = x1.shape[0]
    nt = n // TM_MOE
    row = lambda i, c, f: (i, 0)
    fixed2 = lambda i, c, f: (0, 0)
    fixed3 = lambda i, c, f: (0, 0, 0)
    grid_spec = pltpu.PrefetchScalarGridSpec(
        num_scalar_prefetch=2,
        grid=(nt,),
        in_specs=[
            pl.BlockSpec((1, N_EXPERTS, RC_MOE, D_MODEL), lambda i, c, f: (i, 0, 0, 0)),
            pl.BlockSpec((TM_MOE, LANES), row),
            pl.BlockSpec((LANES, SLOTS), fixed2),
            pl.BlockSpec((1, SLOTS), fixed2),
            pl.BlockSpec((TM_MOE, D_MODEL), row),
            pl.BlockSpec((1, D_MODEL), fixed2),
            pl.BlockSpec((1, D_MODEL), fixed2),
            pl.BlockSpec((TM_MOE, D_MODEL), row),
            pl.BlockSpec((1, N_EXPERTS, TM_MOE), lambda i, c, f: (i, 0, 0)),
            pl.BlockSpec((1, N_EXPERTS, TM_MOE), lambda i, c, f: (i, 0, 0)),
            pl.BlockSpec((N_EXPERTS, 1, 2 * D_FF), fixed3),
            pl.BlockSpec((N_EXPERTS, 1, D_MODEL), fixed3),
            pl.BlockSpec(memory_space=pl.ANY),
            pl.BlockSpec(memory_space=pl.ANY),
        ],
        out_specs=pl.BlockSpec((TM_MOE, D_MODEL), row),
        scratch_shapes=[
            pltpu.VMEM((TM_MOE, D_MODEL), F32),
            pltpu.VMEM((D_MODEL, 2 * D_FF), BF16),
            pltpu.VMEM((D_FF, D_MODEL), BF16),
            pltpu.SemaphoreType.DMA((2,)),
        ],
    )
    return pl.pallas_call(
        _combine_kernel,
        grid_spec=grid_spec,
        out_shape=jax.ShapeDtypeStruct((n, D_MODEL), F32),
        compiler_params=_params("arbitrary"),
        name="combine_ln",
    )(cnt, first, ys, pm, pick, rcol, x1, g, b, x1b, post, gatet, bgu, bd, wgu, wd)


def _block_diag(pool_w):
    groups = pool_w.shape[0]
    out = jnp.zeros((POOL_W, POOL_W), pool_w.dtype)
    for gi in range(groups):
        out = out.at[gi * POOL_CH:(gi + 1) * POOL_CH, gi * POOL_CH:(gi + 1) * POOL_CH].set(pool_w[gi])
    return out


def _layer(x2, batch, seq, consts, layer, w_in, b_in, pool_w, pool_scale, ret_norm_g, w_out, b_out, ln1_g,
           ln1_b, router_w, router_b, w_gate_up, b_gate_up, w_down, b_down, ln2_g, ln2_b):
    ret_consts, tri2, tril, pick, rcol = consts
    row = lambda a: a.reshape(1, -1).astype(F32)
    u_pool, sb, ret = _inproj(x2, w_in.astype(BF16), row(b_in))
    pool_o = _pool(u_pool, _block_diag(pool_w).astype(BF16), row(pool_scale), batch, seq)
    sb_o = _stick_breaking(sb, tri2, batch, seq)
    ret_o = _retention(ret, ret_consts, row(ret_norm_g), batch, seq)
    rw = jnp.pad(router_w.astype(F32), ((0, 0), (0, LANES - N_EXPERTS)))
    rw_hi = rw.astype(BF16)
    rw = jnp.concatenate([rw_hi, (rw - rw_hi.astype(F32)).astype(BF16)], axis=1)
    rb = jnp.pad(router_b.astype(F32), (0, LANES - N_EXPERTS)).reshape(1, LANES)
    x1, x1b, gates = _outproj(x2, pool_o, sb_o, ret_o, w_out.astype(BF16), row(b_out), row(ln1_g),
                              row(ln1_b), rw, rb)
    xs, gs, pm, post, gatet, cnt = _dispatch(gates, x1b, tril)
    bgu = b_gate_up.reshape(N_EXPERTS, 1, 2 * D_FF).astype(F32)
    bd = b_down.reshape(N_EXPERTS, 1, D_MODEL).astype(F32)
    ys, wgu, wd = _ffn(xs, gs, w_gate_up, bgu, w_down, bd, layer)
    cnt2 = cnt[:, 0, :N_EXPERTS]
    over = cnt2 > RC_MOE
    first = jnp.where(jnp.any(over, axis=1), jnp.argmax(over, axis=1), N_EXPERTS).astype(jnp.int32)
    cnt_flat = cnt2.reshape(-1)
    return _combine(cnt_flat, first, ys, pm, pick, rcol, x1, row(ln2_g), row(ln2_b), x1b, post, gatet, bgu, bd,
                    wgu, wd)


def kernel(x, w_in, b_in, pool_w, pool_scale, ret_norm_g, w_out, b_out, ln1_g, ln1_b, router_w, router_b,
           w_gate_up, b_gate_up, w_down, b_down, ln2_g, ln2_b):
    batch, seq, d = x.shape
    n = batch * seq
    assert d == D_MODEL and seq % TS_RET == 0 and seq % TQ_SB == 0 and seq % TS_POOL == 0
    assert n % (TM_MOE * FFN_TILES) == 0 and n % TM_ROW == 0
    j = np.arange(TQ_SB)
    tri = (j[:, None] >= j[None, :]).astype(np.float32)
    half = np.concatenate([tri, np.ones_like(tri)], axis=1)
    tri2 = jnp.asarray(np.concatenate([half, half], axis=0), BF16)
    r = np.arange(TM_MOE)
    tril = jnp.asarray((r[:, None] >= r[None, :]).astype(np.float32), BF16)
    s = np.arange(SLOTS)
    pick = jnp.asarray((np.arange(LANES)[:, None] == (s // RC_MOE)[None, :]).astype(np.float32), BF16)
    rcol = jnp.asarray((s % RC_MOE).astype(np.float32).reshape(1, SLOTS))
    consts = (_retention_consts(seq), tri2, tril, pick, rcol)
    x2 = x.reshape(n, d)
    for l in range(DEPTH):
        x2 = _layer(x2, batch, seq, consts, l, w_in[l], b_in[l], pool_w[l], pool_scale[l], ret_norm_g[l],
                    w_out[l], b_out[l], ln1_g[l], ln1_b[l], router_w[l], router_b[l], w_gate_up,
                    b_gate_up[l], w_down, b_down[l], ln2_g[l], ln2_b[l])
    return x2.reshape(batch, seq, d)
```

```python
import numpy as np
import jax
import jax.numpy as jnp
from jax import lax
from jax.experimental import pallas as pl
from jax.experimental.pallas import tpu as pltpu

D_MODEL = 1024
DEPTH = 2
CHUNK = 64
HEAD_DIM = 64
POOL_CH = 64
POOL_W = 256
POOL_HALO = 16
SB_W = 384
RET_W = 384
RET_HEADS = 6
IN_W = POOL_W + 3 * SB_W + 4 * RET_W
ROPE_BASE = 10000.0
N_EXPERTS = 32
TOP_K = 4
D_FF = D_MODEL
SWIGLU_LIMIT = 7.0
SWIGLU_ALPHA = 1.702
DN_ALPHA = (2.0 * DEPTH) ** 0.25
LN_EPS = 1e-5

LANES = 128
VMEM_LIMIT = 56 * 1024 * 1024

TM_ROW = 512
TS_POOL = 512
TQ_SB = 128
SB_SPAN = 384
TS_RET = 512
TM_MOE = 512
RC_MOE = 96
RC_MAIN = 80
FFN_TILES = 8
SLOTS = N_EXPERTS * RC_MOE
SB_SKIP = -100.0

BF16 = jnp.bfloat16
F32 = jnp.float32

_NT = (((1,), (1,)), ((), ()))
_TN = (((0,), (0,)), ((), ()))


def _dot(a, b):
    return jnp.dot(a, b, preferred_element_type=F32)


def _params(*sem):
    return pltpu.CompilerParams(dimension_semantics=sem, vmem_limit_bytes=VMEM_LIMIT)


def _inproj_kernel(x_ref, w_ref, b_ref, pool_ref, sb_ref, ret_ref):
    xb = x_ref[...].astype(BF16)
    o1 = POOL_W + 3 * SB_W
    pool_ref[...] = _dot(xb, w_ref[:, :POOL_W]) + b_ref[:, :POOL_W]
    sb_ref[...] = (_dot(xb, w_ref[:, POOL_W:o1]) + b_ref[:, POOL_W:o1]).astype(BF16)
    ret_ref[...] = _dot(xb, w_ref[:, o1:]) + b_ref[:, o1:]


def _inproj(x2, w_bf, b):
    n = x2.shape[0]
    return pl.pallas_call(
        _inproj_kernel,
        grid=(n // TM_ROW,),
        in_specs=[
            pl.BlockSpec((TM_ROW, D_MODEL), lambda i: (i, 0)),
            pl.BlockSpec((D_MODEL, IN_W), lambda i: (0, 0)),
            pl.BlockSpec((1, IN_W), lambda i: (0, 0)),
        ],
        out_specs=[
            pl.BlockSpec((TM_ROW, POOL_W), lambda i: (i, 0)),
            pl.BlockSpec((TM_ROW, 3 * SB_W), lambda i: (i, 0)),
            pl.BlockSpec((TM_ROW, 4 * RET_W), lambda i: (i, 0)),
        ],
        out_shape=[
            jax.ShapeDtypeStruct((n, POOL_W), F32),
            jax.ShapeDtypeStruct((n, 3 * SB_W), BF16),
            jax.ShapeDtypeStruct((n, 4 * RET_W), F32),
        ],
        compiler_params=_params("parallel"),
        name="inproj",
    )(x2, w_bf, b)


def _pool_kernel(cur_ref, halo_ref, w_ref, scale_ref, o_ref):
    i = pl.program_id(1)
    cur = cur_ref[0]
    halo = jnp.where(i > 0, halo_ref[0], 0.0)
    ext = jnp.concatenate([halo, cur], axis=0)
    ts = cur.shape[0]
    a2 = ext[1:] + ext[:-1]
    a4 = a2[2:] + a2[:-2]
    a8 = a4[4:] + a4[:-4]
    a16 = a8[8:] + a8[:-8]
    lane = lax.broadcasted_iota(jnp.int32, (ts, POOL_W), 1)
    grp = lane // POOL_CH
    win = jnp.where(grp == 0, a2[15:15 + ts],
                    jnp.where(grp == 1, a4[13:13 + ts],
                              jnp.where(grp == 2, a8[9:9 + ts], a16[1:1 + ts])))
    width = jnp.where(grp == 0, 2, jnp.where(grp == 1, 4, jnp.where(grp == 2, 8, 16)))
    t = i * ts + lax.broadcasted_iota(jnp.int32, (ts, POOL_W), 0)
    cnt = jnp.minimum(t + 1, width).astype(F32)
    pooled = win / cnt - cur
    mixed = _dot(pooled.astype(BF16), w_ref[...]) * scale_ref[...]
    o_ref[0] = mixed.astype(BF16)


def _pool(u, w_bd, scale, batch, seq):
    u3 = u.reshape(batch, seq, POOL_W)
    per = TS_POOL // POOL_HALO
    out = pl.pallas_call(
        _pool_kernel,
        grid=(batch, seq // TS_POOL),
        in_specs=[
            pl.BlockSpec((1, TS_POOL, POOL_W), lambda b, i: (b, i, 0)),
            pl.BlockSpec((1, POOL_HALO, POOL_W), lambda b, i: (b, jnp.maximum(i * per - 1, 0), 0)),
            pl.BlockSpec((POOL_W, POOL_W), lambda b, i: (0, 0)),
            pl.BlockSpec((1, POOL_W), lambda b, i: (0, 0)),
        ],
        out_specs=pl.BlockSpec((1, TS_POOL, POOL_W), lambda b, i: (b, i, 0)),
        out_shape=jax.ShapeDtypeStruct((batch, seq, POOL_W), BF16),
        compiler_params=_params("parallel", "parallel"),
        name="pool_mixer",
    )(u3, u3, w_bd, scale)
    return out.reshape(batch * seq, POOL_W)


def _sb_kernel(q_ref, k_ref, v_ref, tri_ref, o_ref, acc_ref, carry_ref):
    qi = pl.program_id(1)
    tq = q_ref.shape[1]
    groups = q_ref.shape[2] // LANES
    lane = lax.broadcasted_iota(jnp.int32, (tq, LANES), 1)
    qh = []
    for p in range(groups):
        q = q_ref[0, :, p * LANES:(p + 1) * LANES] * jnp.asarray(HEAD_DIM ** -0.5, BF16)
        zero = jnp.zeros_like(q)
        qh += [jnp.where(lane < HEAD_DIM, q, zero), jnp.where(lane >= HEAD_DIM, q, zero)]

    def log_keep(z):
        return jnp.minimum(-z, 0.0) - jnp.log(1.0 + jnp.exp(-jnp.abs(z)))

    def suffix_sums(lk):
        hi = lk.astype(BF16)
        lo = (lk - hi.astype(F32)).astype(BF16)
        out = []
        for b in range(lk.shape[1] // tq):
            sl = slice(b * tq, (b + 1) * tq)
            rs = _dot(jnp.concatenate([hi[:, sl], lo[:, sl]], axis=1), tri_ref[...])
            out.append((rs[:, :tq], rs[:, tq:]))
        return out

    s0 = pl.multiple_of(jnp.maximum(qi * tq + tq - SB_SPAN, 0), tq)
    qpos = qi * tq + lax.broadcasted_iota(jnp.int32, (tq, SB_SPAN), 0)
    kpos = s0 + lax.broadcasted_iota(jnp.int32, (tq, SB_SPAN), 1)
    below = kpos < qpos
    heads = range(2 * groups)
    cols = [slice((h // 2) * LANES, (h // 2 + 1) * LANES) for h in heads]
    zs = [lax.dot_general(qh[h], k_ref[0, pl.ds(s0, SB_SPAN), cols[h]], _NT, preferred_element_type=F32)
          for h in heads]
    sums = [suffix_sums(jnp.where(below, log_keep(z), 0.0)) for z in zs]
    weights, runs = [], []
    for h in heads:
        run = None
        r = [None] * len(sums[h])
        for b in reversed(range(len(sums[h]))):
            r_in, tot = sums[h][b]
            r[b] = r_in if run is None else r_in + run
            run = tot if run is None else run + tot
        weights.append(jnp.where(below, jnp.exp(zs[h] + jnp.concatenate(r, axis=1)), 0.0).astype(BF16))
        runs.append(run)
    worst = None
    for h in heads:
        acc_ref[h] = _dot(weights[h], v_ref[0, pl.ds(s0, SB_SPAN), cols[h]])
        carry_ref[h] = runs[h]
        m = jnp.max(runs[h])
        worst = m if worst is None else jnp.maximum(worst, m)

    def tile(kb):
        start = pl.multiple_of(kb * tq, tq)
        worst = None
        for h in range(2 * groups):
            cols = slice((h // 2) * LANES, (h // 2 + 1) * LANES)
            k = k_ref[0, pl.ds(start, tq), cols]
            v = v_ref[0, pl.ds(start, tq), cols]
            z = lax.dot_general(qh[h], k, _NT, preferred_element_type=F32)
            (r_in, tot), = suffix_sums(log_keep(z))
            carry = carry_ref[h]
            a = jnp.exp(z + r_in + carry)
            acc_ref[h] += _dot(a.astype(BF16), v)
            carry = carry + tot
            carry_ref[h] = carry
            m = jnp.max(carry)
            worst = m if worst is None else jnp.maximum(worst, m)
        return worst

    def cond(state):
        kb, worst = state
        return jnp.logical_and(kb >= 0, worst > SB_SKIP)

    def body(state):
        kb, _ = state
        return kb - 1, tile(kb)

    lax.while_loop(cond, body, (s0 // tq - 1, worst))
    for p in range(groups):
        o_ref[0, :, p * LANES:(p + 1) * LANES] = jnp.where(
            lane < HEAD_DIM, acc_ref[2 * p], acc_ref[2 * p + 1]).astype(BF16)


def _stick_breaking(sb, tri2, batch, seq):
    sb3 = sb.reshape(batch, seq, 3 * SB_W)
    heads = SB_W // HEAD_DIM
    once = pl.Buffered(1)
    out = pl.pallas_call(
        _sb_kernel,
        grid=(batch, seq // TQ_SB),
        in_specs=[
            pl.BlockSpec((1, TQ_SB, SB_W), lambda b, i: (b, i, 0)),
            pl.BlockSpec((1, seq, SB_W), lambda b, i: (b, 0, 1), pipeline_mode=once),
            pl.BlockSpec((1, seq, SB_W), lambda b, i: (b, 0, 2), pipeline_mode=once),
            pl.BlockSpec((2 * TQ_SB, 2 * TQ_SB), lambda b, i: (0, 0)),
        ],
        out_specs=pl.BlockSpec((1, TQ_SB, SB_W), lambda b, i: (b, i, 0)),
        out_shape=jax.ShapeDtypeStruct((batch, seq, SB_W), BF16),
        scratch_shapes=[
            pltpu.VMEM((heads, TQ_SB, LANES), F32),
            pltpu.VMEM((heads, TQ_SB, TQ_SB), F32),
        ],
        compiler_params=_params("parallel", "arbitrary"),
        name="stick_breaking",
    )(sb3, sb3, sb3, tri2)
    return out.reshape(batch * seq, SB_W)


def _ret_kernel(q_ref, k_ref, v_ref, g_ref, cos_ref, sin_ref, dmat_ref, dec_ref, xi_ref, gch_ref,
                ng_ref, o_ref, state_ref, obuf_ref):
    si = pl.program_id(2)
    ts = q_ref.shape[1]

    @pl.when(si == 0)
    def _():
        state_ref[...] = jnp.zeros_like(state_ref)

    lane = lax.broadcasted_iota(jnp.int32, (ts, LANES), 1)
    first_half = (lane % HEAD_DIM) < (HEAD_DIM // 2)
    cos = cos_ref[...]
    sin = sin_ref[...]

    def rot(x):
        partner = jnp.where(first_half, pltpu.roll(x, LANES - HEAD_DIM // 2, 1),
                            pltpu.roll(x, HEAD_DIM // 2, 1))
        return x * cos + partner * sin

    qr = rot(q_ref[0])
    kr = rot(k_ref[0]) * (HEAD_DIM ** -0.5)
    v = v_ref[0]

    lane_c = lax.broadcasted_iota(jnp.int32, (CHUNK, LANES), 1)
    head0 = lane_c < HEAD_DIM
    r_i = lax.broadcasted_iota(jnp.int32, (LANES, LANES), 0) // HEAD_DIM
    c_i = lax.broadcasted_iota(jnp.int32, (LANES, LANES), 1) // HEAD_DIM
    same_head = r_i == c_i
    dmat = dmat_ref[0]
    dec = dec_ref[0]
    xi = xi_ref[0]
    gch = gch_ref[0]

    chunks = [slice(n * CHUNK, (n + 1) * CHUNK) for n in range(ts // CHUNK)]
    kcb = [kr[sl].astype(BF16) for sl in chunks]
    vcb = [v[sl].astype(BF16) for sl in chunks]
    kvs = [lax.dot_general(kcb[n], (v[sl] * dec).astype(BF16), _TN, preferred_element_type=F32)
           for n, sl in enumerate(chunks)]
    scores = []
    for n, sl in enumerate(chunks):
        qc = qr[sl]
        qstack = jnp.concatenate([jnp.where(head0, qc, 0.0), jnp.where(head0, 0.0, qc)], axis=0)
        scores.append((lax.dot_general(qstack.astype(BF16), kcb[n], _NT, preferred_element_type=F32)
                       * dmat).astype(BF16))
    o2s = [_dot(scores[n], vcb[n]) for n in range(len(chunks))]
    states = []
    state = state_ref[...]
    for n in range(len(chunks)):
        states.append(state.astype(BF16))
        state = gch * state + jnp.where(same_head, kvs[n], 0.0)
    state_ref[...] = state
    for n, sl in enumerate(chunks):
        o_intra = jnp.where(head0, o2s[n][:CHUNK], o2s[n][CHUNK:])
        o_inter = _dot((qr[sl] * xi).astype(BF16), states[n])
        obuf_ref[sl, :] = o_intra + o_inter

    o = obuf_ref[...]
    lane_head0 = lane < HEAD_DIM
    inv = 1.0 / HEAD_DIM

    def head_mean(x):
        m0 = jnp.sum(jnp.where(lane_head0, x, 0.0), axis=-1, keepdims=True) * inv
        m1 = jnp.sum(jnp.where(lane_head0, 0.0, x), axis=-1, keepdims=True) * inv
        return jnp.where(lane_head0, m0, m1)

    mu = head_mean(o)
    oc = o - mu
    var = head_mean(oc * oc)
    normed = oc * lax.rsqrt(var + LN_EPS) * ng_ref[...]
    gate = g_ref[0]
    o_ref[0] = (gate / (1.0 + jnp.exp(-gate)) * normed).astype(BF16)


def _retention(ret, consts, norm_g, batch, seq):
    cos, sin, dmat, dec, xi, gch = consts
    ret3 = ret.reshape(batch, seq, 4 * RET_W)
    pairs = RET_W // LANES
    out = pl.pallas_call(
        _ret_kernel,
        grid=(batch, pairs, seq // TS_RET),
        in_specs=[
            pl.BlockSpec((1, TS_RET, LANES), lambda b, p, i: (b, i, p)),
            pl.BlockSpec((1, TS_RET, LANES), lambda b, p, i: (b, i, pairs + p)),
            pl.BlockSpec((1, TS_RET, LANES), lambda b, p, i: (b, i, 2 * pairs + p)),
            pl.BlockSpec((1, TS_RET, LANES), lambda b, p, i: (b, i, 3 * pairs + p)),
            pl.BlockSpec((TS_RET, LANES), lambda b, p, i: (i, 0)),
            pl.BlockSpec((TS_RET, LANES), lambda b, p, i: (i, 0)),
            pl.BlockSpec((1, 2 * CHUNK, CHUNK), lambda b, p, i: (p, 0, 0)),
            pl.BlockSpec((1, CHUNK, LANES), lambda b, p, i: (p, 0, 0)),
            pl.BlockSpec((1, CHUNK, LANES), lambda b, p, i: (p, 0, 0)),
            pl.BlockSpec((1, 1, LANES), lambda b, p, i: (p, 0, 0)),
            pl.BlockSpec((1, LANES), lambda b, p, i: (0, p)),
        ],
        out_specs=pl.BlockSpec((1, TS_RET, LANES), lambda b, p, i: (b, i, p)),
        out_shape=jax.ShapeDtypeStruct((batch, seq, RET_W), BF16),
        scratch_shapes=[
            pltpu.VMEM((LANES, LANES), F32),
            pltpu.VMEM((TS_RET, LANES), F32),
        ],
        compiler_params=_params("parallel", "parallel", "arbitrary"),
        name="retention",
    )(ret3, ret3, ret3, ret3, cos, sin, dmat, dec, xi, gch, norm_g)
    return out.reshape(batch * seq, RET_W)


def _retention_consts(seq):
    half = HEAD_DIM // 2
    pos = jnp.arange(seq, dtype=F32)
    inv = ROPE_BASE ** (-jnp.arange(half, dtype=F32) / half)
    ang = pos[:, None] * inv[None, :]
    cos_h, sin_h = jnp.cos(ang), jnp.sin(ang)
    cos = jnp.tile(cos_h, (1, 2 * LANES // HEAD_DIM))
    sin = jnp.tile(jnp.concatenate([-sin_h, sin_h], axis=1), (1, LANES // HEAD_DIM))
    log_g = jnp.log(1.0 - 2.0 ** (-5.0 - jnp.arange(RET_HEADS, dtype=F32)))
    c = jnp.arange(CHUNK, dtype=F32)
    dmat = jnp.exp(jnp.abs(c[:, None] - c[None, :])[None] * log_g[:, None, None])
    dmat = dmat.reshape(RET_HEADS // 2, 2 * CHUNK, CHUNK)
    lane_log_g = jnp.repeat(log_g, HEAD_DIM).reshape(RET_HEADS // 2, 1, LANES)
    dec = jnp.exp((CHUNK - 1 - c)[None, :, None] * lane_log_g)
    xi = jnp.exp((c + 1.0)[None, :, None] * lane_log_g)
    gch = jnp.exp(CHUNK * lane_log_g)
    return cos, sin, dmat, dec, xi, gch


def _layer_norm(z, g, b):
    mu = jnp.mean(z, axis=-1, keepdims=True)
    zc = z - mu
    var = jnp.mean(zc * zc, axis=-1, keepdims=True)
    return zc * lax.rsqrt(var + LN_EPS) * g + b


def _outproj_kernel(x_ref, p_ref, s_ref, r_ref, w_ref, b_ref, g_ref, be_ref, rw_ref, rb_ref,
                    x1_ref, x1b_ref, gates_ref):
    o1 = POOL_W + SB_W
    y = (_dot(p_ref[...], w_ref[:POOL_W, :]) + _dot(s_ref[...], w_ref[POOL_W:o1, :])
         + _dot(r_ref[...], w_ref[o1:, :]))
    x1 = _layer_norm(DN_ALPHA * x_ref[...] + (y + b_ref[...]), g_ref[...], be_ref[...])
    x1_ref[...] = x1
    x_hi = x1.astype(BF16)
    x1b_ref[...] = x_hi

    x_mid = (x1 - x_hi.astype(F32)).astype(BF16)
    t = _dot(x_hi, rw_ref[...])
    logits = t[:, :LANES] + t[:, LANES:] + _dot(x_mid, rw_ref[:, :LANES]) + rb_ref[...]
    tm = logits.shape[0]
    lane = lax.broadcasted_iota(jnp.int32, (tm, LANES), 1).astype(F32)
    vals = jnp.where(lane < N_EXPERTS, logits, -jnp.inf)
    top_v, top_sel = [], []
    for _ in range(TOP_K):
        m = jnp.max(vals, axis=-1, keepdims=True)
        idx = jnp.min(jnp.where(vals == m, lane, float(LANES)), axis=-1, keepdims=True)
        sel = lane == idx
        vals = jnp.where(sel, -jnp.inf, vals)
        top_v.append(m)
        top_sel.append(sel)
    ex = [jnp.exp(m - top_v[0]) for m in top_v]
    den = ex[0] + ex[1] + ex[2] + ex[3]
    gates = jnp.full((tm, LANES), -1.0, F32)
    for sel, e in zip(top_sel, ex):
        gates = jnp.where(sel, e / den, gates)
    gates_ref[...] = gates


def _outproj(x2, pool_o, sb_o, ret_o, w_bf, b, g, be, rw, rb):
    n = x2.shape[0]
    row = lambda i: (i, 0)
    fixed = lambda i: (0, 0)
    return pl.pallas_call(
        _outproj_kernel,
        grid=(n // TM_ROW,),
        in_specs=[
            pl.BlockSpec((TM_ROW, D_MODEL), row),
            pl.BlockSpec((TM_ROW, POOL_W), row),
            pl.BlockSpec((TM_ROW, SB_W), row),
            pl.BlockSpec((TM_ROW, RET_W), row),
            pl.BlockSpec((D_MODEL, D_MODEL), fixed),
            pl.BlockSpec((1, D_MODEL), fixed),
            pl.BlockSpec((1, D_MODEL), fixed),
            pl.BlockSpec((1, D_MODEL), fixed),
            pl.BlockSpec((D_MODEL, 2 * LANES), fixed),
            pl.BlockSpec((1, LANES), fixed),
        ],
        out_specs=[
            pl.BlockSpec((TM_ROW, D_MODEL), row),
            pl.BlockSpec((TM_ROW, D_MODEL), row),
            pl.BlockSpec((TM_ROW, LANES), row),
        ],
        out_shape=[
            jax.ShapeDtypeStruct((n, D_MODEL), F32),
            jax.ShapeDtypeStruct((n, D_MODEL), BF16),
            jax.ShapeDtypeStruct((n, LANES), F32),
        ],
        compiler_params=_params("parallel"),
        name="outproj_ln_router",
    )(x2, pool_o, sb_o, ret_o, w_bf, b, g, be, rw, rb)


def _dispatch_kernel(gates_ref, xb_ref, tril_ref, xs_ref, gs_ref, pm_ref, post_ref, gatet_ref, cnt_ref,
                     oh_ref):
    tm = gates_ref.shape[0]
    gates = gates_ref[...]
    sel = gates >= 0.0
    self_ = jnp.where(sel, 1.0, 0.0)
    incl = _dot(tril_ref[...], self_.astype(BF16))
    pos = incl - self_
    cnt_ref[0] = incl[tm - 1:tm, :].astype(jnp.int32)
    pm_ref[...] = jnp.where(jnp.logical_and(sel, pos < RC_MOE), pos, -1.0).astype(BF16)
    post = jnp.where(sel, pos, -1.0).T[:N_EXPERTS, :]
    gatet = gates.T[:N_EXPERTS, :]
    post_ref[0] = post
    gatet_ref[0] = gatet
    post_i = post.astype(jnp.int32)
    slot = lax.broadcasted_iota(jnp.int32, (RC_MOE, tm), 0)
    for e in range(N_EXPERTS):
        onehot = slot == post_i[e:e + 1, :]
        oh_ref[e * RC_MOE:(e + 1) * RC_MOE, :] = jnp.where(onehot, 1.0, 0.0).astype(BF16)
        gate = jnp.sum(jnp.where(onehot, gatet[e:e + 1, :], 0.0), axis=1, keepdims=True)
        gs_ref[0, e] = jnp.broadcast_to(gate, (RC_MOE, LANES))
    xs = _dot(oh_ref[...], xb_ref[...]).astype(BF16)
    xs_ref[0] = xs.reshape(N_EXPERTS, RC_MOE, D_MODEL)


def _dispatch(gates, x1b, tril):
    n = gates.shape[0]
    nt = n // TM_MOE
    return pl.pallas_call(
        _dispatch_kernel,
        grid=(nt,),
        in_specs=[
            pl.BlockSpec((TM_MOE, LANES), lambda i: (i, 0)),
            pl.BlockSpec((TM_MOE, D_MODEL), lambda i: (i, 0)),
            pl.BlockSpec((TM_MOE, TM_MOE), lambda i: (0, 0)),
        ],
        out_specs=[
            pl.BlockSpec((1, N_EXPERTS, RC_MOE, D_MODEL), lambda i: (i, 0, 0, 0)),
            pl.BlockSpec((1, N_EXPERTS, RC_MOE, LANES), lambda i: (i, 0, 0, 0)),
            pl.BlockSpec((TM_MOE, LANES), lambda i: (i, 0)),
            pl.BlockSpec((1, N_EXPERTS, TM_MOE), lambda i: (i, 0, 0)),
            pl.BlockSpec((1, N_EXPERTS, TM_MOE), lambda i: (i, 0, 0)),
            pl.BlockSpec((1, 1, LANES), lambda i: (i, 0, 0)),
        ],
        out_shape=[
            jax.ShapeDtypeStruct((nt, N_EXPERTS, RC_MOE, D_MODEL), BF16),
            jax.ShapeDtypeStruct((nt, N_EXPERTS, RC_MOE, LANES), F32),
            jax.ShapeDtypeStruct((n, LANES), BF16),
            jax.ShapeDtypeStruct((nt, N_EXPERTS, TM_MOE), F32),
            jax.ShapeDtypeStruct((nt, N_EXPERTS, TM_MOE), F32),
            jax.ShapeDtypeStruct((nt, 1, LANES), jnp.int32),
        ],
        scratch_shapes=[pltpu.VMEM((SLOTS, TM_MOE), BF16)],
        compiler_params=_params("parallel"),
        name="dispatch",
    )(gates, x1b, tril)


def _swiglu_ffn(x, wgu, bgu, wd, bd):
    h = _dot(x, wgu) + bgu
    g = jnp.minimum(h[:, :D_FF], SWIGLU_LIMIT)
    up = jnp.clip(h[:, D_FF:], -SWIGLU_LIMIT, SWIGLU_LIMIT)
    act = (up + 1.0) * (g / (1.0 + jnp.exp(-SWIGLU_ALPHA * g)))
    return _dot(act.astype(BF16), wd) + bd


def _ffn_kernel(tail_ref, xs_ref, gs_ref, wgu_ref, bgu_ref, wd_ref, bd_ref, ys_ref, wgu_bf_ref, wd_bf_ref):
    e = pl.program_id(0)
    j = pl.program_id(1)

    @pl.when(j == 0)
    def _():
        wgu_bf_ref[0] = wgu_ref[0, 0].astype(BF16)
        wd_bf_ref[0] = wd_ref[0, 0].astype(BF16)

    def run(lo, hi):
        rows = FFN_TILES * (hi - lo)
        x = xs_ref[:, 0, lo:hi, :].reshape(rows, D_MODEL)
        y = _swiglu_ffn(x, wgu_bf_ref[0], bgu_ref[0], wd_bf_ref[0], bd_ref[0])
        gate = gs_ref[:, 0, lo:hi, :].reshape(rows, LANES)[:, :1]
        ys_ref[:, 0, lo:hi, :] = (y * gate).astype(BF16).reshape(FFN_TILES, hi - lo, D_MODEL)

    run(0, RC_MAIN)
    used = tail_ref[e * pl.num_programs(1) + j] > 0

    @pl.when(used)
    def _():
        run(RC_MAIN, RC_MOE)

    @pl.when(jnp.logical_not(used))
    def _():
        ys_ref[:, 0, RC_MAIN:RC_MOE, :] = jnp.zeros((FFN_TILES, RC_MOE - RC_MAIN, D_MODEL), BF16)


def _ffn(tail, xs, gs, w_gate_up, bgu, w_down, bd, layer):
    nt = xs.shape[0]
    grid_spec = pltpu.PrefetchScalarGridSpec(
        num_scalar_prefetch=1,
        grid=(N_EXPERTS, nt // FFN_TILES),
        in_specs=[
            pl.BlockSpec((FFN_TILES, 1, RC_MOE, D_MODEL), lambda e, j, t: (j, e, 0, 0)),
            pl.BlockSpec((FFN_TILES, 1, RC_MOE, LANES), lambda e, j, t: (j, e, 0, 0)),
            pl.BlockSpec((1, 1, D_MODEL, 2 * D_FF), lambda e, j, t: (layer, e, 0, 0)),
            pl.BlockSpec((1, 1, 2 * D_FF), lambda e, j, t: (e, 0, 0)),
            pl.BlockSpec((1, 1, D_FF, D_MODEL), lambda e, j, t: (layer, e, 0, 0)),
            pl.BlockSpec((1, 1, D_MODEL), lambda e, j, t: (e, 0, 0)),
        ],
        out_specs=[
            pl.BlockSpec((FFN_TILES, 1, RC_MOE, D_MODEL), lambda e, j, t: (j, e, 0, 0)),
            pl.BlockSpec((1, D_MODEL, 2 * D_FF), lambda e, j, t: (e, 0, 0)),
            pl.BlockSpec((1, D_FF, D_MODEL), lambda e, j, t: (e, 0, 0)),
        ],
    )
    return pl.pallas_call(
        _ffn_kernel,
        grid_spec=grid_spec,
        out_shape=[
            jax.ShapeDtypeStruct(xs.shape, BF16),
            jax.ShapeDtypeStruct((N_EXPERTS, D_MODEL, 2 * D_FF), BF16),
            jax.ShapeDtypeStruct((N_EXPERTS, D_FF, D_MODEL), BF16),
        ],
        compiler_params=_params("parallel", "arbitrary"),
        name="expert_ffn",
    )(tail, xs, gs, w_gate_up, bgu, w_down, bd)


def _combine_kernel(cnt_ref, ys_ref, pm_ref, pick_ref, rcol_ref, x1_ref, g_ref, b_ref, xb_ref, post_ref,
                    gatet_ref, bgu_ref, bd_ref, wgu_hbm, wd_hbm, o_ref, acc_ref, wgu_buf, wd_buf, sem):
    i = pl.program_id(0)
    tm = pm_ref.shape[0]
    ranks = _dot(pm_ref[...], pick_ref[...])
    back = jnp.where(ranks == rcol_ref[...], 1.0, 0.0).astype(BF16)
    acc_ref[...] = _dot(back, ys_ref[0].reshape(SLOTS, D_MODEL))

    slot = lax.broadcasted_iota(jnp.int32, (RC_MOE, tm), 0)

    def per_expert(e, carry):
        cnt = cnt_ref[i * N_EXPERTS + e]

        @pl.when(cnt > RC_MOE)
        def _():
            copies = (pltpu.make_async_copy(wgu_hbm.at[e], wgu_buf, sem.at[0]),
                      pltpu.make_async_copy(wd_hbm.at[e], wd_buf, sem.at[1]))
            for cp in copies:
                cp.start()
            for cp in copies:
                cp.wait()
            prow = post_ref[0, pl.ds(e, 1), :].astype(jnp.int32)
            grow = gatet_ref[0, pl.ds(e, 1), :]

            def chunk(c, carry2):
                onehot = (slot + c * RC_MOE) == prow
                oh = jnp.where(onehot, 1.0, 0.0).astype(BF16)
                xg = _dot(oh, xb_ref[...]).astype(BF16)
                y = _swiglu_ffn(xg, wgu_buf[...], bgu_ref[e], wd_buf[...], bd_ref[e])
                gate = jnp.sum(jnp.where(onehot, grow, 0.0), axis=1, keepdims=True)
                yg = (y * gate).astype(BF16)
                acc_ref[...] += lax.dot_general(oh, yg, _TN, preferred_element_type=F32)
                return carry2

            lax.fori_loop(1, (cnt + RC_MOE - 1) // RC_MOE, chunk, 0)

        return carry

    lax.fori_loop(0, N_EXPERTS, per_expert, 0)
    o_ref[...] = _layer_norm(DN_ALPHA * x1_ref[...] + acc_ref[...], g_ref[...], b_ref[...])


def _combine(cnt, ys, pm, pick, rcol, x1, g, b, x1b, post, gatet, bgu, bd, wgu, wd):
    n = x1.shape[0]
    nt = n // TM_MOE
    row = lambda i, c: (i, 0)
    fixed2 = lambda i, c: (0, 0)
    fixed3 = lambda i, c: (0, 0, 0)
    grid_spec = pltpu.PrefetchScalarGridSpec(
        num_scalar_prefetch=1,
        grid=(nt,),
        in_specs=[
            pl.BlockSpec((1, N_EXPERTS, RC_MOE, D_MODEL), lambda i, c: (i, 0, 0, 0)),
            pl.BlockSpec((TM_MOE, LANES), row),
            pl.BlockSpec((LANES, SLOTS), fixed2),
            pl.BlockSpec((1, SLOTS), fixed2),
            pl.BlockSpec((TM_MOE, D_MODEL), row),
            pl.BlockSpec((1, D_MODEL), fixed2),
            pl.BlockSpec((1, D_MODEL), fixed2),
            pl.BlockSpec((TM_MOE, D_MODEL), row),
            pl.BlockSpec((1, N_EXPERTS, TM_MOE), lambda i, c: (i, 0, 0)),
            pl.BlockSpec((1, N_EXPERTS, TM_MOE), lambda i, c: (i, 0, 0)),
            pl.BlockSpec((N_EXPERTS, 1, 2 * D_FF), fixed3),
            pl.BlockSpec((N_EXPERTS, 1, D_MODEL), fixed3),
            pl.BlockSpec(memory_space=pl.ANY),
            pl.BlockSpec(memory_space=pl.ANY),
        ],
        out_specs=pl.BlockSpec((TM_MOE, D_MODEL), row),
        scratch_shapes=[
            pltpu.VMEM((TM_MOE, D_MODEL), F32),
            pltpu.VMEM((D_MODEL, 2 * D_FF), BF16),
            pltpu.VMEM((D_FF, D_MODEL), BF16),
            pltpu.SemaphoreType.DMA((2,)),
        ],
    )
    return pl.pallas_call(
        _combine_kernel,
        grid_spec=grid_spec,
        out_shape=jax.ShapeDtypeStruct((n, D_MODEL), F32),
        compiler_params=_params("arbitrary"),
        name="combine_ln",
    )(cnt, ys, pm, pick, rcol, x1, g, b, x1b, post, gatet, bgu, bd, wgu, wd)


def _block_diag(pool_w):
    groups = pool_w.shape[0]
    out = jnp.zeros((POOL_W, POOL_W), pool_w.dtype)
    for gi in range(groups):
        out = out.at[gi * POOL_CH:(gi + 1) * POOL_CH, gi * POOL_CH:(gi + 1) * POOL_CH].set(pool_w[gi])
    return out


def _layer(x2, batch, seq, consts, layer, w_in, b_in, pool_w, pool_scale, ret_norm_g, w_out, b_out, ln1_g,
           ln1_b, router_w, router_b, w_gate_up, b_gate_up, w_down, b_down, ln2_g, ln2_b):
    ret_consts, tri2, tril, pick, rcol = consts
    row = lambda a: a.reshape(1, -1).astype(F32)
    u_pool, sb, ret = _inproj(x2, w_in.astype(BF16), row(b_in))
    pool_o = _pool(u_pool, _block_diag(pool_w).astype(BF16), row(pool_scale), batch, seq)
    sb_o = _stick_breaking(sb, tri2, batch, seq)
    ret_o = _retention(ret, ret_consts, row(ret_norm_g), batch, seq)
    rw = jnp.pad(router_w.astype(F32), ((0, 0), (0, LANES - N_EXPERTS)))
    rw_hi = rw.astype(BF16)
    rw = jnp.concatenate([rw_hi, (rw - rw_hi.astype(F32)).astype(BF16)], axis=1)
    rb = jnp.pad(router_b.astype(F32), (0, LANES - N_EXPERTS)).reshape(1, LANES)
    x1, x1b, gates = _outproj(x2, pool_o, sb_o, ret_o, w_out.astype(BF16), row(b_out), row(ln1_g),
                              row(ln1_b), rw, rb)
    xs, gs, pm, post, gatet, cnt = _dispatch(gates, x1b, tril)
    bgu = b_gate_up.reshape(N_EXPERTS, 1, 2 * D_FF).astype(F32)
    bd = b_down.reshape(N_EXPERTS, 1, D_MODEL).astype(F32)
    cnt2 = cnt[:, 0, :N_EXPERTS]
    tail = jnp.any((cnt2 > RC_MAIN).reshape(-1, FFN_TILES, N_EXPERTS), axis=1)
    ys, wgu, wd = _ffn(tail.T.reshape(-1).astype(jnp.int32), xs, gs, w_gate_up, bgu, w_down, bd, layer)
    cnt_flat = cnt2.reshape(-1)
    return _combine(cnt_flat, ys, pm, pick, rcol, x1, row(ln2_g), row(ln2_b), x1b, post, gatet, bgu, bd,
                    wgu, wd)


def kernel(x, w_in, b_in, pool_w, pool_scale, ret_norm_g, w_out, b_out, ln1_g, ln1_b, router_w, router_b,
           w_gate_up, b_gate_up, w_down, b_down, ln2_g, ln2_b):
    batch, seq, d = x.shape
    n = batch * seq
    assert d == D_MODEL and seq % TS_RET == 0 and seq % TQ_SB == 0 and seq % TS_POOL == 0
    assert n % (TM_MOE * FFN_TILES) == 0 and n % TM_ROW == 0
    j = np.arange(TQ_SB)
    tri = (j[:, None] >= j[None, :]).astype(np.float32)
    half = np.concatenate([tri, np.ones_like(tri)], axis=1)
    tri2 = jnp.asarray(np.concatenate([half, half], axis=0), BF16)
    r = np.arange(TM_MOE)
    tril = jnp.asarray((r[:, None] >= r[None, :]).astype(np.float32), BF16)
    s = np.arange(SLOTS)
    pick = jnp.asarray((np.arange(LANES)[:, None] == (s // RC_MOE)[None, :]).astype(np.float32), BF16)
    rcol = jnp.asarray((s % RC_MOE).astype(np.float32).reshape(1, SLOTS))
    consts = (_retention_consts(seq), tri2, tril, pick, rcol)
    x2 = x.reshape(n, d)
    for l in range(DEPTH):
        x2 = _layer(x2, batch, seq, consts, l, w_in[l], b_in[l], pool_w[l], pool_scale[l], ret_norm_g[l],
                    w_out[l], b_out[l], ln1_g[l], ln1_b[l], router_w[l], router_b[l], w_gate_up,
                    b_gate_up[l], w_down, b_down[l], ln2_g[l], ln2_b[l])
    return x2.reshape(batch, seq, d)
```

```python
import numpy as np
import jax
import jax.numpy as jnp
from jax import lax
from jax.experimental import pallas as pl
from jax.experimental.pallas import tpu as pltpu

D_MODEL = 1024
DEPTH = 2
CHUNK = 64
HEAD_DIM = 64
POOL_CH = 64
POOL_W = 256
POOL_HALO = 16
SB_W = 384
RET_W = 384
RET_HEADS = 6
IN_W = POOL_W + 3 * SB_W + 4 * RET_W
ROPE_BASE = 10000.0
N_EXPERTS = 32
TOP_K = 4
D_FF = D_MODEL
SWIGLU_LIMIT = 7.0
SWIGLU_ALPHA = 1.702
DN_ALPHA = (2.0 * DEPTH) ** 0.25
LN_EPS = 1e-5

LANES = 128
VMEM_LIMIT = 56 * 1024 * 1024

TM_ROW = 512
TS_POOL = 512
TQ_SB = 128
SB_ROWS = 64
SB_SPAN = 256
TS_RET = 512
TM_MOE = 512
RC_MOE = 96
RC_MAIN = 80
FFN_TILES = 8
SLOTS = N_EXPERTS * RC_MOE
SB_SKIP = -100.0

BF16 = jnp.bfloat16
F32 = jnp.float32

_NT = (((1,), (1,)), ((), ()))
_TN = (((0,), (0,)), ((), ()))


def _dot(a, b):
    return jnp.dot(a, b, preferred_element_type=F32)


def _params(*sem):
    return pltpu.CompilerParams(dimension_semantics=sem, vmem_limit_bytes=VMEM_LIMIT)


def _inproj_kernel(x_ref, w_ref, b_ref, pool_ref, sb_ref, ret_ref):
    xb = x_ref[...].astype(BF16)
    o1 = POOL_W + 3 * SB_W
    pool_ref[...] = _dot(xb, w_ref[:, :POOL_W]) + b_ref[:, :POOL_W]
    sb_ref[...] = (_dot(xb, w_ref[:, POOL_W:o1]) + b_ref[:, POOL_W:o1]).astype(BF16)
    ret_ref[...] = _dot(xb, w_ref[:, o1:]) + b_ref[:, o1:]


def _inproj(x2, w_bf, b):
    n = x2.shape[0]
    return pl.pallas_call(
        _inproj_kernel,
        grid=(n // TM_ROW,),
        in_specs=[
            pl.BlockSpec((TM_ROW, D_MODEL), lambda i: (i, 0)),
            pl.BlockSpec((D_MODEL, IN_W), lambda i: (0, 0)),
            pl.BlockSpec((1, IN_W), lambda i: (0, 0)),
        ],
        out_specs=[
            pl.BlockSpec((TM_ROW, POOL_W), lambda i: (i, 0)),
            pl.BlockSpec((TM_ROW, 3 * SB_W), lambda i: (i, 0)),
            pl.BlockSpec((TM_ROW, 4 * RET_W), lambda i: (i, 0)),
        ],
        out_shape=[
            jax.ShapeDtypeStruct((n, POOL_W), F32),
            jax.ShapeDtypeStruct((n, 3 * SB_W), BF16),
            jax.ShapeDtypeStruct((n, 4 * RET_W), F32),
        ],
        compiler_params=_params("parallel"),
        name="inproj",
    )(x2, w_bf, b)


def _pool_kernel(cur_ref, halo_ref, w_ref, scale_ref, o_ref):
    i = pl.program_id(1)
    cur = cur_ref[0]
    halo = jnp.where(i > 0, halo_ref[0], 0.0)
    ext = jnp.concatenate([halo, cur], axis=0)
    ts = cur.shape[0]
    a2 = ext[1:] + ext[:-1]
    a4 = a2[2:] + a2[:-2]
    a8 = a4[4:] + a4[:-4]
    a16 = a8[8:] + a8[:-8]
    lane = lax.broadcasted_iota(jnp.int32, (ts, POOL_W), 1)
    grp = lane // POOL_CH
    win = jnp.where(grp == 0, a2[15:15 + ts],
                    jnp.where(grp == 1, a4[13:13 + ts],
                              jnp.where(grp == 2, a8[9:9 + ts], a16[1:1 + ts])))
    width = jnp.where(grp == 0, 2, jnp.where(grp == 1, 4, jnp.where(grp == 2, 8, 16)))
    t = i * ts + lax.broadcasted_iota(jnp.int32, (ts, POOL_W), 0)
    cnt = jnp.minimum(t + 1, width).astype(F32)
    pooled = win / cnt - cur
    mixed = _dot(pooled.astype(BF16), w_ref[...]) * scale_ref[...]
    o_ref[0] = mixed.astype(BF16)


def _pool(u, w_bd, scale, batch, seq):
    u3 = u.reshape(batch, seq, POOL_W)
    per = TS_POOL // POOL_HALO
    out = pl.pallas_call(
        _pool_kernel,
        grid=(batch, seq // TS_POOL),
        in_specs=[
            pl.BlockSpec((1, TS_POOL, POOL_W), lambda b, i: (b, i, 0)),
            pl.BlockSpec((1, POOL_HALO, POOL_W), lambda b, i: (b, jnp.maximum(i * per - 1, 0), 0)),
            pl.BlockSpec((POOL_W, POOL_W), lambda b, i: (0, 0)),
            pl.BlockSpec((1, POOL_W), lambda b, i: (0, 0)),
        ],
        out_specs=pl.BlockSpec((1, TS_POOL, POOL_W), lambda b, i: (b, i, 0)),
        out_shape=jax.ShapeDtypeStruct((batch, seq, POOL_W), BF16),
        compiler_params=_params("parallel", "parallel"),
        name="pool_mixer",
    )(u3, u3, w_bd, scale)
    return out.reshape(batch * seq, POOL_W)


def _sb_kernel(q_ref, k_ref, v_ref, tri_ref, o_ref, acc_ref, carry_ref):
    qi = pl.program_id(1)
    tq = q_ref.shape[1]
    groups = q_ref.shape[2] // LANES
    lane = lax.broadcasted_iota(jnp.int32, (tq, LANES), 1)
    qh = []
    for p in range(groups):
        q = q_ref[0, :, p * LANES:(p + 1) * LANES] * jnp.asarray(HEAD_DIM ** -0.5, BF16)
        zero = jnp.zeros_like(q)
        qh += [jnp.where(lane < HEAD_DIM, q, zero), jnp.where(lane >= HEAD_DIM, q, zero)]

    def log_keep(z):
        return jnp.minimum(-z, 0.0) - jnp.log(1.0 + jnp.exp(-jnp.abs(z)))

    def suffix_sums(lk):
        lk = lk.astype(BF16)
        out = []
        for b in range(lk.shape[1] // LANES):
            rs = _dot(lk[:, b * LANES:(b + 1) * LANES], tri_ref[...])
            out.append((rs[:, :LANES], rs[:, LANES:]))
        return out

    t0 = qi * tq
    halves = range(tq // SB_ROWS)
    heads = range(2 * groups)
    cols = [slice((h // 2) * LANES, (h // 2 + 1) * LANES) for h in heads]
    rows = [slice(u * SB_ROWS, (u + 1) * SB_ROWS) for u in halves]
    starts = [pl.multiple_of(jnp.maximum(t0 + (u + 1) * SB_ROWS - SB_SPAN, 0), SB_ROWS) for u in halves]
    below = []
    for u in halves:
        qpos = t0 + u * SB_ROWS + lax.broadcasted_iota(jnp.int32, (SB_ROWS, SB_SPAN), 0)
        kpos = starts[u] + lax.broadcasted_iota(jnp.int32, (SB_ROWS, SB_SPAN), 1)
        below.append(kpos < qpos)
    items = [(u, h) for u in halves for h in heads]
    zs = {(u, h): lax.dot_general(qh[h][rows[u]], k_ref[0, pl.ds(starts[u], SB_SPAN), cols[h]], _NT,
                                  preferred_element_type=F32) for u, h in items}
    sums = {(u, h): suffix_sums(jnp.where(below[u], log_keep(zs[u, h]), 0.0)) for u, h in items}
    weights, runs = {}, {}
    for u, h in items:
        run = None
        r = [None] * len(sums[u, h])
        for b in reversed(range(len(r))):
            r_in, tot = sums[u, h][b]
            r[b] = r_in if run is None else r_in + run
            run = tot if run is None else run + tot
        weights[u, h] = jnp.where(below[u], jnp.exp(zs[u, h] + jnp.concatenate(r, axis=1)), 0.0).astype(BF16)
        runs[u, h] = run
    worst = [None for _ in halves]
    for u, h in items:
        acc_ref[h, rows[u], :] = _dot(weights[u, h], v_ref[0, pl.ds(starts[u], SB_SPAN), cols[h]])
        carry_ref[h, rows[u], :] = runs[u, h]
        m = jnp.max(runs[u, h])
        worst[u] = m if worst[u] is None else jnp.maximum(worst[u], m)

    def tile(u, kb):
        start = pl.multiple_of(kb * LANES, LANES)
        older = (start + lax.broadcasted_iota(jnp.int32, (SB_ROWS, LANES), 1)) < starts[u]
        worst = None
        for h in heads:
            k = k_ref[0, pl.ds(start, LANES), cols[h]]
            v = v_ref[0, pl.ds(start, LANES), cols[h]]
            z = lax.dot_general(qh[h][rows[u]], k, _NT, preferred_element_type=F32)
            (r_in, tot), = suffix_sums(jnp.where(older, log_keep(z), 0.0))
            carry = carry_ref[h, rows[u], :]
            a = jnp.where(older, jnp.exp(z + r_in + carry), 0.0)
            acc_ref[h, rows[u], :] += _dot(a.astype(BF16), v)
            carry = carry + tot
            carry_ref[h, rows[u], :] = carry
            m = jnp.max(carry)
            worst = m if worst is None else jnp.maximum(worst, m)
        return worst

    def cond(state):
        kb, worst = state
        return jnp.logical_and(kb >= 0, worst > SB_SKIP)

    for u in halves:
        lax.while_loop(cond, lambda state, u=u: (state[0] - 1, tile(u, state[0])),
                       ((starts[u] + LANES - 1) // LANES - 1, worst[u]))
    for p in range(groups):
        o_ref[0, :, p * LANES:(p + 1) * LANES] = jnp.where(
            lane < HEAD_DIM, acc_ref[2 * p], acc_ref[2 * p + 1]).astype(BF16)


def _stick_breaking(sb, tri2, batch, seq):
    sb3 = sb.reshape(batch, seq, 3 * SB_W)
    heads = SB_W // HEAD_DIM
    once = pl.Buffered(1)
    out = pl.pallas_call(
        _sb_kernel,
        grid=(batch, seq // TQ_SB),
        in_specs=[
            pl.BlockSpec((1, TQ_SB, SB_W), lambda b, i: (b, i, 0)),
            pl.BlockSpec((1, seq, SB_W), lambda b, i: (b, 0, 1), pipeline_mode=once),
            pl.BlockSpec((1, seq, SB_W), lambda b, i: (b, 0, 2), pipeline_mode=once),
            pl.BlockSpec((LANES, 2 * LANES), lambda b, i: (0, 0)),
        ],
        out_specs=pl.BlockSpec((1, TQ_SB, SB_W), lambda b, i: (b, i, 0)),
        out_shape=jax.ShapeDtypeStruct((batch, seq, SB_W), BF16),
        scratch_shapes=[
            pltpu.VMEM((heads, TQ_SB, LANES), F32),
            pltpu.VMEM((heads, TQ_SB, TQ_SB), F32),
        ],
        compiler_params=_params("parallel", "arbitrary"),
        name="stick_breaking",
    )(sb3, sb3, sb3, tri2)
    return out.reshape(batch * seq, SB_W)


def _ret_kernel(q_ref, k_ref, v_ref, g_ref, cos_ref, sin_ref, dmat_ref, dec_ref, xi_ref, gch_ref,
                ng_ref, o_ref, state_ref, obuf_ref):
    si = pl.program_id(1)
    ts = q_ref.shape[1]
    groups = q_ref.shape[2] // LANES
    cols = [slice(p * LANES, (p + 1) * LANES) for p in range(groups)]

    @pl.when(si == 0)
    def _():
        state_ref[...] = jnp.zeros_like(state_ref)

    lane = lax.broadcasted_iota(jnp.int32, (ts, LANES), 1)
    first_half = (lane % HEAD_DIM) < (HEAD_DIM // 2)
    cos = cos_ref[...]
    sin = sin_ref[...]

    def rot(x):
        partner = jnp.where(first_half, pltpu.roll(x, LANES - HEAD_DIM // 2, 1),
                            pltpu.roll(x, HEAD_DIM // 2, 1))
        return x * cos + partner * sin

    qr = [rot(q_ref[0, :, c]) for c in cols]
    kr = [rot(k_ref[0, :, c]) * (HEAD_DIM ** -0.5) for c in cols]
    v = [v_ref[0, :, c] for c in cols]

    lane_c = lax.broadcasted_iota(jnp.int32, (CHUNK, LANES), 1)
    head0 = lane_c < HEAD_DIM
    r_i = lax.broadcasted_iota(jnp.int32, (LANES, LANES), 0) // HEAD_DIM
    c_i = lax.broadcasted_iota(jnp.int32, (LANES, LANES), 1) // HEAD_DIM
    same_head = r_i == c_i

    chunks = [slice(n * CHUNK, (n + 1) * CHUNK) for n in range(ts // CHUNK)]
    items = [(p, n) for p in range(groups) for n in range(len(chunks))]
    kcb = {(p, n): kr[p][chunks[n]].astype(BF16) for p, n in items}
    vcb = {(p, n): v[p][chunks[n]].astype(BF16) for p, n in items}
    kvs = {(p, n): lax.dot_general(kcb[p, n], (v[p][chunks[n]] * dec_ref[p]).astype(BF16), _TN,
                                   preferred_element_type=F32) for p, n in items}
    scores = {}
    for p, n in items:
        qc = qr[p][chunks[n]]
        qstack = jnp.concatenate([jnp.where(head0, qc, 0.0), jnp.where(head0, 0.0, qc)], axis=0)
        scores[p, n] = (lax.dot_general(qstack.astype(BF16), kcb[p, n], _NT, preferred_element_type=F32)
                        * dmat_ref[p]).astype(BF16)
    o2s = {(p, n): _dot(scores[p, n], vcb[p, n]) for p, n in items}
    states = {}
    for p in range(groups):
        state = state_ref[p]
        for n in range(len(chunks)):
            states[p, n] = state.astype(BF16)
            state = gch_ref[p] * state + jnp.where(same_head, kvs[p, n], 0.0)
        state_ref[p] = state
    for p, n in items:
        o_intra = jnp.where(head0, o2s[p, n][:CHUNK], o2s[p, n][CHUNK:])
        o_inter = _dot((qr[p][chunks[n]] * xi_ref[p]).astype(BF16), states[p, n])
        obuf_ref[chunks[n], cols[p]] = o_intra + o_inter

    lane_head0 = lane < HEAD_DIM
    inv = 1.0 / HEAD_DIM

    def head_mean(x):
        m0 = jnp.sum(jnp.where(lane_head0, x, 0.0), axis=-1, keepdims=True) * inv
        m1 = jnp.sum(jnp.where(lane_head0, 0.0, x), axis=-1, keepdims=True) * inv
        return jnp.where(lane_head0, m0, m1)

    for c in cols:
        o = obuf_ref[:, c]
        mu = head_mean(o)
        oc = o - mu
        var = head_mean(oc * oc)
        normed = oc * lax.rsqrt(var + LN_EPS) * ng_ref[:, c]
        gate = g_ref[0, :, c]
        o_ref[0, :, c] = (gate / (1.0 + jnp.exp(-gate)) * normed).astype(BF16)


def _retention(ret, consts, norm_g, batch, seq):
    cos, sin, dmat, dec, xi, gch = consts
    ret3 = ret.reshape(batch, seq, 4 * RET_W)
    pairs = RET_W // LANES
    fixed3 = lambda b, i: (0, 0, 0)
    out = pl.pallas_call(
        _ret_kernel,
        grid=(batch, seq // TS_RET),
        in_specs=[
            pl.BlockSpec((1, TS_RET, RET_W), lambda b, i: (b, i, 0)),
            pl.BlockSpec((1, TS_RET, RET_W), lambda b, i: (b, i, 1)),
            pl.BlockSpec((1, TS_RET, RET_W), lambda b, i: (b, i, 2)),
            pl.BlockSpec((1, TS_RET, RET_W), lambda b, i: (b, i, 3)),
            pl.BlockSpec((TS_RET, LANES), lambda b, i: (i, 0)),
            pl.BlockSpec((TS_RET, LANES), lambda b, i: (i, 0)),
            pl.BlockSpec((pairs, 2 * CHUNK, CHUNK), fixed3),
            pl.BlockSpec((pairs, CHUNK, LANES), fixed3),
            pl.BlockSpec((pairs, CHUNK, LANES), fixed3),
            pl.BlockSpec((pairs, 1, LANES), fixed3),
            pl.BlockSpec((1, RET_W), lambda b, i: (0, 0)),
        ],
        out_specs=pl.BlockSpec((1, TS_RET, RET_W), lambda b, i: (b, i, 0)),
        out_shape=jax.ShapeDtypeStruct((batch, seq, RET_W), BF16),
        scratch_shapes=[
            pltpu.VMEM((pairs, LANES, LANES), F32),
            pltpu.VMEM((TS_RET, RET_W), F32),
        ],
        compiler_params=_params("parallel", "arbitrary"),
        name="retention",
    )(ret3, ret3, ret3, ret3, cos, sin, dmat, dec, xi, gch, norm_g)
    return out.reshape(batch * seq, RET_W)


def _retention_consts(seq):
    half = HEAD_DIM // 2
    pos = jnp.arange(seq, dtype=F32)
    inv = ROPE_BASE ** (-jnp.arange(half, dtype=F32) / half)
    ang = pos[:, None] * inv[None, :]
    cos_h, sin_h = jnp.cos(ang), jnp.sin(ang)
    cos = jnp.tile(cos_h, (1, 2 * LANES // HEAD_DIM))
    sin = jnp.tile(jnp.concatenate([-sin_h, sin_h], axis=1), (1, LANES // HEAD_DIM))
    log_g = jnp.log(1.0 - 2.0 ** (-5.0 - jnp.arange(RET_HEADS, dtype=F32)))
    c = jnp.arange(CHUNK, dtype=F32)
    dmat = jnp.exp(jnp.abs(c[:, None] - c[None, :])[None] * log_g[:, None, None])
    dmat = dmat.reshape(RET_HEADS // 2, 2 * CHUNK, CHUNK)
    lane_log_g = jnp.repeat(log_g, HEAD_DIM).reshape(RET_HEADS // 2, 1, LANES)
    dec = jnp.exp((CHUNK - 1 - c)[None, :, None] * lane_log_g)
    xi = jnp.exp((c + 1.0)[None, :, None] * lane_log_g)
    gch = jnp.exp(CHUNK * lane_log_g)
    return cos, sin, dmat, dec, xi, gch


def _layer_norm(z, g, b):
    mu = jnp.mean(z, axis=-1, keepdims=True)
    zc = z - mu
    var = jnp.mean(zc * zc, axis=-1, keepdims=True)
    return zc * lax.rsqrt(var + LN_EPS) * g + b


def _outproj_kernel(x_ref, p_ref, s_ref, r_ref, w_ref, b_ref, g_ref, be_ref, rw_ref, rb_ref,
                    x1_ref, x1b_ref, gates_ref):
    o1 = POOL_W + SB_W
    y = (_dot(p_ref[...], w_ref[:POOL_W, :]) + _dot(s_ref[...], w_ref[POOL_W:o1, :])
         + _dot(r_ref[...], w_ref[o1:, :]))
    x1 = _layer_norm(DN_ALPHA * x_ref[...] + (y + b_ref[...]), g_ref[...], be_ref[...])
    x1_ref[...] = x1
    x_hi = x1.astype(BF16)
    x1b_ref[...] = x_hi

    x_mid = (x1 - x_hi.astype(F32)).astype(BF16)
    t = _dot(x_hi, rw_ref[...])
    logits = t[:, :LANES] + t[:, LANES:] + _dot(x_mid, rw_ref[:, :LANES]) + rb_ref[...]
    tm = logits.shape[0]
    lane = lax.broadcasted_iota(jnp.int32, (tm, LANES), 1).astype(F32)
    vals = jnp.where(lane < N_EXPERTS, logits, -jnp.inf)
    top_v, top_sel = [], []
    for _ in range(TOP_K):
        m = jnp.max(vals, axis=-1, keepdims=True)
        idx = jnp.min(jnp.where(vals == m, lane, float(LANES)), axis=-1, keepdims=True)
        sel = lane == idx
        vals = jnp.where(sel, -jnp.inf, vals)
        top_v.append(m)
        top_sel.append(sel)
    ex = [jnp.exp(m - top_v[0]) for m in top_v]
    den = ex[0] + ex[1] + ex[2] + ex[3]
    gates = jnp.full((tm, LANES), -1.0, F32)
    for sel, e in zip(top_sel, ex):
        gates = jnp.where(sel, e / den, gates)
    gates_ref[...] = gates


def _outproj(x2, pool_o, sb_o, ret_o, w_bf, b, g, be, rw, rb):
    n = x2.shape[0]
    row = lambda i: (i, 0)
    fixed = lambda i: (0, 0)
    return pl.pallas_call(
        _outproj_kernel,
        grid=(n // TM_ROW,),
        in_specs=[
            pl.BlockSpec((TM_ROW, D_MODEL), row),
            pl.BlockSpec((TM_ROW, POOL_W), row),
            pl.BlockSpec((TM_ROW, SB_W), row),
            pl.BlockSpec((TM_ROW, RET_W), row),
            pl.BlockSpec((D_MODEL, D_MODEL), fixed),
            pl.BlockSpec((1, D_MODEL), fixed),
            pl.BlockSpec((1, D_MODEL), fixed),
            pl.BlockSpec((1, D_MODEL), fixed),
            pl.BlockSpec((D_MODEL, 2 * LANES), fixed),
            pl.BlockSpec((1, LANES), fixed),
        ],
        out_specs=[
            pl.BlockSpec((TM_ROW, D_MODEL), row),
            pl.BlockSpec((TM_ROW, D_MODEL), row),
            pl.BlockSpec((TM_ROW, LANES), row),
        ],
        out_shape=[
            jax.ShapeDtypeStruct((n, D_MODEL), F32),
            jax.ShapeDtypeStruct((n, D_MODEL), BF16),
            jax.ShapeDtypeStruct((n, LANES), F32),
        ],
        compiler_params=_params("parallel"),
        name="outproj_ln_router",
    )(x2, pool_o, sb_o, ret_o, w_bf, b, g, be, rw, rb)


def _dispatch_kernel(gates_ref, xb_ref, tril_ref, xs_ref, gs_ref, pm_ref, post_ref, gatet_ref, cnt_ref,
                     oh_ref):
    tm = gates_ref.shape[0]
    gates = gates_ref[...]
    sel = gates >= 0.0
    self_ = jnp.where(sel, 1.0, 0.0)
    incl = _dot(tril_ref[...], self_.astype(BF16))
    pos = incl - self_
    cnt_ref[0] = incl[tm - 1:tm, :].astype(jnp.int32)
    pm_ref[...] = jnp.where(jnp.logical_and(sel, pos < RC_MOE), pos, -1.0).astype(BF16)
    post = jnp.where(sel, pos, -1.0).T[:N_EXPERTS, :]
    gatet = gates.T[:N_EXPERTS, :]
    post_ref[0] = post
    gatet_ref[0] = gatet
    post_i = post.astype(jnp.int32)
    slot = lax.broadcasted_iota(jnp.int32, (RC_MOE, tm), 0)
    for e in range(N_EXPERTS):
        onehot = slot == post_i[e:e + 1, :]
        oh_ref[e * RC_MOE:(e + 1) * RC_MOE, :] = jnp.where(onehot, 1.0, 0.0).astype(BF16)
        gate = jnp.sum(jnp.where(onehot, gatet[e:e + 1, :], 0.0), axis=1, keepdims=True)
        gs_ref[0, e] = jnp.broadcast_to(gate, (RC_MOE, LANES))
    xs = _dot(oh_ref[...], xb_ref[...]).astype(BF16)
    xs_ref[0] = xs.reshape(N_EXPERTS, RC_MOE, D_MODEL)


def _dispatch(gates, x1b, tril):
    n = gates.shape[0]
    nt = n // TM_MOE
    return pl.pallas_call(
        _dispatch_kernel,
        grid=(nt,),
        in_specs=[
            pl.BlockSpec((TM_MOE, LANES), lambda i: (i, 0)),
            pl.BlockSpec((TM_MOE, D_MODEL), lambda i: (i, 0)),
            pl.BlockSpec((TM_MOE, TM_MOE), lambda i: (0, 0)),
        ],
        out_specs=[
            pl.BlockSpec((1, N_EXPERTS, RC_MOE, D_MODEL), lambda i: (i, 0, 0, 0)),
            pl.BlockSpec((1, N_EXPERTS, RC_MOE, LANES), lambda i: (i, 0, 0, 0)),
            pl.BlockSpec((TM_MOE, LANES), lambda i: (i, 0)),
            pl.BlockSpec((1, N_EXPERTS, TM_MOE), lambda i: (i, 0, 0)),
            pl.BlockSpec((1, N_EXPERTS, TM_MOE), lambda i: (i, 0, 0)),
            pl.BlockSpec((1, 1, LANES), lambda i: (i, 0, 0)),
        ],
        out_shape=[
            jax.ShapeDtypeStruct((nt, N_EXPERTS, RC_MOE, D_MODEL), BF16),
            jax.ShapeDtypeStruct((nt, N_EXPERTS, RC_MOE, LANES), F32),
            jax.ShapeDtypeStruct((n, LANES), BF16),
            jax.ShapeDtypeStruct((nt, N_EXPERTS, TM_MOE), F32),
            jax.ShapeDtypeStruct((nt, N_EXPERTS, TM_MOE), F32),
            jax.ShapeDtypeStruct((nt, 1, LANES), jnp.int32),
        ],
        scratch_shapes=[pltpu.VMEM((SLOTS, TM_MOE), BF16)],
        compiler_params=_params("parallel"),
        name="dispatch",
    )(gates, x1b, tril)


def _swiglu_ffn(x, wgu, bgu, wd, bd):
    h = _dot(x, wgu) + bgu
    g = jnp.minimum(h[:, :D_FF], SWIGLU_LIMIT)
    up = jnp.clip(h[:, D_FF:], -SWIGLU_LIMIT, SWIGLU_LIMIT)
    act = (up + 1.0) * (g / (1.0 + jnp.exp(-SWIGLU_ALPHA * g)))
    return _dot(act.astype(BF16), wd) + bd


def _ffn_kernel(tail_ref, xs_ref, gs_ref, wgu_ref, bgu_ref, wd_ref, bd_ref, ys_ref, wgu_bf_ref, wd_bf_ref):
    e = pl.program_id(0)
    j = pl.program_id(1)

    @pl.when(j == 0)
    def _():
        wgu_bf_ref[0] = wgu_ref[0, 0].astype(BF16)
        wd_bf_ref[0] = wd_ref[0, 0].astype(BF16)

    def run(lo, hi):
        rows = FFN_TILES * (hi - lo)
        x = xs_ref[:, 0, lo:hi, :].reshape(rows, D_MODEL)
        y = _swiglu_ffn(x, wgu_bf_ref[0], bgu_ref[0], wd_bf_ref[0], bd_ref[0])
        gate = gs_ref[:, 0, lo:hi, :].reshape(rows, LANES)[:, :1]
        ys_ref[:, 0, lo:hi, :] = (y * gate).astype(BF16).reshape(FFN_TILES, hi - lo, D_MODEL)

    run(0, RC_MAIN)
    used = tail_ref[e * pl.num_programs(1) + j] > 0

    @pl.when(used)
    def _():
        run(RC_MAIN, RC_MOE)

    @pl.when(jnp.logical_not(used))
    def _():
        ys_ref[:, 0, RC_MAIN:RC_MOE, :] = jnp.zeros((FFN_TILES, RC_MOE - RC_MAIN, D_MODEL), BF16)


def _ffn(tail, xs, gs, w_gate_up, bgu, w_down, bd, layer):
    nt = xs.shape[0]
    grid_spec = pltpu.PrefetchScalarGridSpec(
        num_scalar_prefetch=1,
        grid=(N_EXPERTS, nt // FFN_TILES),
        in_specs=[
            pl.BlockSpec((FFN_TILES, 1, RC_MOE, D_MODEL), lambda e, j, t: (j, e, 0, 0)),
            pl.BlockSpec((FFN_TILES, 1, RC_MOE, LANES), lambda e, j, t: (j, e, 0, 0)),
            pl.BlockSpec((1, 1, D_MODEL, 2 * D_FF), lambda e, j, t: (layer, e, 0, 0)),
            pl.BlockSpec((1, 1, 2 * D_FF), lambda e, j, t: (e, 0, 0)),
            pl.BlockSpec((1, 1, D_FF, D_MODEL), lambda e, j, t: (layer, e, 0, 0)),
            pl.BlockSpec((1, 1, D_MODEL), lambda e, j, t: (e, 0, 0)),
        ],
        out_specs=[
            pl.BlockSpec((FFN_TILES, 1, RC_MOE, D_MODEL), lambda e, j, t: (j, e, 0, 0)),
            pl.BlockSpec((1, D_MODEL, 2 * D_FF), lambda e, j, t: (e, 0, 0)),
            pl.BlockSpec((1, D_FF, D_MODEL), lambda e, j, t: (e, 0, 0)),
        ],
    )
    return pl.pallas_call(
        _ffn_kernel,
        grid_spec=grid_spec,
        out_shape=[
            jax.ShapeDtypeStruct(xs.shape, BF16),
            jax.ShapeDtypeStruct((N_EXPERTS, D_MODEL, 2 * D_FF), BF16),
            jax.ShapeDtypeStruct((N_EXPERTS, D_FF, D_MODEL), BF16),
        ],
        compiler_params=_params("parallel", "arbitrary"),
        name="expert_ffn",
    )(tail, xs, gs, w_gate_up, bgu, w_down, bd)


def _combine_kernel(cnt_ref, ys_ref, pm_ref, pick_ref, rcol_ref, x1_ref, g_ref, b_ref, xb_ref, post_ref,
                    gatet_ref, bgu_ref, bd_ref, wgu_hbm, wd_hbm, o_ref, acc_ref, wgu_buf, wd_buf, sem):
    i = pl.program_id(0)
    tm = pm_ref.shape[0]
    ranks = _dot(pm_ref[...], pick_ref[...])
    back = jnp.where(ranks == rcol_ref[...], 1.0, 0.0).astype(BF16)
    acc_ref[...] = _dot(back, ys_ref[0].reshape(SLOTS, D_MODEL))

    slot = lax.broadcasted_iota(jnp.int32, (RC_MOE, tm), 0)

    def per_expert(e, carry):
        cnt = cnt_ref[i * N_EXPERTS + e]

        @pl.when(cnt > RC_MOE)
        def _():
            copies = (pltpu.make_async_copy(wgu_hbm.at[e], wgu_buf, sem.at[0]),
                      pltpu.make_async_copy(wd_hbm.at[e], wd_buf, sem.at[1]))
            for cp in copies:
                cp.start()
            for cp in copies:
                cp.wait()
            prow = post_ref[0, pl.ds(e, 1), :].astype(jnp.int32)
            grow = gatet_ref[0, pl.ds(e, 1), :]

            def chunk(c, carry2):
                onehot = (slot + c * RC_MOE) == prow
                oh = jnp.where(onehot, 1.0, 0.0).astype(BF16)
                xg = _dot(oh, xb_ref[...]).astype(BF16)
                y = _swiglu_ffn(xg, wgu_buf[...], bgu_ref[e], wd_buf[...], bd_ref[e])
                gate = jnp.sum(jnp.where(onehot, grow, 0.0), axis=1, keepdims=True)
                yg = (y * gate).astype(BF16)
                acc_ref[...] += lax.dot_general(oh, yg, _TN, preferred_element_type=F32)
                return carry2

            lax.fori_loop(1, (cnt + RC_MOE - 1) // RC_MOE, chunk, 0)

        return carry

    lax.fori_loop(0, N_EXPERTS, per_expert, 0)
    o_ref[...] = _layer_norm(DN_ALPHA * x1_ref[...] + acc_ref[...], g_ref[...], b_ref[...])


def _combine(cnt, ys, pm, pick, rcol, x1, g, b, x1b, post, gatet, bgu, bd, wgu, wd):
    n = x1.shape[0]
    nt = n // TM_MOE
    row = lambda i, c: (i, 0)
    fixed2 = lambda i, c: (0, 0)
    fixed3 = lambda i, c: (0, 0, 0)
    grid_spec = pltpu.PrefetchScalarGridSpec(
        num_scalar_prefetch=1,
        grid=(nt,),
        in_specs=[
            pl.BlockSpec((1, N_EXPERTS, RC_MOE, D_MODEL), lambda i, c: (i, 0, 0, 0)),
            pl.BlockSpec((TM_MOE, LANES), row),
            pl.BlockSpec((LANES, SLOTS), fixed2),
            pl.BlockSpec((1, SLOTS), fixed2),
            pl.BlockSpec((TM_MOE, D_MODEL), row),
            pl.BlockSpec((1, D_MODEL), fixed2),
            pl.BlockSpec((1, D_MODEL), fixed2),
            pl.BlockSpec((TM_MOE, D_MODEL), row),
            pl.BlockSpec((1, N_EXPERTS, TM_MOE), lambda i, c: (i, 0, 0)),
            pl.BlockSpec((1, N_EXPERTS, TM_MOE), lambda i, c: (i, 0, 0)),
            pl.BlockSpec((N_EXPERTS, 1, 2 * D_FF), fixed3),
            pl.BlockSpec((N_EXPERTS, 1, D_MODEL), fixed3),
            pl.BlockSpec(memory_space=pl.ANY),
            pl.BlockSpec(memory_space=pl.ANY),
        ],
        out_specs=pl.BlockSpec((TM_MOE, D_MODEL), row),
        scratch_shapes=[
            pltpu.VMEM((TM_MOE, D_MODEL), F32),
            pltpu.VMEM((D_MODEL, 2 * D_FF), BF16),
            pltpu.VMEM((D_FF, D_MODEL), BF16),
            pltpu.SemaphoreType.DMA((2,)),
        ],
    )
    return pl.pallas_call(
        _combine_kernel,
        grid_spec=grid_spec,
        out_shape=jax.ShapeDtypeStruct((n, D_MODEL), F32),
        compiler_params=_params("arbitrary"),
        name="combine_ln",
    )(cnt, ys, pm, pick, rcol, x1, g, b, x1b, post, gatet, bgu, bd, wgu, wd)


def _block_diag(pool_w):
    groups = pool_w.shape[0]
    out = jnp.zeros((POOL_W, POOL_W), pool_w.dtype)
    for gi in range(groups):
        out = out.at[gi * POOL_CH:(gi + 1) * POOL_CH, gi * POOL_CH:(gi + 1) * POOL_CH].set(pool_w[gi])
    return out


def _layer(x2, batch, seq, consts, layer, w_in, b_in, pool_w, pool_scale, ret_norm_g, w_out, b_out, ln1_g,
           ln1_b, router_w, router_b, w_gate_up, b_gate_up, w_down, b_down, ln2_g, ln2_b):
    ret_consts, tri2, tril, pick, rcol = consts
    row = lambda a: a.reshape(1, -1).astype(F32)
    u_pool, sb, ret = _inproj(x2, w_in.astype(BF16), row(b_in))
    pool_o = _pool(u_pool, _block_diag(pool_w).astype(BF16), row(pool_scale), batch, seq)
    sb_o = _stick_breaking(sb, tri2, batch, seq)
    ret_o = _retention(ret, ret_consts, row(ret_norm_g), batch, seq)
    rw = jnp.pad(router_w.astype(F32), ((0, 0), (0, LANES - N_EXPERTS)))
    rw_hi = rw.astype(BF16)
    rw = jnp.concatenate([rw_hi, (rw - rw_hi.astype(F32)).astype(BF16)], axis=1)
    rb = jnp.pad(router_b.astype(F32), (0, LANES - N_EXPERTS)).reshape(1, LANES)
    x1, x1b, gates = _outproj(x2, pool_o, sb_o, ret_o, w_out.astype(BF16), row(b_out), row(ln1_g),
                              row(ln1_b), rw, rb)
    xs, gs, pm, post, gatet, cnt = _dispatch(gates, x1b, tril)
    bgu = b_gate_up.reshape(N_EXPERTS, 1, 2 * D_FF).astype(F32)
    bd = b_down.reshape(N_EXPERTS, 1, D_MODEL).astype(F32)
    cnt2 = cnt[:, 0, :N_EXPERTS]
    tail = jnp.any((cnt2 > RC_MAIN).reshape(-1, FFN_TILES, N_EXPERTS), axis=1)
    ys, wgu, wd = _ffn(tail.T.reshape(-1).astype(jnp.int32), xs, gs, w_gate_up, bgu, w_down, bd, layer)
    cnt_flat = cnt2.reshape(-1)
    return _combine(cnt_flat, ys, pm, pick, rcol, x1, row(ln2_g), row(ln2_b), x1b, post, gatet, bgu, bd,
                    wgu, wd)


def kernel(x, w_in, b_in, pool_w, pool_scale, ret_norm_g, w_out, b_out, ln1_g, ln1_b, router_w, router_b,
           w_gate_up, b_gate_up, w_down, b_down, ln2_g, ln2_b):
    batch, seq, d = x.shape
    n = batch * seq
    assert d == D_MODEL and seq % TS_RET == 0 and seq % TQ_SB == 0 and seq % TS_POOL == 0
    assert n % (TM_MOE * FFN_TILES) == 0 and n % TM_ROW == 0 and seq >= SB_SPAN
    j = np.arange(LANES)
    tri = (j[:, None] >= j[None, :]).astype(np.float32)
    tri2 = jnp.asarray(np.concatenate([tri, np.ones_like(tri)], axis=1), BF16)
    r = np.arange(TM_MOE)
    tril = jnp.asarray((r[:, None] >= r[None, :]).astype(np.float32), BF16)
    s = np.arange(SLOTS)
    pick = jnp.asarray((np.arange(LANES)[:, None] == (s // RC_MOE)[None, :]).astype(np.float32), BF16)
    rcol = jnp.asarray((s % RC_MOE).astype(np.float32).reshape(1, SLOTS))
    consts = (_retention_consts(seq), tri2, tril, pick, rcol)
    x2 = x.reshape(n, d)
    for l in range(DEPTH):
        x2 = _layer(x2, batch, seq, consts, l, w_in[l], b_in[l], pool_w[l], pool_scale[l], ret_norm_g[l],
                    w_out[l], b_out[l], ln1_g[l], ln1_b[l], router_w[l], router_b[l], w_gate_up,
                    b_gate_up[l], w_down, b_down[l], ln2_g[l], ln2_b[l])
    return x2.reshape(batch, seq, d)
```

```python
import numpy as np
import jax
import jax.numpy as jnp
from jax import lax
from jax.experimental import pallas as pl
from jax.experimental.pallas import tpu as pltpu

D_MODEL = 1024
DEPTH = 2
CHUNK = 64
HEAD_DIM = 64
POOL_CH = 64
POOL_W = 256
POOL_HALO = 16
SB_W = 384
RET_W = 384
RET_HEADS = 6
IN_W = POOL_W + 3 * SB_W + 4 * RET_W
ROPE_BASE = 10000.0
N_EXPERTS = 32
TOP_K = 4
D_FF = D_MODEL
SWIGLU_LIMIT = 7.0
SWIGLU_ALPHA = 1.702
DN_ALPHA = (2.0 * DEPTH) ** 0.25
LN_EPS = 1e-5

LANES = 128
VMEM_LIMIT = 56 * 1024 * 1024

TM_ROW = 512
OUTPROJ_SUB = 256
TS_POOL = 512
TQ_SB = 256
SB_ROWS = 64
SB_SPAN = 256
TS_RET = 512
TM_MOE = 512
RC_MOE = 96
RC_MAIN = 80
FFN_TILES = 8
SLOTS = N_EXPERTS * RC_MOE
DISPATCH_GROUP = 8
SB_SKIP = -100.0

BF16 = jnp.bfloat16
F32 = jnp.float32

_NT = (((1,), (1,)), ((), ()))
_TN = (((0,), (0,)), ((), ()))


def _dot(a, b):
    return jnp.dot(a, b, preferred_element_type=F32)


def _params(*sem):
    return pltpu.CompilerParams(dimension_semantics=sem, vmem_limit_bytes=VMEM_LIMIT)


def _inproj_kernel(x_ref, w_ref, b_ref, pool_ref, sb_ref, ret_ref):
    xb = x_ref[...].astype(BF16)
    o1 = POOL_W + 3 * SB_W
    pool_ref[...] = _dot(xb, w_ref[:, :POOL_W]) + b_ref[:, :POOL_W]
    sb_ref[...] = (_dot(xb, w_ref[:, POOL_W:o1]) + b_ref[:, POOL_W:o1]).astype(BF16)
    ret_ref[...] = _dot(xb, w_ref[:, o1:]) + b_ref[:, o1:]


def _inproj(x2, w_bf, b):
    n = x2.shape[0]
    return pl.pallas_call(
        _inproj_kernel,
        grid=(n // TM_ROW,),
        in_specs=[
            pl.BlockSpec((TM_ROW, D_MODEL), lambda i: (i, 0)),
            pl.BlockSpec((D_MODEL, IN_W), lambda i: (0, 0)),
            pl.BlockSpec((1, IN_W), lambda i: (0, 0)),
        ],
        out_specs=[
            pl.BlockSpec((TM_ROW, POOL_W), lambda i: (i, 0)),
            pl.BlockSpec((TM_ROW, 3 * SB_W), lambda i: (i, 0)),
            pl.BlockSpec((TM_ROW, 4 * RET_W), lambda i: (i, 0)),
        ],
        out_shape=[
            jax.ShapeDtypeStruct((n, POOL_W), F32),
            jax.ShapeDtypeStruct((n, 3 * SB_W), BF16),
            jax.ShapeDtypeStruct((n, 4 * RET_W), F32),
        ],
        compiler_params=_params("parallel"),
        name="inproj",
    )(x2, w_bf, b)


def _pool_kernel(cur_ref, halo_ref, w_ref, scale_ref, o_ref):
    i = pl.program_id(1)
    cur = cur_ref[0]
    halo = jnp.where(i > 0, halo_ref[0], 0.0)
    ext = jnp.concatenate([halo, cur], axis=0)
    ts = cur.shape[0]
    a2 = ext[1:] + ext[:-1]
    a4 = a2[2:] + a2[:-2]
    a8 = a4[4:] + a4[:-4]
    a16 = a8[8:] + a8[:-8]
    lane = lax.broadcasted_iota(jnp.int32, (ts, POOL_W), 1)
    grp = lane // POOL_CH
    win = jnp.where(grp == 0, a2[15:15 + ts],
                    jnp.where(grp == 1, a4[13:13 + ts],
                              jnp.where(grp == 2, a8[9:9 + ts], a16[1:1 + ts])))
    width = jnp.where(grp == 0, 2, jnp.where(grp == 1, 4, jnp.where(grp == 2, 8, 16)))
    t = i * ts + lax.broadcasted_iota(jnp.int32, (ts, POOL_W), 0)
    cnt = jnp.minimum(t + 1, width).astype(F32)
    pooled = win / cnt - cur
    mixed = _dot(pooled.astype(BF16), w_ref[...]) * scale_ref[...]
    o_ref[0] = mixed.astype(BF16)


def _pool(u, w_bd, scale, batch, seq):
    u3 = u.reshape(batch, seq, POOL_W)
    per = TS_POOL // POOL_HALO
    out = pl.pallas_call(
        _pool_kernel,
        grid=(batch, seq // TS_POOL),
        in_specs=[
            pl.BlockSpec((1, TS_POOL, POOL_W), lambda b, i: (b, i, 0)),
            pl.BlockSpec((1, POOL_HALO, POOL_W), lambda b, i: (b, jnp.maximum(i * per - 1, 0), 0)),
            pl.BlockSpec((POOL_W, POOL_W), lambda b, i: (0, 0)),
            pl.BlockSpec((1, POOL_W), lambda b, i: (0, 0)),
        ],
        out_specs=pl.BlockSpec((1, TS_POOL, POOL_W), lambda b, i: (b, i, 0)),
        out_shape=jax.ShapeDtypeStruct((batch, seq, POOL_W), BF16),
        compiler_params=_params("parallel", "parallel"),
        name="pool_mixer",
    )(u3, u3, w_bd, scale)
    return out.reshape(batch * seq, POOL_W)


def _sb_kernel(q_ref, k_ref, v_ref, tri_ref, o_ref, acc_ref, carry_ref):
    qi = pl.program_id(1)
    tq = q_ref.shape[1]
    groups = q_ref.shape[2] // LANES
    lane = lax.broadcasted_iota(jnp.int32, (tq, LANES), 1)
    qh = []
    for p in range(groups):
        q = q_ref[0, :, p * LANES:(p + 1) * LANES] * jnp.asarray(HEAD_DIM ** -0.5, BF16)
        zero = jnp.zeros_like(q)
        qh += [jnp.where(lane < HEAD_DIM, q, zero), jnp.where(lane >= HEAD_DIM, q, zero)]

    def log_keep(z):
        return jnp.minimum(-z, 0.0) - jnp.log(1.0 + jnp.exp(-jnp.abs(z)))

    def suffix_sums(lk):
        lk = lk.astype(BF16)
        out = []
        for b in range(lk.shape[1] // LANES):
            rs = _dot(lk[:, b * LANES:(b + 1) * LANES], tri_ref[...])
            out.append((rs[:, :LANES], rs[:, LANES:]))
        return out

    t0 = qi * tq
    halves = range(tq // SB_ROWS)
    heads = range(2 * groups)
    cols = [slice((h // 2) * LANES, (h // 2 + 1) * LANES) for h in heads]
    rows = [slice(u * SB_ROWS, (u + 1) * SB_ROWS) for u in halves]
    starts = [pl.multiple_of(jnp.maximum(t0 + (u + 1) * SB_ROWS - SB_SPAN, 0), SB_ROWS) for u in halves]
    below = []
    for u in halves:
        qpos = t0 + u * SB_ROWS + lax.broadcasted_iota(jnp.int32, (SB_ROWS, SB_SPAN), 0)
        kpos = starts[u] + lax.broadcasted_iota(jnp.int32, (SB_ROWS, SB_SPAN), 1)
        below.append(kpos < qpos)
    items = [(u, h) for u in halves for h in heads]
    zs = {(u, h): lax.dot_general(qh[h][rows[u]], k_ref[0, pl.ds(starts[u], SB_SPAN), cols[h]], _NT,
                                  preferred_element_type=F32) for u, h in items}
    sums = {(u, h): suffix_sums(jnp.where(below[u], log_keep(zs[u, h]), 0.0)) for u, h in items}
    weights, runs = {}, {}
    for u, h in items:
        run = None
        r = [None] * len(sums[u, h])
        for b in reversed(range(len(r))):
            r_in, tot = sums[u, h][b]
            r[b] = r_in if run is None else r_in + run
            run = tot if run is None else run + tot
        weights[u, h] = jnp.where(below[u], jnp.exp(zs[u, h] + jnp.concatenate(r, axis=1)), 0.0).astype(BF16)
        runs[u, h] = run
    for u, h in items:
        acc_ref[h, rows[u], :] = _dot(weights[u, h], v_ref[0, pl.ds(starts[u], SB_SPAN), cols[h]])
        carry_ref[h, rows[u], :] = runs[u, h]
    worst = []
    for u in halves:
        top = runs[u, 0]
        for h in heads[1:]:
            top = jnp.maximum(top, runs[u, h])
        worst.append(jnp.max(top))

    def tile(u, kb):
        start = pl.multiple_of(kb * LANES, LANES)
        older = (start + lax.broadcasted_iota(jnp.int32, (SB_ROWS, LANES), 1)) < starts[u]
        worst = None
        for h in heads:
            k = k_ref[0, pl.ds(start, LANES), cols[h]]
            v = v_ref[0, pl.ds(start, LANES), cols[h]]
            z = lax.dot_general(qh[h][rows[u]], k, _NT, preferred_element_type=F32)
            (r_in, tot), = suffix_sums(jnp.where(older, log_keep(z), 0.0))
            carry = carry_ref[h, rows[u], :]
            a = jnp.where(older, jnp.exp(z + r_in + carry), 0.0)
            acc_ref[h, rows[u], :] += _dot(a.astype(BF16), v)
            carry = carry + tot
            carry_ref[h, rows[u], :] = carry
            m = jnp.max(carry)
            worst = m if worst is None else jnp.maximum(worst, m)
        return worst

    def cond(state):
        kb, worst = state
        return jnp.logical_and(kb >= 0, worst > SB_SKIP)

    for u in halves:
        lax.while_loop(cond, lambda state, u=u: (state[0] - 1, tile(u, state[0])),
                       ((starts[u] + LANES - 1) // LANES - 1, worst[u]))
    for p in range(groups):
        o_ref[0, :, p * LANES:(p + 1) * LANES] = jnp.where(
            lane < HEAD_DIM, acc_ref[2 * p], acc_ref[2 * p + 1]).astype(BF16)


def _stick_breaking(sb, tri2, batch, seq):
    sb3 = sb.reshape(batch, seq, 3 * SB_W)
    heads = SB_W // HEAD_DIM
    once = pl.Buffered(1)
    out = pl.pallas_call(
        _sb_kernel,
        grid=(batch, seq // TQ_SB),
        in_specs=[
            pl.BlockSpec((1, TQ_SB, SB_W), lambda b, i: (b, i, 0)),
            pl.BlockSpec((1, seq, SB_W), lambda b, i: (b, 0, 1), pipeline_mode=once),
            pl.BlockSpec((1, seq, SB_W), lambda b, i: (b, 0, 2), pipeline_mode=once),
            pl.BlockSpec((LANES, 2 * LANES), lambda b, i: (0, 0)),
        ],
        out_specs=pl.BlockSpec((1, TQ_SB, SB_W), lambda b, i: (b, i, 0)),
        out_shape=jax.ShapeDtypeStruct((batch, seq, SB_W), BF16),
        scratch_shapes=[
            pltpu.VMEM((heads, TQ_SB, LANES), F32),
            pltpu.VMEM((heads, TQ_SB, LANES), F32),
        ],
        compiler_params=_params("parallel", "arbitrary"),
        name="stick_breaking",
    )(sb3, sb3, sb3, tri2)
    return out.reshape(batch * seq, SB_W)


def _ret_kernel(q_ref, k_ref, v_ref, g_ref, cos_ref, sin_ref, dmat_ref, dec_ref, xi_ref, gch_ref,
                swap_ref, avg_ref, ng_ref, o_ref, state_ref, obuf_ref):
    si = pl.program_id(1)
    ts = q_ref.shape[1]
    groups = q_ref.shape[2] // LANES
    cols = [slice(p * LANES, (p + 1) * LANES) for p in range(groups)]

    @pl.when(si == 0)
    def _():
        state_ref[...] = jnp.zeros_like(state_ref)

    cos = cos_ref[...]
    sin = sin_ref[...]

    def lane_mix(x, m_ref):
        hi = x.astype(BF16)
        lo = (x - hi.astype(F32)).astype(BF16)
        return _dot(jnp.concatenate([hi, lo], axis=1), m_ref[...])

    def rot(x):
        return x * cos + lane_mix(x, swap_ref) * sin

    qr = [rot(q_ref[0, :, c]) for c in cols]
    kr = [rot(k_ref[0, :, c]) * (HEAD_DIM ** -0.5) for c in cols]
    v = [v_ref[0, :, c] for c in cols]

    lane_c = lax.broadcasted_iota(jnp.int32, (CHUNK, LANES), 1)
    head0 = lane_c < HEAD_DIM
    r_i = lax.broadcasted_iota(jnp.int32, (LANES, LANES), 0) // HEAD_DIM
    c_i = lax.broadcasted_iota(jnp.int32, (LANES, LANES), 1) // HEAD_DIM
    same_head = r_i == c_i

    chunks = [slice(n * CHUNK, (n + 1) * CHUNK) for n in range(ts // CHUNK)]
    items = [(p, n) for p in range(groups) for n in range(len(chunks))]
    kcb = {(p, n): kr[p][chunks[n]].astype(BF16) for p, n in items}
    vcb = {(p, n): v[p][chunks[n]].astype(BF16) for p, n in items}
    kvs = {(p, n): lax.dot_general(kcb[p, n], (v[p][chunks[n]] * dec_ref[p]).astype(BF16), _TN,
                                   preferred_element_type=F32) for p, n in items}
    scores = {}
    for p, n in items:
        qc = qr[p][chunks[n]]
        qstack = jnp.concatenate([jnp.where(head0, qc, 0.0), jnp.where(head0, 0.0, qc)], axis=0)
        scores[p, n] = (lax.dot_general(qstack.astype(BF16), kcb[p, n], _NT, preferred_element_type=F32)
                        * dmat_ref[p]).astype(BF16)
    o2s = {(p, n): _dot(scores[p, n], vcb[p, n]) for p, n in items}
    states = {}
    for p in range(groups):
        state = state_ref[p]
        for n in range(len(chunks)):
            states[p, n] = state.astype(BF16)
            state = gch_ref[p] * state + jnp.where(same_head, kvs[p, n], 0.0)
        state_ref[p] = state
    for p, n in items:
        o_intra = jnp.where(head0, o2s[p, n][:CHUNK], o2s[p, n][CHUNK:])
        o_inter = _dot((qr[p][chunks[n]] * xi_ref[p]).astype(BF16), states[p, n])
        obuf_ref[chunks[n], cols[p]] = o_intra + o_inter

    for c in cols:
        o = obuf_ref[:, c]
        mu = lane_mix(o, avg_ref)
        oc = o - mu
        var = lane_mix(oc * oc, avg_ref)
        normed = oc * lax.rsqrt(var + LN_EPS) * ng_ref[:, c]
        gate = g_ref[0, :, c]
        o_ref[0, :, c] = (gate / (1.0 + jnp.exp(-gate)) * normed).astype(BF16)


def _retention(ret, consts, norm_g, batch, seq):
    cos, sin, dmat, dec, xi, gch, swap, avg = consts
    ret3 =ret.reshape(batch, seq, 4 * RET_W)
    pairs = RET_W // LANES
    fixed3 = lambda b, i: (0, 0, 0)
    out = pl.pallas_call(
        _ret_kernel,
        grid=(batch, seq // TS_RET),
        in_specs=[
            pl.BlockSpec((1, TS_RET, RET_W), lambda b, i: (b, i, 0)),
            pl.BlockSpec((1, TS_RET, RET_W), lambda b, i: (b, i, 1)),
            pl.BlockSpec((1, TS_RET, RET_W), lambda b, i: (b, i, 2)),
            pl.BlockSpec((1, TS_RET, RET_W), lambda b, i: (b, i, 3)),
            pl.BlockSpec((TS_RET, LANES), lambda b, i: (i, 0)),
            pl.BlockSpec((TS_RET, LANES), lambda b, i: (i, 0)),
            pl.BlockSpec((pairs, 2 * CHUNK, CHUNK), fixed3),
            pl.BlockSpec((pairs, CHUNK, LANES), fixed3),
            pl.BlockSpec((pairs, CHUNK, LANES), fixed3),
            pl.BlockSpec((pairs, 1, LANES), fixed3),
            pl.BlockSpec((2 * LANES, LANES), lambda b, i: (0, 0)),
            pl.BlockSpec((2 * LANES, LANES), lambda b, i: (0, 0)),
            pl.BlockSpec((1, RET_W), lambda b, i: (0, 0)),
        ],
        out_specs=pl.BlockSpec((1, TS_RET, RET_W), lambda b, i: (b, i, 0)),
        out_shape=jax.ShapeDtypeStruct((batch, seq, RET_W), BF16),
        scratch_shapes=[
            pltpu.VMEM((pairs, LANES, LANES), F32),
            pltpu.VMEM((TS_RET, RET_W), F32),
        ],
        compiler_params=_params("parallel", "arbitrary"),
        name="retention",
    )(ret3, ret3, ret3, ret3, cos, sin, dmat, dec, xi, gch, swap, avg, norm_g)
    return out.reshape(batch * seq, RET_W)


def _retention_consts(seq):
    half = HEAD_DIM // 2
    pos = jnp.arange(seq, dtype=F32)
    inv = ROPE_BASE ** (-jnp.arange(half, dtype=F32) / half)
    ang = pos[:, None] * inv[None, :]
    cos_h, sin_h = jnp.cos(ang), jnp.sin(ang)
    cos = jnp.tile(cos_h, (1, 2 * LANES // HEAD_DIM))
    sin = jnp.tile(jnp.concatenate([-sin_h, sin_h], axis=1), (1, LANES // HEAD_DIM))
    log_g = jnp.log(1.0 - 2.0 ** (-5.0 - jnp.arange(RET_HEADS, dtype=F32)))
    c = jnp.arange(CHUNK, dtype=F32)
    dmat = jnp.exp(jnp.abs(c[:, None] - c[None, :])[None] * log_g[:, None, None])
    dmat = dmat.reshape(RET_HEADS // 2, 2 * CHUNK, CHUNK)
    lane_log_g = jnp.repeat(log_g, HEAD_DIM).reshape(RET_HEADS // 2, 1, LANES)
    dec = jnp.exp((CHUNK - 1 - c)[None, :, None] * lane_log_g)
    xi = jnp.exp((c + 1.0)[None, :, None] * lane_log_g)
    gch = jnp.exp(CHUNK * lane_log_g)
    l = np.arange(LANES)
    partner = np.where(l % HEAD_DIM < half, l + half, l - half)
    swap = (l[:, None] == partner[None, :]).astype(np.float32)
    avg = (l[:, None] // HEAD_DIM == l[None, :] // HEAD_DIM).astype(np.float32) / HEAD_DIM
    swap = jnp.asarray(np.concatenate([swap, swap], axis=0), BF16)
    avg = jnp.asarray(np.concatenate([avg, avg], axis=0), BF16)
    return cos, sin, dmat, dec, xi, gch, swap, avg


def _layer_norm(z, g, b):
    mu = jnp.mean(z, axis=-1, keepdims=True)
    zc = z - mu
    var = jnp.mean(zc * zc, axis=-1, keepdims=True)
    return zc * lax.rsqrt(var + LN_EPS) * g + b


def _outproj_kernel(x_ref, p_ref, s_ref, r_ref, w_ref, b_ref, g_ref, be_ref, rw_ref, rb_ref,
                    x1_ref, x1b_ref, gates_ref):
    o1 = POOL_W + SB_W
    tm = x_ref.shape[0]
    subs = [slice(s, s + OUTPROJ_SUB) for s in range(0, tm, OUTPROJ_SUB)]
    ys = [_dot(p_ref[sl, :], w_ref[:POOL_W, :]) + _dot(s_ref[sl, :], w_ref[POOL_W:o1, :])
          + _dot(r_ref[sl, :], w_ref[o1:, :]) for sl in subs]
    x1s = [_layer_norm(DN_ALPHA * x_ref[sl, :] + (y + b_ref[...]), g_ref[...], be_ref[...])
           for sl, y in zip(subs, ys)]
    lane = lax.broadcasted_iota(jnp.int32, (OUTPROJ_SUB, LANES), 1).astype(F32)
    logits = []
    for sl, x1 in zip(subs, x1s):
        x1_ref[sl, :] = x1
        x_hi = x1.astype(BF16)
        x1b_ref[sl, :] = x_hi
        x_mid = (x1 - x_hi.astype(F32)).astype(BF16)
        t = _dot(x_hi, rw_ref[...])
        logits.append(t[:, :LANES] + t[:, LANES:] + _dot(x_mid, rw_ref[:, :LANES]) + rb_ref[...])
    for sl, lg in zip(subs, logits):
        vals = jnp.where(lane < N_EXPERTS, lg, -jnp.inf)
        top_v, top_sel = [], []
        for _ in range(TOP_K):
            m = jnp.max(vals, axis=-1, keepdims=True)
            idx = jnp.min(jnp.where(vals == m, lane, float(LANES)), axis=-1, keepdims=True)
            sel = lane == idx
            vals = jnp.where(sel, -jnp.inf, vals)
            top_v.append(m)
            top_sel.append(sel)
        ex = [jnp.exp(m - top_v[0]) for m in top_v]
        den = ex[0] + ex[1] + ex[2] + ex[3]
        gates = jnp.full((OUTPROJ_SUB, LANES), -1.0, F32)
        for sel, e in zip(top_sel, ex):
            gates = jnp.where(sel, e / den, gates)
        gates_ref[sl, :] = gates


def _outproj(x2, pool_o, sb_o, ret_o, w_bf, b, g, be, rw, rb):
    n = x2.shape[0]
    row = lambda i: (i, 0)
    fixed = lambda i: (0, 0)
    return pl.pallas_call(
        _outproj_kernel,
        grid=(n // TM_ROW,),
        in_specs=[
            pl.BlockSpec((TM_ROW, D_MODEL), row),
            pl.BlockSpec((TM_ROW, POOL_W), row),
            pl.BlockSpec((TM_ROW, SB_W), row),
            pl.BlockSpec((TM_ROW, RET_W), row),
            pl.BlockSpec((D_MODEL, D_MODEL), fixed),
            pl.BlockSpec((1, D_MODEL), fixed),
            pl.BlockSpec((1, D_MODEL), fixed),
            pl.BlockSpec((1, D_MODEL), fixed),
            pl.BlockSpec((D_MODEL, 2 * LANES), fixed),
            pl.BlockSpec((1, LANES), fixed),
        ],
        out_specs=[
            pl.BlockSpec((TM_ROW, D_MODEL), row),
            pl.BlockSpec((TM_ROW, D_MODEL), row),
            pl.BlockSpec((TM_ROW, LANES), row),
        ],
        out_shape=[
            jax.ShapeDtypeStruct((n, D_MODEL), F32),
            jax.ShapeDtypeStruct((n, D_MODEL), BF16),
            jax.ShapeDtypeStruct((n, LANES), F32),
        ],
        compiler_params=_params("parallel"),
        name="outproj_ln_router",
    )(x2, pool_o, sb_o, ret_o, w_bf, b, g, be, rw, rb)


def _dispatch_kernel(gates_ref, xb_ref, tril_ref, xs_ref, gs_ref, pm_ref, post_ref, gatet_ref, cnt_ref):
    tm = gates_ref.shape[0]
    gates = gates_ref[...]
    sel = gates >= 0.0
    self_ = jnp.where(sel, 1.0, 0.0)
    incl = _dot(tril_ref[...], self_.astype(BF16))
    pos = incl - self_
    cnt_ref[0] = incl[tm - 1:tm, :].astype(jnp.int32)
    pm_ref[...] = jnp.where(jnp.logical_and(sel, pos < RC_MOE), pos, -1.0).astype(BF16)
    post = jnp.where(sel, pos, -1.0).T[:N_EXPERTS, :]
    gatet = gates.T[:N_EXPERTS, :]
    post_ref[0] = post
    gatet_ref[0] = gatet
    post_i = post.astype(jnp.int32)
    slot = lax.broadcasted_iota(jnp.int32, (RC_MOE, tm), 0)
    for e0 in range(0, N_EXPERTS, DISPATCH_GROUP):
        pieces = []
        for e in range(e0, e0 + DISPATCH_GROUP):
            onehot = slot == post_i[e:e + 1, :]
            pieces.append(jnp.where(onehot, 1.0, 0.0).astype(BF16))
            gate = jnp.sum(jnp.where(onehot, gatet[e:e + 1, :], 0.0), axis=1, keepdims=True)
            gs_ref[0, e] = jnp.broadcast_to(gate, (RC_MOE, LANES))
        xs = _dot(jnp.concatenate(pieces, axis=0), xb_ref[...]).astype(BF16)
        xs_ref[0, e0:e0 + DISPATCH_GROUP] = xs.reshape(DISPATCH_GROUP, RC_MOE, D_MODEL)


def _dispatch(gates, x1b, tril):
    n = gates.shape[0]
    nt = n // TM_MOE
    return pl.pallas_call(
        _dispatch_kernel,
        grid=(nt,),
        in_specs=[
            pl.BlockSpec((TM_MOE, LANES), lambda i: (i, 0)),
            pl.BlockSpec((TM_MOE, D_MODEL), lambda i: (i, 0)),
            pl.BlockSpec((TM_MOE, TM_MOE), lambda i: (0, 0)),
        ],
        out_specs=[
            pl.BlockSpec((1, N_EXPERTS, RC_MOE, D_MODEL), lambda i: (i, 0, 0, 0)),
            pl.BlockSpec((1, N_EXPERTS, RC_MOE, LANES), lambda i: (i, 0, 0, 0)),
            pl.BlockSpec((TM_MOE, LANES), lambda i: (i, 0)),
            pl.BlockSpec((1, N_EXPERTS, TM_MOE), lambda i: (i, 0, 0)),
            pl.BlockSpec((1, N_EXPERTS, TM_MOE), lambda i: (i, 0, 0)),
            pl.BlockSpec((1, 1, LANES), lambda i: (i, 0, 0)),
        ],
        out_shape=[
            jax.ShapeDtypeStruct((nt, N_EXPERTS, RC_MOE, D_MODEL), BF16),
            jax.ShapeDtypeStruct((nt, N_EXPERTS, RC_MOE, LANES), F32),
            jax.ShapeDtypeStruct((n, LANES), BF16),
            jax.ShapeDtypeStruct((nt, N_EXPERTS, TM_MOE), F32),
            jax.ShapeDtypeStruct((nt, N_EXPERTS, TM_MOE), F32),
            jax.ShapeDtypeStruct((nt, 1, LANES), jnp.int32),
        ],
        compiler_params=_params("parallel"),
        name="dispatch",
    )(gates, x1b, tril)


def _swiglu_ffn(x, wgu, bgu, wd, bd):
    h = _dot(x, wgu) + bgu
    g = jnp.minimum(h[:, :D_FF], SWIGLU_LIMIT)
    up = jnp.clip(h[:, D_FF:], -SWIGLU_LIMIT, SWIGLU_LIMIT)
    act = (up + 1.0) * (g / (1.0 + jnp.exp(-SWIGLU_ALPHA * g)))
    return _dot(act.astype(BF16), wd) + bd


def _ffn_kernel(tail_ref, xs_ref, gs_ref, wgu_ref, bgu_ref, wd_ref, bd_ref, ys_ref, wgu_bf_ref, wd_bf_ref):
    e = pl.program_id(0)
    j = pl.program_id(1)

    @pl.when(j == 0)
    def _():
        wgu_bf_ref[0] = wgu_ref[0, 0].astype(BF16)
        wd_bf_ref[0] = wd_ref[0, 0].astype(BF16)

    def run(lo, hi):
        rows = FFN_TILES * (hi - lo)
        x = xs_ref[:, 0, lo:hi, :].reshape(rows, D_MODEL)
        y = _swiglu_ffn(x, wgu_bf_ref[0], bgu_ref[0], wd_bf_ref[0], bd_ref[0])
        gate = gs_ref[:, 0, lo:hi, :].reshape(rows, LANES)[:, :1]
        ys_ref[:, 0, lo:hi, :] = (y * gate).astype(BF16).reshape(FFN_TILES, hi - lo, D_MODEL)

    run(0, RC_MAIN)
    used = tail_ref[e * pl.num_programs(1) + j] > 0

    @pl.when(used)
    def _():
        run(RC_MAIN, RC_MOE)

    @pl.when(jnp.logical_not(used))
    def _():
        ys_ref[:, 0, RC_MAIN:RC_MOE, :] = jnp.zeros((FFN_TILES, RC_MOE - RC_MAIN, D_MODEL), BF16)


def _ffn(tail, xs, gs, w_gate_up, bgu, w_down, bd, layer):
    nt = xs.shape[0]
    grid_spec = pltpu.PrefetchScalarGridSpec(
        num_scalar_prefetch=1,
        grid=(N_EXPERTS, nt // FFN_TILES),
        in_specs=[
            pl.BlockSpec((FFN_TILES, 1, RC_MOE, D_MODEL), lambda e, j, t: (j, e, 0, 0)),
            pl.BlockSpec((FFN_TILES, 1, RC_MOE, LANES), lambda e, j, t: (j, e, 0, 0)),
            pl.BlockSpec((1, 1, D_MODEL, 2 * D_FF), lambda e, j, t: (layer, e, 0, 0)),
            pl.BlockSpec((1, 1, 2 * D_FF), lambda e, j, t: (e, 0, 0)),
            pl.BlockSpec((1, 1, D_FF, D_MODEL), lambda e, j, t: (layer, e, 0, 0)),
            pl.BlockSpec((1, 1, D_MODEL), lambda e, j, t: (e, 0, 0)),
        ],
        out_specs=[
            pl.BlockSpec((FFN_TILES, 1, RC_MOE, D_MODEL), lambda e, j, t: (j, e, 0, 0)),
            pl.BlockSpec((1, D_MODEL, 2 * D_FF), lambda e, j, t: (e, 0, 0)),
            pl.BlockSpec((1, D_FF, D_MODEL), lambda e, j, t: (e, 0, 0)),
        ],
    )
    return pl.pallas_call(
        _ffn_kernel,
        grid_spec=grid_spec,
        out_shape=[
            jax.ShapeDtypeStruct(xs.shape, BF16),
            jax.ShapeDtypeStruct((N_EXPERTS, D_MODEL, 2 * D_FF), BF16),
            jax.ShapeDtypeStruct((N_EXPERTS, D_FF, D_MODEL), BF16),
        ],
        compiler_params=_params("parallel", "arbitrary"),
        name="expert_ffn",
    )(tail, xs, gs, w_gate_up, bgu, w_down, bd)


def _combine_kernel(cnt_ref, ys_ref, pm_ref, pick_ref, rcol_ref, x1_ref, g_ref, b_ref, xb_ref, post_ref,
                    gatet_ref, bgu_ref, bd_ref, wgu_hbm, wd_hbm, o_ref, acc_ref, wgu_buf, wd_buf, sem):
    i = pl.program_id(0)
    tm = pm_ref.shape[0]
    width = DISPATCH_GROUP * RC_MOE
    acc = None
    for e0 in range(0, N_EXPERTS, DISPATCH_GROUP):
        sl = slice(e0 * RC_MOE, e0 * RC_MOE + width)
        ranks = _dot(pm_ref[...], pick_ref[:, sl])
        back = jnp.where(ranks == rcol_ref[:, sl], 1.0, 0.0).astype(BF16)
        part = _dot(back, ys_ref[0, e0:e0 + DISPATCH_GROUP].reshape(width, D_MODEL))
        acc = part if acc is None else acc + part
    acc_ref[...] = acc

    slot = lax.broadcasted_iota(jnp.int32, (RC_MOE, tm), 0)

    def per_expert(e, carry):
        cnt = cnt_ref[i * N_EXPERTS + e]

        @pl.when(cnt > RC_MOE)
        def _():
            copies = (pltpu.make_async_copy(wgu_hbm.at[e], wgu_buf, sem.at[0]),
                      pltpu.make_async_copy(wd_hbm.at[e], wd_buf, sem.at[1]))
            for cp in copies:
                cp.start()
            for cp in copies:
                cp.wait()
            prow = post_ref[0, pl.ds(e, 1), :].astype(jnp.int32)
            grow = gatet_ref[0, pl.ds(e, 1), :]

            def chunk(c, carry2):
                onehot = (slot + c * RC_MOE) == prow
                oh = jnp.where(onehot, 1.0, 0.0).astype(BF16)
                xg = _dot(oh, xb_ref[...]).astype(BF16)
                y = _swiglu_ffn(xg, wgu_buf[...], bgu_ref[e], wd_buf[...], bd_ref[e])
                gate = jnp.sum(jnp.where(onehot, grow, 0.0), axis=1, keepdims=True)
                yg = (y * gate).astype(BF16)
                acc_ref[...] += lax.dot_general(oh, yg, _TN, preferred_element_type=F32)
                return carry2

            lax.fori_loop(1, (cnt + RC_MOE - 1) // RC_MOE, chunk, 0)

        return carry

    lax.fori_loop(0, N_EXPERTS, per_expert, 0)
    o_ref[...] = _layer_norm(DN_ALPHA * x1_ref[...] + acc_ref[...], g_ref[...], b_ref[...])


def _combine(cnt, ys, pm, pick, rcol, x1, g, b, x1b, post, gatet, bgu, bd, wgu, wd):
    n = x1.shape[0]
    nt = n // TM_MOE
    row = lambda i, c: (i, 0)
    fixed2 = lambda i, c: (0, 0)
    fixed3 = lambda i, c: (0, 0, 0)
    grid_spec = pltpu.PrefetchScalarGridSpec(
        num_scalar_prefetch=1,
        grid=(nt,),
        in_specs=[
            pl.BlockSpec((1, N_EXPERTS, RC_MOE, D_MODEL), lambda i, c: (i, 0, 0, 0)),
            pl.BlockSpec((TM_MOE, LANES), row),
            pl.BlockSpec((LANES, SLOTS), fixed2),
            pl.BlockSpec((1, SLOTS), fixed2),
            pl.BlockSpec((TM_MOE, D_MODEL), row),
            pl.BlockSpec((1, D_MODEL), fixed2),
            pl.BlockSpec((1, D_MODEL), fixed2),
            pl.BlockSpec((TM_MOE, D_MODEL), row),
            pl.BlockSpec((1, N_EXPERTS, TM_MOE), lambda i, c: (i, 0, 0)),
            pl.BlockSpec((1, N_EXPERTS, TM_MOE), lambda i, c: (i, 0, 0)),
            pl.BlockSpec((N_EXPERTS, 1, 2 * D_FF), fixed3),
            pl.BlockSpec((N_EXPERTS, 1, D_MODEL), fixed3),
            pl.BlockSpec(memory_space=pl.ANY),
            pl.BlockSpec(memory_space=pl.ANY),
        ],
        out_specs=pl.BlockSpec((TM_MOE, D_MODEL), row),
        scratch_shapes=[
            pltpu.VMEM((TM_MOE, D_MODEL), F32),
            pltpu.VMEM((D_MODEL, 2 * D_FF), BF16),
            pltpu.VMEM((D_FF, D_MODEL), BF16),
            pltpu.SemaphoreType.DMA((2,)),
        ],
    )
    return pl.pallas_call(
        _combine_kernel,
        grid_spec=grid_spec,
        out_shape=jax.ShapeDtypeStruct((n, D_MODEL), F32),
        compiler_params=_params("arbitrary"),
        name="combine_ln",
    )(cnt, ys, pm, pick, rcol, x1, g, b, x1b, post, gatet, bgu, bd, wgu, wd)


def _block_diag(pool_w):
    groups = pool_w.shape[0]
    out = jnp.zeros((POOL_W, POOL_W), pool_w.dtype)
    for gi in range(groups):
        out = out.at[gi * POOL_CH:(gi + 1) * POOL_CH, gi * POOL_CH:(gi + 1) * POOL_CH].set(pool_w[gi])
    return out


def _layer(x2, batch, seq, consts, layer, w_in, b_in, pool_w, pool_scale, ret_norm_g, w_out, b_out, ln1_g,
           ln1_b, router_w, router_b, w_gate_up, b_gate_up, w_down, b_down, ln2_g, ln2_b):
    ret_consts, tri2, tril, pick, rcol = consts
    row = lambda a: a.reshape(1, -1).astype(F32)
    u_pool, sb, ret = _inproj(x2, w_in.astype(BF16), row(b_in))
    pool_o = _pool(u_pool, _block_diag(pool_w).astype(BF16), row(pool_scale), batch, seq)
    sb_o = _stick_breaking(sb, tri2, batch, seq)
    ret_o = _retention(ret, ret_consts, row(ret_norm_g), batch, seq)
    rw = jnp.pad(router_w.astype(F32), ((0, 0), (0, LANES - N_EXPERTS)))
    rw_hi = rw.astype(BF16)
    rw = jnp.concatenate([rw_hi, (rw - rw_hi.astype(F32)).astype(BF16)], axis=1)
    rb = jnp.pad(router_b.astype(F32), (0, LANES - N_EXPERTS)).reshape(1, LANES)
    x1, x1b, gates = _outproj(x2, pool_o, sb_o, ret_o, w_out.astype(BF16), row(b_out), row(ln1_g),
                              row(ln1_b), rw, rb)
    xs, gs, pm, post, gatet, cnt = _dispatch(gates, x1b, tril)
    bgu = b_gate_up.reshape(N_EXPERTS, 1, 2 * D_FF).astype(F32)
    bd = b_down.reshape(N_EXPERTS, 1, D_MODEL).astype(F32)
    cnt2 = cnt[:, 0, :N_EXPERTS]
    tail = jnp.any((cnt2 > RC_MAIN).reshape(-1, FFN_TILES, N_EXPERTS), axis=1)
    ys, wgu, wd = _ffn(tail.T.reshape(-1).astype(jnp.int32), xs, gs, w_gate_up, bgu, w_down, bd, layer)
    cnt_flat = cnt2.reshape(-1)
    return _combine(cnt_flat, ys, pm, pick, rcol, x1, row(ln2_g), row(ln2_b), x1b, post, gatet, bgu, bd,
                    wgu, wd)


def kernel(x, w_in, b_in, pool_w, pool_scale, ret_norm_g, w_out, b_out, ln1_g, ln1_b, router_w, router_b,
           w_gate_up, b_gate_up, w_down, b_down, ln2_g, ln2_b):
    batch, seq, d = x.shape
    n = batch * seq
    assert d == D_MODEL and seq % TS_RET == 0 and seq % TQ_SB == 0 and seq % TS_POOL == 0
    assert n % (TM_MOE * FFN_TILES) == 0 and n % TM_ROW == 0 and seq >= SB_SPAN
    j = np.arange(LANES)
    tri = (j[:, None] >= j[None, :]).astype(np.float32)
    tri2 = jnp.asarray(np.concatenate([tri, np.ones_like(tri)], axis=1), BF16)
    r = np.arange(TM_MOE)
    tril = jnp.asarray((r[:, None] >= r[None, :]).astype(np.float32), BF16)
    s = np.arange(SLOTS)
    pick = jnp.asarray((np.arange(LANES)[:, None] == (s // RC_MOE)[None, :]).astype(np.float32), BF16)
    rcol = jnp.asarray((s % RC_MOE).astype(np.float32).reshape(1, SLOTS))
    consts = (_retention_consts(seq), tri2, tril, pick, rcol)
    x2 = x.reshape(n, d)
    for l in range(DEPTH):
        x2 = _layer(x2, batch, seq, consts, l, w_in[l], b_in[l], pool_w[l], pool_scale[l], ret_norm_g[l],
                    w_out[l], b_out[l], ln1_g[l], ln1_b[l], router_w[l], router_b[l], w_gate_up,
                    b_gate_up[l], w_down, b_down[l], ln2_g[l], ln2_b[l])
    return x2.reshape(batch, seq, d)
```

```python
import functools

import numpy as np
import jax
import jax.numpy as jnp
from jax import lax
from jax.experimental import pallas as pl
from jax.experimental.pallas import tpu as pltpu

D_MODEL = 1024
DEPTH = 2
CHUNK = 64
HEAD_DIM = 64
POOL_CH = 64
POOL_W = 256
POOL_HALO = 16
SB_W = 384
RET_W = 384
RET_HEADS = 6
IN_W = POOL_W + 3 * SB_W + 4 * RET_W
ROPE_BASE = 10000.0
N_EXPERTS = 32
TOP_K = 4
D_FF = D_MODEL
SWIGLU_LIMIT = 7.0
SWIGLU_ALPHA = 1.702
DN_ALPHA = (2.0 * DEPTH) ** 0.25
LN_EPS = 1e-5

LANES = 128
VMEM_LIMIT = 56 * 1024 * 1024

TM_ROW = 512
OUTPROJ_SUB = 256
TQ_SB = 256
SB_ROWS = 64
SB_SPAN = 256
TS_RET = 512
TM_MOE = 512
RC_MOE = 96
RC_MAIN = 80
FFN_TILES = 8
SLOTS = N_EXPERTS * RC_MOE
DISPATCH_GROUP = 8
SB_SKIP = -100.0

BF16 = jnp.bfloat16
F32 = jnp.float32

_NT = (((1,), (1,)), ((), ()))
_TN = (((0,), (0,)), ((), ()))


def _dot(a, b):
    return jnp.dot(a, b, preferred_element_type=F32)


def _params(*sem):
    return pltpu.CompilerParams(dimension_semantics=sem, vmem_limit_bytes=VMEM_LIMIT)


def _pool_mix(cur, halo, t0, w, scale):
    ts = cur.shape[0]
    ext = jnp.concatenate([halo, cur], axis=0)
    a2 = ext[1:] + ext[:-1]
    a4 = a2[2:] + a2[:-2]
    a8 = a4[4:] + a4[:-4]
    a16 = a8[8:] + a8[:-8]
    lane = lax.broadcasted_iota(jnp.int32, (ts, POOL_W), 1)
    grp = lane // POOL_CH
    win = jnp.where(grp == 0, a2[15:15 + ts],
                    jnp.where(grp == 1, a4[13:13 + ts],
                              jnp.where(grp == 2, a8[9:9 + ts], a16[1:1 + ts])))
    width = jnp.where(grp == 0, 2, jnp.where(grp == 1, 4, jnp.where(grp == 2, 8, 16)))
    t = t0 + lax.broadcasted_iota(jnp.int32, (ts, POOL_W), 0)
    cnt = jnp.minimum(t + 1, width).astype(F32)
    pooled = win / cnt - cur
    return _dot(pooled.astype(BF16), w) * scale


def _inproj_kernel(tiles_per_seq, x_ref, w_ref, b_ref, pw_ref, ps_ref, pool_ref, sb_ref, ret_ref, halo_ref):
    it = pl.program_id(0) % tiles_per_seq
    xb = x_ref[...].astype(BF16)
    o1 = POOL_W + 3 * SB_W
    @pl.when(it == 0)
    def _():
        halo_ref[...] = jnp.zeros_like(halo_ref)

    u = _dot(xb, w_ref[:, :POOL_W]) + b_ref[:, :POOL_W]
    sb_ref[...] = (_dot(xb, w_ref[:, POOL_W:o1]) + b_ref[:, POOL_W:o1]).astype(BF16)
    ret_ref[...] = _dot(xb, w_ref[:, o1:]) + b_ref[:, o1:]
    pool_ref[...] = _pool_mix(u, halo_ref[...], it * TM_ROW, pw_ref[...], ps_ref[...]).astype(BF16)
    halo_ref[...] = u[TM_ROW - POOL_HALO:, :]


def _inproj(x2, w_bf, b, pool_w_bd, pool_scale, seq):
    n = x2.shape[0]
    fixed = lambda i: (0, 0)
    return pl.pallas_call(
        functools.partial(_inproj_kernel, seq // TM_ROW),
        grid=(n // TM_ROW,),
        in_specs=[
            pl.BlockSpec((TM_ROW, D_MODEL), lambda i: (i, 0)),
            pl.BlockSpec((D_MODEL, IN_W), fixed),
            pl.BlockSpec((1, IN_W), fixed),
            pl.BlockSpec((POOL_W, POOL_W), fixed),
            pl.BlockSpec((1, POOL_W), fixed),
        ],
        out_specs=[
            pl.BlockSpec((TM_ROW, POOL_W), lambda i: (i, 0)),
            pl.BlockSpec((TM_ROW, 3 * SB_W), lambda i: (i, 0)),
            pl.BlockSpec((TM_ROW, 4 * RET_W), lambda i: (i, 0)),
        ],
        out_shape=[
            jax.ShapeDtypeStruct((n, POOL_W), BF16),
            jax.ShapeDtypeStruct((n, 3 * SB_W), BF16),
            jax.ShapeDtypeStruct((n, 4 * RET_W), F32),
        ],
        scratch_shapes=[pltpu.VMEM((POOL_HALO, POOL_W), F32)],
        compiler_params=_params("arbitrary"),
        name="inproj_pool",
    )(x2, w_bf, b, pool_w_bd, pool_scale)


def _sb_kernel(q_ref, k_ref, v_ref, tri_ref, o_ref, acc_ref, carry_ref):
    qi = pl.program_id(1)
    tq = q_ref.shape[1]
    groups = q_ref.shape[2] // LANES
    lane = lax.broadcasted_iota(jnp.int32, (tq, LANES), 1)
    qh = []
    for p in range(groups):
        q = q_ref[0, :, p * LANES:(p + 1) * LANES] * jnp.asarray(HEAD_DIM ** -0.5, BF16)
        zero = jnp.zeros_like(q)
        qh += [jnp.where(lane < HEAD_DIM, q, zero), jnp.where(lane >= HEAD_DIM, q, zero)]

    def log_keep(z):
        return jnp.minimum(-z, 0.0) - jnp.log(1.0 + jnp.exp(-jnp.abs(z)))

    def suffix_sums(lk):
        lk = lk.astype(BF16)
        out = []
        for b in range(lk.shape[1] // LANES):
            rs = _dot(lk[:, b * LANES:(b + 1) * LANES], tri_ref[...])
            out.append((rs[:, :LANES], rs[:, LANES:]))
        return out

    t0 = qi * tq
    halves = range(tq // SB_ROWS)
    heads = range(2 * groups)
    cols = [slice((h // 2) * LANES, (h // 2 + 1) * LANES) for h in heads]
    rows = [slice(u * SB_ROWS, (u + 1) * SB_ROWS) for u in halves]
    starts = [pl.multiple_of(jnp.maximum(t0 + (u + 1) * SB_ROWS - SB_SPAN, 0), SB_ROWS) for u in halves]
    below = []
    for u in halves:
        qpos = t0 + u * SB_ROWS + lax.broadcasted_iota(jnp.int32, (SB_ROWS, SB_SPAN), 0)
        kpos = starts[u] + lax.broadcasted_iota(jnp.int32, (SB_ROWS, SB_SPAN), 1)
        below.append(kpos < qpos)
    items = [(u, h) for u in halves for h in heads]
    zs = {(u, h): lax.dot_general(qh[h][rows[u]], k_ref[0, pl.ds(starts[u], SB_SPAN), cols[h]], _NT,
                                  preferred_element_type=F32) for u, h in items}
    sums = {(u, h): suffix_sums(jnp.where(below[u], log_keep(zs[u, h]), 0.0)) for u, h in items}
    weights, runs = {}, {}
    for u, h in items:
        run = None
        r = [None] * len(sums[u, h])
        for b in reversed(range(len(r))):
            r_in, tot = sums[u, h][b]
            r[b] = r_in if run is None else r_in + run
            run = tot if run is None else run + tot
        weights[u, h] = jnp.where(below[u], jnp.exp(zs[u, h] + jnp.concatenate(r, axis=1)), 0.0).astype(BF16)
        runs[u, h] = run
    for u, h in items:
        acc_ref[h, rows[u], :] = _dot(weights[u, h], v_ref[0, pl.ds(starts[u], SB_SPAN), cols[h]])
        carry_ref[h, rows[u], :] = runs[u, h]
    worst = []
    for u in halves:
        top = runs[u, 0]
        for h in heads[1:]:
            top = jnp.maximum(top, runs[u, h])
        worst.append(jnp.max(top))

    def tile(u, kb):
        start = pl.multiple_of(kb * LANES, LANES)
        older = (start + lax.broadcasted_iota(jnp.int32, (SB_ROWS, LANES), 1)) < starts[u]
        worst = None
        for h in heads:
            k = k_ref[0, pl.ds(start, LANES), cols[h]]
            v = v_ref[0, pl.ds(start, LANES), cols[h]]
            z = lax.dot_general(qh[h][rows[u]], k, _NT, preferred_element_type=F32)
            (r_in, tot), = suffix_sums(jnp.where(older, log_keep(z), 0.0))
            carry = carry_ref[h, rows[u], :]
            a = jnp.where(older, jnp.exp(z + r_in + carry), 0.0)
            acc_ref[h, rows[u], :] += _dot(a.astype(BF16), v)
            carry = carry + tot
            carry_ref[h, rows[u], :] = carry
            m = jnp.max(carry)
            worst = m if worst is None else jnp.maximum(worst, m)
        return worst

    def cond(state):
        kb, worst = state
        return jnp.logical_and(kb >= 0, worst > SB_SKIP)

    for u in halves:
        lax.while_loop(cond, lambda state, u=u: (state[0] - 1, tile(u, state[0])),
                       ((starts[u] + LANES - 1) // LANES - 1, worst[u]))
    for p in range(groups):
        o_ref[0, :, p * LANES:(p + 1) * LANES] = jnp.where(
            lane < HEAD_DIM, acc_ref[2 * p], acc_ref[2 * p + 1]).astype(BF16)


def _stick_breaking(sb, tri2, batch, seq):
    sb3 = sb.reshape(batch, seq, 3 * SB_W)
    heads = SB_W // HEAD_DIM
    once = pl.Buffered(1)
    out = pl.pallas_call(
        _sb_kernel,
        grid=(batch, seq // TQ_SB),
        in_specs=[
            pl.BlockSpec((1, TQ_SB, SB_W), lambda b, i: (b, i, 0)),
            pl.BlockSpec((1, seq, SB_W), lambda b, i: (b, 0, 1), pipeline_mode=once),
            pl.BlockSpec((1, seq, SB_W), lambda b, i: (b, 0, 2), pipeline_mode=once),
            pl.BlockSpec((LANES, 2 * LANES), lambda b, i: (0, 0)),
        ],
        out_specs=pl.BlockSpec((1, TQ_SB, SB_W), lambda b, i: (b, i, 0)),
        out_shape=jax.ShapeDtypeStruct((batch, seq, SB_W), BF16),
        scratch_shapes=[
            pltpu.VMEM((heads, TQ_SB, LANES), F32),
            pltpu.VMEM((heads, TQ_SB, LANES), F32),
        ],
        compiler_params=_params("parallel", "arbitrary"),
        name="stick_breaking",
    )(sb3, sb3, sb3, tri2)
    return out.reshape(batch * seq, SB_W)


def _ret_kernel(q_ref, k_ref, v_ref, g_ref, cos_ref, sin_ref, dmat_ref, dec_ref, xi_ref, gch_ref,
                swap_ref, avg_ref, ng_ref, o_ref, state_ref, obuf_ref):
    si = pl.program_id(1)
    ts = q_ref.shape[1]
    groups = q_ref.shape[2] // LANES
    cols = [slice(p * LANES, (p + 1) * LANES) for p in range(groups)]

    @pl.when(si == 0)
    def _():
        state_ref[...] = jnp.zeros_like(state_ref)

    cos = cos_ref[...]
    sin = sin_ref[...]

    def lane_mix(x, m_ref):
        hi = x.astype(BF16)
        lo = (x - hi.astype(F32)).astype(BF16)
        return _dot(jnp.concatenate([hi, lo], axis=1), m_ref[...])

    def rot(x):
        return x * cos + lane_mix(x, swap_ref) * sin

    qr = [rot(q_ref[0, :, c]) for c in cols]
    kr = [rot(k_ref[0, :, c]) * (HEAD_DIM ** -0.5) for c in cols]
    v = [v_ref[0, :, c] for c in cols]

    lane_c = lax.broadcasted_iota(jnp.int32, (CHUNK, LANES), 1)
    head0 = lane_c < HEAD_DIM
    r_i = lax.broadcasted_iota(jnp.int32, (LANES, LANES), 0) // HEAD_DIM
    c_i = lax.broadcasted_iota(jnp.int32, (LANES, LANES), 1) // HEAD_DIM
    same_head = r_i == c_i

    chunks = [slice(n * CHUNK, (n + 1) * CHUNK) for n in range(ts // CHUNK)]
    items = [(p, n) for p in range(groups) for n in range(len(chunks))]
    kcb = {(p, n): kr[p][chunks[n]].astype(BF16) for p, n in items}
    vcb = {(p, n): v[p][chunks[n]].astype(BF16) for p, n in items}
    kvs = {(p, n): lax.dot_general(kcb[p, n], (v[p][chunks[n]] * dec_ref[p]).astype(BF16), _TN,
                                   preferred_element_type=F32) for p, n in items}
    scores = {}
    for p, n in items:
        qc = qr[p][chunks[n]]
        qstack = jnp.concatenate([jnp.where(head0, qc, 0.0), jnp.where(head0, 0.0, qc)], axis=0)
        scores[p, n] = (lax.dot_general(qstack.astype(BF16), kcb[p, n], _NT, preferred_element_type=F32)
                        * dmat_ref[p]).astype(BF16)
    o2s = {(p, n): _dot(scores[p, n], vcb[p, n]) for p, n in items}
    states = {}
    for p in range(groups):
        state = state_ref[p]
        for n in range(len(chunks)):
            states[p, n] = state.astype(BF16)
            state = gch_ref[p] * state + jnp.where(same_head, kvs[p, n], 0.0)
        state_ref[p] = state
    for p, n in items:
        o_intra = jnp.where(head0, o2s[p, n][:CHUNK], o2s[p, n][CHUNK:])
        o_inter = _dot((qr[p][chunks[n]] * xi_ref[p]).astype(BF16), states[p, n])
        obuf_ref[chunks[n], cols[p]] = o_intra + o_inter

    for c in cols:
        o = obuf_ref[:, c]
        mu = lane_mix(o, avg_ref)
        oc = o - mu
        var = lane_mix(oc * oc, avg_ref)
        normed = oc * lax.rsqrt(var + LN_EPS) * ng_ref[:, c]
        gate = g_ref[0, :, c]
        o_ref[0, :, c] = (gate / (1.0 + jnp.exp(-gate)) * normed).astype(BF16)


def _retention(ret, consts, norm_g, batch, seq):
    cos, sin, dmat, dec, xi, gch, swap, avg = consts
    ret3 =ret.reshape(batch, seq, 4 * RET_W)
    pairs = RET_W // LANES
    fixed3 = lambda b, i: (0, 0, 0)
    out = pl.pallas_call(
        _ret_kernel,
        grid=(batch, seq // TS_RET),
        in_specs=[
            pl.BlockSpec((1, TS_RET, RET_W), lambda b, i: (b, i, 0)),
            pl.BlockSpec((1, TS_RET, RET_W), lambda b, i: (b, i, 1)),
            pl.BlockSpec((1, TS_RET, RET_W), lambda b, i: (b, i, 2)),
            pl.BlockSpec((1, TS_RET, RET_W), lambda b, i: (b, i, 3)),
            pl.BlockSpec((TS_RET, LANES), lambda b, i: (i, 0)),
            pl.BlockSpec((TS_RET, LANES), lambda b, i: (i, 0)),
            pl.BlockSpec((pairs, 2 * CHUNK, CHUNK), fixed3),
            pl.BlockSpec((pairs, CHUNK, LANES), fixed3),
            pl.BlockSpec((pairs, CHUNK, LANES), fixed3),
            pl.BlockSpec((pairs, 1, LANES), fixed3),
            pl.BlockSpec((2 * LANES, LANES), lambda b, i: (0, 0)),
            pl.BlockSpec((2 * LANES, LANES), lambda b, i: (0, 0)),
            pl.BlockSpec((1, RET_W), lambda b, i: (0, 0)),
        ],
        out_specs=pl.BlockSpec((1, TS_RET, RET_W), lambda b, i: (b, i, 0)),
        out_shape=jax.ShapeDtypeStruct((batch, seq, RET_W), BF16),
        scratch_shapes=[
            pltpu.VMEM((pairs, LANES, LANES), F32),
            pltpu.VMEM((TS_RET, RET_W), F32),
        ],
        compiler_params=_params("parallel", "arbitrary"),
        name="retention",
    )(ret3, ret3, ret3, ret3, cos, sin, dmat, dec, xi, gch, swap, avg, norm_g)
    return out.reshape(batch * seq, RET_W)


def _retention_consts(seq):
    half = HEAD_DIM // 2
    pos = jnp.arange(seq, dtype=F32)
    inv = ROPE_BASE ** (-jnp.arange(half, dtype=F32) / half)
    ang = pos[:, None] * inv[None, :]
    cos_h, sin_h = jnp.cos(ang), jnp.sin(ang)
    cos = jnp.tile(cos_h, (1, 2 * LANES // HEAD_DIM))
    sin = jnp.tile(jnp.concatenate([-sin_h, sin_h], axis=1), (1, LANES // HEAD_DIM))
    log_g = jnp.log(1.0 - 2.0 ** (-5.0 - jnp.arange(RET_HEADS, dtype=F32)))
    c = jnp.arange(CHUNK, dtype=F32)
    dmat = jnp.exp(jnp.abs(c[:, None] - c[None, :])[None] * log_g[:, None, None])
    dmat = dmat.reshape(RET_HEADS // 2, 2 * CHUNK, CHUNK)
    lane_log_g = jnp.repeat(log_g, HEAD_DIM).reshape(RET_HEADS // 2, 1, LANES)
    dec = jnp.exp((CHUNK - 1 - c)[None, :, None] * lane_log_g)
    xi = jnp.exp((c + 1.0)[None, :, None] * lane_log_g)
    gch = jnp.exp(CHUNK * lane_log_g)
    l = np.arange(LANES)
    partner = np.where(l % HEAD_DIM < half, l + half, l - half)
    swap = (l[:, None] == partner[None, :]).astype(np.float32)
    avg = (l[:, None] // HEAD_DIM == l[None, :] // HEAD_DIM).astype(np.float32) / HEAD_DIM
    swap = jnp.asarray(np.concatenate([swap, swap], axis=0), BF16)
    avg = jnp.asarray(np.concatenate([avg, avg], axis=0), BF16)
    return cos, sin, dmat, dec, xi, gch, swap, avg


def _layer_norm(z, g, b):
    mu = jnp.mean(z, axis=-1, keepdims=True)
    zc = z - mu
    var = jnp.mean(zc * zc, axis=-1, keepdims=True)
    return zc * lax.rsqrt(var + LN_EPS) * g + b


def _outproj_kernel(x_ref, p_ref, s_ref, r_ref, w_ref, b_ref, g_ref, be_ref, rw_ref, rb_ref,
                    x1_ref, x1b_ref, gates_ref):
    o1 = POOL_W + SB_W
    tm = x_ref.shape[0]
    subs = [slice(s, s + OUTPROJ_SUB) for s in range(0, tm, OUTPROJ_SUB)]
    ys = [_dot(p_ref[sl, :], w_ref[:POOL_W, :]) + _dot(s_ref[sl, :], w_ref[POOL_W:o1, :])
          + _dot(r_ref[sl, :], w_ref[o1:, :]) for sl in subs]
    x1s = [_layer_norm(DN_ALPHA * x_ref[sl, :] + (y + b_ref[...]), g_ref[...], be_ref[...])
           for sl, y in zip(subs, ys)]
    lane = lax.broadcasted_iota(jnp.int32, (OUTPROJ_SUB, LANES), 1).astype(F32)
    logits = []
    for sl, x1 in zip(subs, x1s):
        x1_ref[sl, :] = x1
        x_hi = x1.astype(BF16)
        x1b_ref[sl, :] = x_hi
        x_mid = (x1 - x_hi.astype(F32)).astype(BF16)
        t = _dot(x_hi, rw_ref[...])
        logits.append(t[:, :LANES] + t[:, LANES:] + _dot(x_mid, rw_ref[:, :LANES]) + rb_ref[...])
    for sl, lg in zip(subs, logits):
        vals = jnp.where(lane < N_EXPERTS, lg, -jnp.inf)
        top_v, top_sel = [], []
        for _ in range(TOP_K):
            m = jnp.max(vals, axis=-1, keepdims=True)
            idx = jnp.min(jnp.where(vals == m, lane, float(LANES)), axis=-1, keepdims=True)
            sel = lane == idx
            vals = jnp.where(sel, -jnp.inf, vals)
            top_v.append(m)
            top_sel.append(sel)
        ex = [jnp.exp(m - top_v[0]) for m in top_v]
        den = ex[0] + ex[1] + ex[2] + ex[3]
        gates = jnp.full((OUTPROJ_SUB, LANES), -1.0, F32)
        for sel, e in zip(top_sel, ex):
            gates = jnp.where(sel, e / den, gates)
        gates_ref[sl, :] = gates


def _outproj(x2, pool_o, sb_o, ret_o, w_bf, b, g, be, rw, rb):
    n = x2.shape[0]
    row = lambda i: (i, 0)
    fixed = lambda i: (0, 0)
    return pl.pallas_call(
        _outproj_kernel,
        grid=(n // TM_ROW,),
        in_specs=[
            pl.BlockSpec((TM_ROW, D_MODEL), row),
            pl.BlockSpec((TM_ROW, POOL_W), row),
            pl.BlockSpec((TM_ROW, SB_W), row),
            pl.BlockSpec((TM_ROW, RET_W), row),
            pl.BlockSpec((D_MODEL, D_MODEL), fixed),
            pl.BlockSpec((1, D_MODEL), fixed),
            pl.BlockSpec((1, D_MODEL), fixed),
            pl.BlockSpec((1, D_MODEL), fixed),
            pl.BlockSpec((D_MODEL, 2 * LANES), fixed),
            pl.BlockSpec((1, LANES), fixed),
        ],
        out_specs=[
            pl.BlockSpec((TM_ROW, D_MODEL), row),
            pl.BlockSpec((TM_ROW, D_MODEL), row),
            pl.BlockSpec((TM_ROW, LANES), row),
        ],
        out_shape=[
            jax.ShapeDtypeStruct((n, D_MODEL), F32),
            jax.ShapeDtypeStruct((n, D_MODEL), BF16),
            jax.ShapeDtypeStruct((n, LANES), F32),
        ],
        compiler_params=_params("parallel"),
        name="outproj_ln_router",
    )(x2, pool_o, sb_o, ret_o, w_bf, b, g, be, rw, rb)


def _dispatch_kernel(gates_ref, xb_ref, tril_ref, xs_ref, gs_ref, pm_ref, post_ref, gatet_ref, cnt_ref):
    tm = gates_ref.shape[0]
    gates = gates_ref[...]
    sel = gates >= 0.0
    self_ = jnp.where(sel, 1.0, 0.0)
    incl = _dot(tril_ref[...], self_.astype(BF16))
    pos = incl - self_
    cnt_ref[0] = incl[tm - 1:tm, :].astype(jnp.int32)
    pm_ref[...] = jnp.where(jnp.logical_and(sel, pos < RC_MOE), pos, -1.0).astype(BF16)
    post = jnp.where(sel, pos, -1.0).T[:N_EXPERTS, :]
    gatet = gates.T[:N_EXPERTS, :]
    post_ref[0] = post
    gatet_ref[0] = gatet
    post_i = post.astype(jnp.int32)
    slot = lax.broadcasted_iota(jnp.int32, (RC_MOE, tm), 0)
    for e0 in range(0, N_EXPERTS, DISPATCH_GROUP):
        pieces = []
        for e in range(e0, e0 + DISPATCH_GROUP):
            onehot = slot == post_i[e:e + 1, :]
            pieces.append(jnp.where(onehot, 1.0, 0.0).astype(BF16))
            gate = jnp.sum(jnp.where(onehot, gatet[e:e + 1, :], 0.0), axis=1, keepdims=True)
            gs_ref[0, e] = jnp.broadcast_to(gate, (RC_MOE, LANES))
        xs = _dot(jnp.concatenate(pieces, axis=0), xb_ref[...]).astype(BF16)
        xs_ref[0, e0:e0 + DISPATCH_GROUP] = xs.reshape(DISPATCH_GROUP, RC_MOE, D_MODEL)


def _dispatch(gates, x1b, tril):
    n = gates.shape[0]
    nt = n // TM_MOE
    return pl.pallas_call(
        _dispatch_kernel,
        grid=(nt,),
        in_specs=[
            pl.BlockSpec((TM_MOE, LANES), lambda i: (i, 0)),
            pl.BlockSpec((TM_MOE, D_MODEL), lambda i: (i, 0)),
            pl.BlockSpec((TM_MOE, TM_MOE), lambda i: (0, 0)),
        ],
        out_specs=[
            pl.BlockSpec((1, N_EXPERTS, RC_MOE, D_MODEL), lambda i: (i, 0, 0, 0)),
            pl.BlockSpec((1, N_EXPERTS, RC_MOE, LANES), lambda i: (i, 0, 0, 0)),
            pl.BlockSpec((TM_MOE, LANES), lambda i: (i, 0)),
            pl.BlockSpec((1, N_EXPERTS, TM_MOE), lambda i: (i, 0, 0)),
            pl.BlockSpec((1, N_EXPERTS, TM_MOE), lambda i: (i, 0, 0)),
            pl.BlockSpec((1, 1, LANES), lambda i: (i, 0, 0)),
        ],
        out_shape=[
            jax.ShapeDtypeStruct((nt, N_EXPERTS, RC_MOE, D_MODEL), BF16),
            jax.ShapeDtypeStruct((nt, N_EXPERTS, RC_MOE, LANES), F32),
            jax.ShapeDtypeStruct((n, LANES), BF16),
            jax.ShapeDtypeStruct((nt, N_EXPERTS, TM_MOE), F32),
            jax.ShapeDtypeStruct((nt, N_EXPERTS, TM_MOE), F32),
            jax.ShapeDtypeStruct((nt, 1, LANES), jnp.int32),
        ],
        compiler_params=_params("parallel"),
        name="dispatch",
    )(gates, x1b, tril)


def _swiglu_ffn(x, wgu, bgu, wd, bd):
    h = _dot(x, wgu) + bgu
    g = jnp.minimum(h[:, :D_FF], SWIGLU_LIMIT)
    up = jnp.clip(h[:, D_FF:], -SWIGLU_LIMIT, SWIGLU_LIMIT)
    act = (up + 1.0) * (g / (1.0 + jnp.exp(-SWIGLU_ALPHA * g)))
    return _dot(act.astype(BF16), wd) + bd


def _ffn_kernel(tail_ref, xs_ref, gs_ref, wgu_ref, bgu_ref, wd_ref, bd_ref, ys_ref, wgu_bf_ref, wd_bf_ref):
    e = pl.program_id(0)
    j = pl.program_id(1)

    @pl.when(j == 0)
    def _():
        wgu_bf_ref[0] = wgu_ref[0, 0].astype(BF16)
        wd_bf_ref[0] = wd_ref[0, 0].astype(BF16)

    def run(lo, hi):
        rows = FFN_TILES * (hi - lo)
        x = xs_ref[:, 0, lo:hi, :].reshape(rows, D_MODEL)
        y = _swiglu_ffn(x, wgu_bf_ref[0], bgu_ref[0], wd_bf_ref[0], bd_ref[0])
        gate = gs_ref[:, 0, lo:hi, :].reshape(rows, LANES)[:, :1]
        ys_ref[:, 0, lo:hi, :] = (y * gate).astype(BF16).reshape(FFN_TILES, hi - lo, D_MODEL)

    run(0, RC_MAIN)
    used = tail_ref[e * pl.num_programs(1) + j] > 0

    @pl.when(used)
    def _():
        run(RC_MAIN, RC_MOE)

    @pl.when(jnp.logical_not(used))
    def _():
        ys_ref[:, 0, RC_MAIN:RC_MOE, :] = jnp.zeros((FFN_TILES, RC_MOE - RC_MAIN, D_MODEL), BF16)


def _ffn(tail, xs, gs, w_gate_up, bgu, w_down, bd, layer):
    nt = xs.shape[0]
    grid_spec = pltpu.PrefetchScalarGridSpec(
        num_scalar_prefetch=1,
        grid=(N_EXPERTS, nt // FFN_TILES),
        in_specs=[
            pl.BlockSpec((FFN_TILES, 1, RC_MOE, D_MODEL), lambda e, j, t: (j, e, 0, 0)),
            pl.BlockSpec((FFN_TILES, 1, RC_MOE, LANES), lambda e, j, t: (j, e, 0, 0)),
            pl.BlockSpec((1, 1, D_MODEL, 2 * D_FF), lambda e, j, t: (layer, e, 0, 0)),
            pl.BlockSpec((1, 1, 2 * D_FF), lambda e, j, t: (e, 0, 0)),
            pl.BlockSpec((1, 1, D_FF, D_MODEL), lambda e, j, t: (layer, e, 0, 0)),
            pl.BlockSpec((1, 1, D_MODEL), lambda e, j, t: (e, 0, 0)),
        ],
        out_specs=[
            pl.BlockSpec((FFN_TILES, 1, RC_MOE, D_MODEL), lambda e, j, t: (j, e, 0, 0)),
            pl.BlockSpec((1, D_MODEL, 2 * D_FF), lambda e, j, t: (e, 0, 0)),
            pl.BlockSpec((1, D_FF, D_MODEL), lambda e, j, t: (e, 0, 0)),
        ],
    )
    return pl.pallas_call(
        _ffn_kernel,
        grid_spec=grid_spec,
        out_shape=[
            jax.ShapeDtypeStruct(xs.shape, BF16),
            jax.ShapeDtypeStruct((N_EXPERTS, D_MODEL, 2 * D_FF), BF16),
            jax.ShapeDtypeStruct((N_EXPERTS, D_FF, D_MODEL), BF16),
        ],
        compiler_params=_params("parallel", "arbitrary"),
        name="expert_ffn",
    )(tail, xs, gs, w_gate_up, bgu, w_down, bd)


def _combine_kernel(cnt_ref, ys_ref, pm_ref, pick_ref, rcol_ref, x1_ref, g_ref, b_ref, xb_ref, post_ref,
                    gatet_ref, bgu_ref, bd_ref, wgu_hbm, wd_hbm, o_ref, acc_ref, wgu_buf, wd_buf, sem):
    i = pl.program_id(0)
    tm = pm_ref.shape[0]
    width = DISPATCH_GROUP * RC_MOE
    acc = None
    for e0 in range(0, N_EXPERTS, DISPATCH_GROUP):
        sl = slice(e0 * RC_MOE, e0 * RC_MOE + width)
        ranks = _dot(pm_ref[...], pick_ref[:, sl])
        back = jnp.where(ranks == rcol_ref[:, sl], 1.0, 0.0).astype(BF16)
        part = _dot(back, ys_ref[0, e0:e0 + DISPATCH_GROUP].reshape(width, D_MODEL))
        acc = part if acc is None else acc + part
    acc_ref[...] = acc

    slot = lax.broadcasted_iota(jnp.int32, (RC_MOE, tm), 0)

    def per_expert(e, carry):
        cnt = cnt_ref[i * N_EXPERTS + e]

        @pl.when(cnt > RC_MOE)
        def _():
            copies = (pltpu.make_async_copy(wgu_hbm.at[e], wgu_buf, sem.at[0]),
                      pltpu.make_async_copy(wd_hbm.at[e], wd_buf, sem.at[1]))
            for cp in copies:
                cp.start()
            for cp in copies:
                cp.wait()
            prow = post_ref[0, pl.ds(e, 1), :].astype(jnp.int32)
            grow = gatet_ref[0, pl.ds(e, 1), :]

            def chunk(c, carry2):
                onehot = (slot + c * RC_MOE) == prow
                oh = jnp.where(onehot, 1.0, 0.0).astype(BF16)
                xg = _dot(oh, xb_ref[...]).astype(BF16)
                y = _swiglu_ffn(xg, wgu_buf[...], bgu_ref[e], wd_buf[...], bd_ref[e])
                gate = jnp.sum(jnp.where(onehot, grow, 0.0), axis=1, keepdims=True)
                yg = (y * gate).astype(BF16)
                acc_ref[...] += lax.dot_general(oh, yg, _TN, preferred_element_type=F32)
                return carry2

            lax.fori_loop(1, (cnt + RC_MOE - 1) // RC_MOE, chunk, 0)

        return carry

    lax.fori_loop(0, N_EXPERTS, per_expert, 0)
    o_ref[...] = _layer_norm(DN_ALPHA * x1_ref[...] + acc_ref[...], g_ref[...], b_ref[...])


def _combine(cnt, ys, pm, pick, rcol, x1, g, b, x1b, post, gatet, bgu, bd, wgu, wd):
    n = x1.shape[0]
    nt = n // TM_MOE
    row = lambda i, c: (i, 0)
    fixed2 = lambda i, c: (0, 0)
    fixed3 = lambda i, c: (0, 0, 0)
    grid_spec = pltpu.PrefetchScalarGridSpec(
        num_scalar_prefetch=1,
        grid=(nt,),
        in_specs=[
            pl.BlockSpec((1, N_EXPERTS, RC_MOE, D_MODEL), lambda i, c: (i, 0, 0, 0)),
            pl.BlockSpec((TM_MOE, LANES), row),
            pl.BlockSpec((LANES, SLOTS), fixed2),
            pl.BlockSpec((1, SLOTS), fixed2),
            pl.BlockSpec((TM_MOE, D_MODEL), row),
            pl.BlockSpec((1, D_MODEL), fixed2),
            pl.BlockSpec((1, D_MODEL), fixed2),
            pl.BlockSpec((TM_MOE, D_MODEL), row),
            pl.BlockSpec((1, N_EXPERTS, TM_MOE), lambda i, c: (i, 0, 0)),
            pl.BlockSpec((1, N_EXPERTS, TM_MOE), lambda i, c: (i, 0, 0)),
            pl.BlockSpec((N_EXPERTS, 1, 2 * D_FF), fixed3),
            pl.BlockSpec((N_EXPERTS, 1, D_MODEL), fixed3),
            pl.BlockSpec(memory_space=pl.ANY),
            pl.BlockSpec(memory_space=pl.ANY),
        ],
        out_specs=pl.BlockSpec((TM_MOE, D_MODEL), row),
        scratch_shapes=[
            pltpu.VMEM((TM_MOE, D_MODEL), F32),
            pltpu.VMEM((D_MODEL, 2 * D_FF), BF16),
            pltpu.VMEM((D_FF, D_MODEL), BF16),
            pltpu.SemaphoreType.DMA((2,)),
        ],
    )
    return pl.pallas_call(
        _combine_kernel,
        grid_spec=grid_spec,
        out_shape=jax.ShapeDtypeStruct((n, D_MODEL), F32),
        compiler_params=_params("arbitrary"),
        name="combine_ln",
    )(cnt, ys, pm, pick, rcol, x1, g, b, x1b, post, gatet, bgu, bd, wgu, wd)


def _block_diag(pool_w):
    groups = pool_w.shape[0]
    out = jnp.zeros((POOL_W, POOL_W), pool_w.dtype)
    for gi in range(groups):
        out = out.at[gi * POOL_CH:(gi + 1) * POOL_CH, gi * POOL_CH:(gi + 1) * POOL_CH].set(pool_w[gi])
    return out


def _layer(x2, batch, seq, consts, layer, w_in, b_in, pool_w, pool_scale, ret_norm_g, w_out, b_out, ln1_g,
           ln1_b, router_w, router_b, w_gate_up, b_gate_up, w_down, b_down, ln2_g, ln2_b):
    ret_consts, tri2, tril, pick, rcol = consts
    row = lambda a: a.reshape(1, -1).astype(F32)
    pool_o, sb, ret = _inproj(x2, w_in.astype(BF16), row(b_in), _block_diag(pool_w).astype(BF16),
                              row(pool_scale), seq)
    sb_o = _stick_breaking(sb, tri2, batch, seq)
    ret_o = _retention(ret, ret_consts, row(ret_norm_g), batch, seq)
    rw = jnp.pad(router_w.astype(F32), ((0, 0), (0, LANES - N_EXPERTS)))
    rw_hi = rw.astype(BF16)
    rw = jnp.concatenate([rw_hi, (rw - rw_hi.astype(F32)).astype(BF16)], axis=1)
    rb = jnp.pad(router_b.astype(F32), (0, LANES - N_EXPERTS)).reshape(1, LANES)
    x1, x1b, gates = _outproj(x2, pool_o, sb_o, ret_o, w_out.astype(BF16), row(b_out), row(ln1_g),
                              row(ln1_b), rw, rb)
    xs, gs, pm, post, gatet, cnt = _dispatch(gates, x1b, tril)
    bgu = b_gate_up.reshape(N_EXPERTS, 1, 2 * D_FF).astype(F32)
    bd = b_down.reshape(N_EXPERTS, 1, D_MODEL).astype(F32)
    cnt2 = cnt[:, 0, :N_EXPERTS]
    tail = jnp.any((cnt2 > RC_MAIN).reshape(-1, FFN_TILES, N_EXPERTS), axis=1)
    ys, wgu, wd = _ffn(tail.T.reshape(-1).astype(jnp.int32), xs, gs, w_gate_up, bgu, w_down, bd, layer)
    cnt_flat = cnt2.reshape(-1)
    return _combine(cnt_flat, ys, pm, pick, rcol, x1, row(ln2_g), row(ln2_b), x1b, post, gatet, bgu, bd,
                    wgu, wd)


def kernel(x, w_in, b_in, pool_w, pool_scale, ret_norm_g, w_out, b_out, ln1_g, ln1_b, router_w, router_b,
           w_gate_up, b_gate_up, w_down, b_down, ln2_g, ln2_b):
    batch, seq, d = x.shape
    n = batch * seq
    assert d == D_MODEL and seq % TS_RET == 0 and seq % TQ_SB == 0 and seq % TM_ROW == 0
    assert n % (TM_MOE * FFN_TILES) == 0 and n % TM_ROW == 0 and seq >= SB_SPAN
    j = np.arange(LANES)
    tri = (j[:, None] >= j[None, :]).astype(np.float32)
    tri2 = jnp.asarray(np.concatenate([tri, np.ones_like(tri)], axis=1), BF16)
    r = np.arange(TM_MOE)
    tril = jnp.asarray((r[:, None] >= r[None, :]).astype(np.float32), BF16)
    s = np.arange(SLOTS)
    pick = jnp.asarray((np.arange(LANES)[:, None] == (s // RC_MOE)[None, :]).astype(np.float32), BF16)
    rcol = jnp.asarray((s % RC_MOE).astype(np.float32).reshape(1, SLOTS))
    consts = (_retention_consts(seq), tri2, tril, pick, rcol)
    x2 = x.reshape(n, d)
    for l in range(DEPTH):
        x2 = _layer(x2, batch, seq, consts, l, w_in[l], b_in[l], pool_w[l], pool_scale[l], ret_norm_g[l],
                    w_out[l], b_out[l], ln1_g[l], ln1_b[l], router_w[l], router_b[l], w_gate_up,
                    b_gate_up[l], w_down, b_down[l], ln2_g[l], ln2_b[l])
    return x2.reshape(batch, seq, d)
```

```python
import functools

import numpy as np
import jax
import jax.numpy as jnp
from jax import lax
from jax.experimental import pallas as pl
from jax.experimental.pallas import tpu as pltpu

D_MODEL = 1024
DEPTH = 2
CHUNK = 64
HEAD_DIM = 64
POOL_CH = 64
POOL_W = 256
POOL_HALO = 16
SB_W = 384
RET_W = 384
RET_HEADS = 6
IN_W = POOL_W + 3 * SB_W + 4 * RET_W
ROPE_BASE = 10000.0
N_EXPERTS = 32
TOP_K = 4
D_FF = D_MODEL
SWIGLU_LIMIT = 7.0
SWIGLU_ALPHA = 1.702
DN_ALPHA = (2.0 * DEPTH) ** 0.25
LN_EPS = 1e-5

LANES = 128
VMEM_LIMIT = 56 * 1024 * 1024

TM_ROW = 1024
OUTPROJ_SUB = 256
TQ_SB = 512
SB_ROWS = 64
SB_SPAN = 256
TS_RET = 1024
TM_MOE = 512
RC_MOE = 96
RC_MAIN = 80
FFN_TILES = 8
SLOTS = N_EXPERTS * RC_MOE
DISPATCH_GROUP = 8
SB_SKIP = -100.0

BF16 = jnp.bfloat16
F32 = jnp.float32

_NT = (((1,), (1,)), ((), ()))
_TN = (((0,), (0,)), ((), ()))


def _dot(a, b):
    return jnp.dot(a, b, preferred_element_type=F32)


def _params(*sem):
    return pltpu.CompilerParams(dimension_semantics=sem, vmem_limit_bytes=VMEM_LIMIT)


def _pool_mix(cur, halo, t0, w, scale):
    ts = cur.shape[0]
    ext = jnp.concatenate([halo, cur], axis=0)
    a2 = ext[1:] + ext[:-1]
    a4 = a2[2:] + a2[:-2]
    a8 = a4[4:] + a4[:-4]
    a16 = a8[8:] + a8[:-8]
    lane = lax.broadcasted_iota(jnp.int32, (ts, POOL_W), 1)
    grp = lane // POOL_CH
    win = jnp.where(grp == 0, a2[15:15 + ts],
                    jnp.where(grp == 1, a4[13:13 + ts],
                              jnp.where(grp == 2, a8[9:9 + ts], a16[1:1 + ts])))
    width = jnp.where(grp == 0, 2, jnp.where(grp == 1, 4, jnp.where(grp == 2, 8, 16)))
    t = t0 + lax.broadcasted_iota(jnp.int32, (ts, POOL_W), 0)
    cnt = jnp.minimum(t + 1, width).astype(F32)
    pooled = win / cnt - cur
    return _dot(pooled.astype(BF16), w) * scale


def _inproj_kernel(tiles_per_seq, x_ref, w_ref, b_ref, pw_ref, ps_ref, pool_ref, sb_ref, ret_ref, halo_ref):
    it = pl.program_id(0) % tiles_per_seq
    xb = x_ref[...].astype(BF16)
    o1 = POOL_W + 3 * SB_W
    @pl.when(it == 0)
    def _():
        halo_ref[...] = jnp.zeros_like(halo_ref)

    u = _dot(xb, w_ref[:, :POOL_W]) + b_ref[:, :POOL_W]
    sb_ref[...] = (_dot(xb, w_ref[:, POOL_W:o1]) + b_ref[:, POOL_W:o1]).astype(BF16)
    ret_ref[...] = _dot(xb, w_ref[:, o1:]) + b_ref[:, o1:]
    pool_ref[...] = _pool_mix(u, halo_ref[...], it * TM_ROW, pw_ref[...], ps_ref[...]).astype(BF16)
    halo_ref[...] = u[TM_ROW - POOL_HALO:, :]


def _inproj(x2, w_bf, b, pool_w_bd, pool_scale, seq):
    n = x2.shape[0]
    fixed = lambda i: (0, 0)
    return pl.pallas_call(
        functools.partial(_inproj_kernel, seq // TM_ROW),
        grid=(n // TM_ROW,),
        in_specs=[
            pl.BlockSpec((TM_ROW, D_MODEL), lambda i: (i, 0)),
            pl.BlockSpec((D_MODEL, IN_W), fixed),
            pl.BlockSpec((1, IN_W), fixed),
            pl.BlockSpec((POOL_W, POOL_W), fixed),
            pl.BlockSpec((1, POOL_W), fixed),
        ],
        out_specs=[
            pl.BlockSpec((TM_ROW, POOL_W), lambda i: (i, 0)),
            pl.BlockSpec((TM_ROW, 3 * SB_W), lambda i: (i, 0)),
            pl.BlockSpec((TM_ROW, 4 * RET_W), lambda i: (i, 0)),
        ],
        out_shape=[
            jax.ShapeDtypeStruct((n, POOL_W), BF16),
            jax.ShapeDtypeStruct((n, 3 * SB_W), BF16),
            jax.ShapeDtypeStruct((n, 4 * RET_W), F32),
        ],
        scratch_shapes=[pltpu.VMEM((POOL_HALO, POOL_W), F32)],
        compiler_params=_params("arbitrary"),
        name="inproj_pool",
    )(x2, w_bf, b, pool_w_bd, pool_scale)


def _sb_kernel(q_ref, k_ref, v_ref, tri_ref, o_ref, acc_ref, carry_ref):
    qi = pl.program_id(1)
    tq = q_ref.shape[1]
    groups = q_ref.shape[2] // LANES
    lane = lax.broadcasted_iota(jnp.int32, (tq, LANES), 1)
    qh = []
    for p in range(groups):
        q = q_ref[0, :, p * LANES:(p + 1) * LANES] * jnp.asarray(HEAD_DIM ** -0.5, BF16)
        zero = jnp.zeros_like(q)
        qh += [jnp.where(lane < HEAD_DIM, q, zero), jnp.where(lane >= HEAD_DIM, q, zero)]

    def log_keep(z):
        return jnp.minimum(-z, 0.0) - jnp.log(1.0 + jnp.exp(-jnp.abs(z)))

    def suffix_sums(lk):
        lk = lk.astype(BF16)
        out = []
        for b in range(lk.shape[1] // LANES):
            rs = _dot(lk[:, b * LANES:(b + 1) * LANES], tri_ref[...])
            out.append((rs[:, :LANES], rs[:, LANES:]))
        return out

    t0 = qi * tq
    halves = range(tq // SB_ROWS)
    heads = range(2 * groups)
    cols = [slice((h // 2) * LANES, (h // 2 + 1) * LANES) for h in heads]
    rows = [slice(u * SB_ROWS, (u + 1) * SB_ROWS) for u in halves]
    starts = [pl.multiple_of(jnp.maximum(t0 + (u + 1) * SB_ROWS - SB_SPAN, 0), SB_ROWS) for u in halves]
    below = []
    for u in halves:
        qpos = t0 + u * SB_ROWS + lax.broadcasted_iota(jnp.int32, (SB_ROWS, SB_SPAN), 0)
        kpos = starts[u] + lax.broadcasted_iota(jnp.int32, (SB_ROWS, SB_SPAN), 1)
        below.append(kpos < qpos)
    items = [(u, h) for u in halves for h in heads]
    zs = {(u, h): lax.dot_general(qh[h][rows[u]], k_ref[0, pl.ds(starts[u], SB_SPAN), cols[h]], _NT,
                                  preferred_element_type=F32) for u, h in items}
    sums = {(u, h): suffix_sums(jnp.where(below[u], log_keep(zs[u, h]), 0.0)) for u, h in items}
    weights, runs = {}, {}
    for u, h in items:
        run = None
        r = [None] * len(sums[u, h])
        for b in reversed(range(len(r))):
            r_in, tot = sums[u, h][b]
            r[b] = r_in if run is None else r_in + run
            run = tot if run is None else run + tot
        weights[u, h] = jnp.where(below[u], jnp.exp(zs[u, h] + jnp.concatenate(r, axis=1)), 0.0).astype(BF16)
        runs[u, h] = run
    for u, h in items:
        acc_ref[h, rows[u], :] = _dot(weights[u, h], v_ref[0, pl.ds(starts[u], SB_SPAN), cols[h]])
        carry_ref[h, rows[u], :] = runs[u, h]
    worst = []
    for u in halves:
        top = runs[u, 0]
        for h in heads[1:]:
            top = jnp.maximum(top, runs[u, h])
        worst.append(jnp.max(top))

    def tile(u, kb):
        start = pl.multiple_of(kb * LANES, LANES)
        older = (start + lax.broadcasted_iota(jnp.int32, (SB_ROWS, LANES), 1)) < starts[u]
        worst = None
        for h in heads:
            k = k_ref[0, pl.ds(start, LANES), cols[h]]
            v = v_ref[0, pl.ds(start, LANES), cols[h]]
            z = lax.dot_general(qh[h][rows[u]], k, _NT, preferred_element_type=F32)
            (r_in, tot), = suffix_sums(jnp.where(older, log_keep(z), 0.0))
            carry = carry_ref[h, rows[u], :]
            a = jnp.where(older, jnp.exp(z + r_in + carry), 0.0)
            acc_ref[h, rows[u], :] += _dot(a.astype(BF16), v)
            carry = carry + tot
            carry_ref[h, rows[u], :] = carry
            m = jnp.max(carry)
            worst = m if worst is None else jnp.maximum(worst, m)
        return worst

    def cond(state):
        kb, worst = state
        return jnp.logical_and(kb >= 0, worst > SB_SKIP)

    for u in halves:
        lax.while_loop(cond, lambda state, u=u: (state[0] - 1, tile(u, state[0])),
                       ((starts[u] + LANES - 1) // LANES - 1, worst[u]))
    for p in range(groups):
        o_ref[0, :, p * LANES:(p + 1) * LANES] = jnp.where(
            lane < HEAD_DIM, acc_ref[2 * p], acc_ref[2 * p + 1]).astype(BF16)


def _stick_breaking(sb, tri2, batch, seq):
    sb3 = sb.reshape(batch, seq, 3 * SB_W)
    heads = SB_W // HEAD_DIM
    once = pl.Buffered(1)
    out = pl.pallas_call(
        _sb_kernel,
        grid=(batch, seq // TQ_SB),
        in_specs=[
            pl.BlockSpec((1, TQ_SB, SB_W), lambda b, i: (b, i, 0)),
            pl.BlockSpec((1, seq, SB_W), lambda b, i: (b, 0, 1), pipeline_mode=once),
            pl.BlockSpec((1, seq, SB_W), lambda b, i: (b, 0, 2), pipeline_mode=once),
            pl.BlockSpec((LANES, 2 * LANES), lambda b, i: (0, 0)),
        ],
        out_specs=pl.BlockSpec((1, TQ_SB, SB_W), lambda b, i: (b, i, 0)),
        out_shape=jax.ShapeDtypeStruct((batch, seq, SB_W), BF16),
        scratch_shapes=[
            pltpu.VMEM((heads, TQ_SB, LANES), F32),
            pltpu.VMEM((heads, TQ_SB, LANES), F32),
        ],
        compiler_params=_params("parallel", "arbitrary"),
        name="stick_breaking",
    )(sb3, sb3, sb3, tri2)
    return out.reshape(batch * seq, SB_W)


def _ret_kernel(q_ref, k_ref, v_ref, g_ref, cos_ref, sin_ref, dmat_ref, dec_ref, xi_ref, gch_ref,
                swap_ref, avg_ref, ng_ref, o_ref, state_ref, obuf_ref):
    si = pl.program_id(1)
    ts = q_ref.shape[1]
    groups = q_ref.shape[2] // LANES
    cols = [slice(p * LANES, (p + 1) * LANES) for p in range(groups)]

    @pl.when(si == 0)
    def _():
        state_ref[...] = jnp.zeros_like(state_ref)

    cos = cos_ref[...]
    sin = sin_ref[...]

    def lane_mix(x, m_ref):
        hi = x.astype(BF16)
        lo = (x - hi.astype(F32)).astype(BF16)
        return _dot(jnp.concatenate([hi, lo], axis=1), m_ref[...])

    def rot(x):
        return x * cos + lane_mix(x, swap_ref) * sin

    qr = [rot(q_ref[0, :, c]) for c in cols]
    kr = [rot(k_ref[0, :, c]) * (HEAD_DIM ** -0.5) for c in cols]
    v = [v_ref[0, :, c] for c in cols]

    lane_c = lax.broadcasted_iota(jnp.int32, (CHUNK, LANES), 1)
    head0 = lane_c < HEAD_DIM
    r_i = lax.broadcasted_iota(jnp.int32, (LANES, LANES), 0) // HEAD_DIM
    c_i = lax.broadcasted_iota(jnp.int32, (LANES, LANES), 1) // HEAD_DIM
    same_head = r_i == c_i

    chunks = [slice(n * CHUNK, (n + 1) * CHUNK) for n in range(ts // CHUNK)]
    items = [(p, n) for p in range(groups) for n in range(len(chunks))]
    kcb = {(p, n): kr[p][chunks[n]].astype(BF16) for p, n in items}
    vcb = {(p, n): v[p][chunks[n]].astype(BF16) for p, n in items}
    kvs = {(p, n): lax.dot_general(kcb[p, n], (v[p][chunks[n]] * dec_ref[p]).astype(BF16), _TN,
                                   preferred_element_type=F32) for p, n in items}
    scores = {}
    for p, n in items:
        qc = qr[p][chunks[n]]
        qstack = jnp.concatenate([jnp.where(head0, qc, 0.0), jnp.where(head0, 0.0, qc)], axis=0)
        scores[p, n] = (lax.dot_general(qstack.astype(BF16), kcb[p, n], _NT, preferred_element_type=F32)
                        * dmat_ref[p]).astype(BF16)
    o2s = {(p, n): _dot(scores[p, n], vcb[p, n]) for p, n in items}
    states = {}
    for p in range(groups):
        state = state_ref[p]
        for n in range(len(chunks)):
            states[p, n] = state.astype(BF16)
            state = gch_ref[p] * state + jnp.where(same_head, kvs[p, n], 0.0)
        state_ref[p] = state
    for p, n in items:
        o_intra = jnp.where(head0, o2s[p, n][:CHUNK], o2s[p, n][CHUNK:])
        o_inter = _dot((qr[p][chunks[n]] * xi_ref[p]).astype(BF16), states[p, n])
        obuf_ref[chunks[n], cols[p]] = o_intra + o_inter

    for c in cols:
        o = obuf_ref[:, c]
        mu = lane_mix(o, avg_ref)
        oc = o - mu
        var = lane_mix(oc * oc, avg_ref)
        normed = oc * lax.rsqrt(var + LN_EPS) * ng_ref[:, c]
        gate = g_ref[0, :, c]
        o_ref[0, :, c] = (gate / (1.0 + jnp.exp(-gate)) * normed).astype(BF16)


def _retention(ret, consts, norm_g, batch, seq):
    cos, sin, dmat, dec, xi, gch, swap, avg = consts
    ret3 =ret.reshape(batch, seq, 4 * RET_W)
    pairs = RET_W // LANES
    fixed3 = lambda b, i: (0, 0, 0)
    out = pl.pallas_call(
        _ret_kernel,
        grid=(batch, seq // TS_RET),
        in_specs=[
            pl.BlockSpec((1, TS_RET, RET_W), lambda b, i: (b, i, 0)),
            pl.BlockSpec((1, TS_RET, RET_W), lambda b, i: (b, i, 1)),
            pl.BlockSpec((1, TS_RET, RET_W), lambda b, i: (b, i, 2)),
            pl.BlockSpec((1, TS_RET, RET_W), lambda b, i: (b, i, 3)),
            pl.BlockSpec((TS_RET, LANES), lambda b, i: (i, 0)),
            pl.BlockSpec((TS_RET, LANES), lambda b, i: (i, 0)),
            pl.BlockSpec((pairs, 2 * CHUNK, CHUNK), fixed3),
            pl.BlockSpec((pairs, CHUNK, LANES), fixed3),
            pl.BlockSpec((pairs, CHUNK, LANES), fixed3),
            pl.BlockSpec((pairs, 1, LANES), fixed3),
            pl.BlockSpec((2 * LANES, LANES), lambda b, i: (0, 0)),
            pl.BlockSpec((2 * LANES, LANES), lambda b, i: (0, 0)),
            pl.BlockSpec((1, RET_W), lambda b, i: (0, 0)),
        ],
        out_specs=pl.BlockSpec((1, TS_RET, RET_W), lambda b, i: (b, i, 0)),
        out_shape=jax.ShapeDtypeStruct((batch, seq, RET_W), BF16),
        scratch_shapes=[
            pltpu.VMEM((pairs, LANES, LANES), F32),
            pltpu.VMEM((TS_RET, RET_W), F32),
        ],
        compiler_params=_params("parallel", "arbitrary"),
        name="retention",
    )(ret3, ret3, ret3, ret3, cos, sin, dmat, dec, xi, gch, swap, avg, norm_g)
    return out.reshape(batch * seq, RET_W)


def _retention_consts(seq):
    half = HEAD_DIM // 2
    pos = jnp.arange(seq, dtype=F32)
    inv = ROPE_BASE ** (-jnp.arange(half, dtype=F32) / half)
    ang = pos[:, None] * inv[None, :]
    cos_h, sin_h = jnp.cos(ang), jnp.sin(ang)
    cos = jnp.tile(cos_h, (1, 2 * LANES // HEAD_DIM))
    sin = jnp.tile(jnp.concatenate([-sin_h, sin_h], axis=1), (1, LANES // HEAD_DIM))
    log_g = jnp.log(1.0 - 2.0 ** (-5.0 - jnp.arange(RET_HEADS, dtype=F32)))
    c = jnp.arange(CHUNK, dtype=F32)
    dmat = jnp.exp(jnp.abs(c[:, None] - c[None, :])[None] * log_g[:, None, None])
    dmat = dmat.reshape(RET_HEADS // 2, 2 * CHUNK, CHUNK)
    lane_log_g = jnp.repeat(log_g, HEAD_DIM).reshape(RET_HEADS // 2, 1, LANES)
    dec = jnp.exp((CHUNK - 1 - c)[None, :, None] * lane_log_g)
    xi = jnp.exp((c + 1.0)[None, :, None] * lane_log_g)
    gch = jnp.exp(CHUNK * lane_log_g)
    l = np.arange(LANES)
    partner = np.where(l % HEAD_DIM < half, l + half, l - half)
    swap = (l[:, None] == partner[None, :]).astype(np.float32)
    avg = (l[:, None] // HEAD_DIM == l[None, :] // HEAD_DIM).astype(np.float32) / HEAD_DIM
    swap = jnp.asarray(np.concatenate([swap, swap], axis=0), BF16)
    avg = jnp.asarray(np.concatenate([avg, avg], axis=0), BF16)
    return cos, sin, dmat, dec, xi, gch, swap, avg


def _layer_norm(z, g, b):
    mu = jnp.mean(z, axis=-1, keepdims=True)
    zc = z - mu
    var = jnp.mean(zc * zc, axis=-1, keepdims=True)
    return zc * lax.rsqrt(var + LN_EPS) * g + b


def _outproj_kernel(x_ref, p_ref, s_ref, r_ref, w_ref, b_ref, g_ref, be_ref, rw_ref, rb_ref,
                    x1_ref, x1b_ref, gates_ref):
    o1 = POOL_W + SB_W
    tm = x_ref.shape[0]
    subs = [slice(s, s + OUTPROJ_SUB) for s in range(0, tm, OUTPROJ_SUB)]
    ys = [_dot(p_ref[sl, :], w_ref[:POOL_W, :]) + _dot(s_ref[sl, :], w_ref[POOL_W:o1, :])
          + _dot(r_ref[sl, :], w_ref[o1:, :]) for sl in subs]
    x1s = [_layer_norm(DN_ALPHA * x_ref[sl, :] + (y + b_ref[...]), g_ref[...], be_ref[...])
           for sl, y in zip(subs, ys)]
    lane = lax.broadcasted_iota(jnp.int32, (OUTPROJ_SUB, LANES), 1).astype(F32)
    logits = []
    for sl, x1 in zip(subs, x1s):
        x1_ref[sl, :] = x1
        x_hi = x1.astype(BF16)
        x1b_ref[sl, :] = x_hi
        x_mid = (x1 - x_hi.astype(F32)).astype(BF16)
        t = _dot(x_hi, rw_ref[...])
        logits.append(t[:, :LANES] + t[:, LANES:] + _dot(x_mid, rw_ref[:, :LANES]) + rb_ref[...])
    for sl, lg in zip(subs, logits):
        vals = jnp.where(lane < N_EXPERTS, lg, -jnp.inf)
        top_v, top_sel = [], []
        for _ in range(TOP_K):
            m = jnp.max(vals, axis=-1, keepdims=True)
            idx = jnp.min(jnp.where(vals == m, lane, float(LANES)), axis=-1, keepdims=True)
            sel = lane == idx
            vals = jnp.where(sel, -jnp.inf, vals)
            top_v.append(m)
            top_sel.append(sel)
        ex = [jnp.exp(m - top_v[0]) for m in top_v]
        den = ex[0] + ex[1] + ex[2] + ex[3]
        gates = jnp.full((OUTPROJ_SUB, LANES), -1.0, F32)
        for sel, e in zip(top_sel, ex):
            gates = jnp.where(sel, e / den, gates)
        gates_ref[sl, :] = gates


def _outproj(x2, pool_o, sb_o, ret_o, w_bf, b, g, be, rw, rb):
    n = x2.shape[0]
    row = lambda i: (i, 0)
    fixed = lambda i: (0, 0)
    return pl.pallas_call(
        _outproj_kernel,
        grid=(n // TM_ROW,),
        in_specs=[
            pl.BlockSpec((TM_ROW, D_MODEL), row),
            pl.BlockSpec((TM_ROW, POOL_W), row),
            pl.BlockSpec((TM_ROW, SB_W), row),
            pl.BlockSpec((TM_ROW, RET_W), row),
            pl.BlockSpec((D_MODEL, D_MODEL), fixed),
            pl.BlockSpec((1, D_MODEL), fixed),
            pl.BlockSpec((1, D_MODEL), fixed),
            pl.BlockSpec((1, D_MODEL), fixed),
            pl.BlockSpec((D_MODEL, 2 * LANES), fixed),
            pl.BlockSpec((1, LANES), fixed),
        ],
        out_specs=[
            pl.BlockSpec((TM_ROW, D_MODEL), row),
            pl.BlockSpec((TM_ROW, D_MODEL), row),
            pl.BlockSpec((TM_ROW, LANES), row),
        ],
        out_shape=[
            jax.ShapeDtypeStruct((n, D_MODEL), F32),
            jax.ShapeDtypeStruct((n, D_MODEL), BF16),
            jax.ShapeDtypeStruct((n, LANES), F32),
        ],
        compiler_params=_params("parallel"),
        name="outproj_ln_router",
    )(x2, pool_o, sb_o, ret_o, w_bf, b, g, be, rw, rb)


def _dispatch_kernel(gates_ref, xb_ref, tril_ref, xs_ref, gs_ref, pm_ref, post_ref, gatet_ref, cnt_ref):
    tm = gates_ref.shape[0]
    gates = gates_ref[...]
    sel = gates >= 0.0
    self_ = jnp.where(sel, 1.0, 0.0)
    incl = _dot(tril_ref[...], self_.astype(BF16))
    pos = incl - self_
    cnt_ref[0] = incl[tm - 1:tm, :].astype(jnp.int32)
    pm_ref[...] = jnp.where(jnp.logical_and(sel, pos < RC_MOE), pos, -1.0).astype(BF16)
    post = jnp.where(sel, pos, -1.0).T[:N_EXPERTS, :]
    gatet = gates.T[:N_EXPERTS, :]
    post_ref[0] = post
    gatet_ref[0] = gatet
    post_i = post.astype(jnp.int32)
    slot = lax.broadcasted_iota(jnp.int32, (RC_MOE, tm), 0)
    for e0 in range(0, N_EXPERTS, DISPATCH_GROUP):
        pieces = []
        for e in range(e0, e0 + DISPATCH_GROUP):
            onehot = slot == post_i[e:e + 1, :]
            pieces.append(jnp.where(onehot, 1.0, 0.0).astype(BF16))
            gate = jnp.sum(jnp.where(onehot, gatet[e:e + 1, :], 0.0), axis=1, keepdims=True)
            gs_ref[0, e] = jnp.broadcast_to(gate, (RC_MOE, LANES))
        xs = _dot(jnp.concatenate(pieces, axis=0), xb_ref[...]).astype(BF16)
        xs_ref[0, e0:e0 + DISPATCH_GROUP] = xs.reshape(DISPATCH_GROUP, RC_MOE, D_MODEL)


def _dispatch(gates, x1b, tril):
    n = gates.shape[0]
    nt = n // TM_MOE
    return pl.pallas_call(
        _dispatch_kernel,
        grid=(nt,),
        in_specs=[
            pl.BlockSpec((TM_MOE, LANES), lambda i: (i, 0)),
            pl.BlockSpec((TM_MOE, D_MODEL), lambda i: (i, 0)),
            pl.BlockSpec((TM_MOE, TM_MOE), lambda i: (0, 0)),
        ],
        out_specs=[
            pl.BlockSpec((1, N_EXPERTS, RC_MOE, D_MODEL), lambda i: (i, 0, 0, 0)),
            pl.BlockSpec((1, N_EXPERTS, RC_MOE, LANES), lambda i: (i, 0, 0, 0)),
            pl.BlockSpec((TM_MOE, LANES), lambda i: (i, 0)),
            pl.BlockSpec((1, N_EXPERTS, TM_MOE), lambda i: (i, 0, 0)),
            pl.BlockSpec((1, N_EXPERTS, TM_MOE), lambda i: (i, 0, 0)),
            pl.BlockSpec((1, 1, LANES), lambda i: (i, 0, 0)),
        ],
        out_shape=[
            jax.ShapeDtypeStruct((nt, N_EXPERTS, RC_MOE, D_MODEL), BF16),
            jax.ShapeDtypeStruct((nt, N_EXPERTS, RC_MOE, LANES), F32),
            jax.ShapeDtypeStruct((n, LANES), BF16),
            jax.ShapeDtypeStruct((nt, N_EXPERTS, TM_MOE), F32),
            jax.ShapeDtypeStruct((nt, N_EXPERTS, TM_MOE), F32),
            jax.ShapeDtypeStruct((nt, 1, LANES), jnp.int32),
        ],
        compiler_params=_params("parallel"),
        name="dispatch",
    )(gates, x1b, tril)


def _swiglu_ffn(x, wgu, bgu, wd, bd):
    h = _dot(x, wgu) + bgu
    g = jnp.minimum(h[:, :D_FF], SWIGLU_LIMIT)
    up = jnp.clip(h[:, D_FF:], -SWIGLU_LIMIT, SWIGLU_LIMIT)
    act = (up + 1.0) * (g / (1.0 + jnp.exp(-SWIGLU_ALPHA * g)))
    return _dot(act.astype(BF16), wd) + bd


def _ffn_kernel(tail_ref, xs_ref, gs_ref, wgu_ref, bgu_ref, wd_ref, bd_ref, ys_ref, wgu_bf_ref, wd_bf_ref):
    e = pl.program_id(0)
    j = pl.program_id(1)

    @pl.when(j == 0)
    def _():
        wgu_bf_ref[0] = wgu_ref[0, 0].astype(BF16)
        wd_bf_ref[0] = wd_ref[0, 0].astype(BF16)

    def run(lo, hi):
        rows = FFN_TILES * (hi - lo)
        x = xs_ref[:, 0, lo:hi, :].reshape(rows, D_MODEL)
        y = _swiglu_ffn(x, wgu_bf_ref[0], bgu_ref[0], wd_bf_ref[0], bd_ref[0])
        gate = gs_ref[:, 0, lo:hi, :].reshape(rows, LANES)[:, :1]
        ys_ref[:, 0, lo:hi, :] = (y * gate).astype(BF16).reshape(FFN_TILES, hi - lo, D_MODEL)

    run(0, RC_MAIN)
    used = tail_ref[e * pl.num_programs(1) + j] > 0

    @pl.when(used)
    def _():
        run(RC_MAIN, RC_MOE)

    @pl.when(jnp.logical_not(used))
    def _():
        ys_ref[:, 0, RC_MAIN:RC_MOE, :] = jnp.zeros((FFN_TILES, RC_MOE - RC_MAIN, D_MODEL), BF16)


def _ffn(tail, xs, gs, w_gate_up, bgu, w_down, bd, layer):
    nt = xs.shape[0]
    grid_spec = pltpu.PrefetchScalarGridSpec(
        num_scalar_prefetch=1,
        grid=(N_EXPERTS, nt // FFN_TILES),
        in_specs=[
            pl.BlockSpec((FFN_TILES, 1, RC_MOE, D_MODEL), lambda e, j, t: (j, e, 0, 0)),
            pl.BlockSpec((FFN_TILES, 1, RC_MOE, LANES), lambda e, j, t: (j, e, 0, 0)),
            pl.BlockSpec((1, 1, D_MODEL, 2 * D_FF), lambda e, j, t: (layer, e, 0, 0)),
            pl.BlockSpec((1, 1, 2 * D_FF), lambda e, j, t: (e, 0, 0)),
            pl.BlockSpec((1, 1, D_FF, D_MODEL), lambda e, j, t: (layer, e, 0, 0)),
            pl.BlockSpec((1, 1, D_MODEL), lambda e, j, t: (e, 0, 0)),
        ],
        out_specs=[
            pl.BlockSpec((FFN_TILES, 1, RC_MOE, D_MODEL), lambda e, j, t: (j, e, 0, 0)),
            pl.BlockSpec((1, D_MODEL, 2 * D_FF), lambda e, j, t: (e, 0, 0)),
            pl.BlockSpec((1, D_FF, D_MODEL), lambda e, j, t: (e, 0, 0)),
        ],
    )
    return pl.pallas_call(
        _ffn_kernel,
        grid_spec=grid_spec,
        out_shape=[
            jax.ShapeDtypeStruct(xs.shape, BF16),
            jax.ShapeDtypeStruct((N_EXPERTS, D_MODEL, 2 * D_FF), BF16),
            jax.ShapeDtypeStruct((N_EXPERTS, D_FF, D_MODEL), BF16),
        ],
        compiler_params=_params("parallel", "arbitrary"),
        name="expert_ffn",
    )(tail, xs, gs, w_gate_up, bgu, w_down, bd)


def _combine_kernel(cnt_ref, ys_ref, pm_ref, pick_ref, rcol_ref, x1_ref, g_ref, b_ref, xb_ref, post_ref,
                    gatet_ref, bgu_ref, bd_ref, wgu_hbm, wd_hbm, o_ref, acc_ref, wgu_buf, wd_buf, sem):
    i = pl.program_id(0)
    tm = pm_ref.shape[0]
    width = DISPATCH_GROUP * RC_MOE
    acc = None
    for e0 in range(0, N_EXPERTS, DISPATCH_GROUP):
        sl = slice(e0 * RC_MOE, e0 * RC_MOE + width)
        ranks = _dot(pm_ref[...], pick_ref[:, sl])
        back = jnp.where(ranks == rcol_ref[:, sl], 1.0, 0.0).astype(BF16)
        part = _dot(back, ys_ref[0, e0:e0 + DISPATCH_GROUP].reshape(width, D_MODEL))
        acc = part if acc is None else acc + part
    acc_ref[...] = acc

    slot = lax.broadcasted_iota(jnp.int32, (RC_MOE, tm), 0)

    def per_expert(e, carry):
        cnt = cnt_ref[i * N_EXPERTS + e]

        @pl.when(cnt > RC_MOE)
        def _():
            copies = (pltpu.make_async_copy(wgu_hbm.at[e], wgu_buf, sem.at[0]),
                      pltpu.make_async_copy(wd_hbm.at[e], wd_buf, sem.at[1]))
            for cp in copies:
                cp.start()
            for cp in copies:
                cp.wait()
            prow = post_ref[0, pl.ds(e, 1), :].astype(jnp.int32)
            grow = gatet_ref[0, pl.ds(e, 1), :]

            def chunk(c, carry2):
                onehot = (slot + c * RC_MOE) == prow
                oh = jnp.where(onehot, 1.0, 0.0).astype(BF16)
                xg = _dot(oh, xb_ref[...]).astype(BF16)
                y = _swiglu_ffn(xg, wgu_buf[...], bgu_ref[e], wd_buf[...], bd_ref[e])
                gate = jnp.sum(jnp.where(onehot, grow, 0.0), axis=1, keepdims=True)
                yg = (y * gate).astype(BF16)
                acc_ref[...] += lax.dot_general(oh, yg, _TN, preferred_element_type=F32)
                return carry2

            lax.fori_loop(1, (cnt + RC_MOE - 1) // RC_MOE, chunk, 0)

        return carry

    lax.fori_loop(0, N_EXPERTS, per_expert, 0)
    o_ref[...] = _layer_norm(DN_ALPHA * x1_ref[...] + acc_ref[...], g_ref[...], b_ref[...])


def _combine(cnt, ys, pm, pick, rcol, x1, g, b, x1b, post, gatet, bgu, bd, wgu, wd):
    n = x1.shape[0]
    nt = n // TM_MOE
    row = lambda i, c: (i, 0)
    fixed2 = lambda i, c: (0, 0)
    fixed3 = lambda i, c: (0, 0, 0)
    grid_spec = pltpu.PrefetchScalarGridSpec(
        num_scalar_prefetch=1,
        grid=(nt,),
        in_specs=[
            pl.BlockSpec((1, N_EXPERTS, RC_MOE, D_MODEL), lambda i, c: (i, 0, 0, 0)),
            pl.BlockSpec((TM_MOE, LANES), row),
            pl.BlockSpec((LANES, SLOTS), fixed2),
            pl.BlockSpec((1, SLOTS), fixed2),
            pl.BlockSpec((TM_MOE, D_MODEL), row),
            pl.BlockSpec((1, D_MODEL), fixed2),
            pl.BlockSpec((1, D_MODEL), fixed2),
            pl.BlockSpec((TM_MOE, D_MODEL), row),
            pl.BlockSpec((1, N_EXPERTS, TM_MOE), lambda i, c: (i, 0, 0)),
            pl.BlockSpec((1, N_EXPERTS, TM_MOE), lambda i, c: (i, 0, 0)),
            pl.BlockSpec((N_EXPERTS, 1, 2 * D_FF), fixed3),
            pl.BlockSpec((N_EXPERTS, 1, D_MODEL), fixed3),
            pl.BlockSpec(memory_space=pl.ANY),
            pl.BlockSpec(memory_space=pl.ANY),
        ],
        out_specs=pl.BlockSpec((TM_MOE, D_MODEL), row),
        scratch_shapes=[
            pltpu.VMEM((TM_MOE, D_MODEL), F32),
            pltpu.VMEM((D_MODEL, 2 * D_FF), BF16),
            pltpu.VMEM((D_FF, D_MODEL), BF16),
            pltpu.SemaphoreType.DMA((2,)),
        ],
    )
    return pl.pallas_call(
        _combine_kernel,
        grid_spec=grid_spec,
        out_shape=jax.ShapeDtypeStruct((n, D_MODEL), F32),
        compiler_params=_params("arbitrary"),
        name="combine_ln",
    )(cnt, ys, pm, pick, rcol, x1, g, b, x1b, post, gatet, bgu, bd, wgu, wd)


def _block_diag(pool_w):
    groups = pool_w.shape[0]
    out = jnp.zeros((POOL_W, POOL_W), pool_w.dtype)
    for gi in range(groups):
        out = out.at[gi * POOL_CH:(gi + 1) * POOL_CH, gi * POOL_CH:(gi + 1) * POOL_CH].set(pool_w[gi])
    return out


def _layer(x2, batch, seq, consts, layer, w_in, b_in, pool_w, pool_scale, ret_norm_g, w_out, b_out, ln1_g,
           ln1_b, router_w, router_b, w_gate_up, b_gate_up, w_down, b_down, ln2_g, ln2_b):
    ret_consts, tri2, tril, pick, rcol = consts
    row = lambda a: a.reshape(1, -1).astype(F32)
    pool_o, sb, ret = _inproj(x2, w_in.astype(BF16), row(b_in), _block_diag(pool_w).astype(BF16),
                              row(pool_scale), seq)
    sb_o = _stick_breaking(sb, tri2, batch, seq)
    ret_o = _retention(ret, ret_consts, row(ret_norm_g), batch, seq)
    rw = jnp.pad(router_w.astype(F32), ((0, 0), (0, LANES - N_EXPERTS)))
    rw_hi = rw.astype(BF16)
    rw = jnp.concatenate([rw_hi, (rw - rw_hi.astype(F32)).astype(BF16)], axis=1)
    rb = jnp.pad(router_b.astype(F32), (0, LANES - N_EXPERTS)).reshape(1, LANES)
    x1, x1b, gates = _outproj(x2, pool_o, sb_o, ret_o, w_out.astype(BF16), row(b_out), row(ln1_g),
                              row(ln1_b), rw, rb)
    xs, gs, pm, post, gatet, cnt = _dispatch(gates, x1b, tril)
    bgu = b_gate_up.reshape(N_EXPERTS, 1, 2 * D_FF).astype(F32)
    bd = b_down.reshape(N_EXPERTS, 1, D_MODEL).astype(F32)
    cnt2 = cnt[:, 0, :N_EXPERTS]
    tail = jnp.any((cnt2 > RC_MAIN).reshape(-1, FFN_TILES, N_EXPERTS), axis=1)
    ys, wgu, wd = _ffn(tail.T.reshape(-1).astype(jnp.int32), xs, gs, w_gate_up, bgu, w_down, bd, layer)
    cnt_flat = cnt2.reshape(-1)
    return _combine(cnt_flat, ys, pm, pick, rcol, x1, row(ln2_g), row(ln2_b), x1b, post, gatet, bgu, bd,
                    wgu, wd)


def kernel(x, w_in, b_in, pool_w, pool_scale, ret_norm_g, w_out, b_out, ln1_g, ln1_b, router_w, router_b,
           w_gate_up, b_gate_up, w_down, b_down, ln2_g, ln2_b):
    batch, seq, d = x.shape
    n = batch * seq
    assert d == D_MODEL and seq % TS_RET == 0 and seq % TQ_SB == 0 and seq % TM_ROW == 0
    assert n % (TM_MOE * FFN_TILES) == 0 and n % TM_ROW == 0 and seq >= SB_SPAN
    j = np.arange(LANES)
    tri = (j[:, None] >= j[None, :]).astype(np.float32)
    tri2 = jnp.asarray(np.concatenate([tri, np.ones_like(tri)], axis=1), BF16)
    r = np.arange(TM_MOE)
    tril = jnp.asarray((r[:, None] >= r[None, :]).astype(np.float32), BF16)
    s = np.arange(SLOTS)
    pick = jnp.asarray((np.arange(LANES)[:, None] == (s // RC_MOE)[None, :]).astype(np.float32), BF16)
    rcol = jnp.asarray((s % RC_MOE).astype(np.float32).reshape(1, SLOTS))
    consts = (_retention_consts(seq), tri2, tril, pick, rcol)
    x2 = x.reshape(n, d)
    for l in range(DEPTH):
        x2 = _layer(x2, batch, seq, consts, l, w_in[l], b_in[l], pool_w[l], pool_scale[l], ret_norm_g[l],
                    w_out[l], b_out[l], ln1_g[l], ln1_b[l], router_w[l], router_b[l], w_gate_up,
                    b_gate_up[l], w_down, b_down[l], ln2_g[l], ln2_b[l])
    return x2.reshape(batch, seq, d)
```

```python
import functools

import numpy as np
import jax
import jax.numpy as jnp
from jax import lax
from jax.experimental import pallas as pl
from jax.experimental.pallas import tpu as pltpu

D_MODEL = 1024
DEPTH = 2
CHUNK = 64
HEAD_DIM = 64
POOL_CH = 64
POOL_W = 256
POOL_HALO = 16
SB_W = 384
RET_W = 384
RET_HEADS = 6
IN_W = POOL_W + 3 * SB_W + 4 * RET_W
ROPE_BASE = 10000.0
N_EXPERTS = 32
TOP_K = 4
D_FF = D_MODEL
SWIGLU_LIMIT = 7.0
SWIGLU_ALPHA = 1.702
DN_ALPHA = (2.0 * DEPTH) ** 0.25
LN_EPS = 1e-5

LANES = 128
VMEM_LIMIT = 56 * 1024 * 1024

TM_ROW = 1024
OUTPROJ_SUB = 256
TQ_SB = 512
SB_ROWS = 64
SB_SPAN = 256
TS_RET = 1024
TM_MOE = 512
RC_MOE = 96
RC_MAIN = 80
FFN_TILES = 8
SLOTS = N_EXPERTS * RC_MOE
DISPATCH_GROUP = 8
SB_SKIP = -100.0

BF16 = jnp.bfloat16
F32 = jnp.float32

_NT = (((1,), (1,)), ((), ()))
_TN = (((0,), (0,)), ((), ()))


def _dot(a, b):
    return jnp.dot(a, b, preferred_element_type=F32)


def _params(*sem):
    return pltpu.CompilerParams(dimension_semantics=sem, vmem_limit_bytes=VMEM_LIMIT)


def _pool_mix(cur, halo, t0, w, scale):
    ts = cur.shape[0]
    ext = jnp.concatenate([halo, cur], axis=0)
    a2 = ext[1:] + ext[:-1]
    a4 = a2[2:] + a2[:-2]
    a8 = a4[4:] + a4[:-4]
    a16 = a8[8:] + a8[:-8]
    lane = lax.broadcasted_iota(jnp.int32, (ts, POOL_W), 1)
    grp = lane // POOL_CH
    win = jnp.where(grp == 0, a2[15:15 + ts],
                    jnp.where(grp == 1, a4[13:13 + ts],
                              jnp.where(grp == 2, a8[9:9 + ts], a16[1:1 + ts])))
    width = jnp.where(grp == 0, 2, jnp.where(grp == 1, 4, jnp.where(grp == 2, 8, 16)))
    t = t0 + lax.broadcasted_iota(jnp.int32, (ts, POOL_W), 0)
    cnt = jnp.minimum(t + 1, width).astype(F32)
    pooled = win / cnt - cur
    return _dot(pooled.astype(BF16), w) * scale


def _inproj_kernel(tiles_per_seq, x_ref, w_ref, b_ref, pw_ref, ps_ref, pool_ref, sb_ref, ret_ref, halo_ref):
    it = pl.program_id(0) % tiles_per_seq
    xb = x_ref[...].astype(BF16)
    o1 = POOL_W + 3 * SB_W
    @pl.when(it == 0)
    def _():
        halo_ref[...] = jnp.zeros_like(halo_ref)

    u = _dot(xb, w_ref[:, :POOL_W]) + b_ref[:, :POOL_W]
    sb_ref[...] = (_dot(xb, w_ref[:, POOL_W:o1]) + b_ref[:, POOL_W:o1]).astype(BF16)
    ret_ref[...] = _dot(xb, w_ref[:, o1:]) + b_ref[:, o1:]
    pool_ref[...] = _pool_mix(u, halo_ref[...], it * TM_ROW, pw_ref[...], ps_ref[...]).astype(BF16)
    halo_ref[...] = u[TM_ROW - POOL_HALO:, :]


def _inproj(x2, w_bf, b, pool_w_bd, pool_scale, seq):
    n = x2.shape[0]
    fixed = lambda i: (0, 0)
    return pl.pallas_call(
        functools.partial(_inproj_kernel, seq // TM_ROW),
        grid=(n // TM_ROW,),
        in_specs=[
            pl.BlockSpec((TM_ROW, D_MODEL), lambda i: (i, 0)),
            pl.BlockSpec((D_MODEL, IN_W), fixed),
            pl.BlockSpec((1, IN_W), fixed),
            pl.BlockSpec((POOL_W, POOL_W), fixed),
            pl.BlockSpec((1, POOL_W), fixed),
        ],
        out_specs=[
            pl.BlockSpec((TM_ROW, POOL_W), lambda i: (i, 0)),
            pl.BlockSpec((TM_ROW, 3 * SB_W), lambda i: (i, 0)),
            pl.BlockSpec((TM_ROW, 4 * RET_W), lambda i: (i, 0)),
        ],
        out_shape=[
            jax.ShapeDtypeStruct((n, POOL_W), BF16),
            jax.ShapeDtypeStruct((n, 3 * SB_W), BF16),
            jax.ShapeDtypeStruct((n, 4 * RET_W), F32),
        ],
        scratch_shapes=[pltpu.VMEM((POOL_HALO, POOL_W), F32)],
        compiler_params=_params("arbitrary"),
        name="inproj_pool",
    )(x2, w_bf, b, pool_w_bd, pool_scale)


def _sb_kernel(q_ref, k_ref, v_ref, tri_ref, o_ref, acc_ref, carry_ref):
    qi = pl.program_id(1)
    tq = q_ref.shape[1]
    groups = q_ref.shape[2] // LANES
    lane = lax.broadcasted_iota(jnp.int32, (tq, LANES), 1)
    qh = []
    for p in range(groups):
        q = q_ref[0, :, p * LANES:(p + 1) * LANES] * jnp.asarray(HEAD_DIM ** -0.5, BF16)
        zero = jnp.zeros_like(q)
        qh += [jnp.where(lane < HEAD_DIM, q, zero), jnp.where(lane >= HEAD_DIM, q, zero)]

    def log_keep(z):
        return jnp.minimum(-z, 0.0) - jnp.log(1.0 + jnp.exp(-jnp.abs(z)))

    def suffix_sums(lk):
        lk = lk.astype(BF16)
        out = []
        for b in range(lk.shape[1] // LANES):
            rs = _dot(lk[:, b * LANES:(b + 1) * LANES], tri_ref[...])
            out.append((rs[:, :LANES], rs[:, LANES:]))
        return out

    t0 = qi * tq
    halves = range(tq // SB_ROWS)
    heads = range(2 * groups)
    cols = [slice((h // 2) * LANES, (h // 2 + 1) * LANES) for h in heads]
    rows = [slice(u * SB_ROWS, (u + 1) * SB_ROWS) for u in halves]
    starts = [pl.multiple_of(jnp.maximum(t0 + (u + 1) * SB_ROWS - SB_SPAN, 0), SB_ROWS) for u in halves]
    below = []
    for u in halves:
        qpos = t0 + u * SB_ROWS + lax.broadcasted_iota(jnp.int32, (SB_ROWS, SB_SPAN), 0)
        kpos = starts[u] + lax.broadcasted_iota(jnp.int32, (SB_ROWS, SB_SPAN), 1)
        below.append(kpos < qpos)
    items = [(u, h) for u in halves for h in heads]
    zs = {(u, h): lax.dot_general(qh[h][rows[u]], k_ref[0, pl.ds(starts[u], SB_SPAN), cols[h]], _NT,
                                  preferred_element_type=F32) for u, h in items}
    sums = {(u, h): suffix_sums(jnp.where(below[u], log_keep(zs[u, h]), 0.0)) for u, h in items}
    weights, runs = {}, {}
    for u, h in items:
        run = None
        r = [None] * len(sums[u, h])
        for b in reversed(range(len(r))):
            r_in, tot = sums[u, h][b]
            r[b] = r_in if run is None else r_in + run
            run = tot if run is None else run + tot
        weights[u, h] = jnp.where(below[u], jnp.exp(zs[u, h] + jnp.concatenate(r, axis=1)), 0.0).astype(BF16)
        runs[u, h] = run
    for u, h in items:
        acc_ref[h, rows[u], :] = _dot(weights[u, h], v_ref[0, pl.ds(starts[u], SB_SPAN), cols[h]])
        carry_ref[h, rows[u], :] = runs[u, h]
    worst = []
    for u in halves:
        top = runs[u, 0]
        for h in heads[1:]:
            top = jnp.maximum(top, runs[u, h])
        worst.append(jnp.max(top))

    def tile(u, kb):
        start = pl.multiple_of(kb * LANES, LANES)
        older = (start + lax.broadcasted_iota(jnp.int32, (SB_ROWS, LANES), 1)) < starts[u]
        worst = None
        for h in heads:
            k = k_ref[0, pl.ds(start, LANES), cols[h]]
            v = v_ref[0, pl.ds(start, LANES), cols[h]]
            z = lax.dot_general(qh[h][rows[u]], k, _NT, preferred_element_type=F32)
            (r_in, tot), = suffix_sums(jnp.where(older, log_keep(z), 0.0))
            carry = carry_ref[h, rows[u], :]
            a = jnp.where(older, jnp.exp(z + r_in + carry), 0.0)
            acc_ref[h, rows[u], :] += _dot(a.astype(BF16), v)
            carry = carry + tot
            carry_ref[h, rows[u], :] = carry
            m = jnp.max(carry)
            worst = m if worst is None else jnp.maximum(worst, m)
        return worst

    def cond(state):
        kb, worst = state
        return jnp.logical_and(kb >= 0, worst > SB_SKIP)

    for u in halves:
        lax.while_loop(cond, lambda state, u=u: (state[0] - 1, tile(u, state[0])),
                       ((starts[u] + LANES - 1) // LANES - 1, worst[u]))
    for p in range(groups):
        o_ref[0, :, p * LANES:(p + 1) * LANES] = jnp.where(
            lane < HEAD_DIM, acc_ref[2 * p], acc_ref[2 * p + 1]).astype(BF16)


def _stick_breaking(sb, tri2, batch, seq):
    sb3 = sb.reshape(batch, seq, 3 * SB_W)
    heads = SB_W // HEAD_DIM
    once = pl.Buffered(1)
    out = pl.pallas_call(
        _sb_kernel,
        grid=(batch, seq // TQ_SB),
        in_specs=[
            pl.BlockSpec((1, TQ_SB, SB_W), lambda b, i: (b, i, 0)),
            pl.BlockSpec((1, seq, SB_W), lambda b, i: (b, 0, 1), pipeline_mode=once),
            pl.BlockSpec((1, seq, SB_W), lambda b, i: (b, 0, 2), pipeline_mode=once),
            pl.BlockSpec((LANES, 2 * LANES), lambda b, i: (0, 0)),
        ],
        out_specs=pl.BlockSpec((1, TQ_SB, SB_W), lambda b, i: (b, i, 0)),
        out_shape=jax.ShapeDtypeStruct((batch, seq, SB_W), BF16),
        scratch_shapes=[
            pltpu.VMEM((heads, TQ_SB, LANES), F32),
            pltpu.VMEM((heads, TQ_SB, LANES), F32),
        ],
        compiler_params=_params("parallel", "arbitrary"),
        name="stick_breaking",
    )(sb3, sb3, sb3, tri2)
    return out.reshape(batch * seq, SB_W)


def _ret_kernel(q_ref, k_ref, v_ref, g_ref, cos_ref, sin_ref, dmat_ref, dec_ref, xi_ref, gch_ref,
                swap_ref, avg_ref, ng_ref, o_ref, state_ref, obuf_ref):
    si = pl.program_id(1)
    ts = q_ref.shape[1]
    groups = q_ref.shape[2] // LANES
    cols = [slice(p * LANES, (p + 1) * LANES) for p in range(groups)]

    @pl.when(si == 0)
    def _():
        state_ref[...] = jnp.zeros_like(state_ref)

    cos = cos_ref[...]
    sin = sin_ref[...]

    def lane_mix(x, m_ref):
        hi = x.astype(BF16)
        lo = (x - hi.astype(F32)).astype(BF16)
        return _dot(jnp.concatenate([hi, lo], axis=1), m_ref[...])

    def rot(x):
        return x * cos + lane_mix(x, swap_ref) * sin

    qr = [rot(q_ref[0, :, c]) for c in cols]
    kr = [rot(k_ref[0, :, c]) * (HEAD_DIM ** -0.5) for c in cols]
    v = [v_ref[0, :, c] for c in cols]

    lane_c = lax.broadcasted_iota(jnp.int32, (CHUNK, LANES), 1)
    head0 = lane_c < HEAD_DIM
    r_i = lax.broadcasted_iota(jnp.int32, (LANES, LANES), 0) // HEAD_DIM
    c_i = lax.broadcasted_iota(jnp.int32, (LANES, LANES), 1) // HEAD_DIM
    same_head = r_i == c_i

    chunks = [slice(n * CHUNK, (n + 1) * CHUNK) for n in range(ts // CHUNK)]
    items = [(p, n) for p in range(groups) for n in range(len(chunks))]
    kcb = {(p, n): kr[p][chunks[n]].astype(BF16) for p, n in items}
    vcb = {(p, n): v[p][chunks[n]].astype(BF16) for p, n in items}
    kvs = {(p, n): lax.dot_general(kcb[p, n], (v[p][chunks[n]] * dec_ref[p]).astype(BF16), _TN,
                                   preferred_element_type=F32) for p, n in items}
    scores = {}
    for p, n in items:
        qc = qr[p][chunks[n]]
        qstack = jnp.concatenate([jnp.where(head0, qc, 0.0), jnp.where(head0, 0.0, qc)], axis=0)
        scores[p, n] = (lax.dot_general(qstack.astype(BF16), kcb[p, n], _NT, preferred_element_type=F32)
                        * dmat_ref[p]).astype(BF16)
    o2s = {(p, n): _dot(scores[p, n], vcb[p, n]) for p, n in items}
    states = {}
    for p in range(groups):
        state = state_ref[p]
        for n in range(len(chunks)):
            states[p, n] = state.astype(BF16)
            state = gch_ref[p] * state + jnp.where(same_head, kvs[p, n], 0.0)
        state_ref[p] = state
    for p, n in items:
        o_intra = jnp.where(head0, o2s[p, n][:CHUNK], o2s[p, n][CHUNK:])
        o_inter = _dot((qr[p][chunks[n]] * xi_ref[p]).astype(BF16), states[p, n])
        obuf_ref[chunks[n], cols[p]] = o_intra + o_inter

    for c in cols:
        o = obuf_ref[:, c]
        mu = lane_mix(o, avg_ref)
        oc = o - mu
        var = lane_mix(oc * oc, avg_ref)
        normed = oc * lax.rsqrt(var + LN_EPS) * ng_ref[:, c]
        gate = g_ref[0, :, c]
        o_ref[0, :, c] = (gate / (1.0 + jnp.exp(-gate)) * normed).astype(BF16)


def _retention(ret, consts, norm_g, batch, seq):
    cos, sin, dmat, dec, xi, gch, swap, avg = consts
    ret3 =ret.reshape(batch, seq, 4 * RET_W)
    pairs = RET_W // LANES
    fixed3 = lambda b, i: (0, 0, 0)
    out = pl.pallas_call(
        _ret_kernel,
        grid=(batch, seq // TS_RET),
        in_specs=[
            pl.BlockSpec((1, TS_RET, RET_W), lambda b, i: (b, i, 0)),
            pl.BlockSpec((1, TS_RET, RET_W), lambda b, i: (b, i, 1)),
            pl.BlockSpec((1, TS_RET, RET_W), lambda b, i: (b, i, 2)),
            pl.BlockSpec((1, TS_RET, RET_W), lambda b, i: (b, i, 3)),
            pl.BlockSpec((TS_RET, LANES), lambda b, i: (i, 0)),
            pl.BlockSpec((TS_RET, LANES), lambda b, i: (i, 0)),
            pl.BlockSpec((pairs, 2 * CHUNK, CHUNK), fixed3),
            pl.BlockSpec((pairs, CHUNK, LANES), fixed3),
            pl.BlockSpec((pairs, CHUNK, LANES), fixed3),
            pl.BlockSpec((pairs, 1, LANES), fixed3),
            pl.BlockSpec((2 * LANES, LANES), lambda b, i: (0, 0)),
            pl.BlockSpec((2 * LANES, LANES), lambda b, i: (0, 0)),
            pl.BlockSpec((1, RET_W), lambda b, i: (0, 0)),
        ],
        out_specs=pl.BlockSpec((1, TS_RET, RET_W), lambda b, i: (b, i, 0)),
        out_shape=jax.ShapeDtypeStruct((batch, seq, RET_W), BF16),
        scratch_shapes=[
            pltpu.VMEM((pairs, LANES, LANES), F32),
            pltpu.VMEM((TS_RET, RET_W), F32),
        ],
        compiler_params=_params("parallel", "arbitrary"),
        name="retention",
    )(ret3, ret3, ret3, ret3, cos, sin, dmat, dec, xi, gch, swap, avg, norm_g)
    return out.reshape(batch * seq, RET_W)


def _retention_consts(seq):
    half = HEAD_DIM // 2
    pos = jnp.arange(seq, dtype=F32)
    inv = ROPE_BASE ** (-jnp.arange(half, dtype=F32) / half)
    ang = pos[:, None] * inv[None, :]
    cos_h, sin_h = jnp.cos(ang), jnp.sin(ang)
    cos = jnp.tile(cos_h, (1, 2 * LANES // HEAD_DIM))
    sin = jnp.tile(jnp.concatenate([-sin_h, sin_h], axis=1), (1, LANES // HEAD_DIM))
    log_g = jnp.log(1.0 - 2.0 ** (-5.0 - jnp.arange(RET_HEADS, dtype=F32)))
    c = jnp.arange(CHUNK, dtype=F32)
    dmat = jnp.exp(jnp.abs(c[:, None] - c[None, :])[None] * log_g[:, None, None])
    dmat = dmat.reshape(RET_HEADS // 2, 2 * CHUNK, CHUNK)
    lane_log_g = jnp.repeat(log_g, HEAD_DIM).reshape(RET_HEADS // 2, 1, LANES)
    dec = jnp.exp((CHUNK - 1 - c)[None, :, None] * lane_log_g)
    xi = jnp.exp((c + 1.0)[None, :, None] * lane_log_g)
    gch = jnp.exp(CHUNK * lane_log_g)
    l = np.arange(LANES)
    partner = np.where(l % HEAD_DIM < half, l + half, l - half)
    swap = (l[:, None] == partner[None, :]).astype(np.float32)
    avg = (l[:, None] // HEAD_DIM == l[None, :] // HEAD_DIM).astype(np.float32) / HEAD_DIM
    swap = jnp.asarray(np.concatenate([swap, swap], axis=0), BF16)
    avg = jnp.asarray(np.concatenate([avg, avg], axis=0), BF16)
    return cos, sin, dmat, dec, xi, gch, swap, avg


def _layer_norm(z, g, b):
    mu = jnp.mean(z, axis=-1, keepdims=True)
    zc = z - mu
    var = jnp.mean(zc * zc, axis=-1, keepdims=True)
    return zc * lax.rsqrt(var + LN_EPS) * g + b


def _outproj_kernel(x_ref, p_ref, s_ref, r_ref, w_ref, b_ref, g_ref, be_ref, rw_ref, rb_ref,
                    x1_ref, x1b_ref, gates_ref):
    o1 = POOL_W + SB_W
    tm = x_ref.shape[0]
    subs = [slice(s, s + OUTPROJ_SUB) for s in range(0, tm, OUTPROJ_SUB)]
    ys = [_dot(p_ref[sl, :], w_ref[:POOL_W, :]) + _dot(s_ref[sl, :], w_ref[POOL_W:o1, :])
          + _dot(r_ref[sl, :], w_ref[o1:, :]) for sl in subs]
    x1s = [_layer_norm(DN_ALPHA * x_ref[sl, :] + (y + b_ref[...]), g_ref[...], be_ref[...])
           for sl, y in zip(subs, ys)]
    lane = lax.broadcasted_iota(jnp.int32, (OUTPROJ_SUB, LANES), 1).astype(F32)
    logits = []
    for sl, x1 in zip(subs, x1s):
        x1_ref[sl, :] = x1
        x_hi = x1.astype(BF16)
        x1b_ref[sl, :] = x_hi
        x_mid = (x1 - x_hi.astype(F32)).astype(BF16)
        t = _dot(x_hi, rw_ref[...])
        logits.append(t[:, :LANES] + t[:, LANES:] + _dot(x_mid, rw_ref[:, :LANES]) + rb_ref[...])
    for sl, lg in zip(subs, logits):
        vals = jnp.where(lane < N_EXPERTS, lg, -jnp.inf)
        top_v, top_sel = [], []
        for _ in range(TOP_K):
            m = jnp.max(vals, axis=-1, keepdims=True)
            idx = jnp.min(jnp.where(vals == m, lane, float(LANES)), axis=-1, keepdims=True)
            sel = lane == idx
            vals = jnp.where(sel, -jnp.inf, vals)
            top_v.append(m)
            top_sel.append(sel)
        ex = [jnp.exp(m - top_v[0]) for m in top_v]
        den = ex[0] + ex[1] + ex[2] + ex[3]
        gates = jnp.full((OUTPROJ_SUB, LANES), -1.0, F32)
        for sel, e in zip(top_sel, ex):
            gates = jnp.where(sel, e / den, gates)
        gates_ref[sl, :] = gates


def _outproj(x2, pool_o, sb_o, ret_o, w_bf, b, g, be, rw, rb):
    n = x2.shape[0]
    row = lambda i: (i, 0)
    fixed = lambda i: (0, 0)
    return pl.pallas_call(
        _outproj_kernel,
        grid=(n // TM_ROW,),
        in_specs=[
            pl.BlockSpec((TM_ROW, D_MODEL), row),
            pl.BlockSpec((TM_ROW, POOL_W), row),
            pl.BlockSpec((TM_ROW, SB_W), row),
            pl.BlockSpec((TM_ROW, RET_W), row),
            pl.BlockSpec((D_MODEL, D_MODEL), fixed),
            pl.BlockSpec((1, D_MODEL), fixed),
            pl.BlockSpec((1, D_MODEL), fixed),
            pl.BlockSpec((1, D_MODEL), fixed),
            pl.BlockSpec((D_MODEL, 2 * LANES), fixed),
            pl.BlockSpec((1, LANES), fixed),
        ],
        out_specs=[
            pl.BlockSpec((TM_ROW, D_MODEL), row),
            pl.BlockSpec((TM_ROW, D_MODEL), row),
            pl.BlockSpec((TM_ROW, LANES), row),
        ],
        out_shape=[
            jax.ShapeDtypeStruct((n, D_MODEL), F32),
            jax.ShapeDtypeStruct((n, D_MODEL), BF16),
            jax.ShapeDtypeStruct((n, LANES), F32),
        ],
        compiler_params=_params("parallel"),
        name="outproj_ln_router",
    )(x2, pool_o, sb_o, ret_o, w_bf, b, g, be, rw, rb)


def _dispatch_kernel(gates_ref, xb_ref, tril_ref, xs_ref, gs_ref, oh_ref, post_ref, gatet_ref, cnt_ref):
    tm = gates_ref.shape[0]
    gates = gates_ref[...]
    sel = gates >= 0.0
    self_ = jnp.where(sel, 1.0, 0.0)
    incl = _dot(tril_ref[...], self_.astype(BF16))
    pos = incl - self_
    cnt_ref[0] = incl[tm - 1:tm, :].astype(jnp.int32)
    post =jnp.where(sel, pos, -1.0).T[:N_EXPERTS, :]
    gatet = gates.T[:N_EXPERTS, :]
    post_ref[0] = post
    gatet_ref[0] = gatet
    post_i = post.astype(jnp.int32)
    slot = lax.broadcasted_iota(jnp.int32, (RC_MOE, tm), 0)
    for e0 in range(0, N_EXPERTS, DISPATCH_GROUP):
        pieces = []
        for e in range(e0, e0 + DISPATCH_GROUP):
            onehot = slot == post_i[e:e + 1, :]
            pieces.append(jnp.where(onehot, 1.0, 0.0).astype(BF16))
            gate = jnp.sum(jnp.where(onehot, gatet[e:e + 1, :], 0.0), axis=1, keepdims=True)
            gs_ref[0, e] = jnp.broadcast_to(gate, (RC_MOE, LANES))
        onehots = jnp.concatenate(pieces, axis=0)
        oh_ref[0, e0 * RC_MOE:(e0 + DISPATCH_GROUP) * RC_MOE, :] = onehots
        xs = _dot(onehots, xb_ref[...]).astype(BF16)
        xs_ref[0, e0:e0 + DISPATCH_GROUP] = xs.reshape(DISPATCH_GROUP, RC_MOE, D_MODEL)


def _dispatch(gates, x1b, tril):
    n = gates.shape[0]
    nt = n // TM_MOE
    return pl.pallas_call(
        _dispatch_kernel,
        grid=(nt,),
        in_specs=[
            pl.BlockSpec((TM_MOE, LANES), lambda i: (i, 0)),
            pl.BlockSpec((TM_MOE, D_MODEL), lambda i: (i, 0)),
            pl.BlockSpec((TM_MOE, TM_MOE), lambda i: (0, 0)),
        ],
        out_specs=[
            pl.BlockSpec((1, N_EXPERTS, RC_MOE, D_MODEL), lambda i: (i, 0, 0, 0)),
            pl.BlockSpec((1, N_EXPERTS, RC_MOE, LANES), lambda i: (i, 0, 0, 0)),
            pl.BlockSpec((1, SLOTS, TM_MOE), lambda i: (i, 0, 0)),
            pl.BlockSpec((1, N_EXPERTS, TM_MOE), lambda i: (i, 0, 0)),
            pl.BlockSpec((1, N_EXPERTS, TM_MOE), lambda i: (i, 0, 0)),
            pl.BlockSpec((1, 1, LANES), lambda i: (i, 0, 0)),
        ],
        out_shape=[
            jax.ShapeDtypeStruct((nt, N_EXPERTS, RC_MOE, D_MODEL), BF16),
            jax.ShapeDtypeStruct((nt, N_EXPERTS, RC_MOE, LANES), F32),
            jax.ShapeDtypeStruct((nt, SLOTS, TM_MOE), BF16),
            jax.ShapeDtypeStruct((nt, N_EXPERTS, TM_MOE), F32),
            jax.ShapeDtypeStruct((nt, N_EXPERTS, TM_MOE), F32),
            jax.ShapeDtypeStruct((nt, 1, LANES), jnp.int32),
        ],
        compiler_params=_params("parallel"),
        name="dispatch",
    )(gates, x1b, tril)


def _swiglu_ffn(x, wgu, bgu, wd, bd):
    h = _dot(x, wgu) + bgu
    g = jnp.minimum(h[:, :D_FF], SWIGLU_LIMIT)
    up = jnp.clip(h[:, D_FF:], -SWIGLU_LIMIT, SWIGLU_LIMIT)
    act = (up + 1.0) * (g / (1.0 + jnp.exp(-SWIGLU_ALPHA * g)))
    return _dot(act.astype(BF16), wd) + bd


def _ffn_kernel(tail_ref, xs_ref, gs_ref, wgu_ref, bgu_ref, wd_ref, bd_ref, ys_ref, wgu_bf_ref, wd_bf_ref):
    e = pl.program_id(0)
    j = pl.program_id(1)

    @pl.when(j == 0)
    def _():
        wgu_bf_ref[0] = wgu_ref[0, 0].astype(BF16)
        wd_bf_ref[0] = wd_ref[0, 0].astype(BF16)

    def run(lo, hi):
        rows = FFN_TILES * (hi - lo)
        x = xs_ref[:, 0, lo:hi, :].reshape(rows, D_MODEL)
        y = _swiglu_ffn(x, wgu_bf_ref[0], bgu_ref[0], wd_bf_ref[0], bd_ref[0])
        gate = gs_ref[:, 0, lo:hi, :].reshape(rows, LANES)[:, :1]
        ys_ref[:, 0, lo:hi, :] = (y * gate).astype(BF16).reshape(FFN_TILES, hi - lo, D_MODEL)

    run(0, RC_MAIN)
    used = tail_ref[e * pl.num_programs(1) + j] > 0

    @pl.when(used)
    def _():
        run(RC_MAIN, RC_MOE)

    @pl.when(jnp.logical_not(used))
    def _():
        ys_ref[:, 0, RC_MAIN:RC_MOE, :] = jnp.zeros((FFN_TILES, RC_MOE - RC_MAIN, D_MODEL), BF16)


def _ffn(tail, xs, gs, w_gate_up, bgu, w_down, bd, layer):
    nt = xs.shape[0]
    grid_spec = pltpu.PrefetchScalarGridSpec(
        num_scalar_prefetch=1,
        grid=(N_EXPERTS, nt // FFN_TILES),
        in_specs=[
            pl.BlockSpec((FFN_TILES, 1, RC_MOE, D_MODEL), lambda e, j, t: (j, e, 0, 0)),
            pl.BlockSpec((FFN_TILES, 1, RC_MOE, LANES), lambda e, j, t: (j, e, 0, 0)),
            pl.BlockSpec((1, 1, D_MODEL, 2 * D_FF), lambda e, j, t: (layer, e, 0, 0)),
            pl.BlockSpec((1, 1, 2 * D_FF), lambda e, j, t: (e, 0, 0)),
            pl.BlockSpec((1, 1, D_FF, D_MODEL), lambda e, j, t: (layer, e, 0, 0)),
            pl.BlockSpec((1, 1, D_MODEL), lambda e, j, t: (e, 0, 0)),
        ],
        out_specs=[
            pl.BlockSpec((FFN_TILES, 1, RC_MOE, D_MODEL), lambda e, j, t: (j, e, 0, 0)),
            pl.BlockSpec((1, D_MODEL, 2 * D_FF), lambda e, j, t: (e, 0, 0)),
            pl.BlockSpec((1, D_FF, D_MODEL), lambda e, j, t: (e, 0, 0)),
        ],
    )
    return pl.pallas_call(
        _ffn_kernel,
        grid_spec=grid_spec,
        out_shape=[
            jax.ShapeDtypeStruct(xs.shape, BF16),
            jax.ShapeDtypeStruct((N_EXPERTS, D_MODEL, 2 * D_FF), BF16),
            jax.ShapeDtypeStruct((N_EXPERTS, D_FF, D_MODEL), BF16),
        ],
        compiler_params=_params("parallel", "arbitrary"),
        name="expert_ffn",
    )(tail, xs, gs, w_gate_up, bgu, w_down, bd)


def _combine_kernel(cnt_ref, ys_ref, oh_ref, x1_ref, g_ref, b_ref, xb_ref, post_ref,
                    gatet_ref, bgu_ref, bd_ref, wgu_hbm, wd_hbm, o_ref, acc_ref, wgu_buf, wd_buf, sem):
    i = pl.program_id(0)
    tm = x1_ref.shape[0]
    width = DISPATCH_GROUP * RC_MOE
    acc = None
    for e0 in range(0, N_EXPERTS, DISPATCH_GROUP):
        sl = slice(e0 * RC_MOE, e0 * RC_MOE + width)
        part = lax.dot_general(oh_ref[0, sl, :], ys_ref[0, e0:e0 + DISPATCH_GROUP].reshape(width, D_MODEL),
                               _TN, preferred_element_type=F32)
        acc = part if acc is None else acc + part
    acc_ref[...] = acc

    slot = lax.broadcasted_iota(jnp.int32, (RC_MOE, tm), 0)

    def per_expert(e, carry):
        cnt = cnt_ref[i * N_EXPERTS + e]

        @pl.when(cnt > RC_MOE)
        def _():
            copies = (pltpu.make_async_copy(wgu_hbm.at[e], wgu_buf, sem.at[0]),
                      pltpu.make_async_copy(wd_hbm.at[e], wd_buf, sem.at[1]))
            for cp in copies:
                cp.start()
            for cp in copies:
                cp.wait()
            prow = post_ref[0, pl.ds(e, 1), :].astype(jnp.int32)
            grow = gatet_ref[0, pl.ds(e, 1), :]

            def chunk(c, carry2):
                onehot = (slot + c * RC_MOE) == prow
                oh = jnp.where(onehot, 1.0, 0.0).astype(BF16)
                xg = _dot(oh, xb_ref[...]).astype(BF16)
                y = _swiglu_ffn(xg, wgu_buf[...], bgu_ref[e], wd_buf[...], bd_ref[e])
                gate = jnp.sum(jnp.where(onehot, grow, 0.0), axis=1, keepdims=True)
                yg = (y * gate).astype(BF16)
                acc_ref[...] += lax.dot_general(oh, yg, _TN, preferred_element_type=F32)
                return carry2

            lax.fori_loop(1, (cnt + RC_MOE - 1) // RC_MOE, chunk, 0)

        return carry

    lax.fori_loop(0, N_EXPERTS, per_expert, 0)
    o_ref[...] = _layer_norm(DN_ALPHA * x1_ref[...] + acc_ref[...], g_ref[...], b_ref[...])


def _combine(cnt, ys, oh, x1, g, b, x1b, post, gatet, bgu, bd, wgu, wd):
    n = x1.shape[0]
    nt = n // TM_MOE
    row = lambda i, c: (i, 0)
    fixed2 = lambda i, c: (0, 0)
    fixed3 = lambda i, c: (0, 0, 0)
    grid_spec = pltpu.PrefetchScalarGridSpec(
        num_scalar_prefetch=1,
        grid=(nt,),
        in_specs=[
            pl.BlockSpec((1, N_EXPERTS, RC_MOE, D_MODEL), lambda i, c: (i, 0, 0, 0)),
            pl.BlockSpec((1, SLOTS, TM_MOE), lambda i, c: (i, 0, 0)),
            pl.BlockSpec((TM_MOE, D_MODEL), row),
            pl.BlockSpec((1, D_MODEL), fixed2),
            pl.BlockSpec((1, D_MODEL), fixed2),
            pl.BlockSpec((TM_MOE, D_MODEL), row),
            pl.BlockSpec((1, N_EXPERTS, TM_MOE), lambda i, c: (i, 0, 0)),
            pl.BlockSpec((1, N_EXPERTS, TM_MOE), lambda i, c: (i, 0, 0)),
            pl.BlockSpec((N_EXPERTS, 1, 2 * D_FF), fixed3),
            pl.BlockSpec((N_EXPERTS, 1, D_MODEL), fixed3),
            pl.BlockSpec(memory_space=pl.ANY),
            pl.BlockSpec(memory_space=pl.ANY),
        ],
        out_specs=pl.BlockSpec((TM_MOE, D_MODEL), row),
        scratch_shapes=[
            pltpu.VMEM((TM_MOE, D_MODEL), F32),
            pltpu.VMEM((D_MODEL, 2 * D_FF), BF16),
            pltpu.VMEM((D_FF, D_MODEL), BF16),
            pltpu.SemaphoreType.DMA((2,)),
        ],
    )
    return pl.pallas_call(
        _combine_kernel,
        grid_spec=grid_spec,
        out_shape=jax.ShapeDtypeStruct((n, D_MODEL), F32),
        compiler_params=_params("arbitrary"),
        name="combine_ln",
    )(cnt, ys, oh, x1, g, b, x1b, post, gatet, bgu, bd, wgu, wd)


def _block_diag(pool_w):
    groups = pool_w.shape[0]
    out = jnp.zeros((POOL_W, POOL_W), pool_w.dtype)
    for gi in range(groups):
        out = out.at[gi * POOL_CH:(gi + 1) * POOL_CH, gi * POOL_CH:(gi + 1) * POOL_CH].set(pool_w[gi])
    return out


def _layer(x2, batch, seq, consts, layer, w_in, b_in, pool_w, pool_scale, ret_norm_g, w_out, b_out, ln1_g,
           ln1_b, router_w, router_b, w_gate_up, b_gate_up, w_down, b_down, ln2_g, ln2_b):
    ret_consts, tri2, tril = consts
    row = lambda a: a.reshape(1, -1).astype(F32)
    pool_o, sb, ret = _inproj(x2, w_in.astype(BF16), row(b_in), _block_diag(pool_w).astype(BF16),
                              row(pool_scale), seq)
    sb_o = _stick_breaking(sb, tri2, batch, seq)
    ret_o = _retention(ret, ret_consts, row(ret_norm_g), batch, seq)
    rw = jnp.pad(router_w.astype(F32), ((0, 0), (0, LANES - N_EXPERTS)))
    rw_hi = rw.astype(BF16)
    rw = jnp.concatenate([rw_hi, (rw - rw_hi.astype(F32)).astype(BF16)], axis=1)
    rb = jnp.pad(router_b.astype(F32), (0, LANES - N_EXPERTS)).reshape(1, LANES)
    x1, x1b, gates = _outproj(x2, pool_o, sb_o, ret_o, w_out.astype(BF16), row(b_out), row(ln1_g),
                              row(ln1_b), rw, rb)
    xs, gs, oh, post, gatet, cnt = _dispatch(gates, x1b, tril)
    bgu = b_gate_up.reshape(N_EXPERTS, 1, 2 * D_FF).astype(F32)
    bd = b_down.reshape(N_EXPERTS, 1, D_MODEL).astype(F32)
    cnt2 = cnt[:, 0, :N_EXPERTS]
    tail = jnp.any((cnt2 > RC_MAIN).reshape(-1, FFN_TILES, N_EXPERTS), axis=1)
    ys, wgu, wd = _ffn(tail.T.reshape(-1).astype(jnp.int32), xs, gs, w_gate_up, bgu, w_down, bd, layer)
    cnt_flat = cnt2.reshape(-1)
    return _combine(cnt_flat, ys, oh, x1, row(ln2_g), row(ln2_b), x1b, post, gatet, bgu, bd,
                    wgu, wd)


def kernel(x, w_in, b_in, pool_w, pool_scale, ret_norm_g, w_out, b_out, ln1_g, ln1_b, router_w, router_b,
           w_gate_up, b_gate_up, w_down, b_down, ln2_g, ln2_b):
    batch, seq, d = x.shape
    n = batch * seq
    assert d == D_MODEL and seq % TS_RET == 0 and seq % TQ_SB == 0 and seq % TM_ROW == 0
    assert n % (TM_MOE * FFN_TILES) == 0 and n % TM_ROW == 0 and seq >= SB_SPAN
    j = np.arange(LANES)
    tri = (j[:, None] >= j[None, :]).astype(np.float32)
    tri2 = jnp.asarray(np.concatenate([tri, np.ones_like(tri)], axis=1), BF16)
    r = np.arange(TM_MOE)
    tril = jnp.asarray((r[:, None] >= r[None, :]).astype(np.float32), BF16)
    consts = (_retention_consts(seq), tri2, tril)
    x2 = x.reshape(n, d)
    for l in range(DEPTH):
        x2 = _layer(x2, batch, seq, consts, l, w_in[l], b_in[l], pool_w[l], pool_scale[l], ret_norm_g[l],
                    w_out[l], b_out[l], ln1_g[l], ln1_b[l], router_w[l], router_b[l], w_gate_up,
                    b_gate_up[l], w_down, b_down[l], ln2_g[l], ln2_b[l])
    return x2.reshape(batch, seq, d)
```

```python
import functools

import numpy as np
import jax
import jax.numpy as jnp
from jax import lax
from jax.experimental import pallas as pl
from jax.experimental.pallas import tpu as pltpu

D_MODEL = 1024
DEPTH = 2
CHUNK = 64
HEAD_DIM = 64
POOL_CH = 64
POOL_W = 256
POOL_HALO = 16
SB_W = 384
RET_W = 384
RET_HEADS = 6
IN_W = POOL_W + 3 * SB_W + 4 * RET_W
ROPE_BASE = 10000.0
N_EXPERTS = 32
TOP_K = 4
D_FF = D_MODEL
SWIGLU_LIMIT = 7.0
SWIGLU_ALPHA = 1.702
DN_ALPHA = (2.0 * DEPTH) ** 0.25
LN_EPS = 1e-5

LANES = 128
VMEM_LIMIT = 56 * 1024 * 1024

TM_ROW = 1024
OUTPROJ_SUB = 256
TQ_SB = 512
SB_ROWS = 64
SB_SPAN = 256
TS_RET = 1024
TM_MOE = 512
RC_MOE = 96
RC_MAIN = 80
FFN_TILES = 8
SLOTS = N_EXPERTS * RC_MOE
DISPATCH_GROUP = 8
SB_SKIP = -100.0

BF16 = jnp.bfloat16
F32 = jnp.float32
ONEHOT_DT = jnp.float8_e4m3fn

_NT = (((1,), (1,)), ((), ()))
_TN = (((0,), (0,)), ((), ()))


def _dot(a, b):
    return jnp.dot(a, b, preferred_element_type=F32)


def _params(*sem):
    return pltpu.CompilerParams(dimension_semantics=sem, vmem_limit_bytes=VMEM_LIMIT)


def _pool_mix(cur, halo, t0, w, scale):
    ts = cur.shape[0]
    ext = jnp.concatenate([halo, cur], axis=0)
    a2 = ext[1:] + ext[:-1]
    a4 = a2[2:] + a2[:-2]
    a8 = a4[4:] + a4[:-4]
    a16 = a8[8:] + a8[:-8]
    lane = lax.broadcasted_iota(jnp.int32, (ts, POOL_W), 1)
    grp = lane // POOL_CH
    win = jnp.where(grp == 0, a2[15:15 + ts],
                    jnp.where(grp == 1, a4[13:13 + ts],
                              jnp.where(grp == 2, a8[9:9 + ts], a16[1:1 + ts])))
    width = jnp.where(grp == 0, 2, jnp.where(grp == 1, 4, jnp.where(grp == 2, 8, 16)))
    t = t0 + lax.broadcasted_iota(jnp.int32, (ts, POOL_W), 0)
    cnt = jnp.minimum(t + 1, width).astype(F32)
    pooled = win / cnt - cur
    return _dot(pooled.astype(BF16), w) * scale


def _inproj_kernel(tiles_per_seq, x_ref, w_ref, b_ref, pw_ref, ps_ref, pool_ref, sb_ref, ret_ref, halo_ref):
    it = pl.program_id(0) % tiles_per_seq
    xb = x_ref[...].astype(BF16)
    o1 = POOL_W + 3 * SB_W
    @pl.when(it == 0)
    def _():
        halo_ref[...] = jnp.zeros_like(halo_ref)

    u = _dot(xb, w_ref[:, :POOL_W]) + b_ref[:, :POOL_W]
    sb_ref[...] = (_dot(xb, w_ref[:, POOL_W:o1]) + b_ref[:, POOL_W:o1]).astype(BF16)
    ret_ref[...] = _dot(xb, w_ref[:, o1:]) + b_ref[:, o1:]
    pool_ref[...] = _pool_mix(u, halo_ref[...], it * TM_ROW, pw_ref[...], ps_ref[...]).astype(BF16)
    halo_ref[...] = u[TM_ROW - POOL_HALO:, :]


def _inproj(x2, w_bf, b, pool_w_bd, pool_scale, seq):
    n = x2.shape[0]
    fixed = lambda i: (0, 0)
    return pl.pallas_call(
        functools.partial(_inproj_kernel, seq // TM_ROW),
        grid=(n // TM_ROW,),
        in_specs=[
            pl.BlockSpec((TM_ROW, D_MODEL), lambda i: (i, 0)),
            pl.BlockSpec((D_MODEL, IN_W), fixed),
            pl.BlockSpec((1, IN_W), fixed),
            pl.BlockSpec((POOL_W, POOL_W), fixed),
            pl.BlockSpec((1, POOL_W), fixed),
        ],
        out_specs=[
            pl.BlockSpec((TM_ROW, POOL_W), lambda i: (i, 0)),
            pl.BlockSpec((TM_ROW, 3 * SB_W), lambda i: (i, 0)),
            pl.BlockSpec((TM_ROW, 4 * RET_W), lambda i: (i, 0)),
        ],
        out_shape=[
            jax.ShapeDtypeStruct((n, POOL_W), BF16),
            jax.ShapeDtypeStruct((n, 3 * SB_W), BF16),
            jax.ShapeDtypeStruct((n, 4 * RET_W), F32),
        ],
        scratch_shapes=[pltpu.VMEM((POOL_HALO, POOL_W), F32)],
        compiler_params=_params("arbitrary"),
        name="inproj_pool",
    )(x2, w_bf, b, pool_w_bd, pool_scale)


def _sb_kernel(q_ref, k_ref, v_ref, tri_ref, o_ref, acc_ref, carry_ref):
    qi = pl.program_id(1)
    tq = q_ref.shape[1]
    groups = q_ref.shape[2] // LANES
    lane = lax.broadcasted_iota(jnp.int32, (tq, LANES), 1)
    qh = []
    for p in range(groups):
        q = q_ref[0, :, p * LANES:(p + 1) * LANES] * jnp.asarray(HEAD_DIM ** -0.5, BF16)
        zero = jnp.zeros_like(q)
        qh += [jnp.where(lane < HEAD_DIM, q, zero), jnp.where(lane >= HEAD_DIM, q, zero)]

    def log_keep(z):
        return jnp.minimum(-z, 0.0) - jnp.log(1.0 + jnp.exp(-jnp.abs(z)))

    def suffix_sums(lk):
        lk = lk.astype(BF16)
        out = []
        for b in range(lk.shape[1] // LANES):
            rs = _dot(lk[:, b * LANES:(b + 1) * LANES], tri_ref[...])
            out.append((rs[:, :LANES], rs[:, LANES:]))
        return out

    t0 = qi * tq
    halves = range(tq // SB_ROWS)
    heads = range(2 * groups)
    cols = [slice((h // 2) * LANES, (h // 2 + 1) * LANES) for h in heads]
    rows = [slice(u * SB_ROWS, (u + 1) * SB_ROWS) for u in halves]
    starts = [pl.multiple_of(jnp.maximum(t0 + (u + 1) * SB_ROWS - SB_SPAN, 0), SB_ROWS) for u in halves]
    below = []
    for u in halves:
        qpos = t0 + u * SB_ROWS + lax.broadcasted_iota(jnp.int32, (SB_ROWS, SB_SPAN), 0)
        kpos = starts[u] + lax.broadcasted_iota(jnp.int32, (SB_ROWS, SB_SPAN), 1)
        below.append(kpos < qpos)
    items = [(u, h) for u in halves for h in heads]
    zs = {(u, h): lax.dot_general(qh[h][rows[u]], k_ref[0, pl.ds(starts[u], SB_SPAN), cols[h]], _NT,
                                  preferred_element_type=F32) for u, h in items}
    sums = {(u, h): suffix_sums(jnp.where(below[u], log_keep(zs[u, h]), 0.0)) for u, h in items}
    weights, runs = {}, {}
    for u, h in items:
        run = None
        r = [None] * len(sums[u, h])
        for b in reversed(range(len(r))):
            r_in, tot = sums[u, h][b]
            r[b] = r_in if run is None else r_in + run
            run = tot if run is None else run + tot
        weights[u, h] = jnp.where(below[u], jnp.exp(zs[u, h] + jnp.concatenate(r, axis=1)), 0.0).astype(BF16)
        runs[u, h] = run
    for u, h in items:
        acc_ref[h, rows[u], :] = _dot(weights[u, h], v_ref[0, pl.ds(starts[u], SB_SPAN), cols[h]])
        carry_ref[h, rows[u], :] = runs[u, h]
    worst = []
    for u in halves:
        top = runs[u, 0]
        for h in heads[1:]:
            top = jnp.maximum(top, runs[u, h])
        worst.append(jnp.max(top))

    def tile(u, kb):
        start = pl.multiple_of(kb * LANES, LANES)
        older = (start + lax.broadcasted_iota(jnp.int32, (SB_ROWS, LANES), 1)) < starts[u]
        worst = None
        for h in heads:
            k = k_ref[0, pl.ds(start, LANES), cols[h]]
            v = v_ref[0, pl.ds(start, LANES), cols[h]]
            z = lax.dot_general(qh[h][rows[u]], k, _NT, preferred_element_type=F32)
            (r_in, tot), = suffix_sums(jnp.where(older, log_keep(z), 0.0))
            carry = carry_ref[h, rows[u], :]
            a = jnp.where(older, jnp.exp(z + r_in + carry), 0.0)
            acc_ref[h, rows[u], :] += _dot(a.astype(BF16), v)
            carry = carry + tot
            carry_ref[h, rows[u], :] = carry
            m = jnp.max(carry)
            worst = m if worst is None else jnp.maximum(worst, m)
        return worst

    def cond(state):
        kb, worst = state
        return jnp.logical_and(kb >= 0, worst > SB_SKIP)

    for u in halves:
        lax.while_loop(cond, lambda state, u=u: (state[0] - 1, tile(u, state[0])),
                       ((starts[u] + LANES - 1) // LANES - 1, worst[u]))
    for p in range(groups):
        o_ref[0, :, p * LANES:(p + 1) * LANES] = jnp.where(
            lane < HEAD_DIM, acc_ref[2 * p], acc_ref[2 * p + 1]).astype(BF16)


def _stick_breaking(sb, tri2, batch, seq):
    sb3 = sb.reshape(batch, seq, 3 * SB_W)
    heads = SB_W // HEAD_DIM
    once = pl.Buffered(1)
    out = pl.pallas_call(
        _sb_kernel,
        grid=(batch, seq // TQ_SB),
        in_specs=[
            pl.BlockSpec((1, TQ_SB, SB_W), lambda b, i: (b, i, 0)),
            pl.BlockSpec((1, seq, SB_W), lambda b, i: (b, 0, 1), pipeline_mode=once),
            pl.BlockSpec((1, seq, SB_W), lambda b, i: (b, 0, 2), pipeline_mode=once),
            pl.BlockSpec((LANES, 2 * LANES), lambda b, i: (0, 0)),
        ],
        out_specs=pl.BlockSpec((1, TQ_SB, SB_W), lambda b, i: (b, i, 0)),
        out_shape=jax.ShapeDtypeStruct((batch, seq, SB_W), BF16),
        scratch_shapes=[
            pltpu.VMEM((heads, TQ_SB, LANES), F32),
            pltpu.VMEM((heads, TQ_SB, LANES), F32),
        ],
        compiler_params=_params("parallel", "arbitrary"),
        name="stick_breaking",
    )(sb3, sb3, sb3, tri2)
    return out.reshape(batch * seq, SB_W)


def _ret_kernel(q_ref, k_ref, v_ref, g_ref, cos_ref, sin_ref, dmat_ref, dec_ref, xi_ref, gch_ref,
                swap_ref, avg_ref, ng_ref, o_ref, state_ref, obuf_ref):
    si = pl.program_id(1)
    ts = q_ref.shape[1]
    groups = q_ref.shape[2] // LANES
    cols = [slice(p * LANES, (p + 1) * LANES) for p in range(groups)]

    @pl.when(si == 0)
    def _():
        state_ref[...] = jnp.zeros_like(state_ref)

    cos = cos_ref[...]
    sin = sin_ref[...]

    def lane_mix(x, m_ref):
        hi = x.astype(BF16)
        lo = (x - hi.astype(F32)).astype(BF16)
        return _dot(jnp.concatenate([hi, lo], axis=1), m_ref[...])

    def rot(x):
        return x * cos + lane_mix(x, swap_ref) * sin

    qr = [rot(q_ref[0, :, c]) for c in cols]
    kr = [rot(k_ref[0, :, c]) * (HEAD_DIM ** -0.5) for c in cols]
    v = [v_ref[0, :, c] for c in cols]

    lane_c = lax.broadcasted_iota(jnp.int32, (CHUNK, LANES), 1)
    head0 = lane_c < HEAD_DIM
    r_i = lax.broadcasted_iota(jnp.int32, (LANES, LANES), 0) // HEAD_DIM
    c_i = lax.broadcasted_iota(jnp.int32, (LANES, LANES), 1) // HEAD_DIM
    same_head = r_i == c_i

    chunks = [slice(n * CHUNK, (n + 1) * CHUNK) for n in range(ts // CHUNK)]
    items = [(p, n) for p in range(groups) for n in range(len(chunks))]
    kcb = {(p, n): kr[p][chunks[n]].astype(BF16) for p, n in items}
    vcb = {(p, n): v[p][chunks[n]].astype(BF16) for p, n in items}
    kvs = {(p, n): lax.dot_general(kcb[p, n], (v[p][chunks[n]] * dec_ref[p]).astype(BF16), _TN,
                                   preferred_element_type=F32) for p, n in items}
    scores = {}
    for p, n in items:
        qc = qr[p][chunks[n]]
        qstack = jnp.concatenate([jnp.where(head0, qc, 0.0), jnp.where(head0, 0.0, qc)], axis=0)
        scores[p, n] = (lax.dot_general(qstack.astype(BF16), kcb[p, n], _NT, preferred_element_type=F32)
                        * dmat_ref[p]).astype(BF16)
    o2s = {(p, n): _dot(scores[p, n], vcb[p, n]) for p, n in items}
    states = {}
    for p in range(groups):
        state = state_ref[p]
        for n in range(len(chunks)):
            states[p, n] = state.astype(BF16)
            state = gch_ref[p] * state + jnp.where(same_head, kvs[p, n], 0.0)
        state_ref[p] = state
    for p, n in items:
        o_intra = jnp.where(head0, o2s[p, n][:CHUNK], o2s[p, n][CHUNK:])
        o_inter = _dot((qr[p][chunks[n]] * xi_ref[p]).astype(BF16), states[p, n])
        obuf_ref[chunks[n], cols[p]] = o_intra + o_inter

    for c in cols:
        o = obuf_ref[:, c]
        mu = lane_mix(o, avg_ref)
        oc = o - mu
        var = lane_mix(oc * oc, avg_ref)
        normed = oc * lax.rsqrt(var + LN_EPS) * ng_ref[:, c]
        gate = g_ref[0, :, c]
        o_ref[0, :, c] = (gate / (1.0 + jnp.exp(-gate)) * normed).astype(BF16)


def _retention(ret, consts, norm_g, batch, seq):
    cos, sin, dmat, dec, xi, gch, swap, avg = consts
    ret3 =ret.reshape(batch, seq, 4 * RET_W)
    pairs = RET_W // LANES
    fixed3 = lambda b, i: (0, 0, 0)
    out = pl.pallas_call(
        _ret_kernel,
        grid=(batch, seq // TS_RET),
        in_specs=[
            pl.BlockSpec((1, TS_RET, RET_W), lambda b, i: (b, i, 0)),
            pl.BlockSpec((1, TS_RET, RET_W), lambda b, i: (b, i, 1)),
            pl.BlockSpec((1, TS_RET, RET_W), lambda b, i: (b, i, 2)),
            pl.BlockSpec((1, TS_RET, RET_W), lambda b, i: (b, i, 3)),
            pl.BlockSpec((TS_RET, LANES), lambda b, i: (i, 0)),
            pl.BlockSpec((TS_RET, LANES), lambda b, i: (i, 0)),
            pl.BlockSpec((pairs, 2 * CHUNK, CHUNK), fixed3),
            pl.BlockSpec((pairs, CHUNK, LANES), fixed3),
            pl.BlockSpec((pairs, CHUNK, LANES), fixed3),
            pl.BlockSpec((pairs, 1, LANES), fixed3),
            pl.BlockSpec((2 * LANES, LANES), lambda b, i: (0, 0)),
            pl.BlockSpec((2 * LANES, LANES), lambda b, i: (0, 0)),
            pl.BlockSpec((1, RET_W), lambda b, i: (0, 0)),
        ],
        out_specs=pl.BlockSpec((1, TS_RET, RET_W), lambda b, i: (b, i, 0)),
        out_shape=jax.ShapeDtypeStruct((batch, seq, RET_W), BF16),
        scratch_shapes=[
            pltpu.VMEM((pairs, LANES, LANES), F32),
            pltpu.VMEM((TS_RET, RET_W), F32),
        ],
        compiler_params=_params("parallel", "arbitrary"),
        name="retention",
    )(ret3, ret3, ret3, ret3, cos, sin, dmat, dec, xi, gch, swap, avg, norm_g)
    return out.reshape(batch * seq, RET_W)


def _retention_consts(seq):
    half = HEAD_DIM // 2
    pos = jnp.arange(seq, dtype=F32)
    inv = ROPE_BASE ** (-jnp.arange(half, dtype=F32) / half)
    ang = pos[:, None] * inv[None, :]
    cos_h, sin_h = jnp.cos(ang), jnp.sin(ang)
    cos = jnp.tile(cos_h, (1, 2 * LANES // HEAD_DIM))
    sin = jnp.tile(jnp.concatenate([-sin_h, sin_h], axis=1), (1, LANES // HEAD_DIM))
    log_g = jnp.log(1.0 - 2.0 ** (-5.0 - jnp.arange(RET_HEADS, dtype=F32)))
    c = jnp.arange(CHUNK, dtype=F32)
    dmat = jnp.exp(jnp.abs(c[:, None] - c[None, :])[None] * log_g[:, None, None])
    dmat = dmat.reshape(RET_HEADS // 2, 2 * CHUNK, CHUNK)
    lane_log_g = jnp.repeat(log_g, HEAD_DIM).reshape(RET_HEADS // 2, 1, LANES)
    dec = jnp.exp((CHUNK - 1 - c)[None, :, None] * lane_log_g)
    xi = jnp.exp((c + 1.0)[None, :, None] * lane_log_g)
    gch = jnp.exp(CHUNK * lane_log_g)
    l = np.arange(LANES)
    partner = np.where(l % HEAD_DIM < half, l + half, l - half)
    swap = (l[:, None] == partner[None, :]).astype(np.float32)
    avg = (l[:, None] // HEAD_DIM == l[None, :] // HEAD_DIM).astype(np.float32) / HEAD_DIM
    swap = jnp.asarray(np.concatenate([swap, swap], axis=0), BF16)
    avg = jnp.asarray(np.concatenate([avg, avg], axis=0), BF16)
    return cos, sin, dmat, dec, xi, gch, swap, avg


def _layer_norm(z, g, b):
    mu = jnp.mean(z, axis=-1, keepdims=True)
    zc = z - mu
    var = jnp.mean(zc * zc, axis=-1, keepdims=True)
    return zc * lax.rsqrt(var + LN_EPS) * g + b


def _outproj_kernel(x_ref, p_ref, s_ref, r_ref, w_ref, b_ref, g_ref, be_ref, rw_ref, rb_ref,
                    x1_ref, x1b_ref, gates_ref):
    o1 = POOL_W + SB_W
    tm = x_ref.shape[0]
    subs = [slice(s, s + OUTPROJ_SUB) for s in range(0, tm, OUTPROJ_SUB)]
    ys = [_dot(p_ref[sl, :], w_ref[:POOL_W, :]) + _dot(s_ref[sl, :], w_ref[POOL_W:o1, :])
          + _dot(r_ref[sl, :], w_ref[o1:, :]) for sl in subs]
    x1s = [_layer_norm(DN_ALPHA * x_ref[sl, :] + (y + b_ref[...]), g_ref[...], be_ref[...])
           for sl, y in zip(subs, ys)]
    lane = lax.broadcasted_iota(jnp.int32, (OUTPROJ_SUB, LANES), 1).astype(F32)
    logits = []
    for sl, x1 in zip(subs, x1s):
        x1_ref[sl, :] = x1
        x_hi = x1.astype(BF16)
        x1b_ref[sl, :] = x_hi
        x_mid = (x1 - x_hi.astype(F32)).astype(BF16)
        t = _dot(x_hi, rw_ref[...])
        logits.append(t[:, :LANES] + t[:, LANES:] + _dot(x_mid, rw_ref[:, :LANES]) + rb_ref[...])
    for sl, lg in zip(subs, logits):
        vals = jnp.where(lane < N_EXPERTS, lg, -jnp.inf)
        top_v, top_sel = [], []
        for _ in range(TOP_K):
            m = jnp.max(vals, axis=-1, keepdims=True)
            idx = jnp.min(jnp.where(vals == m, lane, float(LANES)), axis=-1, keepdims=True)
            sel = lane == idx
            vals = jnp.where(sel, -jnp.inf, vals)
            top_v.append(m)
            top_sel.append(sel)
        ex = [jnp.exp(m - top_v[0]) for m in top_v]
        den = ex[0] + ex[1] + ex[2] + ex[3]
        gates = jnp.full((OUTPROJ_SUB, LANES), -1.0, F32)
        for sel, e in zip(top_sel, ex):
            gates = jnp.where(sel, e / den, gates)
        gates_ref[sl, :] = gates


def _outproj(x2, pool_o, sb_o, ret_o, w_bf, b, g, be, rw, rb):
    n = x2.shape[0]
    row = lambda i: (i, 0)
    fixed = lambda i: (0, 0)
    return pl.pallas_call(
        _outproj_kernel,
        grid=(n // TM_ROW,),
        in_specs=[
            pl.BlockSpec((TM_ROW, D_MODEL), row),
            pl.BlockSpec((TM_ROW, POOL_W), row),
            pl.BlockSpec((TM_ROW, SB_W), row),
            pl.BlockSpec((TM_ROW, RET_W), row),
            pl.BlockSpec((D_MODEL, D_MODEL), fixed),
            pl.BlockSpec((1, D_MODEL), fixed),
            pl.BlockSpec((1, D_MODEL), fixed),
            pl.BlockSpec((1, D_MODEL), fixed),
            pl.BlockSpec((D_MODEL, 2 * LANES), fixed),
            pl.BlockSpec((1, LANES), fixed),
        ],
        out_specs=[
            pl.BlockSpec((TM_ROW, D_MODEL), row),
            pl.BlockSpec((TM_ROW, D_MODEL), row),
            pl.BlockSpec((TM_ROW, LANES), row),
        ],
        out_shape=[
            jax.ShapeDtypeStruct((n, D_MODEL), F32),
            jax.ShapeDtypeStruct((n, D_MODEL), BF16),
            jax.ShapeDtypeStruct((n, LANES), F32),
        ],
        compiler_params=_params("parallel"),
        name="outproj_ln_router",
    )(x2, pool_o, sb_o, ret_o, w_bf, b, g, be, rw, rb)


def _dispatch_kernel(gates_ref, xb_ref, tril_ref, xs_ref, gs_ref, oh_ref, post_ref, gatet_ref, cnt_ref):
    tm = gates_ref.shape[0]
    gates = gates_ref[...]
    sel = gates >= 0.0
    self_ = jnp.where(sel, 1.0, 0.0)
    incl = _dot(tril_ref[...], self_.astype(BF16))
    pos = incl - self_
    cnt_ref[0] = incl[tm - 1:tm, :].astype(jnp.int32)
    post =jnp.where(sel, pos, -1.0).T[:N_EXPERTS, :]
    gatet = gates.T[:N_EXPERTS, :]
    post_ref[0] = post
    gatet_ref[0] = gatet
    post_i = post.astype(jnp.int32)
    slot = lax.broadcasted_iota(jnp.int32, (RC_MOE, tm), 0)
    for e0 in range(0, N_EXPERTS, DISPATCH_GROUP):
        pieces = []
        for e in range(e0, e0 + DISPATCH_GROUP):
            onehot = slot == post_i[e:e + 1, :]
            pieces.append(jnp.where(onehot, 1.0, 0.0).astype(BF16))
            gate = jnp.sum(jnp.where(onehot, gatet[e:e + 1, :], 0.0), axis=1, keepdims=True)
            gs_ref[0, e] = jnp.broadcast_to(gate, (RC_MOE, LANES))
        onehots = jnp.concatenate(pieces, axis=0)
        oh_ref[0, e0 * RC_MOE:(e0 + DISPATCH_GROUP) * RC_MOE, :] = onehots.astype(ONEHOT_DT)
        xs = _dot(onehots, xb_ref[...]).astype(BF16)
        xs_ref[0, e0:e0 + DISPATCH_GROUP] = xs.reshape(DISPATCH_GROUP, RC_MOE, D_MODEL)


def _dispatch(gates, x1b, tril):
    n = gates.shape[0]
    nt = n // TM_MOE
    return pl.pallas_call(
        _dispatch_kernel,
        grid=(nt,),
        in_specs=[
            pl.BlockSpec((TM_MOE, LANES), lambda i: (i, 0)),
            pl.BlockSpec((TM_MOE, D_MODEL), lambda i: (i, 0)),
            pl.BlockSpec((TM_MOE, TM_MOE), lambda i: (0, 0)),
        ],
        out_specs=[
            pl.BlockSpec((1, N_EXPERTS, RC_MOE, D_MODEL), lambda i: (i, 0, 0, 0)),
            pl.BlockSpec((1, N_EXPERTS, RC_MOE, LANES), lambda i: (i, 0, 0, 0)),
            pl.BlockSpec((1, SLOTS, TM_MOE), lambda i: (i, 0, 0)),
            pl.BlockSpec((1, N_EXPERTS, TM_MOE), lambda i: (i, 0, 0)),
            pl.BlockSpec((1, N_EXPERTS, TM_MOE), lambda i: (i, 0, 0)),
            pl.BlockSpec((1, 1, LANES), lambda i: (i, 0, 0)),
        ],
        out_shape=[
            jax.ShapeDtypeStruct((nt, N_EXPERTS, RC_MOE, D_MODEL), BF16),
            jax.ShapeDtypeStruct((nt, N_EXPERTS, RC_MOE, LANES), F32),
            jax.ShapeDtypeStruct((nt, SLOTS, TM_MOE), ONEHOT_DT),
            jax.ShapeDtypeStruct((nt, N_EXPERTS, TM_MOE), F32),
            jax.ShapeDtypeStruct((nt, N_EXPERTS, TM_MOE), F32),
            jax.ShapeDtypeStruct((nt, 1, LANES), jnp.int32),
        ],
        compiler_params=_params("parallel"),
        name="dispatch",
    )(gates, x1b, tril)


def _swiglu_ffn(x, wgu, bgu, wd, bd):
    h = _dot(x, wgu) + bgu
    g = jnp.minimum(h[:, :D_FF], SWIGLU_LIMIT)
    up = jnp.clip(h[:, D_FF:], -SWIGLU_LIMIT, SWIGLU_LIMIT)
    act = (up + 1.0) * (g / (1.0 + jnp.exp(-SWIGLU_ALPHA * g)))
    return _dot(act.astype(BF16), wd) + bd


def _ffn_kernel(tail_ref, xs_ref, gs_ref, wgu_ref, bgu_ref, wd_ref, bd_ref, ys_ref, wgu_bf_ref, wd_bf_ref):
    e = pl.program_id(0)
    j = pl.program_id(1)

    @pl.when(j == 0)
    def _():
        wgu_bf_ref[0] = wgu_ref[0, 0].astype(BF16)
        wd_bf_ref[0] = wd_ref[0, 0].astype(BF16)

    def run(lo, hi):
        rows = FFN_TILES * (hi - lo)
        x = xs_ref[:, 0, lo:hi, :].reshape(rows, D_MODEL)
        y = _swiglu_ffn(x, wgu_bf_ref[0], bgu_ref[0], wd_bf_ref[0], bd_ref[0])
        gate = gs_ref[:, 0, lo:hi, :].reshape(rows, LANES)[:, :1]
        ys_ref[:, 0, lo:hi, :] = (y * gate).astype(BF16).reshape(FFN_TILES, hi - lo, D_MODEL)

    run(0, RC_MAIN)
    used = tail_ref[e * pl.num_programs(1) + j] > 0

    @pl.when(used)
    def _():
        run(RC_MAIN, RC_MOE)

    @pl.when(jnp.logical_not(used))
    def _():
        ys_ref[:, 0, RC_MAIN:RC_MOE, :] = jnp.zeros((FFN_TILES, RC_MOE - RC_MAIN, D_MODEL), BF16)


def _ffn(tail, xs, gs, w_gate_up, bgu, w_down, bd, layer):
    nt = xs.shape[0]
    grid_spec = pltpu.PrefetchScalarGridSpec(
        num_scalar_prefetch=1,
        grid=(N_EXPERTS, nt // FFN_TILES),
        in_specs=[
            pl.BlockSpec((FFN_TILES, 1, RC_MOE, D_MODEL), lambda e, j, t: (j, e, 0, 0)),
            pl.BlockSpec((FFN_TILES, 1, RC_MOE, LANES), lambda e, j, t: (j, e, 0, 0)),
            pl.BlockSpec((1, 1, D_MODEL, 2 * D_FF), lambda e, j, t: (layer, e, 0, 0)),
            pl.BlockSpec((1, 1, 2 * D_FF), lambda e, j, t: (e, 0, 0)),
            pl.BlockSpec((1, 1, D_FF, D_MODEL), lambda e, j, t: (layer, e, 0, 0)),
            pl.BlockSpec((1, 1, D_MODEL), lambda e, j, t: (e, 0, 0)),
        ],
        out_specs=[
            pl.BlockSpec((FFN_TILES, 1, RC_MOE, D_MODEL), lambda e, j, t: (j, e, 0, 0)),
            pl.BlockSpec((1, D_MODEL, 2 * D_FF), lambda e, j, t: (e, 0, 0)),
            pl.BlockSpec((1, D_FF, D_MODEL), lambda e, j, t: (e, 0, 0)),
        ],
    )
    return pl.pallas_call(
        _ffn_kernel,
        grid_spec=grid_spec,
        out_shape=[
            jax.ShapeDtypeStruct(xs.shape, BF16),
            jax.ShapeDtypeStruct((N_EXPERTS, D_MODEL, 2 * D_FF), BF16),
            jax.ShapeDtypeStruct((N_EXPERTS, D_FF, D_MODEL), BF16),
        ],
        compiler_params=_params("parallel", "arbitrary"),
        name="expert_ffn",
    )(tail, xs, gs, w_gate_up, bgu, w_down, bd)


def _combine_kernel(cnt_ref, ys_ref, oh_ref, x1_ref, g_ref, b_ref, xb_ref, post_ref,
                    gatet_ref, bgu_ref, bd_ref, wgu_hbm, wd_hbm, o_ref, acc_ref, wgu_buf, wd_buf, sem):
    i = pl.program_id(0)
    tm = x1_ref.shape[0]
    width = DISPATCH_GROUP * RC_MOE
    acc = None
    for e0 in range(0, N_EXPERTS, DISPATCH_GROUP):
        sl = slice(e0 * RC_MOE, e0 * RC_MOE + width)
        part = lax.dot_general(oh_ref[0, sl, :].astype(BF16),
                               ys_ref[0, e0:e0 + DISPATCH_GROUP].reshape(width, D_MODEL),
                               _TN, preferred_element_type=F32)
        acc = part if acc is None else acc + part
    acc_ref[...] = acc

    slot = lax.broadcasted_iota(jnp.int32, (RC_MOE, tm), 0)

    def per_expert(e, carry):
        cnt = cnt_ref[i * N_EXPERTS + e]

        @pl.when(cnt > RC_MOE)
        def _():
            copies = (pltpu.make_async_copy(wgu_hbm.at[e], wgu_buf, sem.at[0]),
                      pltpu.make_async_copy(wd_hbm.at[e], wd_buf, sem.at[1]))
            for cp in copies:
                cp.start()
            for cp in copies:
                cp.wait()
            prow = post_ref[0, pl.ds(e, 1), :].astype(jnp.int32)
            grow = gatet_ref[0, pl.ds(e, 1), :]

            def chunk(c, carry2):
                onehot = (slot + c * RC_MOE) == prow
                oh = jnp.where(onehot, 1.0, 0.0).astype(BF16)
                xg = _dot(oh, xb_ref[...]).astype(BF16)
                y = _swiglu_ffn(xg, wgu_buf[...], bgu_ref[e], wd_buf[...], bd_ref[e])
                gate = jnp.sum(jnp.where(onehot, grow, 0.0), axis=1, keepdims=True)
                yg = (y * gate).astype(BF16)
                acc_ref[...] += lax.dot_general(oh, yg, _TN, preferred_element_type=F32)
                return carry2

            lax.fori_loop(1, (cnt + RC_MOE - 1) // RC_MOE, chunk, 0)

        return carry

    lax.fori_loop(0, N_EXPERTS, per_expert, 0)
    o_ref[...] = _layer_norm(DN_ALPHA * x1_ref[...] + acc_ref[...], g_ref[...], b_ref[...])


def _combine(cnt, ys, oh, x1, g, b, x1b, post, gatet, bgu, bd, wgu, wd):
    n = x1.shape[0]
    nt = n // TM_MOE
    row = lambda i, c: (i, 0)
    fixed2 = lambda i, c: (0, 0)
    fixed3 = lambda i, c: (0, 0, 0)
    grid_spec = pltpu.PrefetchScalarGridSpec(
        num_scalar_prefetch=1,
        grid=(nt,),
        in_specs=[
            pl.BlockSpec((1, N_EXPERTS, RC_MOE, D_MODEL), lambda i, c: (i, 0, 0, 0)),
            pl.BlockSpec((1, SLOTS, TM_MOE), lambda i, c: (i, 0, 0)),
            pl.BlockSpec((TM_MOE, D_MODEL), row),
            pl.BlockSpec((1, D_MODEL), fixed2),
            pl.BlockSpec((1, D_MODEL), fixed2),
            pl.BlockSpec((TM_MOE, D_MODEL), row),
            pl.BlockSpec((1, N_EXPERTS, TM_MOE), lambda i, c: (i, 0, 0)),
            pl.BlockSpec((1, N_EXPERTS, TM_MOE), lambda i, c: (i, 0, 0)),
            pl.BlockSpec((N_EXPERTS, 1, 2 * D_FF), fixed3),
            pl.BlockSpec((N_EXPERTS, 1, D_MODEL), fixed3),
            pl.BlockSpec(memory_space=pl.ANY),
            pl.BlockSpec(memory_space=pl.ANY),
        ],
        out_specs=pl.BlockSpec((TM_MOE, D_MODEL), row),
        scratch_shapes=[
            pltpu.VMEM((TM_MOE, D_MODEL), F32),
            pltpu.VMEM((D_MODEL, 2 * D_FF), BF16),
            pltpu.VMEM((D_FF, D_MODEL), BF16),
            pltpu.SemaphoreType.DMA((2,)),
        ],
    )
    return pl.pallas_call(
        _combine_kernel,
        grid_spec=grid_spec,
        out_shape=jax.ShapeDtypeStruct((n, D_MODEL), F32),
        compiler_params=_params("arbitrary"),
        name="combine_ln",
    )(cnt, ys, oh, x1, g, b, x1b, post, gatet, bgu, bd, wgu, wd)


def _block_diag(pool_w):
    groups = pool_w.shape[0]
    out = jnp.zeros((POOL_W, POOL_W), pool_w.dtype)
    for gi in range(groups):
        out = out.at[gi * POOL_CH:(gi + 1) * POOL_CH, gi * POOL_CH:(gi + 1) * POOL_CH].set(pool_w[gi])
    return out


def _layer(x2, batch, seq, consts, layer, w_in, b_in, pool_w, pool_scale, ret_norm_g, w_out, b_out, ln1_g,
           ln1_b, router_w, router_b, w_gate_up, b_gate_up, w_down, b_down, ln2_g, ln2_b):
    ret_consts, tri2, tril = consts
    row = lambda a: a.reshape(1, -1).astype(F32)
    pool_o, sb, ret = _inproj(x2, w_in.astype(BF16), row(b_in), _block_diag(pool_w).astype(BF16),
                              row(pool_scale), seq)
    sb_o = _stick_breaking(sb, tri2, batch, seq)
    ret_o = _retention(ret, ret_consts, row(ret_norm_g), batch, seq)
    rw = jnp.pad(router_w.astype(F32), ((0, 0), (0, LANES - N_EXPERTS)))
    rw_hi = rw.astype(BF16)
    rw = jnp.concatenate([rw_hi, (rw - rw_hi.astype(F32)).astype(BF16)], axis=1)
    rb = jnp.pad(router_b.astype(F32), (0, LANES - N_EXPERTS)).reshape(1, LANES)
    x1, x1b, gates = _outproj(x2, pool_o, sb_o, ret_o, w_out.astype(BF16), row(b_out), row(ln1_g),
                              row(ln1_b), rw, rb)
    xs, gs, oh, post, gatet, cnt = _dispatch(gates, x1b, tril)
    bgu = b_gate_up.reshape(N_EXPERTS, 1, 2 * D_FF).astype(F32)
    bd = b_down.reshape(N_EXPERTS, 1, D_MODEL).astype(F32)
    cnt2 = cnt[:, 0, :N_EXPERTS]
    tail = jnp.any((cnt2 > RC_MAIN).reshape(-1, FFN_TILES, N_EXPERTS), axis=1)
    ys, wgu, wd = _ffn(tail.T.reshape(-1).astype(jnp.int32), xs, gs, w_gate_up, bgu, w_down, bd, layer)
    cnt_flat = cnt2.reshape(-1)
    return _combine(cnt_flat, ys, oh, x1, row(ln2_g), row(ln2_b), x1b, post, gatet, bgu, bd,
                    wgu, wd)


def kernel(x, w_in, b_in, pool_w, pool_scale, ret_norm_g, w_out, b_out, ln1_g, ln1_b, router_w, router_b,
           w_gate_up, b_gate_up, w_down, b_down, ln2_g, ln2_b):
    batch, seq, d = x.shape
    n = batch * seq
    assert d == D_MODEL and seq % TS_RET == 0 and seq % TQ_SB == 0 and seq % TM_ROW == 0
    assert n % (TM_MOE * FFN_TILES) == 0 and n % TM_ROW == 0 and seq >= SB_SPAN
    j = np.arange(LANES)
    tri = (j[:, None] >= j[None, :]).astype(np.float32)
    tri2 = jnp.asarray(np.concatenate([tri, np.ones_like(tri)], axis=1), BF16)
    r = np.arange(TM_MOE)
    tril = jnp.asarray((r[:, None] >= r[None, :]).astype(np.float32), BF16)
    consts = (_retention_consts(seq), tri2, tril)
    x2 = x.reshape(n, d)
    for l in range(DEPTH):
        x2 = _layer(x2, batch, seq, consts, l, w_in[l], b_in[l], pool_w[l], pool_scale[l], ret_norm_g[l],
                    w_out[l], b_out[l], ln1_g[l], ln1_b[l], router_w[l], router_b[l], w_gate_up,
                    b_gate_up[l], w_down, b_down[l], ln2_g[l], ln2_b[l])
    return x2.reshape(batch, seq, d)
```

```python
import functools

import numpy as np
import jax
import jax.numpy as jnp
from jax import lax
from jax.experimental import pallas as pl
from jax.experimental.pallas import tpu as pltpu

D_MODEL = 1024
DEPTH = 2
CHUNK = 64
HEAD_DIM = 64
POOL_CH = 64
POOL_W = 256
POOL_HALO = 16
SB_W = 384
RET_W = 384
RET_HEADS = 6
IN_W = POOL_W + 3 * SB_W + 4 * RET_W
ROPE_BASE = 10000.0
N_EXPERTS = 32
TOP_K = 4
D_FF = D_MODEL
SWIGLU_LIMIT = 7.0
SWIGLU_ALPHA = 1.702
DN_ALPHA = (2.0 * DEPTH) ** 0.25
LN_EPS = 1e-5

LANES = 128
VMEM_LIMIT = 56 * 1024 * 1024

TM_ROW = 1024
OUTPROJ_SUB = 256
TQ_SB = 512
SB_ROWS = 64
SB_SPAN = 256
TS_RET = 1024
TM_MOE = 512
RC_MOE = 96
RC_MAIN = 80
FFN_TILES = 8
SLOTS = N_EXPERTS * RC_MOE
DISPATCH_GROUP = 8
SB_SKIP = -100.0

BF16 = jnp.bfloat16
F32 = jnp.float32
ONEHOT_DT = jnp.float8_e4m3fn

_NT = (((1,), (1,)), ((), ()))
_TN = (((0,), (0,)), ((), ()))


def _dot(a, b):
    return jnp.dot(a, b, preferred_element_type=F32)


def _params(*sem):
    return pltpu.CompilerParams(dimension_semantics=sem, vmem_limit_bytes=VMEM_LIMIT)


def _pool_mix(cur, halo, t0, w, scale):
    ts = cur.shape[0]
    ext = jnp.concatenate([halo, cur], axis=0)
    a2 = ext[1:] + ext[:-1]
    a4 = a2[2:] + a2[:-2]
    a8 = a4[4:] + a4[:-4]
    a16 = a8[8:] + a8[:-8]
    lane = lax.broadcasted_iota(jnp.int32, (ts, POOL_W), 1)
    grp = lane // POOL_CH
    win = jnp.where(grp == 0, a2[15:15 + ts],
                    jnp.where(grp == 1, a4[13:13 + ts],
                              jnp.where(grp == 2, a8[9:9 + ts], a16[1:1 + ts])))
    width = jnp.where(grp == 0, 2, jnp.where(grp == 1, 4, jnp.where(grp == 2, 8, 16)))
    t = t0 + lax.broadcasted_iota(jnp.int32, (ts, POOL_W), 0)
    cnt = jnp.minimum(t + 1, width).astype(F32)
    pooled = win / cnt - cur
    return _dot(pooled.astype(BF16), w) * scale


def _inproj_kernel(tiles_per_seq, x_ref, w_ref, b_ref, pw_ref, ps_ref, pool_ref, sb_ref, ret_ref, halo_ref):
    it = pl.program_id(0) % tiles_per_seq
    xb = x_ref[...].astype(BF16)
    o1 = POOL_W + 3 * SB_W
    @pl.when(it == 0)
    def _():
        halo_ref[...] = jnp.zeros_like(halo_ref)

    u = _dot(xb, w_ref[:, :POOL_W]) + b_ref[:, :POOL_W]
    sb_ref[...] = (_dot(xb, w_ref[:, POOL_W:o1]) + b_ref[:, POOL_W:o1]).astype(BF16)
    ret_ref[...] = _dot(xb, w_ref[:, o1:]) + b_ref[:, o1:]
    pool_ref[...] = _pool_mix(u, halo_ref[...], it * TM_ROW, pw_ref[...], ps_ref[...]).astype(BF16)
    halo_ref[...] = u[TM_ROW - POOL_HALO:, :]


def _inproj(x2, w_bf, b, pool_w_bd, pool_scale, seq):
    n = x2.shape[0]
    fixed = lambda i: (0, 0)
    return pl.pallas_call(
        functools.partial(_inproj_kernel, seq // TM_ROW),
        grid=(n // TM_ROW,),
        in_specs=[
            pl.BlockSpec((TM_ROW, D_MODEL), lambda i: (i, 0)),
            pl.BlockSpec((D_MODEL, IN_W), fixed),
            pl.BlockSpec((1, IN_W), fixed),
            pl.BlockSpec((POOL_W, POOL_W), fixed),
            pl.BlockSpec((1, POOL_W), fixed),
        ],
        out_specs=[
            pl.BlockSpec((TM_ROW, POOL_W), lambda i: (i, 0)),
            pl.BlockSpec((TM_ROW, 3 * SB_W), lambda i: (i, 0)),
            pl.BlockSpec((TM_ROW, 4 * RET_W), lambda i: (i, 0)),
        ],
        out_shape=[
            jax.ShapeDtypeStruct((n, POOL_W), BF16),
            jax.ShapeDtypeStruct((n, 3 * SB_W), BF16),
            jax.ShapeDtypeStruct((n, 4 * RET_W), F32),
        ],
        scratch_shapes=[pltpu.VMEM((POOL_HALO, POOL_W), F32)],
        compiler_params=_params("arbitrary"),
        name="inproj_pool",
    )(x2, w_bf, b, pool_w_bd, pool_scale)


def _sb_kernel(q_ref, k_ref, v_ref, tri_ref, o_ref, acc_ref, carry_ref):
    qi = pl.program_id(1)
    tq = q_ref.shape[1]
    groups = q_ref.shape[2] // LANES
    lane = lax.broadcasted_iota(jnp.int32, (tq, LANES), 1)
    qh = []
    for p in range(groups):
        q = q_ref[0, :, p * LANES:(p + 1) * LANES] * jnp.asarray(HEAD_DIM ** -0.5, BF16)
        zero = jnp.zeros_like(q)
        qh += [jnp.where(lane < HEAD_DIM, q, zero), jnp.where(lane >= HEAD_DIM, q, zero)]

    def log_keep(z):
        return jnp.minimum(-z, 0.0) - jnp.log(1.0 + jnp.exp(-jnp.abs(z)))

    def suffix_sums(lk):
        lk = lk.astype(BF16)
        out = []
        for b in range(lk.shape[1] // LANES):
            rs = _dot(lk[:, b * LANES:(b + 1) * LANES], tri_ref[...])
            out.append((rs[:, :LANES], rs[:, LANES:]))
        return out

    t0 = qi * tq
    halves = range(tq // SB_ROWS)
    heads = range(2 * groups)
    cols = [slice((h // 2) * LANES, (h // 2 + 1) * LANES) for h in heads]
    rows = [slice(u * SB_ROWS, (u + 1) * SB_ROWS) for u in halves]
    starts = [pl.multiple_of(jnp.maximum(t0 + (u + 1) * SB_ROWS - SB_SPAN, 0), SB_ROWS) for u in halves]
    below = []
    for u in halves:
        qpos = t0 + u * SB_ROWS + lax.broadcasted_iota(jnp.int32, (SB_ROWS, SB_SPAN), 0)
        kpos = starts[u] + lax.broadcasted_iota(jnp.int32, (SB_ROWS, SB_SPAN), 1)
        below.append(kpos < qpos)
    items = [(u, h) for u in halves for h in heads]
    zs = {(u, h): lax.dot_general(qh[h][rows[u]], k_ref[0, pl.ds(starts[u], SB_SPAN), cols[h]], _NT,
                                  preferred_element_type=F32) for u, h in items}
    sums = {(u, h): suffix_sums(jnp.where(below[u], log_keep(zs[u, h]), 0.0)) for u, h in items}
    weights, runs = {}, {}
    for u, h in items:
        run = None
        r = [None] * len(sums[u, h])
        for b in reversed(range(len(r))):
            r_in, tot = sums[u, h][b]
            r[b] = r_in if run is None else r_in + run
            run = tot if run is None else run + tot
        weights[u, h] = jnp.where(below[u], jnp.exp(zs[u, h] + jnp.concatenate(r, axis=1)), 0.0).astype(BF16)
        runs[u, h] = run
    for u, h in items:
        acc_ref[h, rows[u], :] = _dot(weights[u, h], v_ref[0, pl.ds(starts[u], SB_SPAN), cols[h]])
        carry_ref[h, rows[u], :] = runs[u, h]
    worst = []
    for u in halves:
        top = runs[u, 0]
        for h in heads[1:]:
            top = jnp.maximum(top, runs[u, h])
        worst.append(jnp.max(top))

    def tile(u, kb):
        start = pl.multiple_of(kb * LANES, LANES)
        older = (start + lax.broadcasted_iota(jnp.int32, (SB_ROWS, LANES), 1)) < starts[u]
        worst = None
        for h in heads:
            k = k_ref[0, pl.ds(start, LANES), cols[h]]
            v = v_ref[0, pl.ds(start, LANES), cols[h]]
            z = lax.dot_general(qh[h][rows[u]], k, _NT, preferred_element_type=F32)
            (r_in, tot), = suffix_sums(jnp.where(older, log_keep(z), 0.0))
            carry = carry_ref[h, rows[u], :]
            a = jnp.where(older, jnp.exp(z + r_in + carry), 0.0)
            acc_ref[h, rows[u], :] += _dot(a.astype(BF16), v)
            carry = carry + tot
            carry_ref[h, rows[u], :] = carry
            m = jnp.max(carry)
            worst = m if worst is None else jnp.maximum(worst, m)
        return worst

    def cond(state):
        kb, worst = state
        return jnp.logical_and(kb >= 0, worst > SB_SKIP)

    for u in halves:
        lax.while_loop(cond, lambda state, u=u: (state[0] - 1, tile(u, state[0])),
                       ((starts[u] + LANES - 1) // LANES - 1, worst[u]))
    for p in range(groups):
        o_ref[0, :, p * LANES:(p + 1) * LANES] = jnp.where(
            lane < HEAD_DIM, acc_ref[2 * p], acc_ref[2 * p + 1]).astype(BF16)


def _stick_breaking(sb, tri2, batch, seq):
    sb3 = sb.reshape(batch, seq, 3 * SB_W)
    heads = SB_W // HEAD_DIM
    once = pl.Buffered(1)
    out = pl.pallas_call(
        _sb_kernel,
        grid=(batch, seq // TQ_SB),
        in_specs=[
            pl.BlockSpec((1, TQ_SB, SB_W), lambda b, i: (b, i, 0)),
            pl.BlockSpec((1, seq, SB_W), lambda b, i: (b, 0, 1), pipeline_mode=once),
            pl.BlockSpec((1, seq, SB_W), lambda b, i: (b, 0, 2), pipeline_mode=once),
            pl.BlockSpec((LANES, 2 * LANES), lambda b, i: (0, 0)),
        ],
        out_specs=pl.BlockSpec((1, TQ_SB, SB_W), lambda b, i: (b, i, 0)),
        out_shape=jax.ShapeDtypeStruct((batch, seq, SB_W), BF16),
        scratch_shapes=[
            pltpu.VMEM((heads, TQ_SB, LANES), F32),
            pltpu.VMEM((heads, TQ_SB, LANES), F32),
        ],
        compiler_params=_params("parallel", "arbitrary"),
        name="stick_breaking",
    )(sb3, sb3, sb3, tri2)
    return out.reshape(batch * seq, SB_W)


def _ret_kernel(q_ref, k_ref, v_ref, g_ref, cos_ref, sin_ref, dmat_ref, dec_ref, xi_ref, gch_ref,
                swap_ref, avg_ref, ng_ref, o_ref, state_ref, obuf_ref):
    si = pl.program_id(1)
    ts = q_ref.shape[1]
    groups = q_ref.shape[2] // LANES
    cols = [slice(p * LANES, (p + 1) * LANES) for p in range(groups)]

    @pl.when(si == 0)
    def _():
        state_ref[...] = jnp.zeros_like(state_ref)

    cos = cos_ref[...]
    sin = sin_ref[...]

    def lane_mix(x, m_ref):
        hi = x.astype(BF16)
        lo = (x - hi.astype(F32)).astype(BF16)
        return _dot(jnp.concatenate([hi, lo], axis=1), m_ref[...])

    def rot(x):
        return x * cos + lane_mix(x, swap_ref) * sin

    qr = [rot(q_ref[0, :, c]) for c in cols]
    kr = [rot(k_ref[0, :, c]) * (HEAD_DIM ** -0.5) for c in cols]
    v = [v_ref[0, :, c] for c in cols]

    lane_c = lax.broadcasted_iota(jnp.int32, (CHUNK, LANES), 1)
    head0 = lane_c < HEAD_DIM
    r_i = lax.broadcasted_iota(jnp.int32, (LANES, LANES), 0) // HEAD_DIM
    c_i = lax.broadcasted_iota(jnp.int32, (LANES, LANES), 1) // HEAD_DIM
    same_head = r_i == c_i

    chunks = [slice(n * CHUNK, (n + 1) * CHUNK) for n in range(ts // CHUNK)]
    items = [(p, n) for p in range(groups) for n in range(len(chunks))]
    kcb = {(p, n): kr[p][chunks[n]].astype(BF16) for p, n in items}
    vcb = {(p, n): v[p][chunks[n]].astype(BF16) for p, n in items}
    kvs = {(p, n): lax.dot_general(kcb[p, n], (v[p][chunks[n]] * dec_ref[p]).astype(BF16), _TN,
                                   preferred_element_type=F32) for p, n in items}
    scores = {}
    for p, n in items:
        qc = qr[p][chunks[n]]
        qstack = jnp.concatenate([jnp.where(head0, qc, 0.0), jnp.where(head0, 0.0, qc)], axis=0)
        scores[p, n] = (lax.dot_general(qstack.astype(BF16), kcb[p, n], _NT, preferred_element_type=F32)
                        * dmat_ref[p]).astype(BF16)
    o2s = {(p, n): _dot(scores[p, n], vcb[p, n]) for p, n in items}
    states = {}
    for p in range(groups):
        state = state_ref[p]
        for n in range(len(chunks)):
            states[p, n] = state.astype(BF16)
            state = gch_ref[p] * state + jnp.where(same_head, kvs[p, n], 0.0)
        state_ref[p] = state
    for p, n in items:
        o_intra = jnp.where(head0, o2s[p, n][:CHUNK], o2s[p, n][CHUNK:])
        o_inter = _dot((qr[p][chunks[n]] * xi_ref[p]).astype(BF16), states[p, n])
        obuf_ref[chunks[n], cols[p]] = o_intra + o_inter

    for c in cols:
        o = obuf_ref[:, c]
        mu = lane_mix(o, avg_ref)
        oc = o - mu
        var = lane_mix(oc * oc, avg_ref)
        normed = oc * lax.rsqrt(var + LN_EPS) * ng_ref[:, c]
        gate = g_ref[0, :, c]
        o_ref[0, :, c] = (gate / (1.0 + jnp.exp(-gate)) * normed).astype(BF16)


def _retention(ret, consts, norm_g, batch, seq):
    cos, sin, dmat, dec, xi, gch, swap, avg = consts
    ret3 =ret.reshape(batch, seq, 4 * RET_W)
    pairs = RET_W // LANES
    fixed3 = lambda b, i: (0, 0, 0)
    out = pl.pallas_call(
        _ret_kernel,
        grid=(batch, seq // TS_RET),
        in_specs=[
            pl.BlockSpec((1, TS_RET, RET_W), lambda b, i: (b, i, 0)),
            pl.BlockSpec((1, TS_RET, RET_W), lambda b, i: (b, i, 1)),
            pl.BlockSpec((1, TS_RET, RET_W), lambda b, i: (b, i, 2)),
            pl.BlockSpec((1, TS_RET, RET_W), lambda b, i: (b, i, 3)),
            pl.BlockSpec((TS_RET, LANES), lambda b, i: (i, 0)),
            pl.BlockSpec((TS_RET, LANES), lambda b, i: (i, 0)),
            pl.BlockSpec((pairs, 2 * CHUNK, CHUNK), fixed3),
            pl.BlockSpec((pairs, CHUNK, LANES), fixed3),
            pl.BlockSpec((pairs, CHUNK, LANES), fixed3),
            pl.BlockSpec((pairs, 1, LANES), fixed3),
            pl.BlockSpec((2 * LANES, LANES), lambda b, i: (0, 0)),
            pl.BlockSpec((2 * LANES, LANES), lambda b, i: (0, 0)),
            pl.BlockSpec((1, RET_W), lambda b, i: (0, 0)),
        ],
        out_specs=pl.BlockSpec((1, TS_RET, RET_W), lambda b, i: (b, i, 0)),
        out_shape=jax.ShapeDtypeStruct((batch, seq, RET_W), BF16),
        scratch_shapes=[
            pltpu.VMEM((pairs, LANES, LANES), F32),
            pltpu.VMEM((TS_RET, RET_W), F32),
        ],
        compiler_params=_params("parallel", "arbitrary"),
        name="retention",
    )(ret3, ret3, ret3, ret3, cos, sin, dmat, dec, xi, gch, swap, avg, norm_g)
    return out.reshape(batch * seq, RET_W)


def _retention_consts(seq):
    half = HEAD_DIM // 2
    pos = jnp.arange(seq, dtype=F32)
    inv = ROPE_BASE ** (-jnp.arange(half, dtype=F32) / half)
    ang = pos[:, None] * inv[None, :]
    cos_h, sin_h = jnp.cos(ang), jnp.sin(ang)
    cos = jnp.tile(cos_h, (1, 2 * LANES // HEAD_DIM))
    sin = jnp.tile(jnp.concatenate([-sin_h, sin_h], axis=1), (1, LANES // HEAD_DIM))
    log_g = jnp.log(1.0 - 2.0 ** (-5.0 - jnp.arange(RET_HEADS, dtype=F32)))
    c = jnp.arange(CHUNK, dtype=F32)
    dmat = jnp.exp(jnp.abs(c[:, None] - c[None, :])[None] * log_g[:, None, None])
    dmat = dmat.reshape(RET_HEADS // 2, 2 * CHUNK, CHUNK)
    lane_log_g = jnp.repeat(log_g, HEAD_DIM).reshape(RET_HEADS // 2, 1, LANES)
    dec = jnp.exp((CHUNK - 1 - c)[None, :, None] * lane_log_g)
    xi = jnp.exp((c + 1.0)[None, :, None] * lane_log_g)
    gch = jnp.exp(CHUNK * lane_log_g)
    l = np.arange(LANES)
    partner = np.where(l % HEAD_DIM < half, l + half, l - half)
    swap = (l[:, None] == partner[None, :]).astype(np.float32)
    avg = (l[:, None] // HEAD_DIM == l[None, :] // HEAD_DIM).astype(np.float32) / HEAD_DIM
    swap = jnp.asarray(np.concatenate([swap, swap], axis=0), BF16)
    avg = jnp.asarray(np.concatenate([avg, avg], axis=0), BF16)
    return cos, sin, dmat, dec, xi, gch, swap, avg


def _layer_norm(z, g, b):
    mu = jnp.mean(z, axis=-1, keepdims=True)
    zc = z - mu
    var = jnp.mean(zc * zc, axis=-1, keepdims=True)
    return zc * lax.rsqrt(var + LN_EPS) * g + b


def _outproj_kernel(x_ref, p_ref, s_ref, r_ref, w_ref, b_ref, g_ref, be_ref, rw_ref, rb_ref,
                    x1_ref, x1b_ref, gates_ref):
    o1 = POOL_W + SB_W
    tm = x_ref.shape[0]
    subs = [slice(s, s + OUTPROJ_SUB) for s in range(0, tm, OUTPROJ_SUB)]
    ys = [_dot(p_ref[sl, :], w_ref[:POOL_W, :]) + _dot(s_ref[sl, :], w_ref[POOL_W:o1, :])
          + _dot(r_ref[sl, :], w_ref[o1:, :]) for sl in subs]
    x1s = [_layer_norm(DN_ALPHA * x_ref[sl, :] + (y + b_ref[...]), g_ref[...], be_ref[...])
           for sl, y in zip(subs, ys)]
    lane = lax.broadcasted_iota(jnp.int32, (OUTPROJ_SUB, LANES), 1).astype(F32)
    logits = []
    for sl, x1 in zip(subs, x1s):
        x1_ref[sl, :] = x1
        x_hi = x1.astype(BF16)
        x1b_ref[sl, :] = x_hi
        x_mid = (x1 - x_hi.astype(F32)).astype(BF16)
        t = _dot(x_hi, rw_ref[...])
        logits.append(t[:, :LANES] + t[:, LANES:] + _dot(x_mid, rw_ref[:, :LANES]) + rb_ref[...])
    for sl, lg in zip(subs, logits):
        vals = jnp.where(lane < N_EXPERTS, lg, -jnp.inf)
        top_v, top_sel = [], []
        for _ in range(TOP_K):
            m = jnp.max(vals, axis=-1, keepdims=True)
            idx = jnp.min(jnp.where(vals == m, lane, float(LANES)), axis=-1, keepdims=True)
            sel = lane == idx
            vals = jnp.where(sel, -jnp.inf, vals)
            top_v.append(m)
            top_sel.append(sel)
        ex = [jnp.exp(m - top_v[0]) for m in top_v]
        den = ex[0] + ex[1] + ex[2] + ex[3]
        gates = jnp.full((OUTPROJ_SUB, LANES), -1.0, F32)
        for sel, e in zip(top_sel, ex):
            gates = jnp.where(sel, e / den, gates)
        gates_ref[sl, :] = gates


def _outproj(x2, pool_o, sb_o, ret_o, w_bf, b, g, be, rw, rb):
    n = x2.shape[0]
    row = lambda i: (i, 0)
    fixed = lambda i: (0, 0)
    return pl.pallas_call(
        _outproj_kernel,
        grid=(n // TM_ROW,),
        in_specs=[
            pl.BlockSpec((TM_ROW, D_MODEL), row),
            pl.BlockSpec((TM_ROW, POOL_W), row),
            pl.BlockSpec((TM_ROW, SB_W), row),
            pl.BlockSpec((TM_ROW, RET_W), row),
            pl.BlockSpec((D_MODEL, D_MODEL), fixed),
            pl.BlockSpec((1, D_MODEL), fixed),
            pl.BlockSpec((1, D_MODEL), fixed),
            pl.BlockSpec((1, D_MODEL), fixed),
            pl.BlockSpec((D_MODEL, 2 * LANES), fixed),
            pl.BlockSpec((1, LANES), fixed),
        ],
        out_specs=[
            pl.BlockSpec((TM_ROW, D_MODEL), row),
            pl.BlockSpec((TM_ROW, D_MODEL), row),
            pl.BlockSpec((TM_ROW, LANES), row),
        ],
        out_shape=[
            jax.ShapeDtypeStruct((n, D_MODEL), F32),
            jax.ShapeDtypeStruct((n, D_MODEL), BF16),
            jax.ShapeDtypeStruct((n, LANES), F32),
        ],
        compiler_params=_params("parallel"),
        name="outproj_ln_router",
    )(x2, pool_o, sb_o, ret_o, w_bf, b, g, be, rw, rb)


def _dispatch_kernel(gates_ref, xb_ref, tril_ref, xs_ref, gs_ref, oh_ref, post_ref, gatet_ref, cnt_ref):
    tm = gates_ref.shape[0]
    gates = gates_ref[...]
    sel = gates >= 0.0
    self_ = jnp.where(sel, 1.0, 0.0)
    incl = _dot(tril_ref[...], self_.astype(BF16))
    pos = incl - self_
    cnt_ref[0] = incl[tm - 1:tm, :].astype(jnp.int32)
    post =jnp.where(sel, pos, -1.0).T[:N_EXPERTS, :]
    gatet = gates.T[:N_EXPERTS, :]
    post_ref[0] = post
    gatet_ref[0] = gatet
    post_i = post.astype(jnp.int32)
    slot = lax.broadcasted_iota(jnp.int32, (RC_MOE, tm), 0)
    for e0 in range(0, N_EXPERTS, DISPATCH_GROUP):
        pieces = []
        for e in range(e0, e0 + DISPATCH_GROUP):
            onehot = slot == post_i[e:e + 1, :]
            pieces.append(jnp.where(onehot, 1.0, 0.0).astype(BF16))
            gate = jnp.sum(jnp.where(onehot, gatet[e:e + 1, :], 0.0), axis=1, keepdims=True)
            gs_ref[0, e] = jnp.broadcast_to(gate, (RC_MOE, LANES))
        onehots = jnp.concatenate(pieces, axis=0)
        oh_ref[0, e0 * RC_MOE:(e0 + DISPATCH_GROUP) * RC_MOE, :] = onehots.astype(ONEHOT_DT)
        xs = _dot(onehots, xb_ref[...]).astype(BF16)
        xs_ref[0, e0:e0 + DISPATCH_GROUP] = xs.reshape(DISPATCH_GROUP, RC_MOE, D_MODEL)


def _dispatch(gates, x1b, tril):
    n = gates.shape[0]
    nt = n // TM_MOE
    return pl.pallas_call(
        _dispatch_kernel,
        grid=(nt,),
        in_specs=[
            pl.BlockSpec((TM_MOE, LANES), lambda i: (i, 0)),
            pl.BlockSpec((TM_MOE, D_MODEL), lambda i: (i, 0)),
            pl.BlockSpec((TM_MOE, TM_MOE), lambda i: (0, 0)),
        ],
        out_specs=[
            pl.BlockSpec((1, N_EXPERTS, RC_MOE, D_MODEL), lambda i: (i, 0, 0, 0)),
            pl.BlockSpec((1, N_EXPERTS, RC_MOE, LANES), lambda i: (i, 0, 0, 0)),
            pl.BlockSpec((1, SLOTS, TM_MOE), lambda i: (i, 0, 0)),
            pl.BlockSpec((1, N_EXPERTS, TM_MOE), lambda i: (i, 0, 0)),
            pl.BlockSpec((1, N_EXPERTS, TM_MOE), lambda i: (i, 0, 0)),
            pl.BlockSpec((1, 1, LANES), lambda i: (i, 0, 0)),
        ],
        out_shape=[
            jax.ShapeDtypeStruct((nt, N_EXPERTS, RC_MOE, D_MODEL), BF16),
            jax.ShapeDtypeStruct((nt, N_EXPERTS, RC_MOE, LANES), F32),
            jax.ShapeDtypeStruct((nt, SLOTS, TM_MOE), ONEHOT_DT),
            jax.ShapeDtypeStruct((nt, N_EXPERTS, TM_MOE), F32),
            jax.ShapeDtypeStruct((nt, N_EXPERTS, TM_MOE), F32),
            jax.ShapeDtypeStruct((nt, 1, LANES), jnp.int32),
        ],
        compiler_params=_params("parallel"),
        name="dispatch",
    )(gates, x1b, tril)


def _swiglu_ffn(x, wgu, bgu, wd, bd):
    h = _dot(x, wgu) + bgu
    g = jnp.minimum(h[:, :D_FF], SWIGLU_LIMIT)
    up = jnp.clip(h[:, D_FF:], -SWIGLU_LIMIT, SWIGLU_LIMIT)
    act = (up + 1.0) * (g / (1.0 + jnp.exp(-SWIGLU_ALPHA * g)))
    return _dot(act.astype(BF16), wd) + bd


def _ffn_kernel(tail_ref, xs_ref, gs_ref, wgu_ref, bgu_ref, wd_ref, bd_ref, ys_ref, wgu_bf_ref, wd_bf_ref):
    e = pl.program_id(0)
    j = pl.program_id(1)

    @pl.when(j == 0)
    def _():
        wgu_bf_ref[0] = wgu_ref[0, 0].astype(BF16)
        wd_bf_ref[0] = wd_ref[0, 0].astype(BF16)

    def run(lo, hi):
        rows = FFN_TILES * (hi - lo)
        x = xs_ref[:, 0, lo:hi, :].reshape(rows, D_MODEL)
        y = _swiglu_ffn(x, wgu_bf_ref[0], bgu_ref[0], wd_bf_ref[0], bd_ref[0])
        gate = gs_ref[:, 0, lo:hi, :].reshape(rows, LANES)[:, :1]
        ys_ref[:, 0, lo:hi, :] = (y * gate).astype(BF16).reshape(FFN_TILES, hi - lo, D_MODEL)

    run(0, RC_MAIN)
    used = tail_ref[e * pl.num_programs(1) + j] > 0

    @pl.when(used)
    def _():
        run(RC_MAIN, RC_MOE)

    @pl.when(jnp.logical_not(used))
    def _():
        ys_ref[:, 0, RC_MAIN:RC_MOE, :] = jnp.zeros((FFN_TILES, RC_MOE - RC_MAIN, D_MODEL), BF16)


def _ffn(tail, xs, gs, w_gate_up, bgu, w_down, bd, layer):
    nt = xs.shape[0]
    grid_spec = pltpu.PrefetchScalarGridSpec(
        num_scalar_prefetch=1,
        grid=(N_EXPERTS, nt // FFN_TILES),
        in_specs=[
            pl.BlockSpec((FFN_TILES, 1, RC_MOE, D_MODEL), lambda e, j, t: (j, e, 0, 0)),
            pl.BlockSpec((FFN_TILES, 1, RC_MOE, LANES), lambda e, j, t: (j, e, 0, 0)),
            pl.BlockSpec((1, 1, D_MODEL, 2 * D_FF), lambda e, j, t: (layer, e, 0, 0)),
            pl.BlockSpec((1, 1, 2 * D_FF), lambda e, j, t: (e, 0, 0)),
            pl.BlockSpec((1, 1, D_FF, D_MODEL), lambda e, j, t: (layer, e, 0, 0)),
            pl.BlockSpec((1, 1, D_MODEL), lambda e, j, t: (e, 0, 0)),
        ],
        out_specs=[
            pl.BlockSpec((FFN_TILES, 1, RC_MOE, D_MODEL), lambda e, j, t: (j, e, 0, 0)),
            pl.BlockSpec((1, D_MODEL, 2 * D_FF), lambda e, j, t: (e, 0, 0)),
            pl.BlockSpec((1, D_FF, D_MODEL), lambda e, j, t: (e, 0, 0)),
        ],
    )
    return pl.pallas_call(
        _ffn_kernel,
        grid_spec=grid_spec,
        out_shape=[
            jax.ShapeDtypeStruct(xs.shape, BF16),
            jax.ShapeDtypeStruct((N_EXPERTS, D_MODEL, 2 * D_FF), BF16),
            jax.ShapeDtypeStruct((N_EXPERTS, D_FF, D_MODEL), BF16),
        ],
        compiler_params=_params("parallel", "arbitrary"),
        name="expert_ffn",
    )(tail, xs, gs, w_gate_up, bgu, w_down, bd)


def _combine_kernel(cnt_ref, ys_ref, oh_ref, x1_ref, g_ref, b_ref, post_ref,
                    gatet_ref, bgu_ref, bd_ref, wgu_hbm, wd_hbm, o_ref, acc_ref, wgu_buf, wd_buf, sem):
    i = pl.program_id(0)
    tm = x1_ref.shape[0]
    width = DISPATCH_GROUP * RC_MOE
    acc = None
    for e0 in range(0, N_EXPERTS, DISPATCH_GROUP):
        sl = slice(e0 * RC_MOE, e0 * RC_MOE + width)
        part = lax.dot_general(oh_ref[0, sl, :].astype(BF16),
                               ys_ref[0, e0:e0 + DISPATCH_GROUP].reshape(width, D_MODEL),
                               _TN, preferred_element_type=F32)
        acc = part if acc is None else acc + part
    acc_ref[...] = acc

    slot = lax.broadcasted_iota(jnp.int32, (RC_MOE, tm), 0)

    def per_expert(e, carry):
        cnt = cnt_ref[i * N_EXPERTS + e]

        @pl.when(cnt > RC_MOE)
        def _():
            copies = (pltpu.make_async_copy(wgu_hbm.at[e], wgu_buf, sem.at[0]),
                      pltpu.make_async_copy(wd_hbm.at[e], wd_buf, sem.at[1]))
            for cp in copies:
                cp.start()
            for cp in copies:
                cp.wait()
            prow = post_ref[0, pl.ds(e, 1), :].astype(jnp.int32)
            grow = gatet_ref[0, pl.ds(e, 1), :]

            def chunk(c, carry2):
                onehot = (slot + c * RC_MOE) == prow
                oh = jnp.where(onehot, 1.0, 0.0).astype(BF16)
                xg = _dot(oh, x1_ref[...].astype(BF16)).astype(BF16)
                y = _swiglu_ffn(xg, wgu_buf[...], bgu_ref[e], wd_buf[...], bd_ref[e])
                gate = jnp.sum(jnp.where(onehot, grow, 0.0), axis=1, keepdims=True)
                yg = (y * gate).astype(BF16)
                acc_ref[...] += lax.dot_general(oh, yg, _TN, preferred_element_type=F32)
                return carry2

            lax.fori_loop(1, (cnt + RC_MOE - 1) // RC_MOE, chunk, 0)

        return carry

    lax.fori_loop(0, N_EXPERTS, per_expert, 0)
    o_ref[...] = _layer_norm(DN_ALPHA * x1_ref[...] + acc_ref[...], g_ref[...], b_ref[...])


def _combine(cnt, ys, oh, x1, g, b, post, gatet, bgu, bd, wgu, wd):
    n = x1.shape[0]
    nt = n // TM_MOE
    row = lambda i, c: (i, 0)
    fixed2 = lambda i, c: (0, 0)
    fixed3 = lambda i, c: (0, 0, 0)
    grid_spec = pltpu.PrefetchScalarGridSpec(
        num_scalar_prefetch=1,
        grid=(nt,),
        in_specs=[
            pl.BlockSpec((1, N_EXPERTS, RC_MOE, D_MODEL), lambda i, c: (i, 0, 0, 0)),
            pl.BlockSpec((1, SLOTS, TM_MOE), lambda i, c: (i, 0, 0)),
            pl.BlockSpec((TM_MOE, D_MODEL), row),
            pl.BlockSpec((1, D_MODEL), fixed2),
            pl.BlockSpec((1, D_MODEL), fixed2),
            pl.BlockSpec((1, N_EXPERTS, TM_MOE), lambda i, c: (i, 0, 0)),
            pl.BlockSpec((1, N_EXPERTS, TM_MOE), lambda i, c: (i, 0, 0)),
            pl.BlockSpec((N_EXPERTS, 1, 2 * D_FF), fixed3),
            pl.BlockSpec((N_EXPERTS, 1, D_MODEL), fixed3),
            pl.BlockSpec(memory_space=pl.ANY),
            pl.BlockSpec(memory_space=pl.ANY),
        ],
        out_specs=pl.BlockSpec((TM_MOE, D_MODEL), row),
        scratch_shapes=[
            pltpu.VMEM((TM_MOE, D_MODEL), F32),
            pltpu.VMEM((D_MODEL, 2 * D_FF), BF16),
            pltpu.VMEM((D_FF, D_MODEL), BF16),
            pltpu.SemaphoreType.DMA((2,)),
        ],
    )
    return pl.pallas_call(
        _combine_kernel,
        grid_spec=grid_spec,
        out_shape=jax.ShapeDtypeStruct((n, D_MODEL), F32),
        compiler_params=_params("arbitrary"),
        name="combine_ln",
    )(cnt, ys, oh, x1, g, b, post, gatet, bgu, bd, wgu, wd)


def _block_diag(pool_w):
    groups = pool_w.shape[0]
    out = jnp.zeros((POOL_W, POOL_W), pool_w.dtype)
    for gi in range(groups):
        out = out.at[gi * POOL_CH:(gi + 1) * POOL_CH, gi * POOL_CH:(gi + 1) * POOL_CH].set(pool_w[gi])
    return out


def _layer(x2, batch, seq, consts, layer, w_in, b_in, pool_w, pool_scale, ret_norm_g, w_out, b_out, ln1_g,
           ln1_b, router_w, router_b, w_gate_up, b_gate_up, w_down, b_down, ln2_g, ln2_b):
    ret_consts, tri2, tril = consts
    row = lambda a: a.reshape(1, -1).astype(F32)
    pool_o, sb, ret = _inproj(x2, w_in.astype(BF16), row(b_in), _block_diag(pool_w).astype(BF16),
                              row(pool_scale), seq)
    sb_o = _stick_breaking(sb, tri2, batch, seq)
    ret_o = _retention(ret, ret_consts, row(ret_norm_g), batch, seq)
    rw = jnp.pad(router_w.astype(F32), ((0, 0), (0, LANES - N_EXPERTS)))
    rw_hi = rw.astype(BF16)
    rw = jnp.concatenate([rw_hi, (rw - rw_hi.astype(F32)).astype(BF16)], axis=1)
    rb = jnp.pad(router_b.astype(F32), (0, LANES - N_EXPERTS)).reshape(1, LANES)
    x1, x1b, gates = _outproj(x2, pool_o, sb_o, ret_o, w_out.astype(BF16), row(b_out), row(ln1_g),
                              row(ln1_b), rw, rb)
    xs, gs, oh, post, gatet, cnt = _dispatch(gates, x1b, tril)
    bgu = b_gate_up.reshape(N_EXPERTS, 1, 2 * D_FF).astype(F32)
    bd = b_down.reshape(N_EXPERTS, 1, D_MODEL).astype(F32)
    cnt2 = cnt[:, 0, :N_EXPERTS]
    tail = jnp.any((cnt2 > RC_MAIN).reshape(-1, FFN_TILES, N_EXPERTS), axis=1)
    ys, wgu, wd = _ffn(tail.T.reshape(-1).astype(jnp.int32), xs, gs, w_gate_up, bgu, w_down, bd, layer)
    cnt_flat = cnt2.reshape(-1)
    return _combine(cnt_flat, ys, oh, x1, row(ln2_g), row(ln2_b), post, gatet, bgu, bd, wgu, wd)


def kernel(x, w_in, b_in, pool_w, pool_scale, ret_norm_g, w_out, b_out, ln1_g, ln1_b, router_w, router_b,
           w_gate_up, b_gate_up, w_down, b_down, ln2_g, ln2_b):
    batch, seq, d = x.shape
    n = batch * seq
    assert d == D_MODEL and seq % TS_RET == 0 and seq % TQ_SB == 0 and seq % TM_ROW == 0
    assert n % (TM_MOE * FFN_TILES) == 0 and n % TM_ROW == 0 and seq >= SB_SPAN
    j = np.arange(LANES)
    tri = (j[:, None] >= j[None, :]).astype(np.float32)
    tri2 = jnp.asarray(np.concatenate([tri, np.ones_like(tri)], axis=1), BF16)
    r = np.arange(TM_MOE)
    tril = jnp.asarray((r[:, None] >= r[None, :]).astype(np.float32), BF16)
    consts = (_retention_consts(seq), tri2, tril)
    x2 = x.reshape(n, d)
    for l in range(DEPTH):
        x2 = _layer(x2, batch, seq, consts, l, w_in[l], b_in[l], pool_w[l], pool_scale[l], ret_norm_g[l],
                    w_out[l], b_out[l], ln1_g[l], ln1_b[l], router_w[l], router_b[l], w_gate_up,
                    b_gate_up[l], w_down, b_down[l], ln2_g[l], ln2_b[l])
    return x2.reshape(batch, seq, d)
```

```python
import functools

import numpy as np
import jax
import jax.numpy as jnp
from jax import lax
from jax.experimental import pallas as pl
from jax.experimental.pallas import tpu as pltpu

D_MODEL = 1024
DEPTH = 2
CHUNK = 64
HEAD_DIM = 64
POOL_CH = 64
POOL_W = 256
POOL_HALO = 16
SB_W = 384
RET_W = 384
RET_HEADS = 6
IN_W = POOL_W + 3 * SB_W + 4 * RET_W
ROPE_BASE = 10000.0
N_EXPERTS = 32
TOP_K = 4
D_FF = D_MODEL
SWIGLU_LIMIT = 7.0
SWIGLU_ALPHA = 1.702
DN_ALPHA = (2.0 * DEPTH) ** 0.25
LN_EPS = 1e-5

LANES = 128
VMEM_LIMIT = 56 * 1024 * 1024

TM_ROW = 1024
OUTPROJ_SUB = 256
TQ_SB = 512
SB_ROWS = 64
SB_SPAN = 256
TS_RET = 1024
TM_MOE = 512
RC_MOE = 96
RC_MAIN = 80
FFN_TILES = 8
SLOTS = N_EXPERTS * RC_MOE
DISPATCH_GROUP = 8
SB_SKIP = -100.0

BF16 = jnp.bfloat16
F32 = jnp.float32
ONEHOT_DT = jnp.float8_e4m3fn

_NT = (((1,), (1,)), ((), ()))
_TN = (((0,), (0,)), ((), ()))


def _dot(a, b):
    return jnp.dot(a, b, preferred_element_type=F32)


def _params(*sem):
    return pltpu.CompilerParams(dimension_semantics=sem, vmem_limit_bytes=VMEM_LIMIT)


def _pool_mix(cur, halo, t0, w, scale):
    ts = cur.shape[0]
    ext = jnp.concatenate([halo, cur], axis=0)
    a2 = ext[1:] + ext[:-1]
    a4 = a2[2:] + a2[:-2]
    a8 = a4[4:] + a4[:-4]
    a16 = a8[8:] + a8[:-8]
    lane = lax.broadcasted_iota(jnp.int32, (ts, POOL_W), 1)
    grp = lane // POOL_CH
    win = jnp.where(grp == 0, a2[15:15 + ts],
                    jnp.where(grp == 1, a4[13:13 + ts],
                              jnp.where(grp == 2, a8[9:9 + ts], a16[1:1 + ts])))
    width = jnp.where(grp == 0, 2, jnp.where(grp == 1, 4, jnp.where(grp == 2, 8, 16)))
    t = t0 + lax.broadcasted_iota(jnp.int32, (ts, POOL_W), 0)
    cnt = jnp.minimum(t + 1, width).astype(F32)
    pooled = win / cnt - cur
    return _dot(pooled.astype(BF16), w) * scale


def _inproj_kernel(tiles_per_seq, x_ref, w_ref, b_ref, pw_ref, ps_ref, pool_ref, sb_ref, ret_ref, halo_ref):
    it = pl.program_id(0) % tiles_per_seq
    xb = x_ref[...].astype(BF16)
    o1 = POOL_W + 3 * SB_W
    @pl.when(it == 0)
    def _():
        halo_ref[...] = jnp.zeros_like(halo_ref)

    u = _dot(xb, w_ref[:, :POOL_W]) + b_ref[:, :POOL_W]
    sb_ref[...] = (_dot(xb, w_ref[:, POOL_W:o1]) + b_ref[:, POOL_W:o1]).astype(BF16)
    ret_ref[...] = _dot(xb, w_ref[:, o1:]) + b_ref[:, o1:]
    pool_ref[...] = _pool_mix(u, halo_ref[...], it * TM_ROW, pw_ref[...], ps_ref[...]).astype(BF16)
    halo_ref[...] = u[TM_ROW - POOL_HALO:, :]


def _inproj(x2, w_bf, b, pool_w_bd, pool_scale, seq):
    n = x2.shape[0]
    fixed = lambda i: (0, 0)
    return pl.pallas_call(
        functools.partial(_inproj_kernel, seq // TM_ROW),
        grid=(n // TM_ROW,),
        in_specs=[
            pl.BlockSpec((TM_ROW, D_MODEL), lambda i: (i, 0)),
            pl.BlockSpec((D_MODEL, IN_W), fixed),
            pl.BlockSpec((1, IN_W), fixed),
            pl.BlockSpec((POOL_W, POOL_W), fixed),
            pl.BlockSpec((1, POOL_W), fixed),
        ],
        out_specs=[
            pl.BlockSpec((TM_ROW, POOL_W), lambda i: (i, 0)),
            pl.BlockSpec((TM_ROW, 3 * SB_W), lambda i: (i, 0)),
            pl.BlockSpec((TM_ROW, 4 * RET_W), lambda i: (i, 0)),
        ],
        out_shape=[
            jax.ShapeDtypeStruct((n, POOL_W), BF16),
            jax.ShapeDtypeStruct((n, 3 * SB_W), BF16),
            jax.ShapeDtypeStruct((n, 4 * RET_W), F32),
        ],
        scratch_shapes=[pltpu.VMEM((POOL_HALO, POOL_W), F32)],
        compiler_params=_params("arbitrary"),
        name="inproj_pool",
    )(x2, w_bf, b, pool_w_bd, pool_scale)


def _sb_phases(q_ref, k_ref, v_ref, tri_ref, o_ref, acc_ref, carry_ref):
    qi = pl.program_id(1)
    tq = q_ref.shape[1]
    groups = q_ref.shape[2] // LANES
    lane = lax.broadcasted_iota(jnp.int32, (tq, LANES), 1)
    qh = []
    for p in range(groups):
        q = q_ref[0, :, p * LANES:(p + 1) * LANES] * jnp.asarray(HEAD_DIM ** -0.5, BF16)
        zero = jnp.zeros_like(q)
        qh += [jnp.where(lane < HEAD_DIM, q, zero), jnp.where(lane >= HEAD_DIM, q, zero)]

    def log_keep(z):
        return jnp.minimum(-z, 0.0) - jnp.log(1.0 + jnp.exp(-jnp.abs(z)))

    def suffix_sums(lk):
        lk = lk.astype(BF16)
        out = []
        for b in range(lk.shape[1] // LANES):
            rs = _dot(lk[:, b * LANES:(b + 1) * LANES], tri_ref[...])
            out.append((rs[:, :LANES], rs[:, LANES:]))
        return out

    t0 = qi * tq
    halves = range(tq // SB_ROWS)
    heads = range(2 * groups)
    cols = [slice((h // 2) * LANES, (h // 2 + 1) * LANES) for h in heads]
    rows = [slice(u * SB_ROWS, (u + 1) * SB_ROWS) for u in halves]
    starts = [pl.multiple_of(jnp.maximum(t0 + (u + 1) * SB_ROWS - SB_SPAN, 0), SB_ROWS) for u in halves]
    below = []
    for u in halves:
        qpos = t0 + u * SB_ROWS + lax.broadcasted_iota(jnp.int32, (SB_ROWS, SB_SPAN), 0)
        kpos = starts[u] + lax.broadcasted_iota(jnp.int32, (SB_ROWS, SB_SPAN), 1)
        below.append(kpos < qpos)
    items = [(u, h) for u in halves for h in heads]
    zs, sums, weights, runs = {}, {}, {}, {}
    for n, (u, h) in enumerate(items):
        zs[u, h] = lax.dot_general(qh[h][rows[u]], k_ref[0, pl.ds(starts[u], SB_SPAN), cols[h]], _NT,
                                   preferred_element_type=F32)
        if n % MIX_STEP == MIX_STEP - 1:
            yield
    for n, (u, h) in enumerate(items):
        sums[u, h] = suffix_sums(jnp.where(below[u], log_keep(zs[u, h]), 0.0))
        if n % MIX_STEP == MIX_STEP - 1:
            yield
    for n, (u, h) in enumerate(items):
        run = None
        r = [None] * len(sums[u, h])
        for b in reversed(range(len(r))):
            r_in, tot = sums[u, h][b]
            r[b] = r_in if run is None else r_in + run
            run = tot if run is None else run + tot
        weights[u, h] = jnp.where(below[u], jnp.exp(zs[u, h] + jnp.concatenate(r, axis=1)), 0.0).astype(BF16)
        runs[u, h] = run
        if n % MIX_STEP == MIX_STEP - 1:
            yield
    for n, (u, h) in enumerate(items):
        acc_ref[h, rows[u], :] = _dot(weights[u, h], v_ref[0, pl.ds(starts[u], SB_SPAN), cols[h]])
        carry_ref[h, rows[u], :] = runs[u, h]
        if n % MIX_STEP == MIX_STEP - 1:
            yield
    yield "tail"
    worst = []
    for u in halves:
        top = runs[u, 0]
        for h in heads[1:]:
            top = jnp.maximum(top, runs[u, h])
        worst.append(jnp.max(top))

    def tile(u, kb):
        start = pl.multiple_of(kb * LANES, LANES)
        older = (start + lax.broadcasted_iota(jnp.int32, (SB_ROWS, LANES), 1)) < starts[u]
        worst = None
        for h in heads:
            k = k_ref[0, pl.ds(start, LANES), cols[h]]
            v = v_ref[0, pl.ds(start, LANES), cols[h]]
            z = lax.dot_general(qh[h][rows[u]], k, _NT, preferred_element_type=F32)
            (r_in, tot), = suffix_sums(jnp.where(older, log_keep(z), 0.0))
            carry = carry_ref[h, rows[u], :]
            a = jnp.where(older, jnp.exp(z + r_in + carry), 0.0)
            acc_ref[h, rows[u], :] += _dot(a.astype(BF16), v)
            carry = carry + tot
            carry_ref[h, rows[u], :] = carry
            m = jnp.max(carry)
            worst = m if worst is None else jnp.maximum(worst, m)
        return worst

    def cond(state):
        kb, worst = state
        return jnp.logical_and(kb >= 0, worst > SB_SKIP)

    for u in halves:
        lax.while_loop(cond, lambda state, u=u: (state[0] - 1, tile(u, state[0])),
                       ((starts[u] + LANES - 1) // LANES - 1, worst[u]))
    for p in range(groups):
        o_ref[0, :, p * LANES:(p + 1) * LANES] = jnp.where(
            lane < HEAD_DIM, acc_ref[2 * p], acc_ref[2 * p + 1]).astype(BF16)


def _sb_kernel(*refs):
    for _ in _sb_phases(*refs):
        pass


def _stick_breaking(sb, tri2, batch, seq):
    sb3 = sb.reshape(batch, seq, 3 * SB_W)
    heads = SB_W // HEAD_DIM
    once = pl.Buffered(1)
    out = pl.pallas_call(
        _sb_kernel,
        grid=(batch, seq // TQ_SB),
        in_specs=[
            pl.BlockSpec((1, TQ_SB, SB_W), lambda b, i: (b, i, 0)),
            pl.BlockSpec((1, seq, SB_W), lambda b, i: (b, 0, 1), pipeline_mode=once),
            pl.BlockSpec((1, seq, SB_W), lambda b, i: (b, 0, 2), pipeline_mode=once),
            pl.BlockSpec((LANES, 2 * LANES), lambda b, i: (0, 0)),
        ],
        out_specs=pl.BlockSpec((1, TQ_SB, SB_W), lambda b, i: (b, i, 0)),
        out_shape=jax.ShapeDtypeStruct((batch, seq, SB_W), BF16),
        scratch_shapes=[
            pltpu.VMEM((heads, TQ_SB, LANES), F32),
            pltpu.VMEM((heads, TQ_SB, LANES), F32),
        ],
        compiler_params=_params("parallel", "arbitrary"),
        name="stick_breaking",
    )(sb3, sb3, sb3, tri2)
    return out.reshape(batch * seq, SB_W)


def _ret_phases(q_ref, k_ref, v_ref, g_ref, cos_ref, sin_ref, dmat_ref, dec_ref, xi_ref, gch_ref,
                swap_ref, avg_ref, ng_ref, o_ref, state_ref, obuf_ref):
    si = pl.program_id(1)
    ts = q_ref.shape[1]
    groups = q_ref.shape[2] // LANES
    cols = [slice(p * LANES, (p + 1) * LANES) for p in range(groups)]

    @pl.when(si == 0)
    def _():
        state_ref[...] = jnp.zeros_like(state_ref)

    cos = cos_ref[...]
    sin = sin_ref[...]

    def lane_mix(x, m_ref):
        hi = x.astype(BF16)
        lo = (x - hi.astype(F32)).astype(BF16)
        return _dot(jnp.concatenate([hi, lo], axis=1), m_ref[...])

    def rot(x):
        return x * cos + lane_mix(x, swap_ref) * sin

    qr = [rot(q_ref[0, :, c]) for c in cols]
    kr = [rot(k_ref[0, :, c]) * (HEAD_DIM ** -0.5) for c in cols]
    v = [v_ref[0, :, c] for c in cols]
    yield

    lane_c =lax.broadcasted_iota(jnp.int32, (CHUNK, LANES), 1)
    head0 = lane_c < HEAD_DIM
    r_i = lax.broadcasted_iota(jnp.int32, (LANES, LANES), 0) // HEAD_DIM
    c_i = lax.broadcasted_iota(jnp.int32, (LANES, LANES), 1) // HEAD_DIM
    same_head = r_i == c_i

    chunks = [slice(n * CHUNK, (n + 1) * CHUNK) for n in range(ts // CHUNK)]
    items = [(p, n) for p in range(groups) for n in range(len(chunks))]
    kcb, vcb, kvs, scores, o2s, states = {}, {}, {}, {}, {}, {}
    for i, (p, n) in enumerate(items):
        kcb[p, n] = kr[p][chunks[n]].astype(BF16)
        vcb[p, n] = v[p][chunks[n]].astype(BF16)
        kvs[p, n] = lax.dot_general(kcb[p, n], (v[p][chunks[n]] * dec_ref[p]).astype(BF16), _TN,
                                    preferred_element_type=F32)
        if i % MIX_STEP == MIX_STEP - 1:
            yield
    for i, (p, n) in enumerate(items):
        qc = qr[p][chunks[n]]
        qstack = jnp.concatenate([jnp.where(head0, qc, 0.0), jnp.where(head0, 0.0, qc)], axis=0)
        scores[p, n] = (lax.dot_general(qstack.astype(BF16), kcb[p, n], _NT, preferred_element_type=F32)
                        * dmat_ref[p]).astype(BF16)
        if i % MIX_STEP == MIX_STEP - 1:
            yield
    for i, (p, n) in enumerate(items):
        o2s[p, n] = _dot(scores[p, n], vcb[p, n])
        if i % MIX_STEP == MIX_STEP - 1:
            yield
    for p in range(groups):
        state = state_ref[p]
        for n in range(len(chunks)):
            states[p, n] = state.astype(BF16)
            state = gch_ref[p] * state + jnp.where(same_head, kvs[p, n], 0.0)
        state_ref[p] = state
        yield
    for i, (p, n) in enumerate(items):
        o_intra = jnp.where(head0, o2s[p, n][:CHUNK], o2s[p, n][CHUNK:])
        o_inter = _dot((qr[p][chunks[n]] * xi_ref[p]).astype(BF16), states[p, n])
        obuf_ref[chunks[n], cols[p]] = o_intra + o_inter
        if i % MIX_STEP == MIX_STEP - 1:
            yield

    for c in cols:
        o = obuf_ref[:, c]
        mu = lane_mix(o, avg_ref)
        oc = o - mu
        var = lane_mix(oc * oc, avg_ref)
        normed = oc * lax.rsqrt(var + LN_EPS) * ng_ref[:, c]
        gate = g_ref[0, :, c]
        o_ref[0, :, c] = (gate / (1.0 + jnp.exp(-gate)) * normed).astype(BF16)


def _ret_kernel(*refs):
    for _ in _ret_phases(*refs):
        pass


MIX_STEP = 4


def _mix_kernel(sq_ref, sk_ref, sv_ref, tri_ref, rq_ref, rk_ref, rv_ref, rg_ref, cos_ref, sin_ref, dmat_ref,
                dec_ref, xi_ref, gch_ref, swap_ref, avg_ref, ng_ref, so_ref, ro_ref, acc_ref, carry_ref,
                state_ref, obuf_ref):
    bodies = {
        "s": _sb_phases(sq_ref, sk_ref, sv_ref, tri_ref, so_ref, acc_ref, carry_ref),
        "r": _ret_phases(rq_ref, rk_ref, rv_ref, rg_ref, cos_ref, sin_ref, dmat_ref, dec_ref, xi_ref, gch_ref,
                         swap_ref, avg_ref, ng_ref, ro_ref, state_ref, obuf_ref),
    }
    live = ["r", "s"]
    while live:
        for which in list(live):
            if next(bodies[which], "done") in ("done", "tail"):
                live.remove(which)
    for _ in bodies["s"]:
        pass


def _mixers(sb, ret, tri2, consts, norm_g, batch, seq):
    cos, sin, dmat, dec, xi, gch, swap, avg = consts
    sb3 = sb.reshape(batch, seq, 3 * SB_W)
    ret3 = ret.reshape(batch, seq, 4 * RET_W)
    heads = SB_W // HEAD_DIM
    pairs = RET_W // LANES
    once = pl.Buffered(1)
    fixed2 = lambda b, i: (0, 0)
    fixed3 = lambda b, i: (0, 0, 0)
    tile = lambda c: (lambda b, i: (b, i, c))
    sb_o, ret_o = pl.pallas_call(
        _mix_kernel,
        grid=(batch, seq // TQ_SB),
        in_specs=[
            pl.BlockSpec((1, TQ_SB, SB_W), tile(0)),
            pl.BlockSpec((1, seq, SB_W), lambda b, i: (b, 0, 1), pipeline_mode=once),
            pl.BlockSpec((1, seq, SB_W), lambda b, i: (b, 0, 2), pipeline_mode=once),
            pl.BlockSpec((LANES, 2 * LANES), fixed2),
            pl.BlockSpec((1, TQ_SB, RET_W), tile(0)),
            pl.BlockSpec((1, TQ_SB, RET_W), tile(1)),
            pl.BlockSpec((1, TQ_SB, RET_W), tile(2)),
            pl.BlockSpec((1, TQ_SB, RET_W), tile(3)),
            pl.BlockSpec((TQ_SB, LANES), lambda b, i: (i, 0)),
            pl.BlockSpec((TQ_SB, LANES), lambda b, i: (i, 0)),
            pl.BlockSpec((pairs, 2 * CHUNK, CHUNK), fixed3),
            pl.BlockSpec((pairs, CHUNK, LANES), fixed3),
            pl.BlockSpec((pairs, CHUNK, LANES), fixed3),
            pl.BlockSpec((pairs, 1, LANES), fixed3),
            pl.BlockSpec((2 * LANES, LANES), fixed2),
            pl.BlockSpec((2 * LANES, LANES), fixed2),
            pl.BlockSpec((1, RET_W), fixed2),
        ],
        out_specs=[
            pl.BlockSpec((1, TQ_SB, SB_W), tile(0)),
            pl.BlockSpec((1, TQ_SB, RET_W), tile(0)),
        ],
        out_shape=[
            jax.ShapeDtypeStruct((batch, seq, SB_W), BF16),
            jax.ShapeDtypeStruct((batch, seq, RET_W), BF16),
        ],
        scratch_shapes=[
            pltpu.VMEM((heads, TQ_SB, LANES), F32),
            pltpu.VMEM((heads, TQ_SB, LANES), F32),
            pltpu.VMEM((pairs, LANES, LANES), F32),
            pltpu.VMEM((TQ_SB, RET_W), F32),
        ],
        compiler_params=_params("parallel", "arbitrary"),
        name="mixers",
    )(sb3, sb3, sb3, tri2, ret3, ret3, ret3, ret3, cos, sin, dmat, dec, xi, gch, swap, avg, norm_g)
    return sb_o.reshape(batch * seq, SB_W), ret_o.reshape(batch * seq, RET_W)


def _retention(ret, consts, norm_g, batch, seq):
    cos, sin, dmat, dec, xi, gch, swap, avg = consts
    ret3 =ret.reshape(batch, seq, 4 * RET_W)
    pairs = RET_W // LANES
    fixed3 = lambda b, i: (0, 0, 0)
    out = pl.pallas_call(
        _ret_kernel,
        grid=(batch, seq // TS_RET),
        in_specs=[
            pl.BlockSpec((1, TS_RET, RET_W), lambda b, i: (b, i, 0)),
            pl.BlockSpec((1, TS_RET, RET_W), lambda b, i: (b, i, 1)),
            pl.BlockSpec((1, TS_RET, RET_W), lambda b, i: (b, i, 2)),
            pl.BlockSpec((1, TS_RET, RET_W), lambda b, i: (b, i, 3)),
            pl.BlockSpec((TS_RET, LANES), lambda b, i: (i, 0)),
            pl.BlockSpec((TS_RET, LANES), lambda b, i: (i, 0)),
            pl.BlockSpec((pairs, 2 * CHUNK, CHUNK), fixed3),
            pl.BlockSpec((pairs, CHUNK, LANES), fixed3),
            pl.BlockSpec((pairs, CHUNK, LANES), fixed3),
            pl.BlockSpec((pairs, 1, LANES), fixed3),
            pl.BlockSpec((2 * LANES, LANES), lambda b, i: (0, 0)),
            pl.BlockSpec((2 * LANES, LANES), lambda b, i: (0, 0)),
            pl.BlockSpec((1, RET_W), lambda b, i: (0, 0)),
        ],
        out_specs=pl.BlockSpec((1, TS_RET, RET_W), lambda b, i: (b, i, 0)),
        out_shape=jax.ShapeDtypeStruct((batch, seq, RET_W), BF16),
        scratch_shapes=[
            pltpu.VMEM((pairs, LANES, LANES), F32),
            pltpu.VMEM((TS_RET, RET_W), F32),
        ],
        compiler_params=_params("parallel", "arbitrary"),
        name="retention",
    )(ret3, ret3, ret3, ret3, cos, sin, dmat, dec, xi, gch, swap, avg, norm_g)
    return out.reshape(batch * seq, RET_W)


def _retention_consts(seq):
    half = HEAD_DIM // 2
    pos = jnp.arange(seq, dtype=F32)
    inv = ROPE_BASE ** (-jnp.arange(half, dtype=F32) / half)
    ang = pos[:, None] * inv[None, :]
    cos_h, sin_h = jnp.cos(ang), jnp.sin(ang)
    cos = jnp.tile(cos_h, (1, 2 * LANES // HEAD_DIM))
    sin = jnp.tile(jnp.concatenate([-sin_h, sin_h], axis=1), (1, LANES // HEAD_DIM))
    log_g = jnp.log(1.0 - 2.0 ** (-5.0 - jnp.arange(RET_HEADS, dtype=F32)))
    c = jnp.arange(CHUNK, dtype=F32)
    dmat = jnp.exp(jnp.abs(c[:, None] - c[None, :])[None] * log_g[:, None, None])
    dmat = dmat.reshape(RET_HEADS // 2, 2 * CHUNK, CHUNK)
    lane_log_g = jnp.repeat(log_g, HEAD_DIM).reshape(RET_HEADS // 2, 1, LANES)
    dec = jnp.exp((CHUNK - 1 - c)[None, :, None] * lane_log_g)
    xi = jnp.exp((c + 1.0)[None, :, None] * lane_log_g)
    gch = jnp.exp(CHUNK * lane_log_g)
    l = np.arange(LANES)
    partner = np.where(l % HEAD_DIM < half, l + half, l - half)
    swap = (l[:, None] == partner[None, :]).astype(np.float32)
    avg = (l[:, None] // HEAD_DIM == l[None, :] // HEAD_DIM).astype(np.float32) / HEAD_DIM
    swap = jnp.asarray(np.concatenate([swap, swap], axis=0), BF16)
    avg = jnp.asarray(np.concatenate([avg, avg], axis=0), BF16)
    return cos, sin, dmat, dec, xi, gch, swap, avg


def _layer_norm(z, g, b):
    mu = jnp.mean(z, axis=-1, keepdims=True)
    zc = z - mu
    var = jnp.mean(zc * zc, axis=-1, keepdims=True)
    return zc * lax.rsqrt(var + LN_EPS) * g + b


def _outproj_kernel(x_ref, p_ref, s_ref, r_ref, w_ref, b_ref, g_ref, be_ref, rw_ref, rb_ref,
                    x1_ref, x1b_ref, gates_ref):
    o1 = POOL_W + SB_W
    tm = x_ref.shape[0]
    subs = [slice(s, s + OUTPROJ_SUB) for s in range(0, tm, OUTPROJ_SUB)]
    ys = [_dot(p_ref[sl, :], w_ref[:POOL_W, :]) + _dot(s_ref[sl, :], w_ref[POOL_W:o1, :])
          + _dot(r_ref[sl, :], w_ref[o1:, :]) for sl in subs]
    x1s = [_layer_norm(DN_ALPHA * x_ref[sl, :] + (y + b_ref[...]), g_ref[...], be_ref[...])
           for sl, y in zip(subs, ys)]
    lane = lax.broadcasted_iota(jnp.int32, (OUTPROJ_SUB, LANES), 1).astype(F32)
    logits = []
    for sl, x1 in zip(subs, x1s):
        x1_ref[sl, :] = x1
        x_hi = x1.astype(BF16)
        x1b_ref[sl, :] = x_hi
        x_mid = (x1 - x_hi.astype(F32)).astype(BF16)
        t = _dot(x_hi, rw_ref[...])
        logits.append(t[:, :LANES] + t[:, LANES:] + _dot(x_mid, rw_ref[:, :LANES]) + rb_ref[...])
    for sl, lg in zip(subs, logits):
        vals = jnp.where(lane < N_EXPERTS, lg, -jnp.inf)
        top_v, top_sel = [], []
        for _ in range(TOP_K):
            m = jnp.max(vals, axis=-1, keepdims=True)
            idx = jnp.min(jnp.where(vals == m, lane, float(LANES)), axis=-1, keepdims=True)
            sel = lane == idx
            vals = jnp.where(sel, -jnp.inf, vals)
            top_v.append(m)
            top_sel.append(sel)
        ex = [jnp.exp(m - top_v[0]) for m in top_v]
        den = ex[0] + ex[1] + ex[2] + ex[3]
        gates = jnp.full((OUTPROJ_SUB, LANES), -1.0, F32)
        for sel, e in zip(top_sel, ex):
            gates = jnp.where(sel, e / den, gates)
        gates_ref[sl, :] = gates


def _outproj(x2, pool_o, sb_o, ret_o, w_bf, b, g, be, rw, rb):
    n = x2.shape[0]
    row = lambda i: (i, 0)
    fixed = lambda i: (0, 0)
    return pl.pallas_call(
        _outproj_kernel,
        grid=(n // TM_ROW,),
        in_specs=[
            pl.BlockSpec((TM_ROW, D_MODEL), row),
            pl.BlockSpec((TM_ROW, POOL_W), row),
            pl.BlockSpec((TM_ROW, SB_W), row),
            pl.BlockSpec((TM_ROW, RET_W), row),
            pl.BlockSpec((D_MODEL, D_MODEL), fixed),
            pl.BlockSpec((1, D_MODEL), fixed),
            pl.BlockSpec((1, D_MODEL), fixed),
            pl.BlockSpec((1, D_MODEL), fixed),
            pl.BlockSpec((D_MODEL, 2 * LANES), fixed),
            pl.BlockSpec((1, LANES), fixed),
        ],
        out_specs=[
            pl.BlockSpec((TM_ROW, D_MODEL), row),
            pl.BlockSpec((TM_ROW, D_MODEL), row),
            pl.BlockSpec((TM_ROW, LANES), row),
        ],
        out_shape=[
            jax.ShapeDtypeStruct((n, D_MODEL), F32),
            jax.ShapeDtypeStruct((n, D_MODEL), BF16),
            jax.ShapeDtypeStruct((n, LANES), F32),
        ],
        compiler_params=_params("parallel"),
        name="outproj_ln_router",
    )(x2, pool_o, sb_o, ret_o, w_bf, b, g, be, rw, rb)


def _dispatch_kernel(gates_ref, xb_ref, tril_ref, xs_ref, gs_ref, oh_ref, post_ref, gatet_ref, cnt_ref):
    tm = gates_ref.shape[0]
    gates = gates_ref[...]
    sel = gates >= 0.0
    self_ = jnp.where(sel, 1.0, 0.0)
    incl = _dot(tril_ref[...], self_.astype(BF16))
    pos = incl - self_
    cnt_ref[0] = incl[tm - 1:tm, :].astype(jnp.int32)
    post =jnp.where(sel, pos, -1.0).T[:N_EXPERTS, :]
    gatet = gates.T[:N_EXPERTS, :]
    post_ref[0] = post
    gatet_ref[0] = gatet
    post_i = post.astype(jnp.int32)
    slot = lax.broadcasted_iota(jnp.int32, (RC_MOE, tm), 0)
    for e0 in range(0, N_EXPERTS, DISPATCH_GROUP):
        pieces = []
        for e in range(e0, e0 + DISPATCH_GROUP):
            onehot = slot == post_i[e:e + 1, :]
            pieces.append(jnp.where(onehot, 1.0, 0.0).astype(BF16))
            gate = jnp.sum(jnp.where(onehot, gatet[e:e + 1, :], 0.0), axis=1, keepdims=True)
            gs_ref[0, e] = jnp.broadcast_to(gate, (RC_MOE, LANES))
        onehots = jnp.concatenate(pieces, axis=0)
        oh_ref[0, e0 * RC_MOE:(e0 + DISPATCH_GROUP) * RC_MOE, :] = onehots.astype(ONEHOT_DT)
        xs = _dot(onehots, xb_ref[...]).astype(BF16)
        xs_ref[0, e0:e0 + DISPATCH_GROUP] = xs.reshape(DISPATCH_GROUP, RC_MOE, D_MODEL)


def _dispatch(gates, x1b, tril):
    n = gates.shape[0]
    nt = n // TM_MOE
    return pl.pallas_call(
        _dispatch_kernel,
        grid=(nt,),
        in_specs=[
            pl.BlockSpec((TM_MOE, LANES), lambda i: (i, 0)),
            pl.BlockSpec((TM_MOE, D_MODEL), lambda i: (i, 0)),
            pl.BlockSpec((TM_MOE, TM_MOE), lambda i: (0, 0)),
        ],
        out_specs=[
            pl.BlockSpec((1, N_EXPERTS, RC_MOE, D_MODEL), lambda i: (i, 0, 0, 0)),
            pl.BlockSpec((1, N_EXPERTS, RC_MOE, LANES), lambda i: (i, 0, 0, 0)),
            pl.BlockSpec((1, SLOTS, TM_MOE), lambda i: (i, 0, 0)),
            pl.BlockSpec((1, N_EXPERTS, TM_MOE), lambda i: (i, 0, 0)),
            pl.BlockSpec((1, N_EXPERTS, TM_MOE), lambda i: (i, 0, 0)),
            pl.BlockSpec((1, 1, LANES), lambda i: (i, 0, 0)),
        ],
        out_shape=[
            jax.ShapeDtypeStruct((nt, N_EXPERTS, RC_MOE, D_MODEL), BF16),
            jax.ShapeDtypeStruct((nt, N_EXPERTS, RC_MOE, LANES), F32),
            jax.ShapeDtypeStruct((nt, SLOTS, TM_MOE), ONEHOT_DT),
            jax.ShapeDtypeStruct((nt, N_EXPERTS, TM_MOE), F32),
            jax.ShapeDtypeStruct((nt, N_EXPERTS, TM_MOE), F32),
            jax.ShapeDtypeStruct((nt, 1, LANES), jnp.int32),
        ],
        compiler_params=_params("parallel"),
        name="dispatch",
    )(gates, x1b, tril)


def _swiglu_ffn(x, wgu, bgu, wd, bd):
    h = _dot(x, wgu) + bgu
    g = jnp.minimum(h[:, :D_FF], SWIGLU_LIMIT)
    up = jnp.clip(h[:, D_FF:], -SWIGLU_LIMIT, SWIGLU_LIMIT)
    act = (up + 1.0) * (g / (1.0 + jnp.exp(-SWIGLU_ALPHA * g)))
    return _dot(act.astype(BF16), wd) + bd


def _ffn_kernel(tail_ref, xs_ref, gs_ref, wgu_ref, bgu_ref, wd_ref, bd_ref, ys_ref, wgu_bf_ref, wd_bf_ref):
    e = pl.program_id(0)
    j = pl.program_id(1)

    @pl.when(j == 0)
    def _():
        wgu_bf_ref[0] = wgu_ref[0, 0].astype(BF16)
        wd_bf_ref[0] = wd_ref[0, 0].astype(BF16)

    def run(lo, hi):
        rows = FFN_TILES * (hi - lo)
        x = xs_ref[:, 0, lo:hi, :].reshape(rows, D_MODEL)
        y = _swiglu_ffn(x, wgu_bf_ref[0], bgu_ref[0], wd_bf_ref[0], bd_ref[0])
        gate = gs_ref[:, 0, lo:hi, :].reshape(rows, LANES)[:, :1]
        ys_ref[:, 0, lo:hi, :] = (y * gate).astype(BF16).reshape(FFN_TILES, hi - lo, D_MODEL)

    run(0, RC_MAIN)
    used = tail_ref[e * pl.num_programs(1) + j] > 0

    @pl.when(used)
    def _():
        run(RC_MAIN, RC_MOE)

    @pl.when(jnp.logical_not(used))
    def _():
        ys_ref[:, 0, RC_MAIN:RC_MOE, :] = jnp.zeros((FFN_TILES, RC_MOE - RC_MAIN, D_MODEL), BF16)


def _ffn(tail, xs, gs, w_gate_up, bgu, w_down, bd, layer):
    nt = xs.shape[0]
    grid_spec = pltpu.PrefetchScalarGridSpec(
        num_scalar_prefetch=1,
        grid=(N_EXPERTS, nt // FFN_TILES),
        in_specs=[
            pl.BlockSpec((FFN_TILES, 1, RC_MOE, D_MODEL), lambda e, j, t: (j, e, 0, 0)),
            pl.BlockSpec((FFN_TILES, 1, RC_MOE, LANES), lambda e, j, t: (j, e, 0, 0)),
            pl.BlockSpec((1, 1, D_MODEL, 2 * D_FF), lambda e, j, t: (layer, e, 0, 0)),
            pl.BlockSpec((1, 1, 2 * D_FF), lambda e, j, t: (e, 0, 0)),
            pl.BlockSpec((1, 1, D_FF, D_MODEL), lambda e, j, t: (layer, e, 0, 0)),
            pl.BlockSpec((1, 1, D_MODEL), lambda e, j, t: (e, 0, 0)),
        ],
        out_specs=[
            pl.BlockSpec((FFN_TILES, 1, RC_MOE, D_MODEL), lambda e, j, t: (j, e, 0, 0)),
            pl.BlockSpec((1, D_MODEL, 2 * D_FF), lambda e, j, t: (e, 0, 0)),
            pl.BlockSpec((1, D_FF, D_MODEL), lambda e, j, t: (e, 0, 0)),
        ],
    )
    return pl.pallas_call(
        _ffn_kernel,
        grid_spec=grid_spec,
        out_shape=[
            jax.ShapeDtypeStruct(xs.shape, BF16),
            jax.ShapeDtypeStruct((N_EXPERTS, D_MODEL, 2 * D_FF), BF16),
            jax.ShapeDtypeStruct((N_EXPERTS, D_FF, D_MODEL), BF16),
        ],
        compiler_params=_params("parallel", "arbitrary"),
        name="expert_ffn",
    )(tail, xs, gs, w_gate_up, bgu, w_down, bd)


def _combine_kernel(cnt_ref, ys_ref, oh_ref, x1_ref, g_ref, b_ref, post_ref,
                    gatet_ref, bgu_ref, bd_ref, wgu_hbm, wd_hbm, o_ref, acc_ref, wgu_buf, wd_buf, sem):
    i = pl.program_id(0)
    tm = x1_ref.shape[0]
    width = DISPATCH_GROUP * RC_MOE
    acc = None
    for e0 in range(0, N_EXPERTS, DISPATCH_GROUP):
        sl = slice(e0 * RC_MOE, e0 * RC_MOE + width)
        part = lax.dot_general(oh_ref[0, sl, :].astype(BF16),
                               ys_ref[0, e0:e0 + DISPATCH_GROUP].reshape(width, D_MODEL),
                               _TN, preferred_element_type=F32)
        acc = part if acc is None else acc + part
    acc_ref[...] = acc

    slot = lax.broadcasted_iota(jnp.int32, (RC_MOE, tm), 0)

    def per_expert(e, carry):
        cnt = cnt_ref[i * N_EXPERTS + e]

        @pl.when(cnt > RC_MOE)
        def _():
            copies = (pltpu.make_async_copy(wgu_hbm.at[e], wgu_buf, sem.at[0]),
                      pltpu.make_async_copy(wd_hbm.at[e], wd_buf, sem.at[1]))
            for cp in copies:
                cp.start()
            for cp in copies:
                cp.wait()
            prow = post_ref[0, pl.ds(e, 1), :].astype(jnp.int32)
            grow = gatet_ref[0, pl.ds(e, 1), :]

            def chunk(c, carry2):
                onehot = (slot + c * RC_MOE) == prow
                oh = jnp.where(onehot, 1.0, 0.0).astype(BF16)
                xg = _dot(oh, x1_ref[...].astype(BF16)).astype(BF16)
                y = _swiglu_ffn(xg, wgu_buf[...], bgu_ref[e], wd_buf[...], bd_ref[e])
                gate = jnp.sum(jnp.where(onehot, grow, 0.0), axis=1, keepdims=True)
                yg = (y * gate).astype(BF16)
                acc_ref[...] += lax.dot_general(oh, yg, _TN, preferred_element_type=F32)
                return carry2

            lax.fori_loop(1, (cnt + RC_MOE - 1) // RC_MOE, chunk, 0)

        return carry

    lax.fori_loop(0, N_EXPERTS, per_expert, 0)
    o_ref[...] = _layer_norm(DN_ALPHA * x1_ref[...] + acc_ref[...], g_ref[...], b_ref[...])


def _combine(cnt, ys, oh, x1, g, b, post, gatet, bgu, bd, wgu, wd):
    n = x1.shape[0]
    nt = n // TM_MOE
    row = lambda i, c: (i, 0)
    fixed2 = lambda i, c: (0, 0)
    fixed3 = lambda i, c: (0, 0, 0)
    grid_spec = pltpu.PrefetchScalarGridSpec(
        num_scalar_prefetch=1,
        grid=(nt,),
        in_specs=[
            pl.BlockSpec((1, N_EXPERTS, RC_MOE, D_MODEL), lambda i, c: (i, 0, 0, 0)),
            pl.BlockSpec((1, SLOTS, TM_MOE), lambda i, c: (i, 0, 0)),
            pl.BlockSpec((TM_MOE, D_MODEL), row),
            pl.BlockSpec((1, D_MODEL), fixed2),
            pl.BlockSpec((1, D_MODEL), fixed2),
            pl.BlockSpec((1, N_EXPERTS, TM_MOE), lambda i, c: (i, 0, 0)),
            pl.BlockSpec((1, N_EXPERTS, TM_MOE), lambda i, c: (i, 0, 0)),
            pl.BlockSpec((N_EXPERTS, 1, 2 * D_FF), fixed3),
            pl.BlockSpec((N_EXPERTS, 1, D_MODEL), fixed3),
            pl.BlockSpec(memory_space=pl.ANY),
            pl.BlockSpec(memory_space=pl.ANY),
        ],
        out_specs=pl.BlockSpec((TM_MOE, D_MODEL), row),
        scratch_shapes=[
            pltpu.VMEM((TM_MOE, D_MODEL), F32),
            pltpu.VMEM((D_MODEL, 2 * D_FF), BF16),
            pltpu.VMEM((D_FF, D_MODEL), BF16),
            pltpu.SemaphoreType.DMA((2,)),
        ],
    )
    return pl.pallas_call(
        _combine_kernel,
        grid_spec=grid_spec,
        out_shape=jax.ShapeDtypeStruct((n, D_MODEL), F32),
        compiler_params=_params("arbitrary"),
        name="combine_ln",
    )(cnt, ys, oh, x1, g, b, post, gatet, bgu, bd, wgu, wd)


def _block_diag(pool_w):
    groups = pool_w.shape[0]
    out = jnp.zeros((POOL_W, POOL_W), pool_w.dtype)
    for gi in range(groups):
        out = out.at[gi * POOL_CH:(gi + 1) * POOL_CH, gi * POOL_CH:(gi + 1) * POOL_CH].set(pool_w[gi])
    return out


def _layer(x2, batch, seq, consts, layer, w_in, b_in, pool_w, pool_scale, ret_norm_g, w_out, b_out, ln1_g,
           ln1_b, router_w, router_b, w_gate_up, b_gate_up, w_down, b_down, ln2_g, ln2_b):
    ret_consts, tri2, tril = consts
    row = lambda a: a.reshape(1, -1).astype(F32)
    pool_o, sb, ret = _inproj(x2, w_in.astype(BF16), row(b_in), _block_diag(pool_w).astype(BF16),
                              row(pool_scale), seq)
    sb_o, ret_o = _mixers(sb, ret, tri2, ret_consts, row(ret_norm_g), batch, seq)
    rw = jnp.pad(router_w.astype(F32), ((0, 0), (0, LANES - N_EXPERTS)))
    rw_hi = rw.astype(BF16)
    rw = jnp.concatenate([rw_hi, (rw - rw_hi.astype(F32)).astype(BF16)], axis=1)
    rb = jnp.pad(router_b.astype(F32), (0, LANES - N_EXPERTS)).reshape(1, LANES)
    x1, x1b, gates = _outproj(x2, pool_o, sb_o, ret_o, w_out.astype(BF16), row(b_out), row(ln1_g),
                              row(ln1_b), rw, rb)
    xs, gs, oh, post, gatet, cnt = _dispatch(gates, x1b, tril)
    bgu = b_gate_up.reshape(N_EXPERTS, 1, 2 * D_FF).astype(F32)
    bd = b_down.reshape(N_EXPERTS, 1, D_MODEL).astype(F32)
    cnt2 = cnt[:, 0, :N_EXPERTS]
    tail = jnp.any((cnt2 > RC_MAIN).reshape(-1, FFN_TILES, N_EXPERTS), axis=1)
    ys, wgu, wd = _ffn(tail.T.reshape(-1).astype(jnp.int32), xs, gs, w_gate_up, bgu, w_down, bd, layer)
    cnt_flat = cnt2.reshape(-1)
    return _combine(cnt_flat, ys, oh, x1, row(ln2_g), row(ln2_b), post, gatet, bgu, bd, wgu, wd)


def kernel(x, w_in, b_in, pool_w, pool_scale, ret_norm_g, w_out, b_out, ln1_g, ln1_b, router_w, router_b,
           w_gate_up, b_gate_up, w_down, b_down, ln2_g, ln2_b):
    batch, seq, d = x.shape
    n = batch * seq
    assert d == D_MODEL and seq % TS_RET == 0 and seq % TQ_SB == 0 and seq % TM_ROW == 0
    assert n % (TM_MOE * FFN_TILES) == 0 and n % TM_ROW == 0 and seq >= SB_SPAN
    j = np.arange(LANES)
    tri = (j[:, None] >= j[None, :]).astype(np.float32)
    tri2 = jnp.asarray(np.concatenate([tri, np.ones_like(tri)], axis=1), BF16)
    r = np.arange(TM_MOE)
    tril = jnp.asarray((r[:, None] >= r[None, :]).astype(np.float32), BF16)
    consts = (_retention_consts(seq), tri2, tril)
    x2 = x.reshape(n, d)
    for l in range(DEPTH):
        x2 = _layer(x2, batch, seq, consts, l, w_in[l], b_in[l], pool_w[l], pool_scale[l], ret_norm_g[l],
                    w_out[l], b_out[l], ln1_g[l], ln1_b[l], router_w[l], router_b[l], w_gate_up,
                    b_gate_up[l], w_down, b_down[l], ln2_g[l], ln2_b[l])
    return x2.reshape(batch, seq, d)
```

```python
import functools

import numpy as np
import jax
import jax.numpy as jnp
from jax import lax
from jax.experimental import pallas as pl
from jax.experimental.pallas import tpu as pltpu

D_MODEL = 1024
DEPTH = 2
CHUNK = 64
HEAD_DIM = 64
POOL_CH = 64
POOL_W = 256
POOL_HALO = 16
SB_W = 384
RET_W = 384
RET_HEADS = 6
IN_W = POOL_W + 3 * SB_W + 4 * RET_W
ROPE_BASE = 10000.0
N_EXPERTS = 32
TOP_K = 4
D_FF = D_MODEL
SWIGLU_LIMIT = 7.0
SWIGLU_ALPHA = 1.702
DN_ALPHA = (2.0 * DEPTH) ** 0.25
LN_EPS = 1e-5

LANES = 128
VMEM_LIMIT = 56 * 1024 * 1024

TM_ROW = 1024
OUTPROJ_SUB = 256
TQ_SB = 512
SB_ROWS = 64
SB_SPAN = 256
TS_RET = 1024
TM_MOE = 512
RC_MOE = 96
RC_MAIN = 80
FFN_TILES = 8
SLOTS = N_EXPERTS * RC_MOE
DISPATCH_GROUP = 8
SB_SKIP = -100.0

BF16 = jnp.bfloat16
F32 = jnp.float32
ONEHOT_DT = jnp.float8_e4m3fn

_NT = (((1,), (1,)), ((), ()))
_TN = (((0,), (0,)), ((), ()))


def _dot(a, b):
    return jnp.dot(a, b, preferred_element_type=F32)


def _params(*sem):
    return pltpu.CompilerParams(dimension_semantics=sem, vmem_limit_bytes=VMEM_LIMIT)


def _pool_mix(cur, halo, t0, w, scale):
    ts = cur.shape[0]
    ext = jnp.concatenate([halo, cur], axis=0)
    a2 = ext[1:] + ext[:-1]
    a4 = a2[2:] + a2[:-2]
    a8 = a4[4:] + a4[:-4]
    a16 = a8[8:] + a8[:-8]
    lane = lax.broadcasted_iota(jnp.int32, (ts, POOL_W), 1)
    grp = lane // POOL_CH
    win = jnp.where(grp == 0, a2[15:15 + ts],
                    jnp.where(grp == 1, a4[13:13 + ts],
                              jnp.where(grp == 2, a8[9:9 + ts], a16[1:1 + ts])))
    width = jnp.where(grp == 0, 2, jnp.where(grp == 1, 4, jnp.where(grp == 2, 8, 16)))
    t = t0 + lax.broadcasted_iota(jnp.int32, (ts, POOL_W), 0)
    cnt = jnp.minimum(t + 1, width).astype(F32)
    pooled = win / cnt - cur
    return _dot(pooled.astype(BF16), w) * scale


def _inproj_kernel(tiles_per_seq, x_ref, w_ref, b_ref, pw_ref, ps_ref, pool_ref, sb_ref, ret_ref, halo_ref):
    it = pl.program_id(0) % tiles_per_seq
    xb = x_ref[...].astype(BF16)
    o1 = POOL_W + 3 * SB_W
    @pl.when(it == 0)
    def _():
        halo_ref[...] = jnp.zeros_like(halo_ref)

    u = _dot(xb, w_ref[:, :POOL_W]) + b_ref[:, :POOL_W]
    sb_ref[...] = (_dot(xb, w_ref[:, POOL_W:o1]) + b_ref[:, POOL_W:o1]).astype(BF16)
    ret_ref[...] = _dot(xb, w_ref[:, o1:]) + b_ref[:, o1:]
    pool_ref[...] = _pool_mix(u, halo_ref[...], it * TM_ROW, pw_ref[...], ps_ref[...]).astype(BF16)
    halo_ref[...] = u[TM_ROW - POOL_HALO:, :]


def _inproj(x2, w_bf, b, pool_w_bd, pool_scale, seq):
    n = x2.shape[0]
    fixed = lambda i: (0, 0)
    return pl.pallas_call(
        functools.partial(_inproj_kernel, seq // TM_ROW),
        grid=(n // TM_ROW,),
        in_specs=[
            pl.BlockSpec((TM_ROW, D_MODEL), lambda i: (i, 0)),
            pl.BlockSpec((D_MODEL, IN_W), fixed),
            pl.BlockSpec((1, IN_W), fixed),
            pl.BlockSpec((POOL_W, POOL_W), fixed),
            pl.BlockSpec((1, POOL_W), fixed),
        ],
        out_specs=[
            pl.BlockSpec((TM_ROW, POOL_W), lambda i: (i, 0)),
            pl.BlockSpec((TM_ROW, 3 * SB_W), lambda i: (i, 0)),
            pl.BlockSpec((TM_ROW, 4 * RET_W), lambda i: (i, 0)),
        ],
        out_shape=[
            jax.ShapeDtypeStruct((n, POOL_W), BF16),
            jax.ShapeDtypeStruct((n, 3 * SB_W), BF16),
            jax.ShapeDtypeStruct((n, 4 * RET_W), F32),
        ],
        scratch_shapes=[pltpu.VMEM((POOL_HALO, POOL_W), F32)],
        compiler_params=_params("arbitrary"),
        name="inproj_pool",
    )(x2, w_bf, b, pool_w_bd, pool_scale)


def _sb_kernel(q_ref, k_ref, v_ref, tri_ref, o_ref, acc_ref, carry_ref):
    qi = pl.program_id(1)
    tq = q_ref.shape[1]
    groups = q_ref.shape[2] // LANES
    lane = lax.broadcasted_iota(jnp.int32, (tq, LANES), 1)
    qh = []
    for p in range(groups):
        q = q_ref[0, :, p * LANES:(p + 1) * LANES] * jnp.asarray(HEAD_DIM ** -0.5, BF16)
        zero = jnp.zeros_like(q)
        qh += [jnp.where(lane < HEAD_DIM, q, zero), jnp.where(lane >= HEAD_DIM, q, zero)]

    def log_keep(z):
        return jnp.minimum(-z, 0.0) - jnp.log(1.0 + jnp.exp(-jnp.abs(z)))

    def suffix_sums(lk):
        lk = lk.astype(BF16)
        out = []
        for b in range(lk.shape[1] // LANES):
            rs = _dot(lk[:, b * LANES:(b + 1) * LANES], tri_ref[...])
            out.append((rs[:, :LANES], rs[:, LANES:]))
        return out

    t0 = qi * tq
    halves = range(tq // SB_ROWS)
    heads = range(2 * groups)
    cols = [slice((h // 2) * LANES, (h // 2 + 1) * LANES) for h in heads]
    rows = [slice(u * SB_ROWS, (u + 1) * SB_ROWS) for u in halves]
    starts = [pl.multiple_of(jnp.maximum(t0 + (u + 1) * SB_ROWS - SB_SPAN, 0), SB_ROWS) for u in halves]
    below = []
    for u in halves:
        qpos = t0 + u * SB_ROWS + lax.broadcasted_iota(jnp.int32, (SB_ROWS, SB_SPAN), 0)
        kpos = starts[u] + lax.broadcasted_iota(jnp.int32, (SB_ROWS, SB_SPAN), 1)
        below.append(kpos < qpos)
    items = [(u, h) for u in halves for h in heads]
    zs = {(u, h): lax.dot_general(qh[h][rows[u]], k_ref[0, pl.ds(starts[u], SB_SPAN), cols[h]], _NT,
                                  preferred_element_type=F32) for u, h in items}
    sums = {(u, h): suffix_sums(jnp.where(below[u], log_keep(zs[u, h]), 0.0)) for u, h in items}
    weights, runs = {}, {}
    for u, h in items:
        run = None
        r = [None] * len(sums[u, h])
        for b in reversed(range(len(r))):
            r_in, tot = sums[u, h][b]
            r[b] = r_in if run is None else r_in + run
            run = tot if run is None else run + tot
        weights[u, h] = jnp.where(below[u], jnp.exp(zs[u, h] + jnp.concatenate(r, axis=1)), 0.0).astype(BF16)
        runs[u, h] = run
    for u, h in items:
        acc_ref[h, rows[u], :] = _dot(weights[u, h], v_ref[0, pl.ds(starts[u], SB_SPAN), cols[h]])
        carry_ref[h, rows[u], :] = runs[u, h]
    worst = []
    for u in halves:
        top = runs[u, 0]
        for h in heads[1:]:
            top = jnp.maximum(top, runs[u, h])
        worst.append(jnp.max(top))

    def tile(u, kb):
        start = pl.multiple_of(kb * LANES, LANES)
        older = (start + lax.broadcasted_iota(jnp.int32, (SB_ROWS, LANES), 1)) < starts[u]
        worst = None
        for h in heads:
            k = k_ref[0, pl.ds(start, LANES), cols[h]]
            v = v_ref[0, pl.ds(start, LANES), cols[h]]
            z = lax.dot_general(qh[h][rows[u]], k, _NT, preferred_element_type=F32)
            (r_in, tot), = suffix_sums(jnp.where(older, log_keep(z), 0.0))
            carry = carry_ref[h, rows[u], :]
            a = jnp.where(older, jnp.exp(z + r_in + carry), 0.0)
            acc_ref[h, rows[u], :] += _dot(a.astype(BF16), v)
            carry = carry + tot
            carry_ref[h, rows[u], :] = carry
            m = jnp.max(carry)
            worst = m if worst is None else jnp.maximum(worst, m)
        return worst

    def cond(state):
        kb, worst = state
        return jnp.logical_and(kb >= 0, worst > SB_SKIP)

    for u in halves:
        lax.while_loop(cond, lambda state, u=u: (state[0] - 1, tile(u, state[0])),
                       ((starts[u] + LANES - 1) // LANES - 1, worst[u]))
    for p in range(groups):
        o_ref[0, :, p * LANES:(p + 1) * LANES] = jnp.where(
            lane < HEAD_DIM, acc_ref[2 * p], acc_ref[2 * p + 1]).astype(BF16)


def _stick_breaking(sb, tri2, batch, seq):
    sb3 = sb.reshape(batch, seq, 3 * SB_W)
    heads = SB_W // HEAD_DIM
    once = pl.Buffered(1)
    out = pl.pallas_call(
        _sb_kernel,
        grid=(batch, seq // TQ_SB),
        in_specs=[
            pl.BlockSpec((1, TQ_SB, SB_W), lambda b, i: (b, i, 0)),
            pl.BlockSpec((1, seq, SB_W), lambda b, i: (b, 0, 1), pipeline_mode=once),
            pl.BlockSpec((1, seq, SB_W), lambda b, i: (b, 0, 2), pipeline_mode=once),
            pl.BlockSpec((LANES, 2 * LANES), lambda b, i: (0, 0)),
        ],
        out_specs=pl.BlockSpec((1, TQ_SB, SB_W), lambda b, i: (b, i, 0)),
        out_shape=jax.ShapeDtypeStruct((batch, seq, SB_W), BF16),
        scratch_shapes=[
            pltpu.VMEM((heads, TQ_SB, LANES), F32),
            pltpu.VMEM((heads, TQ_SB, LANES), F32),
        ],
        compiler_params=_params("parallel", "arbitrary"),
        name="stick_breaking",
    )(sb3, sb3, sb3, tri2)
    return out.reshape(batch * seq, SB_W)


def _ret_kernel(q_ref, k_ref, v_ref, g_ref, cos_ref, sin_ref, dmat_ref, dec_ref, xi_ref, gch_ref,
                swap_ref, avg_ref, ng_ref, o_ref, state_ref, obuf_ref):
    si = pl.program_id(1)
    ts = q_ref.shape[1]
    groups = q_ref.shape[2] // LANES
    cols = [slice(p * LANES, (p + 1) * LANES) for p in range(groups)]

    @pl.when(si == 0)
    def _():
        state_ref[...] = jnp.zeros_like(state_ref)

    cos = cos_ref[...]
    sin = sin_ref[...]

    def lane_mix(x, m_ref):
        hi = x.astype(BF16)
        lo = (x - hi.astype(F32)).astype(BF16)
        return _dot(jnp.concatenate([hi, lo], axis=1), m_ref[...])

    def rot(x):
        return x * cos + lane_mix(x, swap_ref) * sin

    qr = [rot(q_ref[0, :, c]) for c in cols]
    kr = [rot(k_ref[0, :, c]) * (HEAD_DIM ** -0.5) for c in cols]
    v = [v_ref[0, :, c] for c in cols]

    lane_c = lax.broadcasted_iota(jnp.int32, (CHUNK, LANES), 1)
    head0 = lane_c < HEAD_DIM
    r_i = lax.broadcasted_iota(jnp.int32, (LANES, LANES), 0) // HEAD_DIM
    c_i = lax.broadcasted_iota(jnp.int32, (LANES, LANES), 1) // HEAD_DIM
    same_head = r_i == c_i

    chunks = [slice(n * CHUNK, (n + 1) * CHUNK) for n in range(ts // CHUNK)]
    items = [(p, n) for p in range(groups) for n in range(len(chunks))]
    kcb = {(p, n): kr[p][chunks[n]].astype(BF16) for p, n in items}
    vcb = {(p, n): v[p][chunks[n]].astype(BF16) for p, n in items}
    kvs = {(p, n): lax.dot_general(kcb[p, n], (v[p][chunks[n]] * dec_ref[p]).astype(BF16), _TN,
                                   preferred_element_type=F32) for p, n in items}
    scores = {}
    for p, n in items:
        qc = qr[p][chunks[n]]
        qstack = jnp.concatenate([jnp.where(head0, qc, 0.0), jnp.where(head0, 0.0, qc)], axis=0)
        scores[p, n] = (lax.dot_general(qstack.astype(BF16), kcb[p, n], _NT, preferred_element_type=F32)
                        * dmat_ref[p]).astype(BF16)
    o2s = {(p, n): _dot(scores[p, n], vcb[p, n]) for p, n in items}
    states = {}
    for p in range(groups):
        state = state_ref[p]
        for n in range(len(chunks)):
            states[p, n] = state.astype(BF16)
            state = gch_ref[p] * state + jnp.where(same_head, kvs[p, n], 0.0)
        state_ref[p] = state
    for p, n in items:
        o_intra = jnp.where(head0, o2s[p, n][:CHUNK], o2s[p, n][CHUNK:])
        o_inter = _dot((qr[p][chunks[n]] * xi_ref[p]).astype(BF16), states[p, n])
        obuf_ref[chunks[n], cols[p]] = o_intra + o_inter

    for c in cols:
        o = obuf_ref[:, c]
        mu = lane_mix(o, avg_ref)
        oc = o - mu
        var = lane_mix(oc * oc, avg_ref)
        normed = oc * lax.rsqrt(var + LN_EPS) * ng_ref[:, c]
        gate = g_ref[0, :, c]
        o_ref[0, :, c] = (gate / (1.0 + jnp.exp(-gate)) * normed).astype(BF16)


def _retention(ret, consts, norm_g, batch, seq):
    cos, sin, dmat, dec, xi, gch, swap, avg = consts
    ret3 =ret.reshape(batch, seq, 4 * RET_W)
    pairs = RET_W // LANES
    fixed3 = lambda b, i: (0, 0, 0)
    out = pl.pallas_call(
        _ret_kernel,
        grid=(batch, seq // TS_RET),
        in_specs=[
            pl.BlockSpec((1, TS_RET, RET_W), lambda b, i: (b, i, 0)),
            pl.BlockSpec((1, TS_RET, RET_W), lambda b, i: (b, i, 1)),
            pl.BlockSpec((1, TS_RET, RET_W), lambda b, i: (b, i, 2)),
            pl.BlockSpec((1, TS_RET, RET_W), lambda b, i: (b, i, 3)),
            pl.BlockSpec((TS_RET, LANES), lambda b, i: (i, 0)),
            pl.BlockSpec((TS_RET, LANES), lambda b, i: (i, 0)),
            pl.BlockSpec((pairs, 2 * CHUNK, CHUNK), fixed3),
            pl.BlockSpec((pairs, CHUNK, LANES), fixed3),
            pl.BlockSpec((pairs, CHUNK, LANES), fixed3),
            pl.BlockSpec((pairs, 1, LANES), fixed3),
            pl.BlockSpec((2 * LANES, LANES), lambda b, i: (0, 0)),
            pl.BlockSpec((2 * LANES, LANES), lambda b, i: (0, 0)),
            pl.BlockSpec((1, RET_W), lambda b, i: (0, 0)),
        ],
        out_specs=pl.BlockSpec((1, TS_RET, RET_W), lambda b, i: (b, i, 0)),
        out_shape=jax.ShapeDtypeStruct((batch, seq, RET_W), BF16),
        scratch_shapes=[
            pltpu.VMEM((pairs, LANES, LANES), F32),
            pltpu.VMEM((TS_RET, RET_W), F32),
        ],
        compiler_params=_params("parallel", "arbitrary"),
        name="retention",
    )(ret3, ret3, ret3, ret3, cos, sin, dmat, dec, xi, gch, swap, avg, norm_g)
    return out.reshape(batch * seq, RET_W)


def _retention_consts(seq):
    half = HEAD_DIM // 2
    pos = jnp.arange(seq, dtype=F32)
    inv = ROPE_BASE ** (-jnp.arange(half, dtype=F32) / half)
    ang = pos[:, None] * inv[None, :]
    cos_h, sin_h = jnp.cos(ang), jnp.sin(ang)
    cos = jnp.tile(cos_h, (1, 2 * LANES // HEAD_DIM))
    sin = jnp.tile(jnp.concatenate([-sin_h, sin_h], axis=1), (1, LANES // HEAD_DIM))
    log_g = jnp.log(1.0 - 2.0 ** (-5.0 - jnp.arange(RET_HEADS, dtype=F32)))
    c = jnp.arange(CHUNK, dtype=F32)
    dmat = jnp.exp(jnp.abs(c[:, None] - c[None, :])[None] * log_g[:, None, None])
    dmat = dmat.reshape(RET_HEADS // 2, 2 * CHUNK, CHUNK)
    lane_log_g = jnp.repeat(log_g, HEAD_DIM).reshape(RET_HEADS // 2, 1, LANES)
    dec = jnp.exp((CHUNK - 1 - c)[None, :, None] * lane_log_g)
    xi = jnp.exp((c + 1.0)[None, :, None] * lane_log_g)
    gch = jnp.exp(CHUNK * lane_log_g)
    l = np.arange(LANES)
    partner = np.where(l % HEAD_DIM < half, l + half, l - half)
    swap = (l[:, None] == partner[None, :]).astype(np.float32)
    avg = (l[:, None] // HEAD_DIM == l[None, :] // HEAD_DIM).astype(np.float32) / HEAD_DIM
    swap = jnp.asarray(np.concatenate([swap, swap], axis=0), BF16)
    avg = jnp.asarray(np.concatenate([avg, avg], axis=0), BF16)
    return cos, sin, dmat, dec, xi, gch, swap, avg


def _layer_norm(z, g, b):
    mu = jnp.mean(z, axis=-1, keepdims=True)
    zc = z - mu
    var = jnp.mean(zc * zc, axis=-1, keepdims=True)
    return zc * lax.rsqrt(var + LN_EPS) * g + b


def _outproj_kernel(x_ref, p_ref, s_ref, r_ref, w_ref, b_ref, g_ref, be_ref, rw_ref, rb_ref,
                    x1_ref, x1b_ref, gates_ref):
    o1 = POOL_W + SB_W
    tm = x_ref.shape[0]
    subs = [slice(s, s + OUTPROJ_SUB) for s in range(0, tm, OUTPROJ_SUB)]
    ys = [_dot(p_ref[sl, :], w_ref[:POOL_W, :]) + _dot(s_ref[sl, :], w_ref[POOL_W:o1, :])
          + _dot(r_ref[sl, :], w_ref[o1:, :]) for sl in subs]
    x1s = [_layer_norm(DN_ALPHA * x_ref[sl, :] + (y + b_ref[...]), g_ref[...], be_ref[...])
           for sl, y in zip(subs, ys)]
    logits = []
    for sl, x1 in zip(subs, x1s):
        x1_ref[sl, :] = x1
        x_hi = x1.astype(BF16)
        x1b_ref[sl, :] = x_hi
        x_mid = (x1 - x_hi.astype(F32)).astype(BF16)
        t = _dot(x_hi, rw_ref[...])
        logits.append(t[:, :LANES] + t[:, LANES:] + _dot(x_mid, rw_ref[:, :LANES]) + rb_ref[...])
    expert = lax.broadcasted_iota(jnp.int32, (N_EXPERTS, OUTPROJ_SUB), 0).astype(F32)
    for sl, lg in zip(subs, logits):
        vals = lg.T[:N_EXPERTS, :]
        top_v, top_sel = [], []
        for _ in range(TOP_K):
            m = jnp.max(vals, axis=0, keepdims=True)
            idx = jnp.min(jnp.where(vals == m, expert, float(N_EXPERTS)), axis=0, keepdims=True)
            sel = expert == idx
            vals = jnp.where(sel, -jnp.inf, vals)
            top_v.append(m)
            top_sel.append(sel)
        ex = [jnp.exp(m - top_v[0]) for m in top_v]
        den = ex[0] + ex[1] + ex[2] + ex[3]
        gates = jnp.full((N_EXPERTS, OUTPROJ_SUB), -1.0, F32)
        for sel, e in zip(top_sel, ex):
            gates = jnp.where(sel, e / den, gates)
        gates_ref[:, sl] = gates


def _outproj(x2, pool_o, sb_o, ret_o, w_bf, b, g, be, rw, rb):
    n = x2.shape[0]
    row = lambda i: (i, 0)
    fixed = lambda i: (0, 0)
    return pl.pallas_call(
        _outproj_kernel,
        grid=(n // TM_ROW,),
        in_specs=[
            pl.BlockSpec((TM_ROW, D_MODEL), row),
            pl.BlockSpec((TM_ROW, POOL_W), row),
            pl.BlockSpec((TM_ROW, SB_W), row),
            pl.BlockSpec((TM_ROW, RET_W), row),
            pl.BlockSpec((D_MODEL, D_MODEL), fixed),
            pl.BlockSpec((1, D_MODEL), fixed),
            pl.BlockSpec((1, D_MODEL), fixed),
            pl.BlockSpec((1, D_MODEL), fixed),
            pl.BlockSpec((D_MODEL, 2 * LANES), fixed),
            pl.BlockSpec((1, LANES), fixed),
        ],
        out_specs=[
            pl.BlockSpec((TM_ROW, D_MODEL), row),
            pl.BlockSpec((TM_ROW, D_MODEL), row),
            pl.BlockSpec((N_EXPERTS, TM_ROW), lambda i: (0, i)),
        ],
        out_shape=[
            jax.ShapeDtypeStruct((n, D_MODEL), F32),
            jax.ShapeDtypeStruct((n, D_MODEL), BF16),
            jax.ShapeDtypeStruct((N_EXPERTS, n), F32),
        ],
        compiler_params=_params("parallel"),
        name="outproj_ln_router",
    )(x2, pool_o, sb_o, ret_o, w_bf, b, g, be, rw, rb)


def _dispatch_kernel(gates_ref, xb_ref, tril_ref, xs_ref, gs_ref, oh_ref, post_ref, gatet_ref, cnt_ref):
    tm = gates_ref.shape[1]
    gatet = gates_ref[...]
    sel = gatet >= 0.0
    self_ = jnp.where(sel, 1.0, 0.0)
    incl = _dot(self_.astype(BF16), tril_ref[...])
    pos = incl - self_
    cnt_ref[0] = jnp.broadcast_to(incl[:, tm - 1:tm], (N_EXPERTS, LANES)).astype(jnp.int32)
    post = jnp.where(sel, pos, -1.0)
    post_ref[0] = post
    gatet_ref[0] = gatet
    post_i = post.astype(jnp.int32)
    slot = lax.broadcasted_iota(jnp.int32, (RC_MOE, tm), 0)
    for e0 in range(0, N_EXPERTS, DISPATCH_GROUP):
        pieces = []
        for e in range(e0, e0 + DISPATCH_GROUP):
            onehot = slot == post_i[e:e + 1, :]
            pieces.append(jnp.where(onehot, 1.0, 0.0).astype(BF16))
            gate = jnp.sum(jnp.where(onehot, gatet[e:e + 1, :], 0.0), axis=1, keepdims=True)
            gs_ref[0, e] = jnp.broadcast_to(gate, (RC_MOE, LANES))
        onehots = jnp.concatenate(pieces, axis=0)
        oh_ref[0, e0 * RC_MOE:(e0 + DISPATCH_GROUP) * RC_MOE, :] = onehots.astype(ONEHOT_DT)
        xs = _dot(onehots, xb_ref[...]).astype(BF16)
        xs_ref[0, e0:e0 + DISPATCH_GROUP] = xs.reshape(DISPATCH_GROUP, RC_MOE, D_MODEL)


def _dispatch(gates, x1b, tril):
    n = gates.shape[1]
    nt = n // TM_MOE
    return pl.pallas_call(
        _dispatch_kernel,
        grid=(nt,),
        in_specs=[
            pl.BlockSpec((N_EXPERTS, TM_MOE), lambda i: (0, i)),
            pl.BlockSpec((TM_MOE, D_MODEL), lambda i: (i, 0)),
            pl.BlockSpec((TM_MOE, TM_MOE), lambda i: (0, 0)),
        ],
        out_specs=[
            pl.BlockSpec((1, N_EXPERTS, RC_MOE, D_MODEL), lambda i: (i, 0, 0, 0)),
            pl.BlockSpec((1, N_EXPERTS, RC_MOE, LANES), lambda i: (i, 0, 0, 0)),
            pl.BlockSpec((1, SLOTS, TM_MOE), lambda i: (i, 0, 0)),
            pl.BlockSpec((1, N_EXPERTS, TM_MOE), lambda i: (i, 0, 0)),
            pl.BlockSpec((1, N_EXPERTS, TM_MOE), lambda i: (i, 0, 0)),
            pl.BlockSpec((1, N_EXPERTS, LANES), lambda i: (i, 0, 0)),
        ],
        out_shape=[
            jax.ShapeDtypeStruct((nt, N_EXPERTS, RC_MOE, D_MODEL), BF16),
            jax.ShapeDtypeStruct((nt, N_EXPERTS, RC_MOE, LANES), F32),
            jax.ShapeDtypeStruct((nt, SLOTS, TM_MOE), ONEHOT_DT),
            jax.ShapeDtypeStruct((nt, N_EXPERTS, TM_MOE), F32),
            jax.ShapeDtypeStruct((nt, N_EXPERTS, TM_MOE), F32),
            jax.ShapeDtypeStruct((nt, N_EXPERTS, LANES), jnp.int32),
        ],
        compiler_params=_params("parallel"),
        name="dispatch",
    )(gates, x1b, tril)


def _swiglu_ffn(x, wgu, bgu, wd, bd):
    h = _dot(x, wgu) + bgu
    g = jnp.minimum(h[:, :D_FF], SWIGLU_LIMIT)
    up = jnp.clip(h[:, D_FF:], -SWIGLU_LIMIT, SWIGLU_LIMIT)
    act = (up + 1.0) * (g / (1.0 + jnp.exp(-SWIGLU_ALPHA * g)))
    return _dot(act.astype(BF16), wd) + bd


def _ffn_kernel(tail_ref, xs_ref, gs_ref, wgu_ref, bgu_ref, wd_ref, bd_ref, ys_ref, wgu_bf_ref, wd_bf_ref):
    e = pl.program_id(0)
    j = pl.program_id(1)

    @pl.when(j == 0)
    def _():
        wgu_bf_ref[0] = wgu_ref[0, 0].astype(BF16)
        wd_bf_ref[0] = wd_ref[0, 0].astype(BF16)

    def run(lo, hi):
        rows = FFN_TILES * (hi - lo)
        x = xs_ref[:, 0, lo:hi, :].reshape(rows, D_MODEL)
        y = _swiglu_ffn(x, wgu_bf_ref[0], bgu_ref[0], wd_bf_ref[0], bd_ref[0])
        gate = gs_ref[:, 0, lo:hi, :].reshape(rows, LANES)[:, :1]
        ys_ref[:, 0, lo:hi, :] = (y * gate).astype(BF16).reshape(FFN_TILES, hi - lo, D_MODEL)

    run(0, RC_MAIN)
    used = tail_ref[e * pl.num_programs(1) + j] > 0

    @pl.when(used)
    def _():
        run(RC_MAIN, RC_MOE)

    @pl.when(jnp.logical_not(used))
    def _():
        ys_ref[:, 0, RC_MAIN:RC_MOE, :] = jnp.zeros((FFN_TILES, RC_MOE - RC_MAIN, D_MODEL), BF16)


def _ffn(tail, xs, gs, w_gate_up, bgu, w_down, bd, layer):
    nt = xs.shape[0]
    grid_spec = pltpu.PrefetchScalarGridSpec(
        num_scalar_prefetch=1,
        grid=(N_EXPERTS, nt // FFN_TILES),
        in_specs=[
            pl.BlockSpec((FFN_TILES, 1, RC_MOE, D_MODEL), lambda e, j, t: (j, e, 0, 0)),
            pl.BlockSpec((FFN_TILES, 1, RC_MOE, LANES), lambda e, j, t: (j, e, 0, 0)),
            pl.BlockSpec((1, 1, D_MODEL, 2 * D_FF), lambda e, j, t: (layer, e, 0, 0)),
            pl.BlockSpec((1, 1, 2 * D_FF), lambda e, j, t: (e, 0, 0)),
            pl.BlockSpec((1, 1, D_FF, D_MODEL), lambda e, j, t: (layer, e, 0, 0)),
            pl.BlockSpec((1, 1, D_MODEL), lambda e, j, t: (e, 0, 0)),
        ],
        out_specs=[
            pl.BlockSpec((FFN_TILES, 1, RC_MOE, D_MODEL), lambda e, j, t: (j, e, 0, 0)),
            pl.BlockSpec((1, D_MODEL, 2 * D_FF), lambda e, j, t: (e, 0, 0)),
            pl.BlockSpec((1, D_FF, D_MODEL), lambda e, j, t: (e, 0, 0)),
        ],
    )
    return pl.pallas_call(
        _ffn_kernel,
        grid_spec=grid_spec,
        out_shape=[
            jax.ShapeDtypeStruct(xs.shape, BF16),
            jax.ShapeDtypeStruct((N_EXPERTS, D_MODEL, 2 * D_FF), BF16),
            jax.ShapeDtypeStruct((N_EXPERTS, D_FF, D_MODEL), BF16),
        ],
        compiler_params=_params("parallel", "arbitrary"),
        name="expert_ffn",
    )(tail, xs, gs, w_gate_up, bgu, w_down, bd)


def _combine_kernel(cnt_ref, ys_ref, oh_ref, x1_ref, g_ref, b_ref, post_ref,
                    gatet_ref, bgu_ref, bd_ref, wgu_hbm, wd_hbm, o_ref, acc_ref, wgu_buf, wd_buf, sem):
    i = pl.program_id(0)
    tm = x1_ref.shape[0]
    width = DISPATCH_GROUP * RC_MOE
    acc = None
    for e0 in range(0, N_EXPERTS, DISPATCH_GROUP):
        sl = slice(e0 * RC_MOE, e0 * RC_MOE + width)
        part = lax.dot_general(oh_ref[0, sl, :].astype(BF16),
                               ys_ref[0, e0:e0 + DISPATCH_GROUP].reshape(width, D_MODEL),
                               _TN, preferred_element_type=F32)
        acc = part if acc is None else acc + part
    acc_ref[...] = acc

    slot = lax.broadcasted_iota(jnp.int32, (RC_MOE, tm), 0)

    def per_expert(e, carry):
        cnt = cnt_ref[i * N_EXPERTS + e]

        @pl.when(cnt > RC_MOE)
        def _():
            copies = (pltpu.make_async_copy(wgu_hbm.at[e], wgu_buf, sem.at[0]),
                      pltpu.make_async_copy(wd_hbm.at[e], wd_buf, sem.at[1]))
            for cp in copies:
                cp.start()
            for cp in copies:
                cp.wait()
            prow = post_ref[0, pl.ds(e, 1), :].astype(jnp.int32)
            grow = gatet_ref[0, pl.ds(e, 1), :]

            def chunk(c, carry2):
                onehot = (slot + c * RC_MOE) == prow
                oh = jnp.where(onehot, 1.0, 0.0).astype(BF16)
                xg = _dot(oh, x1_ref[...].astype(BF16)).astype(BF16)
                y = _swiglu_ffn(xg, wgu_buf[...], bgu_ref[e], wd_buf[...], bd_ref[e])
                gate = jnp.sum(jnp.where(onehot, grow, 0.0), axis=1, keepdims=True)
                yg = (y * gate).astype(BF16)
                acc_ref[...] += lax.dot_general(oh, yg, _TN, preferred_element_type=F32)
                return carry2

            lax.fori_loop(1, (cnt + RC_MOE - 1) // RC_MOE, chunk, 0)

        return carry

    lax.fori_loop(0, N_EXPERTS, per_expert, 0)
    o_ref[...] = _layer_norm(DN_ALPHA * x1_ref[...] + acc_ref[...], g_ref[...], b_ref[...])


def _combine(cnt, ys, oh, x1, g, b, post, gatet, bgu, bd, wgu, wd):
    n = x1.shape[0]
    nt = n // TM_MOE
    row = lambda i, c: (i, 0)
    fixed2 = lambda i, c: (0, 0)
    fixed3 = lambda i, c: (0, 0, 0)
    grid_spec = pltpu.PrefetchScalarGridSpec(
        num_scalar_prefetch=1,
        grid=(nt,),
        in_specs=[
            pl.BlockSpec((1, N_EXPERTS, RC_MOE, D_MODEL), lambda i, c: (i, 0, 0, 0)),
            pl.BlockSpec((1, SLOTS, TM_MOE), lambda i, c: (i, 0, 0)),
            pl.BlockSpec((TM_MOE, D_MODEL), row),
            pl.BlockSpec((1, D_MODEL), fixed2),
            pl.BlockSpec((1, D_MODEL), fixed2),
            pl.BlockSpec((1, N_EXPERTS, TM_MOE), lambda i, c: (i, 0, 0)),
            pl.BlockSpec((1, N_EXPERTS, TM_MOE), lambda i, c: (i, 0, 0)),
            pl.BlockSpec((N_EXPERTS, 1, 2 * D_FF), fixed3),
            pl.BlockSpec((N_EXPERTS, 1, D_MODEL), fixed3),
            pl.BlockSpec(memory_space=pl.ANY),
            pl.BlockSpec(memory_space=pl.ANY),
        ],
        out_specs=pl.BlockSpec((TM_MOE, D_MODEL), row),
        scratch_shapes=[
            pltpu.VMEM((TM_MOE, D_MODEL), F32),
            pltpu.VMEM((D_MODEL, 2 * D_FF), BF16),
            pltpu.VMEM((D_FF, D_MODEL), BF16),
            pltpu.SemaphoreType.DMA((2,)),
        ],
    )
    return pl.pallas_call(
        _combine_kernel,
        grid_spec=grid_spec,
        out_shape=jax.ShapeDtypeStruct((n, D_MODEL), F32),
        compiler_params=_params("arbitrary"),
        name="combine_ln",
    )(cnt, ys, oh, x1, g, b, post, gatet, bgu, bd, wgu, wd)


def _block_diag(pool_w):
    groups = pool_w.shape[0]
    out = jnp.zeros((POOL_W, POOL_W), pool_w.dtype)
    for gi in range(groups):
        out = out.at[gi * POOL_CH:(gi + 1) * POOL_CH, gi * POOL_CH:(gi + 1) * POOL_CH].set(pool_w[gi])
    return out


def _layer(x2, batch, seq, consts, layer, w_in, b_in, pool_w, pool_scale, ret_norm_g, w_out, b_out, ln1_g,
           ln1_b, router_w, router_b, w_gate_up, b_gate_up, w_down, b_down, ln2_g, ln2_b):
    ret_consts, tri2, tril = consts
    row = lambda a: a.reshape(1, -1).astype(F32)
    pool_o, sb, ret = _inproj(x2, w_in.astype(BF16), row(b_in), _block_diag(pool_w).astype(BF16),
                              row(pool_scale), seq)
    sb_o = _stick_breaking(sb, tri2, batch, seq)
    ret_o = _retention(ret, ret_consts, row(ret_norm_g), batch, seq)
    rw = jnp.pad(router_w.astype(F32), ((0, 0), (0, LANES - N_EXPERTS)))
    rw_hi = rw.astype(BF16)
    rw = jnp.concatenate([rw_hi, (rw - rw_hi.astype(F32)).astype(BF16)], axis=1)
    rb = jnp.pad(router_b.astype(F32), (0, LANES - N_EXPERTS)).reshape(1, LANES)
    x1, x1b, gates = _outproj(x2, pool_o, sb_o, ret_o, w_out.astype(BF16), row(b_out), row(ln1_g),
                              row(ln1_b), rw, rb)
    xs, gs, oh, post, gatet, cnt = _dispatch(gates, x1b, tril)
    bgu = b_gate_up.reshape(N_EXPERTS, 1, 2 * D_FF).astype(F32)
    bd = b_down.reshape(N_EXPERTS, 1, D_MODEL).astype(F32)
    cnt2 = cnt[:, :, 0]
    tail = jnp.any((cnt2 > RC_MAIN).reshape(-1, FFN_TILES, N_EXPERTS), axis=1)
    ys, wgu, wd = _ffn(tail.T.reshape(-1).astype(jnp.int32), xs, gs, w_gate_up, bgu, w_down, bd, layer)
    cnt_flat = cnt2.reshape(-1)
    return _combine(cnt_flat, ys, oh, x1, row(ln2_g), row(ln2_b), post, gatet, bgu, bd, wgu, wd)


def kernel(x, w_in, b_in, pool_w, pool_scale, ret_norm_g, w_out, b_out, ln1_g, ln1_b, router_w, router_b,
           w_gate_up, b_gate_up, w_down, b_down, ln2_g, ln2_b):
    batch, seq, d = x.shape
    n = batch * seq
    assert d == D_MODEL and seq % TS_RET == 0 and seq % TQ_SB == 0 and seq % TM_ROW == 0
    assert n % (TM_MOE * FFN_TILES) == 0 and n % TM_ROW == 0 and seq >= SB_SPAN
    j = np.arange(LANES)
    tri = (j[:, None] >= j[None, :]).astype(np.float32)
    tri2 = jnp.asarray(np.concatenate([tri, np.ones_like(tri)], axis=1), BF16)
    r = np.arange(TM_MOE)
    tril = jnp.asarray((r[:, None] <= r[None, :]).astype(np.float32), BF16)
    consts = (_retention_consts(seq), tri2, tril)
    x2 = x.reshape(n, d)
    for l in range(DEPTH):
        x2 = _layer(x2, batch, seq, consts, l, w_in[l], b_in[l], pool_w[l], pool_scale[l], ret_norm_g[l],
                    w_out[l], b_out[l], ln1_g[l], ln1_b[l], router_w[l], router_b[l], w_gate_up,
                    b_gate_up[l], w_down, b_down[l], ln2_g[l], ln2_b[l])
    return x2.reshape(batch, seq, d)
```

```python
import functools

import numpy as np
import jax
import jax.numpy as jnp
from jax import lax
from jax.experimental import pallas as pl
from jax.experimental.pallas import tpu as pltpu

D_MODEL = 1024
DEPTH = 2
CHUNK = 64
HEAD_DIM = 64
POOL_CH = 64
POOL_W = 256
POOL_HALO = 16
SB_W = 384
RET_W = 384
RET_HEADS = 6
IN_W = POOL_W + 3 * SB_W + 4 * RET_W
ROPE_BASE = 10000.0
N_EXPERTS = 32
TOP_K = 4
D_FF = D_MODEL
SWIGLU_LIMIT = 7.0
SWIGLU_ALPHA = 1.702
DN_ALPHA = (2.0 * DEPTH) ** 0.25
LN_EPS = 1e-5

LANES = 128
VMEM_LIMIT = 56 * 1024 * 1024

TM_ROW = 1024
OUTPROJ_SUB = 256
TQ_SB = 512
SB_ROWS = 64
SB_SPAN = 256
TS_RET = 1024
TM_MOE = 512
RC_MOE = 96
RC_MAIN = 80
FFN_TILES = 8
SLOTS = N_EXPERTS * RC_MOE
DISPATCH_GROUP = 8
SB_SKIP = -100.0

BF16 = jnp.bfloat16
F32 = jnp.float32
ONEHOT_DT = jnp.float8_e4m3fn

_NT = (((1,), (1,)), ((), ()))
_TN = (((0,), (0,)), ((), ()))


def _dot(a, b):
    return jnp.dot(a, b, preferred_element_type=F32)


def _params(*sem):
    return pltpu.CompilerParams(dimension_semantics=sem, vmem_limit_bytes=VMEM_LIMIT)


def _pool_mix(cur, halo, t0, w, scale):
    ts = cur.shape[0]
    ext = jnp.concatenate([halo, cur], axis=0)
    a2 = ext[1:] + ext[:-1]
    a4 = a2[2:] + a2[:-2]
    a8 = a4[4:] + a4[:-4]
    a16 = a8[8:] + a8[:-8]
    lane = lax.broadcasted_iota(jnp.int32, (ts, POOL_W), 1)
    grp = lane // POOL_CH
    win = jnp.where(grp == 0, a2[15:15 + ts],
                    jnp.where(grp == 1, a4[13:13 + ts],
                              jnp.where(grp == 2, a8[9:9 + ts], a16[1:1 + ts])))
    width = jnp.where(grp == 0, 2, jnp.where(grp == 1, 4, jnp.where(grp == 2, 8, 16)))
    t = t0 + lax.broadcasted_iota(jnp.int32, (ts, POOL_W), 0)
    cnt = jnp.minimum(t + 1, width).astype(F32)
    pooled = win / cnt - cur
    return _dot(pooled.astype(BF16), w) * scale


def _inproj_kernel(tiles_per_seq, x_ref, w_ref, b_ref, pw_ref, ps_ref, pool_ref, sb_ref, ret_ref, halo_ref):
    it = pl.program_id(0) % tiles_per_seq
    xb = x_ref[...].astype(BF16)
    o1 = POOL_W + 3 * SB_W
    @pl.when(it == 0)
    def _():
        halo_ref[...] = jnp.zeros_like(halo_ref)

    u = _dot(xb, w_ref[:, :POOL_W]) + b_ref[:, :POOL_W]
    sb_ref[...] = (_dot(xb, w_ref[:, POOL_W:o1]) + b_ref[:, POOL_W:o1]).astype(BF16)
    ret_ref[...] = _dot(xb, w_ref[:, o1:]) + b_ref[:, o1:]
    pool_ref[...] = _pool_mix(u, halo_ref[...], it * TM_ROW, pw_ref[...], ps_ref[...]).astype(BF16)
    halo_ref[...] = u[TM_ROW - POOL_HALO:, :]


def _inproj(x2, w_bf, b, pool_w_bd, pool_scale, seq):
    n = x2.shape[0]
    fixed = lambda i: (0, 0)
    return pl.pallas_call(
        functools.partial(_inproj_kernel, seq // TM_ROW),
        grid=(n // TM_ROW,),
        in_specs=[
            pl.BlockSpec((TM_ROW, D_MODEL), lambda i: (i, 0)),
            pl.BlockSpec((D_MODEL, IN_W), fixed),
            pl.BlockSpec((1, IN_W), fixed),
            pl.BlockSpec((POOL_W, POOL_W), fixed),
            pl.BlockSpec((1, POOL_W), fixed),
        ],
        out_specs=[
            pl.BlockSpec((TM_ROW, POOL_W), lambda i: (i, 0)),
            pl.BlockSpec((TM_ROW, 3 * SB_W), lambda i: (i, 0)),
            pl.BlockSpec((TM_ROW, 4 * RET_W), lambda i: (i, 0)),
        ],
        out_shape=[
            jax.ShapeDtypeStruct((n, POOL_W), BF16),
            jax.ShapeDtypeStruct((n, 3 * SB_W), BF16),
            jax.ShapeDtypeStruct((n, 4 * RET_W), F32),
        ],
        scratch_shapes=[pltpu.VMEM((POOL_HALO, POOL_W), F32)],
        compiler_params=_params("arbitrary"),
        name="inproj_pool",
    )(x2, w_bf, b, pool_w_bd, pool_scale)


def _sb_kernel(q_ref, k_ref, v_ref, tri_ref, o_ref, acc_ref, carry_ref):
    qi = pl.program_id(1)
    tq = q_ref.shape[1]
    groups = q_ref.shape[2] // LANES
    lane = lax.broadcasted_iota(jnp.int32, (tq, LANES), 1)
    qh = []
    for p in range(groups):
        q = q_ref[0, :, p * LANES:(p + 1) * LANES] * jnp.asarray(HEAD_DIM ** -0.5, BF16)
        zero = jnp.zeros_like(q)
        qh += [jnp.where(lane < HEAD_DIM, q, zero), jnp.where(lane >= HEAD_DIM, q, zero)]

    def log_keep(z):
        return jnp.minimum(-z, 0.0) - jnp.log(1.0 + jnp.exp(-jnp.abs(z)))

    def suffix_sums(lk):
        lk = lk.astype(BF16)
        out = []
        for b in range(lk.shape[1] // LANES):
            rs = _dot(lk[:, b * LANES:(b + 1) * LANES], tri_ref[...])
            out.append((rs[:, :LANES], rs[:, LANES:]))
        return out

    t0 = qi * tq
    halves = range(tq // SB_ROWS)
    heads = range(2 * groups)
    cols = [slice((h // 2) * LANES, (h // 2 + 1) * LANES) for h in heads]
    rows = [slice(u * SB_ROWS, (u + 1) * SB_ROWS) for u in halves]
    starts = [pl.multiple_of(jnp.maximum(t0 + (u + 1) * SB_ROWS - SB_SPAN, 0), SB_ROWS) for u in halves]
    below = []
    for u in halves:
        qpos = t0 + u * SB_ROWS + lax.broadcasted_iota(jnp.int32, (SB_ROWS, SB_SPAN), 0)
        kpos = starts[u] + lax.broadcasted_iota(jnp.int32, (SB_ROWS, SB_SPAN), 1)
        below.append(kpos < qpos)
    items = [(u, h) for u in halves for h in heads]
    zs = {(u, h): lax.dot_general(qh[h][rows[u]], k_ref[0, pl.ds(starts[u], SB_SPAN), cols[h]], _NT,
                                  preferred_element_type=F32) for u, h in items}
    sums = {(u, h): suffix_sums(jnp.where(below[u], log_keep(zs[u, h]), 0.0)) for u, h in items}
    weights, runs = {}, {}
    for u, h in items:
        run = None
        r = [None] * len(sums[u, h])
        for b in reversed(range(len(r))):
            r_in, tot = sums[u, h][b]
            r[b] = r_in if run is None else r_in + run
            run = tot if run is None else run + tot
        weights[u, h] = jnp.where(below[u], jnp.exp(zs[u, h] + jnp.concatenate(r, axis=1)), 0.0).astype(BF16)
        runs[u, h] = run
    for u, h in items:
        acc_ref[h, rows[u], :] = _dot(weights[u, h], v_ref[0, pl.ds(starts[u], SB_SPAN), cols[h]])
        carry_ref[h, rows[u], :] = runs[u, h]
    worst = []
    for u in halves:
        top = runs[u, 0]
        for h in heads[1:]:
            top = jnp.maximum(top, runs[u, h])
        worst.append(jnp.max(top))

    def tile(u, kb):
        start = pl.multiple_of(kb * LANES, LANES)
        older = (start + lax.broadcasted_iota(jnp.int32, (SB_ROWS, LANES), 1)) < starts[u]
        worst = None
        for h in heads:
            k = k_ref[0, pl.ds(start, LANES), cols[h]]
            v = v_ref[0, pl.ds(start, LANES), cols[h]]
            z = lax.dot_general(qh[h][rows[u]], k, _NT, preferred_element_type=F32)
            (r_in, tot), = suffix_sums(jnp.where(older, log_keep(z), 0.0))
            carry = carry_ref[h, rows[u], :]
            a = jnp.where(older, jnp.exp(z + r_in + carry), 0.0)
            acc_ref[h, rows[u], :] += _dot(a.astype(BF16), v)
            carry = carry + tot
            carry_ref[h, rows[u], :] = carry
            m = jnp.max(carry)
            worst = m if worst is None else jnp.maximum(worst, m)
        return worst

    def cond(state):
        kb, worst = state
        return jnp.logical_and(kb >= 0, worst > SB_SKIP)

    for u in halves:
        lax.while_loop(cond, lambda state, u=u: (state[0] - 1, tile(u, state[0])),
                       ((starts[u] + LANES - 1) // LANES - 1, worst[u]))
    for p in range(groups):
        o_ref[0, :, p * LANES:(p + 1) * LANES] = jnp.where(
            lane < HEAD_DIM, acc_ref[2 * p], acc_ref[2 * p + 1]).astype(BF16)


def _stick_breaking(sb, tri2, batch, seq):
    sb3 = sb.reshape(batch, seq, 3 * SB_W)
    heads = SB_W // HEAD_DIM
    once = pl.Buffered(1)
    out = pl.pallas_call(
        _sb_kernel,
        grid=(batch, seq // TQ_SB),
        in_specs=[
            pl.BlockSpec((1, TQ_SB, SB_W), lambda b, i: (b, i, 0)),
            pl.BlockSpec((1, seq, SB_W), lambda b, i: (b, 0, 1), pipeline_mode=once),
            pl.BlockSpec((1, seq, SB_W), lambda b, i: (b, 0, 2), pipeline_mode=once),
            pl.BlockSpec((LANES, 2 * LANES), lambda b, i: (0, 0)),
        ],
        out_specs=pl.BlockSpec((1, TQ_SB, SB_W), lambda b, i: (b, i, 0)),
        out_shape=jax.ShapeDtypeStruct((batch, seq, SB_W), BF16),
        scratch_shapes=[
            pltpu.VMEM((heads, TQ_SB, LANES), F32),
            pltpu.VMEM((heads, TQ_SB, LANES), F32),
        ],
        compiler_params=_params("parallel", "arbitrary"),
        name="stick_breaking",
    )(sb3, sb3, sb3, tri2)
    return out.reshape(batch * seq, SB_W)


def _ret_kernel(q_ref, k_ref, v_ref, g_ref, cos_ref, sin_ref, dmat_ref, dec_ref, xi_ref, gch_ref,
                swap_ref, avg_ref, ng_ref, o_ref, state_ref, obuf_ref):
    si = pl.program_id(1)
    ts = q_ref.shape[1]
    groups = q_ref.shape[2] // LANES
    cols = [slice(p * LANES, (p + 1) * LANES) for p in range(groups)]

    @pl.when(si == 0)
    def _():
        state_ref[...] = jnp.zeros_like(state_ref)

    cos = cos_ref[...]
    sin = sin_ref[...]

    def lane_mix(x, m_ref):
        hi = x.astype(BF16)
        lo = (x - hi.astype(F32)).astype(BF16)
        return _dot(jnp.concatenate([hi, lo], axis=1), m_ref[...])

    def rot(x):
        return x * cos + lane_mix(x, swap_ref) * sin

    qr = [rot(q_ref[0, :, c]) for c in cols]
    kr = [rot(k_ref[0, :, c]) * (HEAD_DIM ** -0.5) for c in cols]
    v = [v_ref[0, :, c] for c in cols]

    lane_c = lax.broadcasted_iota(jnp.int32, (CHUNK, LANES), 1)
    head0 = lane_c < HEAD_DIM
    r_i = lax.broadcasted_iota(jnp.int32, (LANES, LANES), 0) // HEAD_DIM
    c_i = lax.broadcasted_iota(jnp.int32, (LANES, LANES), 1) // HEAD_DIM
    same_head = r_i == c_i

    chunks = [slice(n * CHUNK, (n + 1) * CHUNK) for n in range(ts // CHUNK)]
    items = [(p, n) for p in range(groups) for n in range(len(chunks))]
    kcb = {(p, n): kr[p][chunks[n]].astype(BF16) for p, n in items}
    vcb = {(p, n): v[p][chunks[n]].astype(BF16) for p, n in items}
    kvs = {(p, n): lax.dot_general(kcb[p, n], (v[p][chunks[n]] * dec_ref[p]).astype(BF16), _TN,
                                   preferred_element_type=F32) for p, n in items}
    scores = {}
    for p, n in items:
        qc = qr[p][chunks[n]]
        qstack = jnp.concatenate([jnp.where(head0, qc, 0.0), jnp.where(head0, 0.0, qc)], axis=0)
        scores[p, n] = (lax.dot_general(qstack.astype(BF16), kcb[p, n], _NT, preferred_element_type=F32)
                        * dmat_ref[p]).astype(BF16)
    o2s = {(p, n): _dot(scores[p, n], vcb[p, n]) for p, n in items}
    states = {}
    for p in range(groups):
        state = state_ref[p]
        for n in range(len(chunks)):
            states[p, n] = state.astype(BF16)
            state = gch_ref[p] * state + jnp.where(same_head, kvs[p, n], 0.0)
        state_ref[p] = state
    for p, n in items:
        o_intra = jnp.where(head0, o2s[p, n][:CHUNK], o2s[p, n][CHUNK:])
        o_inter = _dot((qr[p][chunks[n]] * xi_ref[p]).astype(BF16), states[p, n])
        obuf_ref[chunks[n], cols[p]] = o_intra + o_inter

    for c in cols:
        o = obuf_ref[:, c]
        mu = lane_mix(o, avg_ref)
        oc = o - mu
        var = lane_mix(oc * oc, avg_ref)
        normed = oc * lax.rsqrt(var + LN_EPS) * ng_ref[:, c]
        gate = g_ref[0, :, c]
        o_ref[0, :, c] = (gate / (1.0 + jnp.exp(-gate)) * normed).astype(BF16)


def _retention(ret, consts, norm_g, batch, seq):
    cos, sin, dmat, dec, xi, gch, swap, avg = consts
    ret3 =ret.reshape(batch, seq, 4 * RET_W)
    pairs = RET_W // LANES
    fixed3 = lambda b, i: (0, 0, 0)
    out = pl.pallas_call(
        _ret_kernel,
        grid=(batch, seq // TS_RET),
        in_specs=[
            pl.BlockSpec((1, TS_RET, RET_W), lambda b, i: (b, i, 0)),
            pl.BlockSpec((1, TS_RET, RET_W), lambda b, i: (b, i, 1)),
            pl.BlockSpec((1, TS_RET, RET_W), lambda b, i: (b, i, 2)),
            pl.BlockSpec((1, TS_RET, RET_W), lambda b, i: (b, i, 3)),
            pl.BlockSpec((TS_RET, LANES), lambda b, i: (i, 0)),
            pl.BlockSpec((TS_RET, LANES), lambda b, i: (i, 0)),
            pl.BlockSpec((pairs, 2 * CHUNK, CHUNK), fixed3),
            pl.BlockSpec((pairs, CHUNK, LANES), fixed3),
            pl.BlockSpec((pairs, CHUNK, LANES), fixed3),
            pl.BlockSpec((pairs, 1, LANES), fixed3),
            pl.BlockSpec((2 * LANES, LANES), lambda b, i: (0, 0)),
            pl.BlockSpec((2 * LANES, LANES), lambda b, i: (0, 0)),
            pl.BlockSpec((1, RET_W), lambda b, i: (0, 0)),
        ],
        out_specs=pl.BlockSpec((1, TS_RET, RET_W), lambda b, i: (b, i, 0)),
        out_shape=jax.ShapeDtypeStruct((batch, seq, RET_W), BF16),
        scratch_shapes=[
            pltpu.VMEM((pairs, LANES, LANES), F32),
            pltpu.VMEM((TS_RET, RET_W), F32),
        ],
        compiler_params=_params("parallel", "arbitrary"),
        name="retention",
    )(ret3, ret3, ret3, ret3, cos, sin, dmat, dec, xi, gch, swap, avg, norm_g)
    return out.reshape(batch * seq, RET_W)


def _retention_consts(seq):
    half = HEAD_DIM // 2
    pos = jnp.arange(seq, dtype=F32)
    inv = ROPE_BASE ** (-jnp.arange(half, dtype=F32) / half)
    ang = pos[:, None] * inv[None, :]
    cos_h, sin_h = jnp.cos(ang), jnp.sin(ang)
    cos = jnp.tile(cos_h, (1, 2 * LANES // HEAD_DIM))
    sin = jnp.tile(jnp.concatenate([-sin_h, sin_h], axis=1), (1, LANES // HEAD_DIM))
    log_g = jnp.log(1.0 - 2.0 ** (-5.0 - jnp.arange(RET_HEADS, dtype=F32)))
    c = jnp.arange(CHUNK, dtype=F32)
    dmat = jnp.exp(jnp.abs(c[:, None] - c[None, :])[None] * log_g[:, None, None])
    dmat = dmat.reshape(RET_HEADS // 2, 2 * CHUNK, CHUNK)
    lane_log_g = jnp.repeat(log_g, HEAD_DIM).reshape(RET_HEADS // 2, 1, LANES)
    dec = jnp.exp((CHUNK - 1 - c)[None, :, None] * lane_log_g)
    xi = jnp.exp((c + 1.0)[None, :, None] * lane_log_g)
    gch = jnp.exp(CHUNK * lane_log_g)
    l = np.arange(LANES)
    partner = np.where(l % HEAD_DIM < half, l + half, l - half)
    swap = (l[:, None] == partner[None, :]).astype(np.float32)
    avg = (l[:, None] // HEAD_DIM == l[None, :] // HEAD_DIM).astype(np.float32) / HEAD_DIM
    swap = jnp.asarray(np.concatenate([swap, swap], axis=0), BF16)
    avg = jnp.asarray(np.concatenate([avg, avg], axis=0), BF16)
    return cos, sin, dmat, dec, xi, gch, swap, avg


def _layer_norm(z, g, b):
    mu = jnp.mean(z, axis=-1, keepdims=True)
    zc = z - mu
    var = jnp.mean(zc * zc, axis=-1, keepdims=True)
    return zc * lax.rsqrt(var + LN_EPS) * g + b


def _outproj_kernel(x_ref, p_ref, s_ref, r_ref, w_ref, b_ref, g_ref, be_ref, rw_ref, rb_ref,
                    x1_ref, x1b_ref, gates_ref):
    o1 = POOL_W + SB_W
    tm = x_ref.shape[0]
    subs = [slice(s, s + OUTPROJ_SUB) for s in range(0, tm, OUTPROJ_SUB)]
    ys = [_dot(p_ref[sl, :], w_ref[:POOL_W, :]) + _dot(s_ref[sl, :], w_ref[POOL_W:o1, :])
          + _dot(r_ref[sl, :], w_ref[o1:, :]) for sl in subs]
    x1s = [_layer_norm(DN_ALPHA * x_ref[sl, :] + (y + b_ref[...]), g_ref[...], be_ref[...])
           for sl, y in zip(subs, ys)]
    logits = []
    for sl, x1 in zip(subs, x1s):
        x1_ref[sl, :] = x1
        x_hi = x1.astype(BF16)
        x1b_ref[sl, :] = x_hi
        x_mid = (x1 - x_hi.astype(F32)).astype(BF16)
        t = _dot(x_hi, rw_ref[...])
        logits.append(t[:, :LANES] + t[:, LANES:] + _dot(x_mid, rw_ref[:, :LANES]) + rb_ref[...])
    expert = lax.broadcasted_iota(jnp.int32, (N_EXPERTS, OUTPROJ_SUB), 0).astype(F32)
    for sl, lg in zip(subs, logits):
        vals = lg.T[:N_EXPERTS, :]
        top_v, top_sel = [], []
        for _ in range(TOP_K):
            m = jnp.max(vals, axis=0, keepdims=True)
            idx = jnp.min(jnp.where(vals == m, expert, float(N_EXPERTS)), axis=0, keepdims=True)
            sel = expert == idx
            vals = jnp.where(sel, -jnp.inf, vals)
            top_v.append(m)
            top_sel.append(sel)
        ex = [jnp.exp(m - top_v[0]) for m in top_v]
        den = ex[0] + ex[1] + ex[2] + ex[3]
        gates = jnp.full((N_EXPERTS, OUTPROJ_SUB), -1.0, F32)
        for sel, e in zip(top_sel, ex):
            gates = jnp.where(sel, e / den, gates)
        gates_ref[:, sl] = gates


def _outproj(x2, pool_o, sb_o, ret_o, w_bf, b, g, be, rw, rb):
    n = x2.shape[0]
    row = lambda i: (i, 0)
    fixed = lambda i: (0, 0)
    return pl.pallas_call(
        _outproj_kernel,
        grid=(n // TM_ROW,),
        in_specs=[
            pl.BlockSpec((TM_ROW, D_MODEL), row),
            pl.BlockSpec((TM_ROW, POOL_W), row),
            pl.BlockSpec((TM_ROW, SB_W), row),
            pl.BlockSpec((TM_ROW, RET_W), row),
            pl.BlockSpec((D_MODEL, D_MODEL), fixed),
            pl.BlockSpec((1, D_MODEL), fixed),
            pl.BlockSpec((1, D_MODEL), fixed),
            pl.BlockSpec((1, D_MODEL), fixed),
            pl.BlockSpec((D_MODEL, 2 * LANES), fixed),
            pl.BlockSpec((1, LANES), fixed),
        ],
        out_specs=[
            pl.BlockSpec((TM_ROW, D_MODEL), row),
            pl.BlockSpec((TM_ROW, D_MODEL), row),
            pl.BlockSpec((N_EXPERTS, TM_ROW), lambda i: (0, i)),
        ],
        out_shape=[
            jax.ShapeDtypeStruct((n, D_MODEL), F32),
            jax.ShapeDtypeStruct((n, D_MODEL), BF16),
            jax.ShapeDtypeStruct((N_EXPERTS, n), F32),
        ],
        compiler_params=_params("parallel"),
        name="outproj_ln_router",
    )(x2, pool_o, sb_o, ret_o, w_bf, b, g, be, rw, rb)


def _dispatch_kernel(gates_ref, xb_ref, tril_ref, xs_ref, gs_ref, oh_ref, post_ref, gatet_ref, cnt_ref):
    tm = gates_ref.shape[1]
    gatet = gates_ref[...]
    sel = gatet >= 0.0
    self_ = jnp.where(sel, 1.0, 0.0)
    incl = _dot(self_.astype(BF16), tril_ref[...])
    pos = incl - self_
    cnt_ref[0] = jnp.broadcast_to(incl[:, tm - 1:tm], (N_EXPERTS, LANES)).astype(jnp.int32)
    post = jnp.where(sel, pos, -1.0)
    post_ref[0] = post
    gatet_ref[0] = gatet
    post_i = post.astype(jnp.int32)
    slot = lax.broadcasted_iota(jnp.int32, (RC_MOE, tm), 0)
    for e0 in range(0, N_EXPERTS, DISPATCH_GROUP):
        pieces = []
        for e in range(e0, e0 + DISPATCH_GROUP):
            onehot = slot == post_i[e:e + 1, :]
            pieces.append(jnp.where(onehot, 1.0, 0.0).astype(BF16))
            gate = jnp.sum(jnp.where(onehot, gatet[e:e + 1, :], 0.0), axis=1, keepdims=True)
            gs_ref[0, e] = jnp.broadcast_to(gate, (RC_MOE, LANES))
        onehots = jnp.concatenate(pieces, axis=0)
        oh_ref[0, e0 * RC_MOE:(e0 + DISPATCH_GROUP) * RC_MOE, :] = onehots.astype(ONEHOT_DT)
        xs = _dot(onehots, xb_ref[...]).astype(BF16)
        xs_ref[0, e0:e0 + DISPATCH_GROUP] = xs.reshape(DISPATCH_GROUP, RC_MOE, D_MODEL)


def _dispatch(gates, x1b, tril):
    n = gates.shape[1]
    nt = n // TM_MOE
    return pl.pallas_call(
        _dispatch_kernel,
        grid=(nt,),
        in_specs=[
            pl.BlockSpec((N_EXPERTS, TM_MOE), lambda i: (0, i)),
            pl.BlockSpec((TM_MOE, D_MODEL), lambda i: (i, 0)),
            pl.BlockSpec((TM_MOE, TM_MOE), lambda i: (0, 0)),
        ],
        out_specs=[
            pl.BlockSpec((1, N_EXPERTS, RC_MOE, D_MODEL), lambda i: (i, 0, 0, 0)),
            pl.BlockSpec((1, N_EXPERTS, RC_MOE, LANES), lambda i: (i, 0, 0, 0)),
            pl.BlockSpec((1, SLOTS, TM_MOE), lambda i: (i, 0, 0)),
            pl.BlockSpec((1, N_EXPERTS, TM_MOE), lambda i: (i, 0, 0)),
            pl.BlockSpec((1, N_EXPERTS, TM_MOE), lambda i: (i, 0, 0)),
            pl.BlockSpec((1, N_EXPERTS, LANES), lambda i: (i, 0, 0)),
        ],
        out_shape=[
            jax.ShapeDtypeStruct((nt, N_EXPERTS, RC_MOE, D_MODEL), BF16),
            jax.ShapeDtypeStruct((nt, N_EXPERTS, RC_MOE, LANES), F32),
            jax.ShapeDtypeStruct((nt, SLOTS, TM_MOE), ONEHOT_DT),
            jax.ShapeDtypeStruct((nt, N_EXPERTS, TM_MOE), F32),
            jax.ShapeDtypeStruct((nt, N_EXPERTS, TM_MOE), F32),
            jax.ShapeDtypeStruct((nt, N_EXPERTS, LANES), jnp.int32),
        ],
        compiler_params=_params("parallel"),
        name="dispatch",
    )(gates, x1b, tril)


def _swiglu_ffn(x, wgu, bgu, wd, bd):
    h = _dot(x, wgu) + bgu
    g = jnp.minimum(h[:, :D_FF], SWIGLU_LIMIT)
    up = jnp.clip(h[:, D_FF:], -SWIGLU_LIMIT, SWIGLU_LIMIT)
    act = (up + 1.0) * (g / (1.0 + jnp.exp(-SWIGLU_ALPHA * g)))
    return _dot(act.astype(BF16), wd) + bd


def _ffn_kernel(tail_ref, xs_ref, gs_ref, wgu_ref, bgu_ref, wd_ref, bd_ref, ys_ref, wgu_bf_ref, wd_bf_ref):
    e = pl.program_id(0)
    j = pl.program_id(1)

    @pl.when(j == 0)
    def _():
        wgu_bf_ref[...] = wgu_ref[0, 0].astype(BF16)
        wd_bf_ref[...] = wd_ref[0, 0].astype(BF16)

    def run(lo, hi):
        rows = FFN_TILES * (hi - lo)
        x = xs_ref[:, 0, lo:hi, :].reshape(rows, D_MODEL)
        y = _swiglu_ffn(x, wgu_bf_ref[...], bgu_ref[0], wd_bf_ref[...], bd_ref[0])
        gate = gs_ref[:, 0, lo:hi, :].reshape(rows, LANES)[:, :1]
        ys_ref[:, 0, lo:hi, :] = (y * gate).astype(BF16).reshape(FFN_TILES, hi - lo, D_MODEL)

    run(0, RC_MAIN)
    used = tail_ref[e * pl.num_programs(1) + j] > 0

    @pl.when(used)
    def _():
        run(RC_MAIN, RC_MOE)

    @pl.when(jnp.logical_not(used))
    def _():
        ys_ref[:, 0, RC_MAIN:RC_MOE, :] = jnp.zeros((FFN_TILES, RC_MOE - RC_MAIN, D_MODEL), BF16)


def _ffn(tail, xs, gs, w_gate_up, bgu, w_down, bd, layer):
    nt = xs.shape[0]
    grid_spec = pltpu.PrefetchScalarGridSpec(
        num_scalar_prefetch=1,
        grid=(N_EXPERTS, nt // FFN_TILES),
        in_specs=[
            pl.BlockSpec((FFN_TILES, 1, RC_MOE, D_MODEL), lambda e, j, t: (j, e, 0, 0)),
            pl.BlockSpec((FFN_TILES, 1, RC_MOE, LANES), lambda e, j, t: (j, e, 0, 0)),
            pl.BlockSpec((1, 1, D_MODEL, 2 * D_FF), lambda e, j, t: (layer, e, 0, 0)),
            pl.BlockSpec((1, 1, 2 * D_FF), lambda e, j, t: (e, 0, 0)),
            pl.BlockSpec((1, 1, D_FF, D_MODEL), lambda e, j, t: (layer, e, 0, 0)),
            pl.BlockSpec((1, 1, D_MODEL), lambda e, j, t: (e, 0, 0)),
        ],
        out_specs=pl.BlockSpec((FFN_TILES, 1, RC_MOE, D_MODEL), lambda e, j, t: (j, e, 0, 0)),
        scratch_shapes=[
            pltpu.VMEM((D_MODEL, 2 * D_FF), BF16),
            pltpu.VMEM((D_FF, D_MODEL), BF16),
        ],
    )
    return pl.pallas_call(
        _ffn_kernel,
        grid_spec=grid_spec,
        out_shape=jax.ShapeDtypeStruct(xs.shape, BF16),
        compiler_params=_params("parallel", "arbitrary"),
        name="expert_ffn",
    )(tail, xs, gs, w_gate_up, bgu, w_down, bd)


def _combine_kernel(layer, cnt_ref, ys_ref, oh_ref, x1_ref, g_ref, b_ref, post_ref,
                    gatet_ref, bgu_ref, bd_ref, wgu_hbm, wd_hbm, o_ref, acc_ref, wgu_buf, wd_buf, sem):
    i = pl.program_id(0)
    tm = x1_ref.shape[0]
    width = DISPATCH_GROUP * RC_MOE
    acc = None
    for e0 in range(0, N_EXPERTS, DISPATCH_GROUP):
        sl = slice(e0 * RC_MOE, e0 * RC_MOE + width)
        part = lax.dot_general(oh_ref[0, sl, :].astype(BF16),
                               ys_ref[0, e0:e0 + DISPATCH_GROUP].reshape(width, D_MODEL),
                               _TN, preferred_element_type=F32)
        acc = part if acc is None else acc + part
    acc_ref[...] = acc

    slot = lax.broadcasted_iota(jnp.int32, (RC_MOE, tm), 0)

    def per_expert(e, carry):
        cnt = cnt_ref[i * N_EXPERTS + e]

        @pl.when(cnt > RC_MOE)
        def _():
            copies = (pltpu.make_async_copy(wgu_hbm.at[layer, e], wgu_buf, sem.at[0]),
                      pltpu.make_async_copy(wd_hbm.at[layer, e], wd_buf, sem.at[1]))
            for cp in copies:
                cp.start()
            for cp in copies:
                cp.wait()
            prow = post_ref[0, pl.ds(e, 1), :].astype(jnp.int32)
            grow = gatet_ref[0, pl.ds(e, 1), :]

            def chunk(c, carry2):
                onehot = (slot + c * RC_MOE) == prow
                oh = jnp.where(onehot, 1.0, 0.0).astype(BF16)
                xg = _dot(oh, x1_ref[...].astype(BF16)).astype(BF16)
                y = _swiglu_ffn(xg, wgu_buf[...].astype(BF16), bgu_ref[e], wd_buf[...].astype(BF16), bd_ref[e])
                gate = jnp.sum(jnp.where(onehot, grow, 0.0), axis=1, keepdims=True)
                yg = (y * gate).astype(BF16)
                acc_ref[...] += lax.dot_general(oh, yg, _TN, preferred_element_type=F32)
                return carry2

            lax.fori_loop(1, (cnt + RC_MOE - 1) // RC_MOE, chunk, 0)

        return carry

    lax.fori_loop(0, N_EXPERTS, per_expert, 0)
    o_ref[...] = _layer_norm(DN_ALPHA * x1_ref[...] + acc_ref[...], g_ref[...], b_ref[...])


def _combine(cnt, ys, oh, x1, g, b, post, gatet, bgu, bd, wgu, wd, layer):
    n = x1.shape[0]
    nt = n // TM_MOE
    row = lambda i, c: (i, 0)
    fixed2 = lambda i, c: (0, 0)
    fixed3 = lambda i, c: (0, 0, 0)
    grid_spec = pltpu.PrefetchScalarGridSpec(
        num_scalar_prefetch=1,
        grid=(nt,),
        in_specs=[
            pl.BlockSpec((1, N_EXPERTS, RC_MOE, D_MODEL), lambda i, c: (i, 0, 0, 0)),
            pl.BlockSpec((1, SLOTS, TM_MOE), lambda i, c: (i, 0, 0)),
            pl.BlockSpec((TM_MOE, D_MODEL), row),
            pl.BlockSpec((1, D_MODEL), fixed2),
            pl.BlockSpec((1, D_MODEL), fixed2),
            pl.BlockSpec((1, N_EXPERTS, TM_MOE), lambda i, c: (i, 0, 0)),
            pl.BlockSpec((1, N_EXPERTS, TM_MOE), lambda i, c: (i, 0, 0)),
            pl.BlockSpec((N_EXPERTS, 1, 2 * D_FF), fixed3),
            pl.BlockSpec((N_EXPERTS, 1, D_MODEL), fixed3),
            pl.BlockSpec(memory_space=pl.ANY),
            pl.BlockSpec(memory_space=pl.ANY),
        ],
        out_specs=pl.BlockSpec((TM_MOE, D_MODEL), row),
        scratch_shapes=[
            pltpu.VMEM((TM_MOE, D_MODEL), F32),
            pltpu.VMEM((D_MODEL, 2 * D_FF), F32),
            pltpu.VMEM((D_FF, D_MODEL), F32),
            pltpu.SemaphoreType.DMA((2,)),
        ],
    )
    return pl.pallas_call(
        functools.partial(_combine_kernel, layer),
        grid_spec=grid_spec,
        out_shape=jax.ShapeDtypeStruct((n, D_MODEL), F32),
        compiler_params=_params("arbitrary"),
        name="combine_ln",
    )(cnt, ys, oh, x1, g, b, post, gatet, bgu, bd, wgu, wd)


def _block_diag(pool_w):
    groups = pool_w.shape[0]
    out = jnp.zeros((POOL_W, POOL_W), pool_w.dtype)
    for gi in range(groups):
        out = out.at[gi * POOL_CH:(gi + 1) * POOL_CH, gi * POOL_CH:(gi + 1) * POOL_CH].set(pool_w[gi])
    return out


def _layer(x2, batch, seq, consts, layer, w_in, b_in, pool_w, pool_scale, ret_norm_g, w_out, b_out, ln1_g,
           ln1_b, router_w, router_b, w_gate_up, b_gate_up, w_down, b_down, ln2_g, ln2_b):
    ret_consts, tri2, tril = consts
    row = lambda a: a.reshape(1, -1).astype(F32)
    pool_o, sb, ret = _inproj(x2, w_in.astype(BF16), row(b_in), _block_diag(pool_w).astype(BF16),
                              row(pool_scale), seq)
    sb_o = _stick_breaking(sb, tri2, batch, seq)
    ret_o = _retention(ret, ret_consts, row(ret_norm_g), batch, seq)
    rw = jnp.pad(router_w.astype(F32), ((0, 0), (0, LANES - N_EXPERTS)))
    rw_hi = rw.astype(BF16)
    rw = jnp.concatenate([rw_hi, (rw - rw_hi.astype(F32)).astype(BF16)], axis=1)
    rb = jnp.pad(router_b.astype(F32), (0, LANES - N_EXPERTS)).reshape(1, LANES)
    x1, x1b, gates = _outproj(x2, pool_o, sb_o, ret_o, w_out.astype(BF16), row(b_out), row(ln1_g),
                              row(ln1_b), rw, rb)
    xs, gs, oh, post, gatet, cnt = _dispatch(gates, x1b, tril)
    bgu = b_gate_up.reshape(N_EXPERTS, 1, 2 * D_FF).astype(F32)
    bd = b_down.reshape(N_EXPERTS, 1, D_MODEL).astype(F32)
    cnt2 = cnt[:, :, 0]
    tail = jnp.any((cnt2 > RC_MAIN).reshape(-1, FFN_TILES, N_EXPERTS), axis=1)
    ys = _ffn(tail.T.reshape(-1).astype(jnp.int32), xs, gs, w_gate_up, bgu, w_down, bd, layer)
    cnt_flat = cnt2.reshape(-1)
    return _combine(cnt_flat, ys, oh, x1, row(ln2_g), row(ln2_b), post, gatet, bgu, bd, w_gate_up, w_down,
                    layer)


def kernel(x, w_in, b_in, pool_w, pool_scale, ret_norm_g, w_out, b_out, ln1_g, ln1_b, router_w, router_b,
           w_gate_up, b_gate_up, w_down, b_down, ln2_g, ln2_b):
    batch, seq, d = x.shape
    n = batch * seq
    assert d == D_MODEL and seq % TS_RET == 0 and seq % TQ_SB == 0 and seq % TM_ROW == 0
    assert n % (TM_MOE * FFN_TILES) == 0 and n % TM_ROW == 0 and seq >= SB_SPAN
    j = np.arange(LANES)
    tri = (j[:, None] >= j[None, :]).astype(np.float32)
    tri2 = jnp.asarray(np.concatenate([tri, np.ones_like(tri)], axis=1), BF16)
    r = np.arange(TM_MOE)
    tril = jnp.asarray((r[:, None] <= r[None, :]).astype(np.float32), BF16)
    consts = (_retention_consts(seq), tri2, tril)
    x2 = x.reshape(n, d)
    for l in range(DEPTH):
        x2 = _layer(x2, batch, seq, consts, l, w_in[l], b_in[l], pool_w[l], pool_scale[l], ret_norm_g[l],
                    w_out[l], b_out[l], ln1_g[l], ln1_b[l], router_w[l], router_b[l], w_gate_up,
                    b_gate_up[l], w_down, b_down[l], ln2_g[l], ln2_b[l])
    return x2.reshape(batch, seq, d)
```

```python
import functools

import numpy as np
import jax
import jax.numpy as jnp
from jax import lax
from jax.experimental import pallas as pl
from jax.experimental.pallas import tpu as pltpu

D_MODEL = 1024
DEPTH = 2
CHUNK = 64
HEAD_DIM = 64
POOL_CH = 64
POOL_W = 256
POOL_HALO = 16
SB_W = 384
RET_W = 384
RET_HEADS = 6
IN_W = POOL_W + 3 * SB_W + 4 * RET_W
ROPE_BASE = 10000.0
N_EXPERTS = 32
TOP_K = 4
D_FF = D_MODEL
SWIGLU_LIMIT = 7.0
SWIGLU_ALPHA = 1.702
DN_ALPHA = (2.0 * DEPTH) ** 0.25
LN_EPS = 1e-5

LANES = 128
VMEM_LIMIT = 56 * 1024 * 1024

TM_ROW = 1024
OUTPROJ_SUB = 256
TQ_SB = 512
SB_ROWS = 64
SB_SPAN = 256
TS_RET = 1024
TM_MOE = 512
RC_MOE = 96
RC_MAIN = 80
FFN_TILES = 8
SLOTS = N_EXPERTS * RC_MOE
DISPATCH_GROUP = 8
SB_SKIP = -100.0

BF16 = jnp.bfloat16
F32 = jnp.float32
ONEHOT_DT = jnp.float8_e4m3fn

_NT = (((1,), (1,)), ((), ()))
_TN = (((0,), (0,)), ((), ()))


def _dot(a, b):
    return jnp.dot(a, b, preferred_element_type=F32)


def _params(*sem):
    return pltpu.CompilerParams(dimension_semantics=sem, vmem_limit_bytes=VMEM_LIMIT)


def _pool_mix(cur, halo, t0, w, scale):
    ts = cur.shape[0]
    ext = jnp.concatenate([halo, cur], axis=0)
    a2 = ext[1:] + ext[:-1]
    a4 = a2[2:] + a2[:-2]
    a8 = a4[4:] + a4[:-4]
    a16 = a8[8:] + a8[:-8]
    lane = lax.broadcasted_iota(jnp.int32, (ts, POOL_W), 1)
    grp = lane // POOL_CH
    win = jnp.where(grp == 0, a2[15:15 + ts],
                    jnp.where(grp == 1, a4[13:13 + ts],
                              jnp.where(grp == 2, a8[9:9 + ts], a16[1:1 + ts])))
    width = jnp.where(grp == 0, 2, jnp.where(grp == 1, 4, jnp.where(grp == 2, 8, 16)))
    t = t0 + lax.broadcasted_iota(jnp.int32, (ts, POOL_W), 0)
    cnt = jnp.minimum(t + 1, width).astype(F32)
    pooled = win / cnt - cur
    return _dot(pooled.astype(BF16), w) * scale


def _inproj_kernel(tiles_per_seq, x_ref, wf_ref, b_ref, pw_ref, ps_ref, pool_ref, sb_ref, ret_ref, halo_ref,
                   w_ref):
    it = pl.program_id(0) % tiles_per_seq

    @pl.when(pl.program_id(0) == 0)
    def _():
        w_ref[...] = wf_ref[0].astype(BF16)

    xb = x_ref[...].astype(BF16)
    o1 = POOL_W + 3 * SB_W
    @pl.when(it == 0)
    def _():
        halo_ref[...] = jnp.zeros_like(halo_ref)

    u = _dot(xb, w_ref[:, :POOL_W]) + b_ref[:, :POOL_W]
    sb_ref[...] = (_dot(xb, w_ref[:, POOL_W:o1]) + b_ref[:, POOL_W:o1]).astype(BF16)
    ret_ref[...] = _dot(xb, w_ref[:, o1:]) + b_ref[:, o1:]
    pool_ref[...] = _pool_mix(u, halo_ref[...], it * TM_ROW, pw_ref[...], ps_ref[...]).astype(BF16)
    halo_ref[...] = u[TM_ROW - POOL_HALO:, :]


def _inproj(x2, w_in, layer, b, pool_w_bd, pool_scale, seq):
    n = x2.shape[0]
    fixed = lambda i: (0, 0)
    return pl.pallas_call(
        functools.partial(_inproj_kernel, seq // TM_ROW),
        grid=(n // TM_ROW,),
        in_specs=[
            pl.BlockSpec((TM_ROW, D_MODEL), lambda i: (i, 0)),
            pl.BlockSpec((1, D_MODEL, IN_W), lambda i: (layer, 0, 0), pipeline_mode=pl.Buffered(1)),
            pl.BlockSpec((1, IN_W), fixed),
            pl.BlockSpec((POOL_W, POOL_W), fixed),
            pl.BlockSpec((1, POOL_W), fixed),
        ],
        out_specs=[
            pl.BlockSpec((TM_ROW, POOL_W), lambda i: (i, 0)),
            pl.BlockSpec((TM_ROW, 3 * SB_W), lambda i: (i, 0)),
            pl.BlockSpec((TM_ROW, 4 * RET_W), lambda i: (i, 0)),
        ],
        out_shape=[
            jax.ShapeDtypeStruct((n, POOL_W), BF16),
            jax.ShapeDtypeStruct((n, 3 * SB_W), BF16),
            jax.ShapeDtypeStruct((n, 4 * RET_W), F32),
        ],
        scratch_shapes=[pltpu.VMEM((POOL_HALO, POOL_W), F32), pltpu.VMEM((D_MODEL, IN_W), BF16)],
        compiler_params=_params("arbitrary"),
        name="inproj_pool",
    )(x2, w_in, b, pool_w_bd, pool_scale)


def _sb_kernel(q_ref, k_ref, v_ref, tri_ref, o_ref, acc_ref, carry_ref):
    qi = pl.program_id(1)
    tq = q_ref.shape[1]
    groups = q_ref.shape[2] // LANES
    lane = lax.broadcasted_iota(jnp.int32, (tq, LANES), 1)
    qh = []
    for p in range(groups):
        q = q_ref[0, :, p * LANES:(p + 1) * LANES] * jnp.asarray(HEAD_DIM ** -0.5, BF16)
        zero = jnp.zeros_like(q)
        qh += [jnp.where(lane < HEAD_DIM, q, zero), jnp.where(lane >= HEAD_DIM, q, zero)]

    def log_keep(z):
        return jnp.minimum(-z, 0.0) - jnp.log(1.0 + jnp.exp(-jnp.abs(z)))

    def suffix_sums(lk):
        lk = lk.astype(BF16)
        out = []
        for b in range(lk.shape[1] // LANES):
            rs = _dot(lk[:, b * LANES:(b + 1) * LANES], tri_ref[...])
            out.append((rs[:, :LANES], rs[:, LANES:]))
        return out

    t0 = qi * tq
    halves = range(tq // SB_ROWS)
    heads = range(2 * groups)
    cols = [slice((h // 2) * LANES, (h // 2 + 1) * LANES) for h in heads]
    rows = [slice(u * SB_ROWS, (u + 1) * SB_ROWS) for u in halves]
    starts = [pl.multiple_of(jnp.maximum(t0 + (u + 1) * SB_ROWS - SB_SPAN, 0), SB_ROWS) for u in halves]
    below = []
    for u in halves:
        qpos = t0 + u * SB_ROWS + lax.broadcasted_iota(jnp.int32, (SB_ROWS, SB_SPAN), 0)
        kpos = starts[u] + lax.broadcasted_iota(jnp.int32, (SB_ROWS, SB_SPAN), 1)
        below.append(kpos < qpos)
    items = [(u, h) for u in halves for h in heads]
    zs = {(u, h): lax.dot_general(qh[h][rows[u]], k_ref[0, pl.ds(starts[u], SB_SPAN), cols[h]], _NT,
                                  preferred_element_type=F32) for u, h in items}
    sums = {(u, h): suffix_sums(jnp.where(below[u], log_keep(zs[u, h]), 0.0)) for u, h in items}
    weights, runs = {}, {}
    for u, h in items:
        run = None
        r = [None] * len(sums[u, h])
        for b in reversed(range(len(r))):
            r_in, tot = sums[u, h][b]
            r[b] = r_in if run is None else r_in + run
            run = tot if run is None else run + tot
        weights[u, h] = jnp.where(below[u], jnp.exp(zs[u, h] + jnp.concatenate(r, axis=1)), 0.0).astype(BF16)
        runs[u, h] = run
    for u, h in items:
        acc_ref[h, rows[u], :] = _dot(weights[u, h], v_ref[0, pl.ds(starts[u], SB_SPAN), cols[h]])
        carry_ref[h, rows[u], :] = runs[u, h]
    worst = []
    for u in halves:
        top = runs[u, 0]
        for h in heads[1:]:
            top = jnp.maximum(top, runs[u, h])
        worst.append(jnp.max(top))

    def tile(u, kb):
        start = pl.multiple_of(kb * LANES, LANES)
        older = (start + lax.broadcasted_iota(jnp.int32, (SB_ROWS, LANES), 1)) < starts[u]
        worst = None
        for h in heads:
            k = k_ref[0, pl.ds(start, LANES), cols[h]]
            v = v_ref[0, pl.ds(start, LANES), cols[h]]
            z = lax.dot_general(qh[h][rows[u]], k, _NT, preferred_element_type=F32)
            (r_in, tot), = suffix_sums(jnp.where(older, log_keep(z), 0.0))
            carry = carry_ref[h, rows[u], :]
            a = jnp.where(older, jnp.exp(z + r_in + carry), 0.0)
            acc_ref[h, rows[u], :] += _dot(a.astype(BF16), v)
            carry = carry + tot
            carry_ref[h, rows[u], :] = carry
            m = jnp.max(carry)
            worst = m if worst is None else jnp.maximum(worst, m)
        return worst

    def cond(state):
        kb, worst = state
        return jnp.logical_and(kb >= 0, worst > SB_SKIP)

    for u in halves:
        lax.while_loop(cond, lambda state, u=u: (state[0] - 1, tile(u, state[0])),
                       ((starts[u] + LANES - 1) // LANES - 1, worst[u]))
    for p in range(groups):
        o_ref[0, :, p * LANES:(p + 1) * LANES] = jnp.where(
            lane < HEAD_DIM, acc_ref[2 * p], acc_ref[2 * p + 1]).astype(BF16)


def _stick_breaking(sb, tri2, batch, seq):
    sb3 = sb.reshape(batch, seq, 3 * SB_W)
    heads = SB_W // HEAD_DIM
    once = pl.Buffered(1)
    out = pl.pallas_call(
        _sb_kernel,
        grid=(batch, seq // TQ_SB),
        in_specs=[
            pl.BlockSpec((1, TQ_SB, SB_W), lambda b, i: (b, i, 0)),
            pl.BlockSpec((1, seq, SB_W), lambda b, i: (b, 0, 1), pipeline_mode=once),
            pl.BlockSpec((1, seq, SB_W), lambda b, i: (b, 0, 2), pipeline_mode=once),
            pl.BlockSpec((LANES, 2 * LANES), lambda b, i: (0, 0)),
        ],
        out_specs=pl.BlockSpec((1, TQ_SB, SB_W), lambda b, i: (b, i, 0)),
        out_shape=jax.ShapeDtypeStruct((batch, seq, SB_W), BF16),
        scratch_shapes=[
            pltpu.VMEM((heads, TQ_SB, LANES), F32),
            pltpu.VMEM((heads, TQ_SB, LANES), F32),
        ],
        compiler_params=_params("parallel", "arbitrary"),
        name="stick_breaking",
    )(sb3, sb3, sb3, tri2)
    return out.reshape(batch * seq, SB_W)


def _ret_kernel(q_ref, k_ref, v_ref, g_ref, cos_ref, sin_ref, dmat_ref, dec_ref, xi_ref, gch_ref,
                swap_ref, avg_ref, ng_ref, o_ref, state_ref, obuf_ref):
    si = pl.program_id(1)
    ts = q_ref.shape[1]
    groups = q_ref.shape[2] // LANES
    cols = [slice(p * LANES, (p + 1) * LANES) for p in range(groups)]

    @pl.when(si == 0)
    def _():
        state_ref[...] = jnp.zeros_like(state_ref)

    cos = cos_ref[...]
    sin = sin_ref[...]

    def lane_mix(x, m_ref):
        hi = x.astype(BF16)
        lo = (x - hi.astype(F32)).astype(BF16)
        return _dot(jnp.concatenate([hi, lo], axis=1), m_ref[...])

    def rot(x):
        return x * cos + lane_mix(x, swap_ref) * sin

    qr = [rot(q_ref[0, :, c]) for c in cols]
    kr = [rot(k_ref[0, :, c]) * (HEAD_DIM ** -0.5) for c in cols]
    v = [v_ref[0, :, c] for c in cols]

    lane_c = lax.broadcasted_iota(jnp.int32, (CHUNK, LANES), 1)
    head0 = lane_c < HEAD_DIM
    r_i = lax.broadcasted_iota(jnp.int32, (LANES, LANES), 0) // HEAD_DIM
    c_i = lax.broadcasted_iota(jnp.int32, (LANES, LANES), 1) // HEAD_DIM
    same_head = r_i == c_i

    chunks = [slice(n * CHUNK, (n + 1) * CHUNK) for n in range(ts // CHUNK)]
    items = [(p, n) for p in range(groups) for n in range(len(chunks))]
    kcb = {(p, n): kr[p][chunks[n]].astype(BF16) for p, n in items}
    vcb = {(p, n): v[p][chunks[n]].astype(BF16) for p, n in items}
    kvs = {(p, n): lax.dot_general(kcb[p, n], (v[p][chunks[n]] * dec_ref[p]).astype(BF16), _TN,
                                   preferred_element_type=F32) for p, n in items}
    scores = {}
    for p, n in items:
        qc = qr[p][chunks[n]]
        qstack = jnp.concatenate([jnp.where(head0, qc, 0.0), jnp.where(head0, 0.0, qc)], axis=0)
        scores[p, n] = (lax.dot_general(qstack.astype(BF16), kcb[p, n], _NT, preferred_element_type=F32)
                        * dmat_ref[p]).astype(BF16)
    o2s = {(p, n): _dot(scores[p, n], vcb[p, n]) for p, n in items}
    states = {}
    for p in range(groups):
        state = state_ref[p]
        for n in range(len(chunks)):
            states[p, n] = state.astype(BF16)
            state = gch_ref[p] * state + jnp.where(same_head, kvs[p, n], 0.0)
        state_ref[p] = state
    for p, n in items:
        o_intra = jnp.where(head0, o2s[p, n][:CHUNK], o2s[p, n][CHUNK:])
        o_inter = _dot((qr[p][chunks[n]] * xi_ref[p]).astype(BF16), states[p, n])
        obuf_ref[chunks[n], cols[p]] = o_intra + o_inter

    for c in cols:
        o = obuf_ref[:, c]
        mu = lane_mix(o, avg_ref)
        oc = o - mu
        var = lane_mix(oc * oc, avg_ref)
        normed = oc * lax.rsqrt(var + LN_EPS) * ng_ref[:, c]
        gate = g_ref[0, :, c]
        o_ref[0, :, c] = (gate / (1.0 + jnp.exp(-gate)) * normed).astype(BF16)


def _retention(ret, consts, norm_g, batch, seq):
    cos, sin, dmat, dec, xi, gch, swap, avg = consts
    ret3 =ret.reshape(batch, seq, 4 * RET_W)
    pairs = RET_W // LANES
    fixed3 = lambda b, i: (0, 0, 0)
    out = pl.pallas_call(
        _ret_kernel,
        grid=(batch, seq // TS_RET),
        in_specs=[
            pl.BlockSpec((1, TS_RET, RET_W), lambda b, i: (b, i, 0)),
            pl.BlockSpec((1, TS_RET, RET_W), lambda b, i: (b, i, 1)),
            pl.BlockSpec((1, TS_RET, RET_W), lambda b, i: (b, i, 2)),
            pl.BlockSpec((1, TS_RET, RET_W), lambda b, i: (b, i, 3)),
            pl.BlockSpec((TS_RET, LANES), lambda b, i: (i, 0)),
            pl.BlockSpec((TS_RET, LANES), lambda b, i: (i, 0)),
            pl.BlockSpec((pairs, 2 * CHUNK, CHUNK), fixed3),
            pl.BlockSpec((pairs, CHUNK, LANES), fixed3),
            pl.BlockSpec((pairs, CHUNK, LANES), fixed3),
            pl.BlockSpec((pairs, 1, LANES), fixed3),
            pl.BlockSpec((2 * LANES, LANES), lambda b, i: (0, 0)),
            pl.BlockSpec((2 * LANES, LANES), lambda b, i: (0, 0)),
            pl.BlockSpec((1, RET_W), lambda b, i: (0, 0)),
        ],
        out_specs=pl.BlockSpec((1, TS_RET, RET_W), lambda b, i: (b, i, 0)),
        out_shape=jax.ShapeDtypeStruct((batch, seq, RET_W), BF16),
        scratch_shapes=[
            pltpu.VMEM((pairs, LANES, LANES), F32),
            pltpu.VMEM((TS_RET, RET_W), F32),
        ],
        compiler_params=_params("parallel", "arbitrary"),
        name="retention",
    )(ret3, ret3, ret3, ret3, cos, sin, dmat, dec, xi, gch, swap, avg, norm_g)
    return out.reshape(batch * seq, RET_W)


def _retention_consts(seq):
    half = HEAD_DIM // 2
    pos = jnp.arange(seq, dtype=F32)
    inv = ROPE_BASE ** (-jnp.arange(half, dtype=F32) / half)
    ang = pos[:, None] * inv[None, :]
    cos_h, sin_h = jnp.cos(ang), jnp.sin(ang)
    cos = jnp.tile(cos_h, (1, 2 * LANES // HEAD_DIM))
    sin = jnp.tile(jnp.concatenate([-sin_h, sin_h], axis=1), (1, LANES // HEAD_DIM))
    log_g = jnp.log(1.0 - 2.0 ** (-5.0 - jnp.arange(RET_HEADS, dtype=F32)))
    c = jnp.arange(CHUNK, dtype=F32)
    dmat = jnp.exp(jnp.abs(c[:, None] - c[None, :])[None] * log_g[:, None, None])
    dmat = dmat.reshape(RET_HEADS // 2, 2 * CHUNK, CHUNK)
    lane_log_g = jnp.repeat(log_g, HEAD_DIM).reshape(RET_HEADS // 2, 1, LANES)
    dec = jnp.exp((CHUNK - 1 - c)[None, :, None] * lane_log_g)
    xi = jnp.exp((c + 1.0)[None, :, None] * lane_log_g)
    gch = jnp.exp(CHUNK * lane_log_g)
    l = np.arange(LANES)
    partner = np.where(l % HEAD_DIM < half, l + half, l - half)
    swap = (l[:, None] == partner[None, :]).astype(np.float32)
    avg = (l[:, None] // HEAD_DIM == l[None, :] // HEAD_DIM).astype(np.float32) / HEAD_DIM
    swap = jnp.asarray(np.concatenate([swap, swap], axis=0), BF16)
    avg = jnp.asarray(np.concatenate([avg, avg], axis=0), BF16)
    return cos, sin, dmat, dec, xi, gch, swap, avg


def _layer_norm(z, g, b):
    mu = jnp.mean(z, axis=-1, keepdims=True)
    zc = z - mu
    var = jnp.mean(zc * zc, axis=-1, keepdims=True)
    return zc * lax.rsqrt(var + LN_EPS) * g + b


def _outproj_kernel(x_ref, p_ref, s_ref, r_ref, wf_ref, b_ref, g_ref, be_ref, rw_ref, rb_ref,
                    x1_ref, x1b_ref, gates_ref, w_ref):
    o1 = POOL_W + SB_W
    tm = x_ref.shape[0]

    @pl.when(pl.program_id(0) == 0)
    def _():
        w_ref[...] = wf_ref[0].astype(BF16)

    subs = [slice(s, s + OUTPROJ_SUB) for s in range(0, tm, OUTPROJ_SUB)]
    ys = [_dot(p_ref[sl, :], w_ref[:POOL_W, :]) + _dot(s_ref[sl, :], w_ref[POOL_W:o1, :])
          + _dot(r_ref[sl, :], w_ref[o1:, :]) for sl in subs]
    x1s = [_layer_norm(DN_ALPHA * x_ref[sl, :] + (y + b_ref[...]), g_ref[...], be_ref[...])
           for sl, y in zip(subs, ys)]
    logits = []
    for sl, x1 in zip(subs, x1s):
        x1_ref[sl, :] = x1
        x_hi = x1.astype(BF16)
        x1b_ref[sl, :] = x_hi
        x_mid = (x1 - x_hi.astype(F32)).astype(BF16)
        t = _dot(x_hi, rw_ref[...])
        logits.append(t[:, :LANES] + t[:, LANES:] + _dot(x_mid, rw_ref[:, :LANES]) + rb_ref[...])
    expert = lax.broadcasted_iota(jnp.int32, (N_EXPERTS, OUTPROJ_SUB), 0).astype(F32)
    for sl, lg in zip(subs, logits):
        vals = lg.T[:N_EXPERTS, :]
        top_v, top_sel = [], []
        for _ in range(TOP_K):
            m = jnp.max(vals, axis=0, keepdims=True)
            idx = jnp.min(jnp.where(vals == m, expert, float(N_EXPERTS)), axis=0, keepdims=True)
            sel = expert == idx
            vals = jnp.where(sel, -jnp.inf, vals)
            top_v.append(m)
            top_sel.append(sel)
        ex = [jnp.exp(m - top_v[0]) for m in top_v]
        den = ex[0] + ex[1] + ex[2] + ex[3]
        gates = jnp.full((N_EXPERTS, OUTPROJ_SUB), -1.0, F32)
        for sel, e in zip(top_sel, ex):
            gates = jnp.where(sel, e / den, gates)
        gates_ref[:, sl] = gates


def _outproj(x2, pool_o, sb_o, ret_o, w_out, layer, b, g, be, rw, rb):
    n = x2.shape[0]
    row = lambda i: (i, 0)
    fixed = lambda i: (0, 0)
    return pl.pallas_call(
        _outproj_kernel,
        grid=(n // TM_ROW,),
        in_specs=[
            pl.BlockSpec((TM_ROW, D_MODEL), row),
            pl.BlockSpec((TM_ROW, POOL_W), row),
            pl.BlockSpec((TM_ROW, SB_W), row),
            pl.BlockSpec((TM_ROW, RET_W), row),
            pl.BlockSpec((1, D_MODEL, D_MODEL), lambda i: (layer, 0, 0), pipeline_mode=pl.Buffered(1)),
            pl.BlockSpec((1, D_MODEL), fixed),
            pl.BlockSpec((1, D_MODEL), fixed),
            pl.BlockSpec((1, D_MODEL), fixed),
            pl.BlockSpec((D_MODEL, 2 * LANES), fixed),
            pl.BlockSpec((1, LANES), fixed),
        ],
        out_specs=[
            pl.BlockSpec((TM_ROW, D_MODEL), row),
            pl.BlockSpec((TM_ROW, D_MODEL), row),
            pl.BlockSpec((N_EXPERTS, TM_ROW), lambda i: (0, i)),
        ],
        out_shape=[
            jax.ShapeDtypeStruct((n, D_MODEL), F32),
            jax.ShapeDtypeStruct((n, D_MODEL), BF16),
            jax.ShapeDtypeStruct((N_EXPERTS, n), F32),
        ],
        scratch_shapes=[pltpu.VMEM((D_MODEL, D_MODEL), BF16)],
        compiler_params=_params("arbitrary"),
        name="outproj_ln_router",
    )(x2, pool_o, sb_o, ret_o, w_out, b, g, be, rw, rb)


def _dispatch_kernel(gates_ref, xb_ref, tril_ref, xs_ref, gs_ref, oh_ref, post_ref, gatet_ref, cnt_ref):
    tm = gates_ref.shape[1]
    gatet = gates_ref[...]
    sel = gatet >= 0.0
    self_ = jnp.where(sel, 1.0, 0.0)
    incl = _dot(self_.astype(BF16), tril_ref[...])
    pos = incl - self_
    cnt_ref[0] = jnp.broadcast_to(incl[:, tm - 1:tm], (N_EXPERTS, LANES)).astype(jnp.int32)
    post = jnp.where(sel, pos, -1.0)
    post_ref[0] = post
    gatet_ref[0] = gatet
    post_i = post.astype(jnp.int32)
    slot = lax.broadcasted_iota(jnp.int32, (RC_MOE, tm), 0)
    for e0 in range(0, N_EXPERTS, DISPATCH_GROUP):
        pieces = []
        for e in range(e0, e0 + DISPATCH_GROUP):
            onehot = slot == post_i[e:e + 1, :]
            pieces.append(jnp.where(onehot, 1.0, 0.0).astype(BF16))
            gate = jnp.sum(jnp.where(onehot, gatet[e:e + 1, :], 0.0), axis=1, keepdims=True)
            gs_ref[0, e] = jnp.broadcast_to(gate, (RC_MOE, LANES))
        onehots = jnp.concatenate(pieces, axis=0)
        oh_ref[0, e0 * RC_MOE:(e0 + DISPATCH_GROUP) * RC_MOE, :] = onehots.astype(ONEHOT_DT)
        xs = _dot(onehots, xb_ref[...]).astype(BF16)
        xs_ref[0, e0:e0 + DISPATCH_GROUP] = xs.reshape(DISPATCH_GROUP, RC_MOE, D_MODEL)


def _dispatch(gates, x1b, tril):
    n = gates.shape[1]
    nt = n // TM_MOE
    return pl.pallas_call(
        _dispatch_kernel,
        grid=(nt,),
        in_specs=[
            pl.BlockSpec((N_EXPERTS, TM_MOE), lambda i: (0, i)),
            pl.BlockSpec((TM_MOE, D_MODEL), lambda i: (i, 0)),
            pl.BlockSpec((TM_MOE, TM_MOE), lambda i: (0, 0)),
        ],
        out_specs=[
            pl.BlockSpec((1, N_EXPERTS, RC_MOE, D_MODEL), lambda i: (i, 0, 0, 0)),
            pl.BlockSpec((1, N_EXPERTS, RC_MOE, LANES), lambda i: (i, 0, 0, 0)),
            pl.BlockSpec((1, SLOTS, TM_MOE), lambda i: (i, 0, 0)),
            pl.BlockSpec((1, N_EXPERTS, TM_MOE), lambda i: (i, 0, 0)),
            pl.BlockSpec((1, N_EXPERTS, TM_MOE), lambda i: (i, 0, 0)),
            pl.BlockSpec((1, N_EXPERTS, LANES), lambda i: (i, 0, 0)),
        ],
        out_shape=[
            jax.ShapeDtypeStruct((nt, N_EXPERTS, RC_MOE, D_MODEL), BF16),
            jax.ShapeDtypeStruct((nt, N_EXPERTS, RC_MOE, LANES), F32),
            jax.ShapeDtypeStruct((nt, SLOTS, TM_MOE), ONEHOT_DT),
            jax.ShapeDtypeStruct((nt, N_EXPERTS, TM_MOE), F32),
            jax.ShapeDtypeStruct((nt, N_EXPERTS, TM_MOE), F32),
            jax.ShapeDtypeStruct((nt, N_EXPERTS, LANES), jnp.int32),
        ],
        compiler_params=_params("parallel"),
        name="dispatch",
    )(gates, x1b, tril)


def _swiglu_ffn(x, wgu, bgu, wd, bd):
    h = _dot(x, wgu) + bgu
    g = jnp.minimum(h[:, :D_FF], SWIGLU_LIMIT)
    up = jnp.clip(h[:, D_FF:], -SWIGLU_LIMIT, SWIGLU_LIMIT)
    act = (up + 1.0) * (g / (1.0 + jnp.exp(-SWIGLU_ALPHA * g)))
    return _dot(act.astype(BF16), wd) + bd


def _ffn_kernel(tail_ref, xs_ref, gs_ref, wgu_ref, bgu_ref, wd_ref, bd_ref, ys_ref, wgu_bf_ref, wd_bf_ref):
    e = pl.program_id(0)
    j = pl.program_id(1)

    @pl.when(j == 0)
    def _():
        wgu_bf_ref[...] = wgu_ref[0, 0].astype(BF16)
        wd_bf_ref[...] = wd_ref[0, 0].astype(BF16)

    def run(lo, hi):
        rows = FFN_TILES * (hi - lo)
        x = xs_ref[:, 0, lo:hi, :].reshape(rows, D_MODEL)
        y = _swiglu_ffn(x, wgu_bf_ref[...], bgu_ref[0], wd_bf_ref[...], bd_ref[0])
        gate = gs_ref[:, 0, lo:hi, :].reshape(rows, LANES)[:, :1]
        ys_ref[:, 0, lo:hi, :] = (y * gate).astype(BF16).reshape(FFN_TILES, hi - lo, D_MODEL)

    run(0, RC_MAIN)
    used = tail_ref[e * pl.num_programs(1) + j] > 0

    @pl.when(used)
    def _():
        run(RC_MAIN, RC_MOE)

    @pl.when(jnp.logical_not(used))
    def _():
        ys_ref[:, 0, RC_MAIN:RC_MOE, :] = jnp.zeros((FFN_TILES, RC_MOE - RC_MAIN, D_MODEL), BF16)


def _ffn(tail, xs, gs, w_gate_up, bgu, w_down, bd, layer):
    nt = xs.shape[0]
    grid_spec = pltpu.PrefetchScalarGridSpec(
        num_scalar_prefetch=1,
        grid=(N_EXPERTS, nt // FFN_TILES),
        in_specs=[
            pl.BlockSpec((FFN_TILES, 1, RC_MOE, D_MODEL), lambda e, j, t: (j, e, 0, 0)),
            pl.BlockSpec((FFN_TILES, 1, RC_MOE, LANES), lambda e, j, t: (j, e, 0, 0)),
            pl.BlockSpec((1, 1, D_MODEL, 2 * D_FF), lambda e, j, t: (layer, e, 0, 0)),
            pl.BlockSpec((1, 1, 2 * D_FF), lambda e, j, t: (e, 0, 0)),
            pl.BlockSpec((1, 1, D_FF, D_MODEL), lambda e, j, t: (layer, e, 0, 0)),
            pl.BlockSpec((1, 1, D_MODEL), lambda e, j, t: (e, 0, 0)),
        ],
        out_specs=pl.BlockSpec((FFN_TILES, 1, RC_MOE, D_MODEL), lambda e, j, t: (j, e, 0, 0)),
        scratch_shapes=[
            pltpu.VMEM((D_MODEL, 2 * D_FF), BF16),
            pltpu.VMEM((D_FF, D_MODEL), BF16),
        ],
    )
    return pl.pallas_call(
        _ffn_kernel,
        grid_spec=grid_spec,
        out_shape=jax.ShapeDtypeStruct(xs.shape, BF16),
        compiler_params=_params("parallel", "arbitrary"),
        name="expert_ffn",
    )(tail, xs, gs, w_gate_up, bgu, w_down, bd)


def _combine_kernel(layer, cnt_ref, ys_ref, oh_ref, x1_ref, g_ref, b_ref, post_ref,
                    gatet_ref, bgu_ref, bd_ref, wgu_hbm, wd_hbm, o_ref, acc_ref, wgu_buf, wd_buf, sem):
    i = pl.program_id(0)
    tm = x1_ref.shape[0]
    width = DISPATCH_GROUP * RC_MOE
    acc = None
    for e0 in range(0, N_EXPERTS, DISPATCH_GROUP):
        sl = slice(e0 * RC_MOE, e0 * RC_MOE + width)
        part = lax.dot_general(oh_ref[0, sl, :].astype(BF16),
                               ys_ref[0, e0:e0 + DISPATCH_GROUP].reshape(width, D_MODEL),
                               _TN, preferred_element_type=F32)
        acc = part if acc is None else acc + part
    acc_ref[...] = acc

    slot = lax.broadcasted_iota(jnp.int32, (RC_MOE, tm), 0)

    def per_expert(e, carry):
        cnt = cnt_ref[i * N_EXPERTS + e]

        @pl.when(cnt > RC_MOE)
        def _():
            copies = (pltpu.make_async_copy(wgu_hbm.at[layer, e], wgu_buf, sem.at[0]),
                      pltpu.make_async_copy(wd_hbm.at[layer, e], wd_buf, sem.at[1]))
            for cp in copies:
                cp.start()
            for cp in copies:
                cp.wait()
            prow = post_ref[0, pl.ds(e, 1), :].astype(jnp.int32)
            grow = gatet_ref[0, pl.ds(e, 1), :]

            def chunk(c, carry2):
                onehot = (slot + c * RC_MOE) == prow
                oh = jnp.where(onehot, 1.0, 0.0).astype(BF16)
                xg = _dot(oh, x1_ref[...].astype(BF16)).astype(BF16)
                y = _swiglu_ffn(xg, wgu_buf[...].astype(BF16), bgu_ref[e], wd_buf[...].astype(BF16), bd_ref[e])
                gate = jnp.sum(jnp.where(onehot, grow, 0.0), axis=1, keepdims=True)
                yg = (y * gate).astype(BF16)
                acc_ref[...] += lax.dot_general(oh, yg, _TN, preferred_element_type=F32)
                return carry2

            lax.fori_loop(1, (cnt + RC_MOE - 1) // RC_MOE, chunk, 0)

        return carry

    lax.fori_loop(0, N_EXPERTS, per_expert, 0)
    o_ref[...] = _layer_norm(DN_ALPHA * x1_ref[...] + acc_ref[...], g_ref[...], b_ref[...])


def _combine(cnt, ys, oh, x1, g, b, post, gatet, bgu, bd, wgu, wd, layer):
    n = x1.shape[0]
    nt = n // TM_MOE
    row = lambda i, c: (i, 0)
    fixed2 = lambda i, c: (0, 0)
    fixed3 = lambda i, c: (0, 0, 0)
    grid_spec = pltpu.PrefetchScalarGridSpec(
        num_scalar_prefetch=1,
        grid=(nt,),
        in_specs=[
            pl.BlockSpec((1, N_EXPERTS, RC_MOE, D_MODEL), lambda i, c: (i, 0, 0, 0)),
            pl.BlockSpec((1, SLOTS, TM_MOE), lambda i, c: (i, 0, 0)),
            pl.BlockSpec((TM_MOE, D_MODEL), row),
            pl.BlockSpec((1, D_MODEL), fixed2),
            pl.BlockSpec((1, D_MODEL), fixed2),
            pl.BlockSpec((1, N_EXPERTS, TM_MOE), lambda i, c: (i, 0, 0)),
            pl.BlockSpec((1, N_EXPERTS, TM_MOE), lambda i, c: (i, 0, 0)),
            pl.BlockSpec((N_EXPERTS, 1, 2 * D_FF), fixed3),
            pl.BlockSpec((N_EXPERTS, 1, D_MODEL), fixed3),
            pl.BlockSpec(memory_space=pl.ANY),
            pl.BlockSpec(memory_space=pl.ANY),
        ],
        out_specs=pl.BlockSpec((TM_MOE, D_MODEL), row),
        scratch_shapes=[
            pltpu.VMEM((TM_MOE, D_MODEL), F32),
            pltpu.VMEM((D_MODEL, 2 * D_FF), F32),
            pltpu.VMEM((D_FF, D_MODEL), F32),
            pltpu.SemaphoreType.DMA((2,)),
        ],
    )
    return pl.pallas_call(
        functools.partial(_combine_kernel, layer),
        grid_spec=grid_spec,
        out_shape=jax.ShapeDtypeStruct((n, D_MODEL), F32),
        compiler_params=_params("arbitrary"),
        name="combine_ln",
    )(cnt, ys, oh, x1, g, b, post, gatet, bgu, bd, wgu, wd)


def _block_diag(pool_w):
    groups = pool_w.shape[0]
    out = jnp.zeros((POOL_W, POOL_W), pool_w.dtype)
    for gi in range(groups):
        out = out.at[gi * POOL_CH:(gi + 1) * POOL_CH, gi * POOL_CH:(gi + 1) * POOL_CH].set(pool_w[gi])
    return out


def _layer(x2, batch, seq, consts, layer, w_in, b_in, pool_w, pool_scale, ret_norm_g, w_out, b_out, ln1_g,
           ln1_b, router_w, router_b, w_gate_up, b_gate_up, w_down, b_down, ln2_g, ln2_b):
    ret_consts, tri2, tril = consts
    row = lambda a: a.reshape(1, -1).astype(F32)
    pool_o, sb, ret = _inproj(x2, w_in, layer, row(b_in), _block_diag(pool_w).astype(BF16),
                              row(pool_scale), seq)
    sb_o = _stick_breaking(sb, tri2, batch, seq)
    ret_o = _retention(ret, ret_consts, row(ret_norm_g), batch, seq)
    rw = jnp.pad(router_w.astype(F32), ((0, 0), (0, LANES - N_EXPERTS)))
    rw_hi = rw.astype(BF16)
    rw = jnp.concatenate([rw_hi, (rw - rw_hi.astype(F32)).astype(BF16)], axis=1)
    rb = jnp.pad(router_b.astype(F32), (0, LANES - N_EXPERTS)).reshape(1, LANES)
    x1, x1b, gates = _outproj(x2, pool_o, sb_o, ret_o, w_out, layer, row(b_out), row(ln1_g),
                              row(ln1_b), rw, rb)
    xs, gs, oh, post, gatet, cnt = _dispatch(gates, x1b, tril)
    bgu = b_gate_up.reshape(N_EXPERTS, 1, 2 * D_FF).astype(F32)
    bd = b_down.reshape(N_EXPERTS, 1, D_MODEL).astype(F32)
    cnt2 = cnt[:, :, 0]
    tail = jnp.any((cnt2 > RC_MAIN).reshape(-1, FFN_TILES, N_EXPERTS), axis=1)
    ys = _ffn(tail.T.reshape(-1).astype(jnp.int32), xs, gs, w_gate_up, bgu, w_down, bd, layer)
    cnt_flat = cnt2.reshape(-1)
    return _combine(cnt_flat, ys, oh, x1, row(ln2_g), row(ln2_b), post, gatet, bgu, bd, w_gate_up, w_down,
                    layer)


def kernel(x, w_in, b_in, pool_w, pool_scale, ret_norm_g, w_out, b_out, ln1_g, ln1_b, router_w, router_b,
           w_gate_up, b_gate_up, w_down, b_down, ln2_g, ln2_b):
    batch, seq, d = x.shape
    n = batch * seq
    assert d == D_MODEL and seq % TS_RET == 0 and seq % TQ_SB == 0 and seq % TM_ROW == 0
    assert n % (TM_MOE * FFN_TILES) == 0 and n % TM_ROW == 0 and seq >= SB_SPAN
    j = np.arange(LANES)
    tri = (j[:, None] >= j[None, :]).astype(np.float32)
    tri2 = jnp.asarray(np.concatenate([tri, np.ones_like(tri)], axis=1), BF16)
    r = np.arange(TM_MOE)
    tril = jnp.asarray((r[:, None] <= r[None, :]).astype(np.float32), BF16)
    consts = (_retention_consts(seq), tri2, tril)
    x2 = x.reshape(n, d)
    for l in range(DEPTH):
        x2 = _layer(x2, batch, seq, consts, l, w_in, b_in[l], pool_w[l], pool_scale[l], ret_norm_g[l],
                    w_out, b_out[l], ln1_g[l], ln1_b[l], router_w[l], router_b[l], w_gate_up,
                    b_gate_up[l], w_down, b_down[l], ln2_g[l], ln2_b[l])
    return x2.reshape(batch, seq, d)
```

```python
import functools

import numpy as np
import jax
import jax.numpy as jnp
from jax import lax
from jax.experimental import pallas as pl
from jax.experimental.pallas import tpu as pltpu

D_MODEL = 1024
DEPTH = 2
CHUNK = 64
HEAD_DIM = 64
POOL_CH = 64
POOL_W = 256
POOL_HALO = 16
SB_W = 384
RET_W = 384
RET_HEADS = 6
IN_W = POOL_W + 3 * SB_W + 4 * RET_W
ROPE_BASE = 10000.0
N_EXPERTS = 32
TOP_K = 4
D_FF = D_MODEL
SWIGLU_LIMIT = 7.0
SWIGLU_ALPHA = 1.702
DN_ALPHA = (2.0 * DEPTH) ** 0.25
LN_EPS = 1e-5

LANES = 128
VMEM_LIMIT = 56 * 1024 * 1024

TM_ROW = 1024
OUTPROJ_SUB = 256
TQ_SB = 512
SB_ROWS = 64
SB_SPAN = 256
TS_RET = 1024
TM_MOE = 512
RC_MOE = 96
RC_MAIN = 80
FFN_TILES = 8
SLOTS = N_EXPERTS * RC_MOE
DISPATCH_GROUP = 8
SB_SKIP = -100.0

BF16 = jnp.bfloat16
F32 = jnp.float32
ONEHOT_DT = jnp.float8_e4m3fn

_NT = (((1,), (1,)), ((), ()))
_TN = (((0,), (0,)), ((), ()))


def _dot(a, b):
    return jnp.dot(a, b, preferred_element_type=F32)


def _params(*sem):
    return pltpu.CompilerParams(dimension_semantics=sem, vmem_limit_bytes=VMEM_LIMIT)


def _pool_mix(cur, halo, t0, w, scale):
    ts = cur.shape[0]
    ext = jnp.concatenate([halo, cur], axis=0)
    a2 = ext[1:] + ext[:-1]
    a4 = a2[2:] + a2[:-2]
    a8 = a4[4:] + a4[:-4]
    a16 = a8[8:] + a8[:-8]
    lane = lax.broadcasted_iota(jnp.int32, (ts, POOL_W), 1)
    grp = lane // POOL_CH
    win = jnp.where(grp == 0, a2[15:15 + ts],
                    jnp.where(grp == 1, a4[13:13 + ts],
                              jnp.where(grp == 2, a8[9:9 + ts], a16[1:1 + ts])))
    width = jnp.where(grp == 0, 2, jnp.where(grp == 1, 4, jnp.where(grp == 2, 8, 16)))
    t = t0 + lax.broadcasted_iota(jnp.int32, (ts, POOL_W), 0)
    cnt = jnp.minimum(t + 1, width).astype(F32)
    pooled = win / cnt - cur
    return _dot(pooled.astype(BF16), w) * scale


def _inproj_kernel(tiles_per_seq, x_ref, w_ref, b_ref, pw_ref, ps_ref, pool_ref, sb_ref, ret_ref, halo_ref):
    it = pl.program_id(0) % tiles_per_seq
    xb = x_ref[...].astype(BF16)
    o1 = POOL_W + 3 * SB_W
    @pl.when(it == 0)
    def _():
        halo_ref[...] = jnp.zeros_like(halo_ref)

    u = _dot(xb, w_ref[:, :POOL_W]) + b_ref[:, :POOL_W]
    sb_ref[...] = (_dot(xb, w_ref[:, POOL_W:o1]) + b_ref[:, POOL_W:o1]).astype(BF16)
    ret_ref[...] = _dot(xb, w_ref[:, o1:]) + b_ref[:, o1:]
    pool_ref[...] = _pool_mix(u, halo_ref[...], it * TM_ROW, pw_ref[...], ps_ref[...]).astype(BF16)
    halo_ref[...] = u[TM_ROW - POOL_HALO:, :]


def _inproj(x2, w_bf, b, pool_w_bd, pool_scale, seq):
    n = x2.shape[0]
    fixed = lambda i: (0, 0)
    return pl.pallas_call(
        functools.partial(_inproj_kernel, seq // TM_ROW),
        grid=(n // TM_ROW,),
        in_specs=[
            pl.BlockSpec((TM_ROW, D_MODEL), lambda i: (i, 0)),
            pl.BlockSpec((D_MODEL, IN_W), fixed),
            pl.BlockSpec((1, IN_W), fixed),
            pl.BlockSpec((POOL_W, POOL_W), fixed),
            pl.BlockSpec((1, POOL_W), fixed),
        ],
        out_specs=[
            pl.BlockSpec((TM_ROW, POOL_W), lambda i: (i, 0)),
            pl.BlockSpec((TM_ROW, 3 * SB_W), lambda i: (i, 0)),
            pl.BlockSpec((TM_ROW, 4 * RET_W), lambda i: (i, 0)),
        ],
        out_shape=[
            jax.ShapeDtypeStruct((n, POOL_W), BF16),
            jax.ShapeDtypeStruct((n, 3 * SB_W), BF16),
            jax.ShapeDtypeStruct((n, 4 * RET_W), F32),
        ],
        scratch_shapes=[pltpu.VMEM((POOL_HALO, POOL_W), F32)],
        compiler_params=_params("arbitrary"),
        name="inproj_pool",
    )(x2, w_bf, b, pool_w_bd, pool_scale)


def _sb_kernel(q_ref, k_ref, v_ref, tri_ref, o_ref, acc_ref, carry_ref):
    qi = pl.program_id(1)
    tq = q_ref.shape[1]
    groups = q_ref.shape[2] // LANES
    lane = lax.broadcasted_iota(jnp.int32, (tq, LANES), 1)
    qh = []
    for p in range(groups):
        q = q_ref[0, :, p * LANES:(p + 1) * LANES] * jnp.asarray(HEAD_DIM ** -0.5, BF16)
        zero = jnp.zeros_like(q)
        qh += [jnp.where(lane < HEAD_DIM, q, zero), jnp.where(lane >= HEAD_DIM, q, zero)]

    def log_keep(z):
        return jnp.minimum(-z, 0.0) - jnp.log(1.0 + jnp.exp(-jnp.abs(z)))

    def suffix_sums(lk):
        lk = lk.astype(BF16)
        out = []
        for b in range(lk.shape[1] // LANES):
            rs = _dot(lk[:, b * LANES:(b + 1) * LANES], tri_ref[...])
            out.append((rs[:, :LANES], rs[:, LANES:]))
        return out

    t0 = qi * tq
    halves = range(tq // SB_ROWS)
    heads = range(2 * groups)
    cols = [slice((h // 2) * LANES, (h // 2 + 1) * LANES) for h in heads]
    rows = [slice(u * SB_ROWS, (u + 1) * SB_ROWS) for u in halves]
    starts = [pl.multiple_of(jnp.maximum(t0 + (u + 1) * SB_ROWS - SB_SPAN, 0), SB_ROWS) for u in halves]
    below = []
    for u in halves:
        qpos = t0 + u * SB_ROWS + lax.broadcasted_iota(jnp.int32, (SB_ROWS, SB_SPAN), 0)
        kpos = starts[u] + lax.broadcasted_iota(jnp.int32, (SB_ROWS, SB_SPAN), 1)
        below.append(kpos < qpos)
    items = [(u, h) for u in halves for h in heads]
    zs = {(u, h): lax.dot_general(qh[h][rows[u]], k_ref[0, pl.ds(starts[u], SB_SPAN), cols[h]], _NT,
                                  preferred_element_type=F32) for u, h in items}
    sums = {(u, h): suffix_sums(jnp.where(below[u], log_keep(zs[u, h]), 0.0)) for u, h in items}
    weights, runs = {}, {}
    for u, h in items:
        run = None
        r = [None] * len(sums[u, h])
        for b in reversed(range(len(r))):
            r_in, tot = sums[u, h][b]
            r[b] = r_in if run is None else r_in + run
            run = tot if run is None else run + tot
        weights[u, h] = jnp.where(below[u], jnp.exp(zs[u, h] + jnp.concatenate(r, axis=1)), 0.0).astype(BF16)
        runs[u, h] = run
    for u, h in items:
        acc_ref[h, rows[u], :] = _dot(weights[u, h], v_ref[0, pl.ds(starts[u], SB_SPAN), cols[h]])
        carry_ref[h, rows[u], :] = runs[u, h]
    top = None
    for u, h in items:
        top = runs[u, h] if top is None else jnp.maximum(top, runs[u, h])
    any_left = jnp.max(top) > SB_SKIP

    def tile(u, kb):
        start = pl.multiple_of(kb * LANES, LANES)
        older = (start + lax.broadcasted_iota(jnp.int32, (SB_ROWS, LANES), 1)) < starts[u]
        worst = None
        for h in heads:
            k = k_ref[0, pl.ds(start, LANES), cols[h]]
            v = v_ref[0, pl.ds(start, LANES), cols[h]]
            z = lax.dot_general(qh[h][rows[u]], k, _NT, preferred_element_type=F32)
            (r_in, tot), = suffix_sums(jnp.where(older, log_keep(z), 0.0))
            carry = carry_ref[h, rows[u], :]
            a = jnp.where(older, jnp.exp(z + r_in + carry), 0.0)
            acc_ref[h, rows[u], :] += _dot(a.astype(BF16), v)
            carry = carry + tot
            carry_ref[h, rows[u], :] = carry
            m = jnp.max(carry)
            worst = m if worst is None else jnp.maximum(worst, m)
        return worst

    def cond(state):
        kb, worst = state
        return jnp.logical_and(kb >= 0, worst > SB_SKIP)

    @pl.when(any_left)
    def _():
        for u in halves:
            worst = carry_ref[0, rows[u], :]
            for h in heads[1:]:
                worst = jnp.maximum(worst, carry_ref[h, rows[u], :])
            lax.while_loop(cond, lambda state, u=u: (state[0] - 1, tile(u, state[0])),
                           ((starts[u] + LANES - 1) // LANES - 1, jnp.max(worst)))

    for p in range(groups):
        o_ref[0, :, p * LANES:(p + 1) * LANES] = jnp.where(
            lane < HEAD_DIM, acc_ref[2 * p], acc_ref[2 * p + 1]).astype(BF16)


def _stick_breaking(sb, tri2, batch, seq):
    sb3 = sb.reshape(batch, seq, 3 * SB_W)
    heads = SB_W // HEAD_DIM
    once = pl.Buffered(1)
    out = pl.pallas_call(
        _sb_kernel,
        grid=(batch, seq // TQ_SB),
        in_specs=[
            pl.BlockSpec((1, TQ_SB, SB_W), lambda b, i: (b, i, 0)),
            pl.BlockSpec((1, seq, SB_W), lambda b, i: (b, 0, 1), pipeline_mode=once),
            pl.BlockSpec((1, seq, SB_W), lambda b, i: (b, 0, 2), pipeline_mode=once),
            pl.BlockSpec((LANES, 2 * LANES), lambda b, i: (0, 0)),
        ],
        out_specs=pl.BlockSpec((1, TQ_SB, SB_W), lambda b, i: (b, i, 0)),
        out_shape=jax.ShapeDtypeStruct((batch, seq, SB_W), BF16),
        scratch_shapes=[
            pltpu.VMEM((heads, TQ_SB, LANES), F32),
            pltpu.VMEM((heads, TQ_SB, LANES), F32),
        ],
        compiler_params=_params("parallel", "arbitrary"),
        name="stick_breaking",
    )(sb3, sb3, sb3, tri2)
    return out.reshape(batch * seq, SB_W)


def _ret_kernel(q_ref, k_ref, v_ref, g_ref, cos_ref, sin_ref, dmat_ref, dec_ref, xi_ref, gch_ref,
                swap_ref, avg_ref, ng_ref, o_ref, state_ref, obuf_ref):
    si = pl.program_id(1)
    ts = q_ref.shape[1]
    groups = q_ref.shape[2] // LANES
    cols = [slice(p * LANES, (p + 1) * LANES) for p in range(groups)]

    @pl.when(si == 0)
    def _():
        state_ref[...] = jnp.zeros_like(state_ref)

    cos = cos_ref[...]
    sin = sin_ref[...]

    def lane_mix(x, m_ref):
        hi = x.astype(BF16)
        lo = (x - hi.astype(F32)).astype(BF16)
        return _dot(jnp.concatenate([hi, lo], axis=1), m_ref[...])

    def rot(x):
        return x * cos + lane_mix(x, swap_ref) * sin

    qr = [rot(q_ref[0, :, c]) for c in cols]
    kr = [rot(k_ref[0, :, c]) * (HEAD_DIM ** -0.5) for c in cols]
    v = [v_ref[0, :, c] for c in cols]

    lane_c = lax.broadcasted_iota(jnp.int32, (CHUNK, LANES), 1)
    head0 = lane_c < HEAD_DIM
    r_i = lax.broadcasted_iota(jnp.int32, (LANES, LANES), 0) // HEAD_DIM
    c_i = lax.broadcasted_iota(jnp.int32, (LANES, LANES), 1) // HEAD_DIM
    same_head = r_i == c_i

    chunks = [slice(n * CHUNK, (n + 1) * CHUNK) for n in range(ts // CHUNK)]
    items = [(p, n) for p in range(groups) for n in range(len(chunks))]
    kcb = {(p, n): kr[p][chunks[n]].astype(BF16) for p, n in items}
    vcb = {(p, n): v[p][chunks[n]].astype(BF16) for p, n in items}
    kvs = {(p, n): lax.dot_general(kcb[p, n], (v[p][chunks[n]] * dec_ref[p]).astype(BF16), _TN,
                                   preferred_element_type=F32) for p, n in items}
    scores = {}
    for p, n in items:
        qc = qr[p][chunks[n]]
        qstack = jnp.concatenate([jnp.where(head0, qc, 0.0), jnp.where(head0, 0.0, qc)], axis=0)
        scores[p, n] = (lax.dot_general(qstack.astype(BF16), kcb[p, n], _NT, preferred_element_type=F32)
                        * dmat_ref[p]).astype(BF16)
    o2s = {(p, n): _dot(scores[p, n], vcb[p, n]) for p, n in items}
    states = {}
    for p in range(groups):
        state = state_ref[p]
        for n in range(len(chunks)):
            states[p, n] = state.astype(BF16)
            state = gch_ref[p] * state + jnp.where(same_head, kvs[p, n], 0.0)
        state_ref[p] = state
    for p, n in items:
        o_intra = jnp.where(head0, o2s[p, n][:CHUNK], o2s[p, n][CHUNK:])
        o_inter = _dot((qr[p][chunks[n]] * xi_ref[p]).astype(BF16), states[p, n])
        obuf_ref[chunks[n], cols[p]] = o_intra + o_inter

    for c in cols:
        o = obuf_ref[:, c]
        mu = lane_mix(o, avg_ref)
        oc = o - mu
        var = lane_mix(oc * oc, avg_ref)
        normed = oc * lax.rsqrt(var + LN_EPS) * ng_ref[:, c]
        gate = g_ref[0, :, c]
        o_ref[0, :, c] = (gate / (1.0 + jnp.exp(-gate)) * normed).astype(BF16)


def _retention(ret, consts, norm_g, batch, seq):
    cos, sin, dmat, dec, xi, gch, swap, avg = consts
    ret3 =ret.reshape(batch, seq, 4 * RET_W)
    pairs = RET_W // LANES
    fixed3 = lambda b, i: (0, 0, 0)
    out = pl.pallas_call(
        _ret_kernel,
        grid=(batch, seq // TS_RET),
        in_specs=[
            pl.BlockSpec((1, TS_RET, RET_W), lambda b, i: (b, i, 0)),
            pl.BlockSpec((1, TS_RET, RET_W), lambda b, i: (b, i, 1)),
            pl.BlockSpec((1, TS_RET, RET_W), lambda b, i: (b, i, 2)),
            pl.BlockSpec((1, TS_RET, RET_W), lambda b, i: (b, i, 3)),
            pl.BlockSpec((TS_RET, LANES), lambda b, i: (i, 0)),
            pl.BlockSpec((TS_RET, LANES), lambda b, i: (i, 0)),
            pl.BlockSpec((pairs, 2 * CHUNK, CHUNK), fixed3),
            pl.BlockSpec((pairs, CHUNK, LANES), fixed3),
            pl.BlockSpec((pairs, CHUNK, LANES), fixed3),
            pl.BlockSpec((pairs, 1, LANES), fixed3),
            pl.BlockSpec((2 * LANES, LANES), lambda b, i: (0, 0)),
            pl.BlockSpec((2 * LANES, LANES), lambda b, i: (0, 0)),
            pl.BlockSpec((1, RET_W), lambda b, i: (0, 0)),
        ],
        out_specs=pl.BlockSpec((1, TS_RET, RET_W), lambda b, i: (b, i, 0)),
        out_shape=jax.ShapeDtypeStruct((batch, seq, RET_W), BF16),
        scratch_shapes=[
            pltpu.VMEM((pairs, LANES, LANES), F32),
            pltpu.VMEM((TS_RET, RET_W), F32),
        ],
        compiler_params=_params("parallel", "arbitrary"),
        name="retention",
    )(ret3, ret3, ret3, ret3, cos, sin, dmat, dec, xi, gch, swap, avg, norm_g)
    return out.reshape(batch * seq, RET_W)


def _retention_consts(seq):
    half = HEAD_DIM // 2
    pos = jnp.arange(seq, dtype=F32)
    inv = ROPE_BASE ** (-jnp.arange(half, dtype=F32) / half)
    ang = pos[:, None] * inv[None, :]
    cos_h, sin_h = jnp.cos(ang), jnp.sin(ang)
    cos = jnp.tile(cos_h, (1, 2 * LANES // HEAD_DIM))
    sin = jnp.tile(jnp.concatenate([-sin_h, sin_h], axis=1), (1, LANES // HEAD_DIM))
    log_g = jnp.log(1.0 - 2.0 ** (-5.0 - jnp.arange(RET_HEADS, dtype=F32)))
    c = jnp.arange(CHUNK, dtype=F32)
    dmat = jnp.exp(jnp.abs(c[:, None] - c[None, :])[None] * log_g[:, None, None])
    dmat = dmat.reshape(RET_HEADS // 2, 2 * CHUNK, CHUNK)
    lane_log_g = jnp.repeat(log_g, HEAD_DIM).reshape(RET_HEADS // 2, 1, LANES)
    dec = jnp.exp((CHUNK - 1 - c)[None, :, None] * lane_log_g)
    xi = jnp.exp((c + 1.0)[None, :, None] * lane_log_g)
    gch = jnp.exp(CHUNK * lane_log_g)
    l = np.arange(LANES)
    partner = np.where(l % HEAD_DIM < half, l + half, l - half)
    swap = (l[:, None] == partner[None, :]).astype(np.float32)
    avg = (l[:, None] // HEAD_DIM == l[None, :] // HEAD_DIM).astype(np.float32) / HEAD_DIM
    swap = jnp.asarray(np.concatenate([swap, swap], axis=0), BF16)
    avg = jnp.asarray(np.concatenate([avg, avg], axis=0), BF16)
    return cos, sin, dmat, dec, xi, gch, swap, avg


def _layer_norm(z, g, b):
    mu = jnp.mean(z, axis=-1, keepdims=True)
    zc = z - mu
    var = jnp.mean(zc * zc, axis=-1, keepdims=True)
    return zc * lax.rsqrt(var + LN_EPS) * g + b


def _outproj_kernel(x_ref, p_ref, s_ref, r_ref, w_ref, b_ref, g_ref, be_ref, rw_ref, rb_ref,
                    x1_ref, x1b_ref, gates_ref):
    o1 = POOL_W + SB_W
    tm = x_ref.shape[0]
    subs = [slice(s, s + OUTPROJ_SUB) for s in range(0, tm, OUTPROJ_SUB)]
    ys = [_dot(p_ref[sl, :], w_ref[:POOL_W, :]) + _dot(s_ref[sl, :], w_ref[POOL_W:o1, :])
          + _dot(r_ref[sl, :], w_ref[o1:, :]) for sl in subs]
    x1s = [_layer_norm(DN_ALPHA * x_ref[sl, :] + (y + b_ref[...]), g_ref[...], be_ref[...])
           for sl, y in zip(subs, ys)]
    logits = []
    for sl, x1 in zip(subs, x1s):
        x1_ref[sl, :] = x1
        x_hi = x1.astype(BF16)
        x1b_ref[sl, :] = x_hi
        x_mid = (x1 - x_hi.astype(F32)).astype(BF16)
        t = _dot(x_hi, rw_ref[...])
        logits.append(t[:, :LANES] + t[:, LANES:] + _dot(x_mid, rw_ref[:, :LANES]) + rb_ref[...])
    expert = lax.broadcasted_iota(jnp.int32, (N_EXPERTS, OUTPROJ_SUB), 0).astype(F32)
    for sl, lg in zip(subs, logits):
        vals = lg.T[:N_EXPERTS, :]
        top_v, top_sel = [], []
        for _ in range(TOP_K):
            m = jnp.max(vals, axis=0, keepdims=True)
            idx = jnp.min(jnp.where(vals == m, expert, float(N_EXPERTS)), axis=0, keepdims=True)
            sel = expert == idx
            vals = jnp.where(sel, -jnp.inf, vals)
            top_v.append(m)
            top_sel.append(sel)
        ex = [jnp.exp(m - top_v[0]) for m in top_v]
        den = ex[0] + ex[1] + ex[2] + ex[3]
        gates = jnp.full((N_EXPERTS, OUTPROJ_SUB), -1.0, F32)
        for sel, e in zip(top_sel, ex):
            gates = jnp.where(sel, e / den, gates)
        gates_ref[:, sl] = gates


def _outproj(x2, pool_o, sb_o, ret_o, w_bf, b, g, be, rw, rb):
    n = x2.shape[0]
    row = lambda i: (i, 0)
    fixed = lambda i: (0, 0)
    return pl.pallas_call(
        _outproj_kernel,
        grid=(n // TM_ROW,),
        in_specs=[
            pl.BlockSpec((TM_ROW, D_MODEL), row),
            pl.BlockSpec((TM_ROW, POOL_W), row),
            pl.BlockSpec((TM_ROW, SB_W), row),
            pl.BlockSpec((TM_ROW, RET_W), row),
            pl.BlockSpec((D_MODEL, D_MODEL), fixed),
            pl.BlockSpec((1, D_MODEL), fixed),
            pl.BlockSpec((1, D_MODEL), fixed),
            pl.BlockSpec((1, D_MODEL), fixed),
            pl.BlockSpec((D_MODEL, 2 * LANES), fixed),
            pl.BlockSpec((1, LANES), fixed),
        ],
        out_specs=[
            pl.BlockSpec((TM_ROW, D_MODEL), row),
            pl.BlockSpec((TM_ROW, D_MODEL), row),
            pl.BlockSpec((N_EXPERTS, TM_ROW), lambda i: (0, i)),
        ],
        out_shape=[
            jax.ShapeDtypeStruct((n, D_MODEL), F32),
            jax.ShapeDtypeStruct((n, D_MODEL), BF16),
            jax.ShapeDtypeStruct((N_EXPERTS, n), F32),
        ],
        compiler_params=_params("parallel"),
        name="outproj_ln_router",
    )(x2, pool_o, sb_o, ret_o, w_bf, b, g, be, rw, rb)


def _dispatch_kernel(gates_ref, xb_ref, tril_ref, xs_ref, gs_ref, oh_ref, post_ref, gatet_ref, cnt_ref):
    tm = gates_ref.shape[1]
    gatet = gates_ref[...]
    sel = gatet >= 0.0
    self_ = jnp.where(sel, 1.0, 0.0)
    incl = _dot(self_.astype(BF16), tril_ref[...])
    pos = incl - self_
    cnt_ref[0] = jnp.broadcast_to(incl[:, tm - 1:tm], (N_EXPERTS, LANES)).astype(jnp.int32)
    post = jnp.where(sel, pos, -1.0)
    post_ref[0] = post
    gatet_ref[0] = gatet
    post_i = post.astype(jnp.int32)
    slot = lax.broadcasted_iota(jnp.int32, (RC_MOE, tm), 0)
    for e0 in range(0, N_EXPERTS, DISPATCH_GROUP):
        pieces = []
        for e in range(e0, e0 + DISPATCH_GROUP):
            onehot = slot == post_i[e:e + 1, :]
            pieces.append(jnp.where(onehot, 1.0, 0.0).astype(BF16))
            gate = jnp.sum(jnp.where(onehot, gatet[e:e + 1, :], 0.0), axis=1, keepdims=True)
            gs_ref[0, e] = jnp.broadcast_to(gate, (RC_MOE, LANES))
        onehots = jnp.concatenate(pieces, axis=0)
        oh_ref[0, e0 * RC_MOE:(e0 + DISPATCH_GROUP) * RC_MOE, :] = onehots.astype(ONEHOT_DT)
        xs = _dot(onehots, xb_ref[...]).astype(BF16)
        xs_ref[0, e0:e0 + DISPATCH_GROUP] = xs.reshape(DISPATCH_GROUP, RC_MOE, D_MODEL)


def _dispatch(gates, x1b, tril):
    n = gates.shape[1]
    nt = n // TM_MOE
    return pl.pallas_call(
        _dispatch_kernel,
        grid=(nt,),
        in_specs=[
            pl.BlockSpec((N_EXPERTS, TM_MOE), lambda i: (0, i)),
            pl.BlockSpec((TM_MOE, D_MODEL), lambda i: (i, 0)),
            pl.BlockSpec((TM_MOE, TM_MOE), lambda i: (0, 0)),
        ],
        out_specs=[
            pl.BlockSpec((1, N_EXPERTS, RC_MOE, D_MODEL), lambda i: (i, 0, 0, 0)),
            pl.BlockSpec((1, N_EXPERTS, RC_MOE, LANES), lambda i: (i, 0, 0, 0)),
            pl.BlockSpec((1, SLOTS, TM_MOE), lambda i: (i, 0, 0)),
            pl.BlockSpec((1, N_EXPERTS, TM_MOE), lambda i: (i, 0, 0)),
            pl.BlockSpec((1, N_EXPERTS, TM_MOE), lambda i: (i, 0, 0)),
            pl.BlockSpec((1, N_EXPERTS, LANES), lambda i: (i, 0, 0)),
        ],
        out_shape=[
            jax.ShapeDtypeStruct((nt, N_EXPERTS, RC_MOE, D_MODEL), BF16),
            jax.ShapeDtypeStruct((nt, N_EXPERTS, RC_MOE, LANES), F32),
            jax.ShapeDtypeStruct((nt, SLOTS, TM_MOE), ONEHOT_DT),
            jax.ShapeDtypeStruct((nt, N_EXPERTS, TM_MOE), F32),
            jax.ShapeDtypeStruct((nt, N_EXPERTS, TM_MOE), F32),
            jax.ShapeDtypeStruct((nt, N_EXPERTS, LANES), jnp.int32),
        ],
        compiler_params=_params("parallel"),
        name="dispatch",
    )(gates, x1b, tril)


def _swiglu_ffn(x, wgu, bgu, wd, bd):
    h = _dot(x, wgu) + bgu
    g = jnp.minimum(h[:, :D_FF], SWIGLU_LIMIT)
    up = jnp.clip(h[:, D_FF:], -SWIGLU_LIMIT, SWIGLU_LIMIT)
    act = (up + 1.0) * (g / (1.0 + jnp.exp(-SWIGLU_ALPHA * g)))
    return _dot(act.astype(BF16), wd) + bd


def _ffn_kernel(tail_ref, xs_ref, gs_ref, wgu_ref, bgu_ref, wd_ref, bd_ref, ys_ref, wgu_bf_ref, wd_bf_ref):
    e = pl.program_id(0)
    j = pl.program_id(1)

    @pl.when(j == 0)
    def _():
        wgu_bf_ref[...] = wgu_ref[0, 0].astype(BF16)
        wd_bf_ref[...] = wd_ref[0, 0].astype(BF16)

    def run(lo, hi):
        rows = FFN_TILES * (hi - lo)
        x = xs_ref[:, 0, lo:hi, :].reshape(rows, D_MODEL)
        y = _swiglu_ffn(x, wgu_bf_ref[...], bgu_ref[0], wd_bf_ref[...], bd_ref[0])
        gate = gs_ref[:, 0, lo:hi, :].reshape(rows, LANES)[:, :1]
        ys_ref[:, 0, lo:hi, :] = (y * gate).astype(BF16).reshape(FFN_TILES, hi - lo, D_MODEL)

    run(0, RC_MAIN)
    used = tail_ref[e * pl.num_programs(1) + j] > 0

    @pl.when(used)
    def _():
        run(RC_MAIN, RC_MOE)

    @pl.when(jnp.logical_not(used))
    def _():
        ys_ref[:, 0, RC_MAIN:RC_MOE, :] = jnp.zeros((FFN_TILES, RC_MOE - RC_MAIN, D_MODEL), BF16)


def _ffn(tail, xs, gs, w_gate_up, bgu, w_down, bd, layer):
    nt = xs.shape[0]
    grid_spec = pltpu.PrefetchScalarGridSpec(
        num_scalar_prefetch=1,
        grid=(N_EXPERTS, nt // FFN_TILES),
        in_specs=[
            pl.BlockSpec((FFN_TILES, 1, RC_MOE, D_MODEL), lambda e, j, t: (j, e, 0, 0)),
            pl.BlockSpec((FFN_TILES, 1, RC_MOE, LANES), lambda e, j, t: (j, e, 0, 0)),
            pl.BlockSpec((1, 1, D_MODEL, 2 * D_FF), lambda e, j, t: (layer, e, 0, 0)),
            pl.BlockSpec((1, 1, 2 * D_FF), lambda e, j, t: (e, 0, 0)),
            pl.BlockSpec((1, 1, D_FF, D_MODEL), lambda e, j, t: (layer, e, 0, 0)),
            pl.BlockSpec((1, 1, D_MODEL), lambda e, j, t: (e, 0, 0)),
        ],
        out_specs=pl.BlockSpec((FFN_TILES, 1, RC_MOE, D_MODEL), lambda e, j, t: (j, e, 0, 0)),
        scratch_shapes=[
            pltpu.VMEM((D_MODEL, 2 * D_FF), BF16),
            pltpu.VMEM((D_FF, D_MODEL), BF16),
        ],
    )
    return pl.pallas_call(
        _ffn_kernel,
        grid_spec=grid_spec,
        out_shape=jax.ShapeDtypeStruct(xs.shape, BF16),
        compiler_params=_params("parallel", "arbitrary"),
        name="expert_ffn",
    )(tail, xs, gs, w_gate_up, bgu, w_down, bd)


def _combine_kernel(layer, cnt_ref, ys_ref, oh_ref, x1_ref, g_ref, b_ref, post_ref,
                    gatet_ref, bgu_ref, bd_ref, wgu_hbm, wd_hbm, o_ref, acc_ref, wgu_buf, wd_buf, sem):
    i = pl.program_id(0)
    tm = x1_ref.shape[0]
    width = DISPATCH_GROUP * RC_MOE
    acc = None
    for e0 in range(0, N_EXPERTS, DISPATCH_GROUP):
        sl = slice(e0 * RC_MOE, e0 * RC_MOE + width)
        part = lax.dot_general(oh_ref[0, sl, :].astype(BF16),
                               ys_ref[0, e0:e0 + DISPATCH_GROUP].reshape(width, D_MODEL),
                               _TN, preferred_element_type=F32)
        acc = part if acc is None else acc + part
    acc_ref[...] = acc

    slot = lax.broadcasted_iota(jnp.int32, (RC_MOE, tm), 0)

    def per_expert(e, carry):
        cnt = cnt_ref[i * N_EXPERTS + e]

        @pl.when(cnt > RC_MOE)
        def _():
            copies = (pltpu.make_async_copy(wgu_hbm.at[layer, e], wgu_buf, sem.at[0]),
                      pltpu.make_async_copy(wd_hbm.at[layer, e], wd_buf, sem.at[1]))
            for cp in copies:
                cp.start()
            for cp in copies:
                cp.wait()
            prow = post_ref[0, pl.ds(e, 1), :].astype(jnp.int32)
            grow = gatet_ref[0, pl.ds(e, 1), :]

            def chunk(c, carry2):
                onehot = (slot + c * RC_MOE) == prow
                oh = jnp.where(onehot, 1.0, 0.0).astype(BF16)
                xg = _dot(oh, x1_ref[...].astype(BF16)).astype(BF16)
                y = _swiglu_ffn(xg, wgu_buf[...].astype(BF16), bgu_ref[e], wd_buf[...].astype(BF16), bd_ref[e])
                gate = jnp.sum(jnp.where(onehot, grow, 0.0), axis=1, keepdims=True)
                yg = (y * gate).astype(BF16)
                acc_ref[...] += lax.dot_general(oh, yg, _TN, preferred_element_type=F32)
                return carry2

            lax.fori_loop(1, (cnt + RC_MOE - 1) // RC_MOE, chunk, 0)

        return carry

    lax.fori_loop(0, N_EXPERTS, per_expert, 0)
    o_ref[...] = _layer_norm(DN_ALPHA * x1_ref[...] + acc_ref[...], g_ref[...], b_ref[...])


def _combine(cnt, ys, oh, x1, g, b, post, gatet, bgu, bd, wgu, wd, layer):
    n = x1.shape[0]
    nt = n // TM_MOE
    row = lambda i, c: (i, 0)
    fixed2 = lambda i, c: (0, 0)
    fixed3 = lambda i, c: (0, 0, 0)
    grid_spec = pltpu.PrefetchScalarGridSpec(
        num_scalar_prefetch=1,
        grid=(nt,),
        in_specs=[
            pl.BlockSpec((1, N_EXPERTS, RC_MOE, D_MODEL), lambda i, c: (i, 0, 0, 0)),
            pl.BlockSpec((1, SLOTS, TM_MOE), lambda i, c: (i, 0, 0)),
            pl.BlockSpec((TM_MOE, D_MODEL), row),
            pl.BlockSpec((1, D_MODEL), fixed2),
            pl.BlockSpec((1, D_MODEL), fixed2),
            pl.BlockSpec((1, N_EXPERTS, TM_MOE), lambda i, c: (i, 0, 0)),
            pl.BlockSpec((1, N_EXPERTS, TM_MOE), lambda i, c: (i, 0, 0)),
            pl.BlockSpec((N_EXPERTS, 1, 2 * D_FF), fixed3),
            pl.BlockSpec((N_EXPERTS, 1, D_MODEL), fixed3),
            pl.BlockSpec(memory_space=pl.ANY),
            pl.BlockSpec(memory_space=pl.ANY),
        ],
        out_specs=pl.BlockSpec((TM_MOE, D_MODEL), row),
        scratch_shapes=[
            pltpu.VMEM((TM_MOE, D_MODEL), F32),
            pltpu.VMEM((D_MODEL, 2 * D_FF), F32),
            pltpu.VMEM((D_FF, D_MODEL), F32),
            pltpu.SemaphoreType.DMA((2,)),
        ],
    )
    return pl.pallas_call(
        functools.partial(_combine_kernel, layer),
        grid_spec=grid_spec,
        out_shape=jax.ShapeDtypeStruct((n, D_MODEL), F32),
        compiler_params=_params("arbitrary"),
        name="combine_ln",
    )(cnt, ys, oh, x1, g, b, post, gatet, bgu, bd, wgu, wd)


def _block_diag(pool_w):
    groups = pool_w.shape[0]
    out = jnp.zeros((POOL_W, POOL_W), pool_w.dtype)
    for gi in range(groups):
        out = out.at[gi * POOL_CH:(gi + 1) * POOL_CH, gi * POOL_CH:(gi + 1) * POOL_CH].set(pool_w[gi])
    return out


def _layer(x2, batch, seq, consts, layer, w_in, b_in, pool_w, pool_scale, ret_norm_g, w_out, b_out, ln1_g,
           ln1_b, router_w, router_b, w_gate_up, b_gate_up, w_down, b_down, ln2_g, ln2_b):
    ret_consts, tri2, tril = consts
    row = lambda a: a.reshape(1, -1).astype(F32)
    pool_o, sb, ret = _inproj(x2, w_in.astype(BF16), row(b_in), _block_diag(pool_w).astype(BF16),
                              row(pool_scale), seq)
    sb_o = _stick_breaking(sb, tri2, batch, seq)
    ret_o = _retention(ret, ret_consts, row(ret_norm_g), batch, seq)
    rw = jnp.pad(router_w.astype(F32), ((0, 0), (0, LANES - N_EXPERTS)))
    rw_hi = rw.astype(BF16)
    rw = jnp.concatenate([rw_hi, (rw - rw_hi.astype(F32)).astype(BF16)], axis=1)
    rb = jnp.pad(router_b.astype(F32), (0, LANES - N_EXPERTS)).reshape(1, LANES)
    x1, x1b, gates = _outproj(x2, pool_o, sb_o, ret_o, w_out.astype(BF16), row(b_out), row(ln1_g),
                              row(ln1_b), rw, rb)
    xs, gs, oh, post, gatet, cnt = _dispatch(gates, x1b, tril)
    bgu = b_gate_up.reshape(N_EXPERTS, 1, 2 * D_FF).astype(F32)
    bd = b_down.reshape(N_EXPERTS, 1, D_MODEL).astype(F32)
    cnt2 = cnt[:, :, 0]
    tail = jnp.any((cnt2 > RC_MAIN).reshape(-1, FFN_TILES, N_EXPERTS), axis=1)
    ys = _ffn(tail.T.reshape(-1).astype(jnp.int32), xs, gs, w_gate_up, bgu, w_down, bd, layer)
    cnt_flat = cnt2.reshape(-1)
    return _combine(cnt_flat, ys, oh, x1, row(ln2_g), row(ln2_b), post, gatet, bgu, bd, w_gate_up, w_down,
                    layer)


def kernel(x, w_in, b_in, pool_w, pool_scale, ret_norm_g, w_out, b_out, ln1_g, ln1_b, router_w, router_b,
           w_gate_up, b_gate_up, w_down, b_down, ln2_g, ln2_b):
    batch, seq, d = x.shape
    n = batch * seq
    assert d == D_MODEL and seq % TS_RET == 0 and seq % TQ_SB == 0 and seq % TM_ROW == 0
    assert n % (TM_MOE * FFN_TILES) == 0 and n % TM_ROW == 0 and seq >= SB_SPAN
    j = np.arange(LANES)
    tri = (j[:, None] >= j[None, :]).astype(np.float32)
    tri2 = jnp.asarray(np.concatenate([tri, np.ones_like(tri)], axis=1), BF16)
    r = np.arange(TM_MOE)
    tril = jnp.asarray((r[:, None] <= r[None, :]).astype(np.float32), BF16)
    consts = (_retention_consts(seq), tri2, tril)
    x2 = x.reshape(n, d)
    for l in range(DEPTH):
        x2 = _layer(x2, batch, seq, consts, l, w_in[l], b_in[l], pool_w[l], pool_scale[l], ret_norm_g[l],
                    w_out[l], b_out[l], ln1_g[l], ln1_b[l], router_w[l], router_b[l], w_gate_up,
                    b_gate_up[l], w_down, b_down[l], ln2_g[l], ln2_b[l])
    return x2.reshape(batch, seq, d)
```

```python
import functools

import numpy as np
import jax
import jax.numpy as jnp
from jax import lax
from jax.experimental import pallas as pl
from jax.experimental.pallas import tpu as pltpu

D_MODEL = 1024
DEPTH = 2
CHUNK = 64
HEAD_DIM = 64
POOL_CH = 64
POOL_W = 256
POOL_HALO = 16
SB_W = 384
RET_W = 384
RET_HEADS = 6
IN_W = POOL_W + 3 * SB_W + 4 * RET_W
ROPE_BASE = 10000.0
N_EXPERTS = 32
TOP_K = 4
D_FF = D_MODEL
SWIGLU_LIMIT = 7.0
SWIGLU_ALPHA = 1.702
DN_ALPHA = (2.0 * DEPTH) ** 0.25
LN_EPS = 1e-5

LANES = 128
VMEM_LIMIT = 56 * 1024 * 1024

TM_ROW = 1024
OUTPROJ_SUB = 256
TQ_SB = 512
SB_ROWS = 64
SB_SPAN = 256
TS_RET = 1024
TM_MOE = 512
RC_MOE = 96
RC_MAIN = 80
FFN_TILES = 8
SLOTS = N_EXPERTS * RC_MOE
DISPATCH_GROUP = 8
SB_SKIP = -100.0

BF16 = jnp.bfloat16
F32 = jnp.float32
ONEHOT_DT = jnp.float8_e4m3fn

_NT = (((1,), (1,)), ((), ()))
_TN = (((0,), (0,)), ((), ()))


def _dot(a, b):
    return jnp.dot(a, b, preferred_element_type=F32)


def _params(*sem):
    return pltpu.CompilerParams(dimension_semantics=sem, vmem_limit_bytes=VMEM_LIMIT)


def _pool_mix(cur, halo, t0, w, scale):
    ts = cur.shape[0]
    ext = jnp.concatenate([halo, cur], axis=0)
    a2 = ext[1:] + ext[:-1]
    a4 = a2[2:] + a2[:-2]
    a8 = a4[4:] + a4[:-4]
    a16 = a8[8:] + a8[:-8]
    lane = lax.broadcasted_iota(jnp.int32, (ts, POOL_W), 1)
    grp = lane // POOL_CH
    win = jnp.where(grp == 0, a2[15:15 + ts],
                    jnp.where(grp == 1, a4[13:13 + ts],
                              jnp.where(grp == 2, a8[9:9 + ts], a16[1:1 + ts])))
    width = jnp.where(grp == 0, 2, jnp.where(grp == 1, 4, jnp.where(grp == 2, 8, 16)))
    t = t0 + lax.broadcasted_iota(jnp.int32, (ts, POOL_W), 0)
    cnt = jnp.minimum(t + 1, width).astype(F32)
    pooled = win / cnt - cur
    return _dot(pooled.astype(BF16), w) * scale


def _inproj_kernel(tiles_per_seq, x_ref, w_ref, b_ref, pw_ref, ps_ref, pool_ref, sb_ref, ret_ref, halo_ref):
    it = pl.program_id(0) % tiles_per_seq
    xb = x_ref[...].astype(BF16)
    o1 = POOL_W + 3 * SB_W
    @pl.when(it == 0)
    def _():
        halo_ref[...] = jnp.zeros_like(halo_ref)

    u = _dot(xb, w_ref[:, :POOL_W]) + b_ref[:, :POOL_W]
    sb_ref[...] = (_dot(xb, w_ref[:, POOL_W:o1]) + b_ref[:, POOL_W:o1]).astype(BF16)
    ret_ref[...] = _dot(xb, w_ref[:, o1:]) + b_ref[:, o1:]
    pool_ref[...] = _pool_mix(u, halo_ref[...], it * TM_ROW, pw_ref[...], ps_ref[...]).astype(BF16)
    halo_ref[...] = u[TM_ROW - POOL_HALO:, :]


def _inproj(x2, w_bf, b, pool_w_bd, pool_scale, seq):
    n = x2.shape[0]
    fixed = lambda i: (0, 0)
    return pl.pallas_call(
        functools.partial(_inproj_kernel, seq // TM_ROW),
        grid=(n // TM_ROW,),
        in_specs=[
            pl.BlockSpec((TM_ROW, D_MODEL), lambda i: (i, 0)),
            pl.BlockSpec((D_MODEL, IN_W), fixed),
            pl.BlockSpec((1, IN_W), fixed),
            pl.BlockSpec((POOL_W, POOL_W), fixed),
            pl.BlockSpec((1, POOL_W), fixed),
        ],
        out_specs=[
            pl.BlockSpec((TM_ROW, POOL_W), lambda i: (i, 0)),
            pl.BlockSpec((TM_ROW, 3 * SB_W), lambda i: (i, 0)),
            pl.BlockSpec((TM_ROW, 4 * RET_W), lambda i: (i, 0)),
        ],
        out_shape=[
            jax.ShapeDtypeStruct((n, POOL_W), BF16),
            jax.ShapeDtypeStruct((n, 3 * SB_W), BF16),
            jax.ShapeDtypeStruct((n, 4 * RET_W), F32),
        ],
        scratch_shapes=[pltpu.VMEM((POOL_HALO, POOL_W), F32)],
        compiler_params=_params("arbitrary"),
        name="inproj_pool",
    )(x2, w_bf, b, pool_w_bd, pool_scale)


def _sb_kernel(q_ref, k_ref, v_ref, tri_ref, o_ref, acc_ref, carry_ref):
    qi = pl.program_id(1)
    tq = q_ref.shape[1]
    groups = q_ref.shape[2] // LANES
    lane = lax.broadcasted_iota(jnp.int32, (tq, LANES), 1)
    qh = []
    for p in range(groups):
        q = q_ref[0, :, p * LANES:(p + 1) * LANES] * jnp.asarray(HEAD_DIM ** -0.5, BF16)
        zero = jnp.zeros_like(q)
        qh += [jnp.where(lane < HEAD_DIM, q, zero), jnp.where(lane >= HEAD_DIM, q, zero)]

    def log_keep(z):
        return jnp.minimum(-z, 0.0) - jnp.log(1.0 + jnp.exp(-jnp.abs(z)))

    def suffix_sums(lk):
        lk = lk.astype(BF16)
        out = []
        for b in range(lk.shape[1] // LANES):
            rs = _dot(lk[:, b * LANES:(b + 1) * LANES], tri_ref[...])
            out.append((rs[:, :LANES], rs[:, LANES:]))
        return out

    t0 = qi * tq
    halves = range(tq // SB_ROWS)
    heads = range(2 * groups)
    cols = [slice((h // 2) * LANES, (h // 2 + 1) * LANES) for h in heads]
    rows = [slice(u * SB_ROWS, (u + 1) * SB_ROWS) for u in halves]
    starts = [pl.multiple_of(jnp.maximum(t0 + (u + 1) * SB_ROWS - SB_SPAN, 0), SB_ROWS) for u in halves]
    below = []
    for u in halves:
        qpos = t0 + u * SB_ROWS + lax.broadcasted_iota(jnp.int32, (SB_ROWS, SB_SPAN), 0)
        kpos = starts[u] + lax.broadcasted_iota(jnp.int32, (SB_ROWS, SB_SPAN), 1)
        below.append(kpos < qpos)
    items = [(u, h) for u in halves for h in heads]
    zs = {(u, h): lax.dot_general(qh[h][rows[u]], k_ref[0, pl.ds(starts[u], SB_SPAN), cols[h]], _NT,
                                  preferred_element_type=F32) for u, h in items}
    sums = {(u, h): suffix_sums(jnp.where(below[u], log_keep(zs[u, h]), 0.0)) for u, h in items}
    weights, runs = {}, {}
    for u, h in items:
        run = None
        r = [None] * len(sums[u, h])
        for b in reversed(range(len(r))):
            r_in, tot = sums[u, h][b]
            r[b] = r_in if run is None else r_in + run
            run = tot if run is None else run + tot
        weights[u, h] = jnp.where(below[u], jnp.exp(zs[u, h] + jnp.concatenate(r, axis=1)), 0.0).astype(BF16)
        runs[u, h] = run
    for u, h in items:
        acc_ref[h, rows[u], :] = _dot(weights[u, h], v_ref[0, pl.ds(starts[u], SB_SPAN), cols[h]])
        carry_ref[h, rows[u], :] = runs[u, h]
    top = None
    for u, h in items:
        top = runs[u, h] if top is None else jnp.maximum(top, runs[u, h])
    any_left = jnp.max(top) > SB_SKIP

    def tile(u, kb):
        start = pl.multiple_of(kb * LANES, LANES)
        older = (start + lax.broadcasted_iota(jnp.int32, (SB_ROWS, LANES), 1)) < starts[u]
        worst = None
        for h in heads:
            k = k_ref[0, pl.ds(start, LANES), cols[h]]
            v = v_ref[0, pl.ds(start, LANES), cols[h]]
            z = lax.dot_general(qh[h][rows[u]], k, _NT, preferred_element_type=F32)
            (r_in, tot), = suffix_sums(jnp.where(older, log_keep(z), 0.0))
            carry = carry_ref[h, rows[u], :]
            a = jnp.where(older, jnp.exp(z + r_in + carry), 0.0)
            acc_ref[h, rows[u], :] += _dot(a.astype(BF16), v)
            carry = carry + tot
            carry_ref[h, rows[u], :] = carry
            m = jnp.max(carry)
            worst = m if worst is None else jnp.maximum(worst, m)
        return worst

    def cond(state):
        kb, worst = state
        return jnp.logical_and(kb >= 0, worst > SB_SKIP)

    @pl.when(any_left)
    def _():
        for u in halves:
            worst = carry_ref[0, rows[u], :]
            for h in heads[1:]:
                worst = jnp.maximum(worst, carry_ref[h, rows[u], :])
            lax.while_loop(cond, lambda state, u=u: (state[0] - 1, tile(u, state[0])),
                           ((starts[u] + LANES - 1) // LANES - 1, jnp.max(worst)))

    for p in range(groups):
        o_ref[0, :, p * LANES:(p + 1) * LANES] = jnp.where(
            lane < HEAD_DIM, acc_ref[2 * p], acc_ref[2 * p + 1]).astype(BF16)


def _stick_breaking(sb, tri2, batch, seq):
    sb3 = sb.reshape(batch, seq, 3 * SB_W)
    heads = SB_W // HEAD_DIM
    once = pl.Buffered(1)
    out = pl.pallas_call(
        _sb_kernel,
        grid=(batch, seq // TQ_SB),
        in_specs=[
            pl.BlockSpec((1, TQ_SB, SB_W), lambda b, i: (b, i, 0)),
            pl.BlockSpec((1, seq, SB_W), lambda b, i: (b, 0, 1), pipeline_mode=once),
            pl.BlockSpec((1, seq, SB_W), lambda b, i: (b, 0, 2), pipeline_mode=once),
            pl.BlockSpec((LANES, 2 * LANES), lambda b, i: (0, 0)),
        ],
        out_specs=pl.BlockSpec((1, TQ_SB, SB_W), lambda b, i: (b, i, 0)),
        out_shape=jax.ShapeDtypeStruct((batch, seq, SB_W), BF16),
        scratch_shapes=[
            pltpu.VMEM((heads, TQ_SB, LANES), F32),
            pltpu.VMEM((heads, TQ_SB, LANES), F32),
        ],
        compiler_params=_params("parallel", "arbitrary"),
        name="stick_breaking",
    )(sb3, sb3, sb3, tri2)
    return out.reshape(batch * seq, SB_W)


def _ret_kernel(q_ref, k_ref, v_ref, g_ref, cos_ref, sin_ref, dmat_ref, dec_ref, xi_ref, gch_ref,
                swap_ref, avg_ref, ng_ref, o_ref, state_ref, obuf_ref):
    si = pl.program_id(1)
    ts = q_ref.shape[1]
    groups = q_ref.shape[2] // LANES
    cols = [slice(p * LANES, (p + 1) * LANES) for p in range(groups)]

    @pl.when(si == 0)
    def _():
        state_ref[...] = jnp.zeros_like(state_ref)

    cos = cos_ref[...]
    sin = sin_ref[...]

    def lane_mix(x, m_ref):
        hi = x.astype(BF16)
        lo = (x - hi.astype(F32)).astype(BF16)
        return _dot(jnp.concatenate([hi, lo], axis=1), m_ref[...])

    def rot(x):
        return x * cos + lane_mix(x, swap_ref) * sin

    qr = [rot(q_ref[0, :, c]) for c in cols]
    kr = [rot(k_ref[0, :, c]) * (HEAD_DIM ** -0.5) for c in cols]
    v = [v_ref[0, :, c] for c in cols]

    lane_c = lax.broadcasted_iota(jnp.int32, (CHUNK, LANES), 1)
    head0 = lane_c < HEAD_DIM
    r_i = lax.broadcasted_iota(jnp.int32, (LANES, LANES), 0) // HEAD_DIM
    c_i = lax.broadcasted_iota(jnp.int32, (LANES, LANES), 1) // HEAD_DIM
    same_head = r_i == c_i

    chunks = [slice(n * CHUNK, (n + 1) * CHUNK) for n in range(ts // CHUNK)]
    items = [(p, n) for p in range(groups) for n in range(len(chunks))]
    kcb = {(p, n): kr[p][chunks[n]].astype(BF16) for p, n in items}
    vcb = {(p, n): v[p][chunks[n]].astype(BF16) for p, n in items}
    kvs = {(p, n): lax.dot_general(kcb[p, n], (v[p][chunks[n]] * dec_ref[p]).astype(BF16), _TN,
                                   preferred_element_type=F32) for p, n in items}
    scores = {}
    for p, n in items:
        qc = qr[p][chunks[n]]
        qstack = jnp.concatenate([jnp.where(head0, qc, 0.0), jnp.where(head0, 0.0, qc)], axis=0)
        scores[p, n] = (lax.dot_general(qstack.astype(BF16), kcb[p, n], _NT, preferred_element_type=F32)
                        * dmat_ref[p]).astype(BF16)
    o2s = {(p, n): _dot(scores[p, n], vcb[p, n]) for p, n in items}
    states = {}
    for p in range(groups):
        state = state_ref[p]
        for n in range(len(chunks)):
            states[p, n] = state.astype(BF16)
            state = gch_ref[p] * state + jnp.where(same_head, kvs[p, n], 0.0)
        state_ref[p] = state
    for p, n in items:
        o_intra = jnp.where(head0, o2s[p, n][:CHUNK], o2s[p, n][CHUNK:])
        o_inter = _dot((qr[p][chunks[n]] * xi_ref[p]).astype(BF16), states[p, n])
        obuf_ref[chunks[n], cols[p]] = o_intra + o_inter

    for c in cols:
        o = obuf_ref[:, c]
        mu = lane_mix(o, avg_ref)
        oc = o - mu
        var = lane_mix(oc * oc, avg_ref)
        normed = oc * lax.rsqrt(var + LN_EPS) * ng_ref[:, c]
        gate = g_ref[0, :, c]
        o_ref[0, :, c] = (gate / (1.0 + jnp.exp(-gate)) * normed).astype(BF16)


def _retention(ret, consts, norm_g, batch, seq):
    cos, sin, dmat, dec, xi, gch, swap, avg = consts
    ret3 =ret.reshape(batch, seq, 4 * RET_W)
    pairs = RET_W // LANES
    fixed3 = lambda b, i: (0, 0, 0)
    out = pl.pallas_call(
        _ret_kernel,
        grid=(batch, seq // TS_RET),
        in_specs=[
            pl.BlockSpec((1, TS_RET, RET_W), lambda b, i: (b, i, 0)),
            pl.BlockSpec((1, TS_RET, RET_W), lambda b, i: (b, i, 1)),
            pl.BlockSpec((1, TS_RET, RET_W), lambda b, i: (b, i, 2)),
            pl.BlockSpec((1, TS_RET, RET_W), lambda b, i: (b, i, 3)),
            pl.BlockSpec((TS_RET, LANES), lambda b, i: (i, 0)),
            pl.BlockSpec((TS_RET, LANES), lambda b, i: (i, 0)),
            pl.BlockSpec((pairs, 2 * CHUNK, CHUNK), fixed3),
            pl.BlockSpec((pairs, CHUNK, LANES), fixed3),
            pl.BlockSpec((pairs, CHUNK, LANES), fixed3),
            pl.BlockSpec((pairs, 1, LANES), fixed3),
            pl.BlockSpec((2 * LANES, LANES), lambda b, i: (0, 0)),
            pl.BlockSpec((2 * LANES, LANES), lambda b, i: (0, 0)),
            pl.BlockSpec((1, RET_W), lambda b, i: (0, 0)),
        ],
        out_specs=pl.BlockSpec((1, TS_RET, RET_W), lambda b, i: (b, i, 0)),
        out_shape=jax.ShapeDtypeStruct((batch, seq, RET_W), BF16),
        scratch_shapes=[
            pltpu.VMEM((pairs, LANES, LANES), F32),
            pltpu.VMEM((TS_RET, RET_W), F32),
        ],
        compiler_params=_params("parallel", "arbitrary"),
        name="retention",
    )(ret3, ret3, ret3, ret3, cos, sin, dmat, dec, xi, gch, swap, avg, norm_g)
    return out.reshape(batch * seq, RET_W)


def _retention_consts(seq):
    half = HEAD_DIM // 2
    pos = jnp.arange(seq, dtype=F32)
    inv = ROPE_BASE ** (-jnp.arange(half, dtype=F32) / half)
    ang = pos[:, None] * inv[None, :]
    cos_h, sin_h = jnp.cos(ang), jnp.sin(ang)
    cos = jnp.tile(cos_h, (1, 2 * LANES // HEAD_DIM))
    sin = jnp.tile(jnp.concatenate([-sin_h, sin_h], axis=1), (1, LANES // HEAD_DIM))
    log_g = jnp.log(1.0 - 2.0 ** (-5.0 - jnp.arange(RET_HEADS, dtype=F32)))
    c = jnp.arange(CHUNK, dtype=F32)
    dmat = jnp.exp(jnp.abs(c[:, None] - c[None, :])[None] * log_g[:, None, None])
    dmat = dmat.reshape(RET_HEADS // 2, 2 * CHUNK, CHUNK)
    lane_log_g = jnp.repeat(log_g, HEAD_DIM).reshape(RET_HEADS // 2, 1, LANES)
    dec = jnp.exp((CHUNK - 1 - c)[None, :, None] * lane_log_g)
    xi = jnp.exp((c + 1.0)[None, :, None] * lane_log_g)
    gch = jnp.exp(CHUNK * lane_log_g)
    l = np.arange(LANES)
    partner = np.where(l % HEAD_DIM < half, l + half, l - half)
    swap = (l[:, None] == partner[None, :]).astype(np.float32)
    avg = (l[:, None] // HEAD_DIM == l[None, :] // HEAD_DIM).astype(np.float32) / HEAD_DIM
    swap = jnp.asarray(np.concatenate([swap, swap], axis=0), BF16)
    avg = jnp.asarray(np.concatenate([avg, avg], axis=0), BF16)
    return cos, sin, dmat, dec, xi, gch, swap, avg


def _layer_norm(z, g, b):
    mu = jnp.mean(z, axis=-1, keepdims=True)
    zc = z - mu
    var = jnp.mean(zc * zc, axis=-1, keepdims=True)
    return zc * lax.rsqrt(var + LN_EPS) * g + b


def _outproj_kernel(x_ref, p_ref, s_ref, r_ref, w_ref, b_ref, g_ref, be_ref, rw_ref, rb_ref,
                    x1_ref, x1b_ref, gates_ref):
    o1 = POOL_W + SB_W
    tm = x_ref.shape[0]
    subs = [slice(s, s + OUTPROJ_SUB) for s in range(0, tm, OUTPROJ_SUB)]
    ys = [_dot(p_ref[sl, :], w_ref[:POOL_W, :]) + _dot(s_ref[sl, :], w_ref[POOL_W:o1, :])
          + _dot(r_ref[sl, :], w_ref[o1:, :]) for sl in subs]
    x1s = [_layer_norm(DN_ALPHA * x_ref[sl, :] + (y + b_ref[...]), g_ref[...], be_ref[...])
           for sl, y in zip(subs, ys)]
    logits = []
    for sl, x1 in zip(subs, x1s):
        x1_ref[sl, :] = x1
        x_hi = x1.astype(BF16)
        x1b_ref[sl, :] = x_hi
        x_mid = (x1 - x_hi.astype(F32)).astype(BF16)
        t = _dot(x_hi, rw_ref[...])
        logits.append(t[:, :LANES] + t[:, LANES:] + _dot(x_mid, rw_ref[:, :LANES]) + rb_ref[...])
    expert = lax.broadcasted_iota(jnp.int32, (N_EXPERTS, OUTPROJ_SUB), 0).astype(F32)
    for sl, lg in zip(subs, logits):
        vals = lg.T[:N_EXPERTS, :]
        top_v, top_sel = [], []
        for _ in range(TOP_K):
            m = jnp.max(vals, axis=0, keepdims=True)
            idx = jnp.min(jnp.where(vals == m, expert, float(N_EXPERTS)), axis=0, keepdims=True)
            sel = expert == idx
            vals = jnp.where(sel, -jnp.inf, vals)
            top_v.append(m)
            top_sel.append(sel)
        ex = [jnp.exp(m - top_v[0]) for m in top_v]
        den = ex[0] + ex[1] + ex[2] + ex[3]
        gates = jnp.full((N_EXPERTS, OUTPROJ_SUB), -1.0, F32)
        for sel, e in zip(top_sel, ex):
            gates = jnp.where(sel, e / den, gates)
        gates_ref[:, sl] = gates


def _outproj(x2, pool_o, sb_o, ret_o, w_bf, b, g, be, rw, rb):
    n = x2.shape[0]
    row = lambda i: (i, 0)
    fixed = lambda i: (0, 0)
    return pl.pallas_call(
        _outproj_kernel,
        grid=(n // TM_ROW,),
        in_specs=[
            pl.BlockSpec((TM_ROW, D_MODEL), row),
            pl.BlockSpec((TM_ROW, POOL_W), row),
            pl.BlockSpec((TM_ROW, SB_W), row),
            pl.BlockSpec((TM_ROW, RET_W), row),
            pl.BlockSpec((D_MODEL, D_MODEL), fixed),
            pl.BlockSpec((1, D_MODEL), fixed),
            pl.BlockSpec((1, D_MODEL), fixed),
            pl.BlockSpec((1, D_MODEL), fixed),
            pl.BlockSpec((D_MODEL, 2 * LANES), fixed),
            pl.BlockSpec((1, LANES), fixed),
        ],
        out_specs=[
            pl.BlockSpec((TM_ROW, D_MODEL), row),
            pl.BlockSpec((TM_ROW, D_MODEL), row),
            pl.BlockSpec((N_EXPERTS, TM_ROW), lambda i: (0, i)),
        ],
        out_shape=[
            jax.ShapeDtypeStruct((n, D_MODEL), F32),
            jax.ShapeDtypeStruct((n, D_MODEL), BF16),
            jax.ShapeDtypeStruct((N_EXPERTS, n), F32),
        ],
        compiler_params=_params("parallel"),
        name="outproj_ln_router",
    )(x2, pool_o, sb_o, ret_o, w_bf, b, g, be, rw, rb)


def _dispatch_kernel(gates_ref, xb_ref, tril_ref, xs_ref, gs_ref, oh_ref, post_ref, gatet_ref, cnt_ref):
    tm = gates_ref.shape[1]
    gatet = gates_ref[...]
    sel = gatet >= 0.0
    self_ = jnp.where(sel, 1.0, 0.0)
    incl = _dot(self_.astype(BF16), tril_ref[...])
    pos = incl - self_
    cnt_ref[0] = jnp.broadcast_to(incl[:, tm - 1:tm], (N_EXPERTS, LANES)).astype(jnp.int32)
    post = jnp.where(sel, pos, -1.0)
    post_ref[0] = post
    gatet_ref[0] = gatet
    post_i = post.astype(jnp.int32)
    slot = lax.broadcasted_iota(jnp.int32, (RC_MOE, tm), 0)
    for e0 in range(0, N_EXPERTS, DISPATCH_GROUP):
        pieces = []
        for e in range(e0, e0 + DISPATCH_GROUP):
            onehot = slot == post_i[e:e + 1, :]
            pieces.append(jnp.where(onehot, 1.0, 0.0).astype(BF16))
            gate = jnp.sum(jnp.where(onehot, gatet[e:e + 1, :], 0.0), axis=1, keepdims=True)
            gs_ref[0, e] = jnp.broadcast_to(gate, (RC_MOE, LANES))
        onehots = jnp.concatenate(pieces, axis=0)
        oh_ref[0, e0 * RC_MOE:(e0 + DISPATCH_GROUP) * RC_MOE, :] = onehots.astype(ONEHOT_DT)
        xs = _dot(onehots, xb_ref[...]).astype(BF16)
        xs_ref[0, e0:e0 + DISPATCH_GROUP] = xs.reshape(DISPATCH_GROUP, RC_MOE, D_MODEL)


def _dispatch(gates, x1b, tril):
    n = gates.shape[1]
    nt = n // TM_MOE
    return pl.pallas_call(
        _dispatch_kernel,
        grid=(nt,),
        in_specs=[
            pl.BlockSpec((N_EXPERTS, TM_MOE), lambda i: (0, i)),
            pl.BlockSpec((TM_MOE, D_MODEL), lambda i: (i, 0)),
            pl.BlockSpec((TM_MOE, TM_MOE), lambda i: (0, 0)),
        ],
        out_specs=[
            pl.BlockSpec((1, N_EXPERTS, RC_MOE, D_MODEL), lambda i: (i, 0, 0, 0)),
            pl.BlockSpec((1, N_EXPERTS, RC_MOE, LANES), lambda i: (i, 0, 0, 0)),
            pl.BlockSpec((1, SLOTS, TM_MOE), lambda i: (i, 0, 0)),
            pl.BlockSpec((1, N_EXPERTS, TM_MOE), lambda i: (i, 0, 0)),
            pl.BlockSpec((1, N_EXPERTS, TM_MOE), lambda i: (i, 0, 0)),
            pl.BlockSpec((1, N_EXPERTS, LANES), lambda i: (i, 0, 0)),
        ],
        out_shape=[
            jax.ShapeDtypeStruct((nt, N_EXPERTS, RC_MOE, D_MODEL), BF16),
            jax.ShapeDtypeStruct((nt, N_EXPERTS, RC_MOE, LANES), F32),
            jax.ShapeDtypeStruct((nt, SLOTS, TM_MOE), ONEHOT_DT),
            jax.ShapeDtypeStruct((nt, N_EXPERTS, TM_MOE), F32),
            jax.ShapeDtypeStruct((nt, N_EXPERTS, TM_MOE), F32),
            jax.ShapeDtypeStruct((nt, N_EXPERTS, LANES), jnp.int32),
        ],
        compiler_params=_params("parallel"),
        name="dispatch",
    )(gates, x1b, tril)


def _swiglu_ffn(x, wgu, bgu, wd, bd):
    h = _dot(x, wgu) + bgu
    g = jnp.minimum(h[:, :D_FF], SWIGLU_LIMIT)
    up = jnp.clip(h[:, D_FF:], -SWIGLU_LIMIT, SWIGLU_LIMIT)
    act = (up + 1.0) * (g / (1.0 + jnp.exp(-SWIGLU_ALPHA * g)))
    return _dot(act.astype(BF16), wd) + bd


def _ffn_kernel(tail_ref, xs_ref, gs_ref, wgu_ref, bgu_ref, wd_ref, bd_ref, ys_ref, wgu_bf_ref, wd_bf_ref):
    e = pl.program_id(0)
    j = pl.program_id(1)

    @pl.when(j == 0)
    def _():
        wgu_bf_ref[...] = wgu_ref[0, 0].astype(BF16)
        wd_bf_ref[...] = wd_ref[0, 0].astype(BF16)

    def run(lo, hi):
        rows = FFN_TILES * (hi - lo)
        x = xs_ref[:, 0, lo:hi, :].reshape(rows, D_MODEL)
        y = _swiglu_ffn(x, wgu_bf_ref[...], bgu_ref[0], wd_bf_ref[...], bd_ref[0])
        gate = gs_ref[:, 0, lo:hi, :].reshape(rows, LANES)[:, :1]
        ys_ref[:, 0, lo:hi, :] = (y * gate).astype(BF16).reshape(FFN_TILES, hi - lo, D_MODEL)

    run(0, RC_MAIN)
    used = tail_ref[e * pl.num_programs(1) + j] > 0

    @pl.when(used)
    def _():
        run(RC_MAIN, RC_MOE)

    @pl.when(jnp.logical_not(used))
    def _():
        ys_ref[:, 0, RC_MAIN:RC_MOE, :] = jnp.zeros((FFN_TILES, RC_MOE - RC_MAIN, D_MODEL), BF16)


def _ffn(tail, xs, gs, w_gate_up, bgu, w_down, bd, layer):
    nt = xs.shape[0]
    grid_spec = pltpu.PrefetchScalarGridSpec(
        num_scalar_prefetch=1,
        grid=(N_EXPERTS, nt // FFN_TILES),
        in_specs=[
            pl.BlockSpec((FFN_TILES, 1, RC_MOE, D_MODEL), lambda e, j, t: (j, e, 0, 0)),
            pl.BlockSpec((FFN_TILES, 1, RC_MOE, LANES), lambda e, j, t: (j, e, 0, 0)),
            pl.BlockSpec((1, 1, D_MODEL, 2 * D_FF), lambda e, j, t: (layer, e, 0, 0)),
            pl.BlockSpec((1, 1, 2 * D_FF), lambda e, j, t: (e, 0, 0)),
            pl.BlockSpec((1, 1, D_FF, D_MODEL), lambda e, j, t: (layer, e, 0, 0)),
            pl.BlockSpec((1, 1, D_MODEL), lambda e, j, t: (e, 0, 0)),
        ],
        out_specs=pl.BlockSpec((FFN_TILES, 1, RC_MOE, D_MODEL), lambda e, j, t: (j, e, 0, 0)),
        scratch_shapes=[
            pltpu.VMEM((D_MODEL, 2 * D_FF), BF16),
            pltpu.VMEM((D_FF, D_MODEL), BF16),
        ],
    )
    return pl.pallas_call(
        _ffn_kernel,
        grid_spec=grid_spec,
        out_shape=jax.ShapeDtypeStruct(xs.shape, BF16),
        compiler_params=_params("parallel", "arbitrary"),
        name="expert_ffn",
    )(tail, xs, gs, w_gate_up, bgu, w_down, bd)


def _combine_kernel(layer, cnt_ref, ys_ref, oh_ref, x1_ref, g_ref, b_ref, post_ref,
                    gatet_ref, bgu_ref, bd_ref, wgu_hbm, wd_hbm, o_ref, acc_ref, wgu_buf, wd_buf, sem):
    i = pl.program_id(0)
    tm = x1_ref.shape[0]
    width = DISPATCH_GROUP * RC_MOE
    acc = None
    for e0 in range(0, N_EXPERTS, DISPATCH_GROUP):
        sl = slice(e0 * RC_MOE, e0 * RC_MOE + width)
        part = lax.dot_general(oh_ref[0, sl, :].astype(BF16),
                               ys_ref[0, e0:e0 + DISPATCH_GROUP].reshape(width, D_MODEL),
                               _TN, preferred_element_type=F32)
        acc = part if acc is None else acc + part
    acc_ref[...] = acc

    slot = lax.broadcasted_iota(jnp.int32, (RC_MOE, tm), 0)

    def per_expert(e, carry):
        cnt = cnt_ref[i * N_EXPERTS + e]

        @pl.when(cnt > RC_MOE)
        def _():
            copies = (pltpu.make_async_copy(wgu_hbm.at[layer, e], wgu_buf, sem.at[0]),
                      pltpu.make_async_copy(wd_hbm.at[layer, e], wd_buf, sem.at[1]))
            for cp in copies:
                cp.start()
            for cp in copies:
                cp.wait()
            prow = post_ref[0, pl.ds(e, 1), :].astype(jnp.int32)
            grow = gatet_ref[0, pl.ds(e, 1), :]

            def chunk(c, carry2):
                onehot = (slot + c * RC_MOE) == prow
                oh = jnp.where(onehot, 1.0, 0.0).astype(BF16)
                xg = _dot(oh, x1_ref[...].astype(BF16)).astype(BF16)
                y = _swiglu_ffn(xg, wgu_buf[...].astype(BF16), bgu_ref[e], wd_buf[...].astype(BF16), bd_ref[e])
                gate = jnp.sum(jnp.where(onehot, grow, 0.0), axis=1, keepdims=True)
                yg = (y * gate).astype(BF16)
                acc_ref[...] += lax.dot_general(oh, yg, _TN, preferred_element_type=F32)
                return carry2

            lax.fori_loop(1, (cnt + RC_MOE - 1) // RC_MOE, chunk, 0)

        return carry

    @pl.when(cnt_ref[pl.num_programs(0) * N_EXPERTS + i] > 0)
    def _():
        lax.fori_loop(0, N_EXPERTS, per_expert, 0)

    o_ref[...] = _layer_norm(DN_ALPHA * x1_ref[...] + acc_ref[...], g_ref[...], b_ref[...])


def _combine(cnt, ys, oh, x1, g, b, post, gatet, bgu, bd, wgu, wd, layer):
    n = x1.shape[0]
    nt = n // TM_MOE
    row = lambda i, c: (i, 0)
    fixed2 = lambda i, c: (0, 0)
    fixed3 = lambda i, c: (0, 0, 0)
    grid_spec = pltpu.PrefetchScalarGridSpec(
        num_scalar_prefetch=1,
        grid=(nt,),
        in_specs=[
            pl.BlockSpec((1, N_EXPERTS, RC_MOE, D_MODEL), lambda i, c: (i, 0, 0, 0)),
            pl.BlockSpec((1, SLOTS, TM_MOE), lambda i, c: (i, 0, 0)),
            pl.BlockSpec((TM_MOE, D_MODEL), row),
            pl.BlockSpec((1, D_MODEL), fixed2),
            pl.BlockSpec((1, D_MODEL), fixed2),
            pl.BlockSpec((1, N_EXPERTS, TM_MOE), lambda i, c: (i, 0, 0)),
            pl.BlockSpec((1, N_EXPERTS, TM_MOE), lambda i, c: (i, 0, 0)),
            pl.BlockSpec((N_EXPERTS, 1, 2 * D_FF), fixed3),
            pl.BlockSpec((N_EXPERTS, 1, D_MODEL), fixed3),
            pl.BlockSpec(memory_space=pl.ANY),
            pl.BlockSpec(memory_space=pl.ANY),
        ],
        out_specs=pl.BlockSpec((TM_MOE, D_MODEL), row),
        scratch_shapes=[
            pltpu.VMEM((TM_MOE, D_MODEL), F32),
            pltpu.VMEM((D_MODEL, 2 * D_FF), F32),
            pltpu.VMEM((D_FF, D_MODEL), F32),
            pltpu.SemaphoreType.DMA((2,)),
        ],
    )
    return pl.pallas_call(
        functools.partial(_combine_kernel, layer),
        grid_spec=grid_spec,
        out_shape=jax.ShapeDtypeStruct((n, D_MODEL), F32),
        compiler_params=_params("arbitrary"),
        name="combine_ln",
    )(cnt, ys, oh, x1, g, b, post, gatet, bgu, bd, wgu, wd)


def _block_diag(pool_w):
    groups = pool_w.shape[0]
    out = jnp.zeros((POOL_W, POOL_W), pool_w.dtype)
    for gi in range(groups):
        out = out.at[gi * POOL_CH:(gi + 1) * POOL_CH, gi * POOL_CH:(gi + 1) * POOL_CH].set(pool_w[gi])
    return out


def _layer(x2, batch, seq, consts, layer, w_in, b_in, pool_w, pool_scale, ret_norm_g, w_out, b_out, ln1_g,
           ln1_b, router_w, router_b, w_gate_up, b_gate_up, w_down, b_down, ln2_g, ln2_b):
    ret_consts, tri2, tril = consts
    row = lambda a: a.reshape(1, -1).astype(F32)
    pool_o, sb, ret = _inproj(x2, w_in.astype(BF16), row(b_in), _block_diag(pool_w).astype(BF16),
                              row(pool_scale), seq)
    sb_o = _stick_breaking(sb, tri2, batch, seq)
    ret_o = _retention(ret, ret_consts, row(ret_norm_g), batch, seq)
    rw = jnp.pad(router_w.astype(F32), ((0, 0), (0, LANES - N_EXPERTS)))
    rw_hi = rw.astype(BF16)
    rw = jnp.concatenate([rw_hi, (rw - rw_hi.astype(F32)).astype(BF16)], axis=1)
    rb = jnp.pad(router_b.astype(F32), (0, LANES - N_EXPERTS)).reshape(1, LANES)
    x1, x1b, gates = _outproj(x2, pool_o, sb_o, ret_o, w_out.astype(BF16), row(b_out), row(ln1_g),
                              row(ln1_b), rw, rb)
    xs, gs, oh, post, gatet, cnt = _dispatch(gates, x1b, tril)
    bgu = b_gate_up.reshape(N_EXPERTS, 1, 2 * D_FF).astype(F32)
    bd = b_down.reshape(N_EXPERTS, 1, D_MODEL).astype(F32)
    cnt2 = cnt[:, :, 0]
    tail = jnp.any((cnt2 > RC_MAIN).reshape(-1, FFN_TILES, N_EXPERTS), axis=1)
    ys = _ffn(tail.T.reshape(-1).astype(jnp.int32), xs, gs, w_gate_up, bgu, w_down, bd, layer)
    over = jnp.any(cnt2 > RC_MOE, axis=1).astype(jnp.int32)
    cnt_flat = jnp.concatenate([cnt2.reshape(-1), over])
    return _combine(cnt_flat, ys, oh, x1, row(ln2_g), row(ln2_b), post, gatet, bgu, bd, w_gate_up, w_down,
                    layer)


def kernel(x, w_in, b_in, pool_w, pool_scale, ret_norm_g, w_out, b_out, ln1_g, ln1_b, router_w, router_b,
           w_gate_up, b_gate_up, w_down, b_down, ln2_g, ln2_b):
    batch, seq, d = x.shape
    n = batch * seq
    assert d == D_MODEL and seq % TS_RET == 0 and seq % TQ_SB == 0 and seq % TM_ROW == 0
    assert n % (TM_MOE * FFN_TILES) == 0 and n % TM_ROW == 0 and seq >= SB_SPAN
    j = np.arange(LANES)
    tri = (j[:, None] >= j[None, :]).astype(np.float32)
    tri2 = jnp.asarray(np.concatenate([tri, np.ones_like(tri)], axis=1), BF16)
    r = np.arange(TM_MOE)
    tril = jnp.asarray((r[:, None] <= r[None, :]).astype(np.float32), BF16)
    consts = (_retention_consts(seq), tri2, tril)
    x2 = x.reshape(n, d)
    for l in range(DEPTH):
        x2 = _layer(x2, batch, seq, consts, l, w_in[l], b_in[l], pool_w[l], pool_scale[l], ret_norm_g[l],
                    w_out[l], b_out[l], ln1_g[l], ln1_b[l], router_w[l], router_b[l], w_gate_up,
                    b_gate_up[l], w_down, b_down[l], ln2_g[l], ln2_b[l])
    return x2.reshape(batch, seq, d)
```
